```python
import math
import jax, jax.numpy as jnp
from jax import lax
import numpy as np

D_MODEL = 1024
BATCH = 8
SEQ = 4096
DEPTH = 2

CHUNK = 64
Q_BLOCK = 128
CONV_DIM = D_MODEL // 2
CONV_WIDTH = 3
HG_HEADS = 4
HG_DK = 128
HG_DV = (D_MODEL // 2) // HG_HEADS
HG_FDIM = HG_HEADS * HG_DK
HG_WIDTH = HG_HEADS * HG_DV
SB_HEADS = 16
SB_HEAD_DIM = D_MODEL // SB_HEADS
SB_WIDTH = SB_HEADS * SB_HEAD_DIM
D_FF = 4 * D_MODEL
N_EVEN = (DEPTH + 1) // 2
N_ODD = DEPTH // 2
AB_IN = 3 * CONV_DIM + 2 * HG_FDIM + 2 * HG_WIDTH
AB_MIX = CONV_DIM + HG_WIDTH
AB_SPLITS = [CONV_DIM, 2 * CONV_DIM, 3 * CONV_DIM,
             3 * CONV_DIM + HG_FDIM, 3 * CONV_DIM + 2 * HG_FDIM,
             3 * CONV_DIM + 2 * HG_FDIM + HG_WIDTH]
EPS = 1e-6

kernel_name = "hybrid_chunk_causal_conv_hgrn2_stickbreak"


def rms_norm(x, g):
    xf = x.astype(jnp.float32)
    y = xf * lax.rsqrt(jnp.mean(xf * xf, axis=-1, keepdims=True) + EPS)
    return (y * g.astype(jnp.float32)).astype(x.dtype)


def causal_depthwise_conv(u, w):
    width, ch = w.shape
    return lax.conv_general_dilated(
        u, w[:, None, :], window_strides=(1,), padding=[(width - 1, 0)],
        dimension_numbers=('NWC', 'WIO', 'NWC'), feature_group_count=ch)


def hgrn2_chunkwise(q, k, v, log_f):
    B, S, H, Dk = q.shape
    Dv = v.shape[-1]
    n = S // CHUNK

    def to_chunks(t):
        return t.reshape(B, n, CHUNK, H, t.shape[-1]).transpose(1, 0, 3, 2, 4)

    qc, kc, vc, gc = to_chunks(q), to_chunks(k), to_chunks(v), to_chunks(log_f)
    bc = jnp.cumsum(gc, axis=3)
    causal = jnp.tril(jnp.ones((CHUNK, CHUNK), dtype=bool))

    def step(state, inp):
        q_, k_, v_, b_ = inp
        inter = jnp.einsum('bhtk,bhkv->bhtv', q_ * jnp.exp(b_), state)
        diff = jnp.where(causal[None, None, :, :, None],
                         b_[:, :, :, None, :] - b_[:, :, None, :, :], -jnp.inf)
        decay = jnp.exp(diff)
        scores = jnp.einsum('bhtk,bhsk,bhtsk->bhts', q_, k_, decay)
        intra = jnp.einsum('bhts,bhsv->bhtv', scores, v_)
        b_last = b_[:, :, -1:, :]
        new_state = (jnp.exp(b_last[:, :, 0, :])[..., None] * state
                     + jnp.einsum('bhsk,bhsv->bhkv', k_ * jnp.exp(b_last - b_), v_))
        return new_state, inter + intra

    init = jnp.zeros((B, H, Dk, Dv), jnp.float32)
    _, out = lax.scan(step, init, (qc, kc, vc, bc))
    return out.transpose(1, 0, 3, 2, 4).reshape(B, S, H, Dv)


def stick_breaking_attention(q, k, v):
    S = q.shape[2]
    scale = SB_HEAD_DIM ** -0.5
    outs = []
    for blk in range(S // Q_BLOCK):
        q0 = blk * Q_BLOCK
        end = q0 + Q_BLOCK
        qb = q[:, :, q0:end]
        kb = k[:, :, :end]
        vb = v[:, :, :end]
        z = jnp.einsum('bhqd,bhkd->bhqk', qb, kb) * scale
        qpos = q0 + jnp.arange(Q_BLOCK)
        kpos = jnp.arange(end)
        mask = (kpos[None, :] < qpos[:, None])[None, None]
        log_beta = jax.nn.log_sigmoid(z)
        log_one_minus = jnp.where(mask, log_beta - z, 0.0)
        later = lax.cumsum(log_one_minus, axis=3, reverse=True) - log_one_minus
        w = jnp.where(mask, jnp.exp(log_beta + later), 0.0)
        outs.append(jnp.einsum('bhqk,bhkd->bhqd', w, vb))
    return jnp.concatenate(outs, axis=2)


def mixer_conv_hgrn(h, w_in, conv_w, hg_norm, lower_bound, w_out):
    B, S, _ = h.shape
    u = h @ w_in
    a_b, a_c, a_h, hq, hf, hi, hg = jnp.split(u, AB_SPLITS, axis=-1)
    y_a = a_b * causal_depthwise_conv(a_c * a_h, conv_w)
    f = lower_bound + (1.0 - lower_bound) * jax.nn.sigmoid(hf.astype(jnp.float32))
    log_f = jnp.log(f)
    k_in = 1.0 - f
    heads_k = lambda t: t.reshape(B, S, HG_HEADS, HG_DK)
    o = hgrn2_chunkwise(heads_k(hq.astype(jnp.float32)), heads_k(k_in),
                        hi.astype(jnp.float32).reshape(B, S, HG_HEADS, HG_DV), heads_k(log_f))
    o = rms_norm(o, hg_norm.reshape(HG_HEADS, HG_DV))
    y_b = (o.reshape(B, S, HG_WIDTH) * jax.nn.silu(hg.astype(jnp.float32))).astype(h.dtype)
    return jnp.concatenate([y_a, y_b], axis=-1) @ w_out


def mixer_stick_breaking(h, w_qkv, q_norm, k_norm, w_out):
    B, S, _ = h.shape
    qkv = (h @ w_qkv).reshape(B, S, 3, SB_HEADS, SB_HEAD_DIM)
    q = rms_norm(qkv[:, :, 0], q_norm)
    k = rms_norm(qkv[:, :, 1], k_norm)
    v = qkv[:, :, 2]
    to_bhsd = lambda t: t.astype(jnp.float32).transpose(0, 2, 1, 3)
    o = stick_breaking_attention(to_bhsd(q), to_bhsd(k), to_bhsd(v))
    o = o.transpose(0, 2, 1, 3).reshape(B, S, SB_WIDTH).astype(h.dtype)
    return o @ w_out


def _fwd_setup_inputs(seed: int = 0) -> dict:
    key = jax.random.key(seed)
    ks = jax.random.split(key, 20)
    nrm = lambda k, shape, s: jax.random.normal(k, shape, jnp.float32) * s
    gain = lambda k, shape: 1.0 + 0.02 * jax.random.normal(k, shape, jnp.float32)
    return {
        "x": nrm(ks[0], (BATCH, SEQ, D_MODEL), 1.0),
        "c": nrm(ks[1], (BATCH, D_MODEL), 1.0),
        "ada_w": nrm(ks[2], (DEPTH, D_MODEL, 6 * D_MODEL), 0.5 * D_MODEL ** -0.5),
        "ada_b": nrm(ks[3], (DEPTH, 6 * D_MODEL), 0.02),
        "norm_mix": gain(ks[4], (DEPTH, D_MODEL)),
        "norm_mlp": gain(ks[5], (DEPTH, D_MODEL)),
        "w_in_ab": nrm(ks[6], (N_EVEN, D_MODEL, AB_IN), D_MODEL ** -0.5),
        "conv_w": nrm(ks[7], (N_EVEN, CONV_WIDTH, CONV_DIM), CONV_WIDTH ** -0.5),
        "hg_norm": gain(ks[8], (N_EVEN, HG_WIDTH)),
        "lb_logits": nrm(ks[9], (DEPTH + 1, HG_FDIM), 0.1),
        "w_out_ab": nrm(ks[10], (N_EVEN, AB_MIX, D_MODEL), AB_MIX ** -0.5),
        "w_qkv": nrm(ks[11], (N_ODD, D_MODEL, 3 * SB_WIDTH), D_MODEL ** -0.5),
        "q_norm": gain(ks[12], (N_ODD, SB_HEAD_DIM)),
        "k_norm": gain(ks[13], (N_ODD, SB_HEAD_DIM)),
        "w_out_c": nrm(ks[14], (N_ODD, SB_WIDTH, D_MODEL), SB_WIDTH ** -0.5),
        "mlp_w1": nrm(ks[15], (DEPTH, D_MODEL, D_FF), D_MODEL ** -0.5),
        "mlp_w2": nrm(ks[16], (DEPTH, D_FF, D_MODEL), D_FF ** -0.5),
    }


def _fwd_reference(x, c, ada_w, ada_b, norm_mix, norm_mlp, w_in_ab, conv_w, hg_norm,
              lb_logits, w_out_ab, w_qkv, q_norm, k_norm, w_out_c, mlp_w1, mlp_w2):
    c_act = jax.nn.silu(c)
    lower_bounds = jnp.cumsum(jax.nn.softmax(lb_logits.astype(jnp.float32), axis=0), axis=0)
    for layer in range(DEPTH):
        mod = c_act @ ada_w[layer] + ada_b[layer]
        shift1, scale1, gate1, shift2, scale2, gate2 = jnp.split(mod[:, None, :], 6, axis=-1)
        h = rms_norm(x, norm_mix[layer]) * (1.0 + scale1) + shift1
        j = layer // 2
        if layer % 2 == 0:
            y = mixer_conv_hgrn(h, w_in_ab[j], conv_w[j], hg_norm[j],
                                lower_bounds[layer], w_out_ab[j])
        else:
            y = mixer_stick_breaking(h, w_qkv[j], q_norm[j], k_norm[j], w_out_c[j])
        x = x + gate1 * y
        h = rms_norm(x, norm_mlp[layer]) * (1.0 + scale2) + shift2
        x = x + gate2 * (jnp.square(jax.nn.relu(h @ mlp_w1[layer])) @ mlp_w2[layer])
    return x


import jax as _jax
import jax.numpy as _jnp

TWIN_FORMAT = 'train_step'
FWD_PARAMS = ['x', 'c', 'ada_w', 'ada_b', 'norm_mix', 'norm_mlp', 'w_in_ab', 'conv_w', 'hg_norm', 'lb_logits', 'w_out_ab', 'w_qkv', 'q_norm', 'k_norm', 'w_out_c', 'mlp_w1', 'mlp_w2']
TWIN_WEIGHTS = ['ada_w', 'ada_b', 'norm_mix', 'norm_mlp', 'w_in_ab', 'conv_w', 'hg_norm', 'lb_logits', 'w_out_ab', 'w_qkv', 'q_norm', 'k_norm', 'w_out_c', 'mlp_w1', 'mlp_w2']
TWIN_DIFF_INPUT = 'x'
TWIN_INPUTS = ['x', 'c', 'ada_w', 'ada_b', 'norm_mix', 'norm_mlp', 'w_in_ab', 'conv_w', 'hg_norm', 'lb_logits', 'w_out_ab', 'w_qkv', 'q_norm', 'k_norm', 'w_out_c', 'mlp_w1', 'mlp_w2', 'loss_target', 'm_ada_w', 'm_ada_b', 'm_norm_mix', 'm_norm_mlp', 'm_w_in_ab', 'm_conv_w', 'm_hg_norm', 'm_lb_logits', 'm_w_out_ab', 'm_w_qkv', 'm_q_norm', 'm_k_norm', 'm_w_out_c', 'm_mlp_w1', 'm_mlp_w2', 'v_ada_w', 'v_ada_b', 'v_norm_mix', 'v_norm_mlp', 'v_w_in_ab', 'v_conv_w', 'v_hg_norm', 'v_lb_logits', 'v_w_out_ab', 'v_w_qkv', 'v_q_norm', 'v_k_norm', 'v_w_out_c', 'v_mlp_w1', 'v_mlp_w2']
TWIN_OUTPUTS = ['loss', 'grad_x', 'grad_ada_w', 'grad_ada_b', 'grad_norm_mix', 'grad_norm_mlp', 'grad_w_in_ab', 'grad_conv_w', 'grad_hg_norm', 'grad_lb_logits', 'grad_w_out_ab', 'grad_w_qkv', 'grad_q_norm', 'grad_k_norm', 'grad_w_out_c', 'grad_mlp_w1', 'grad_mlp_w2', 'delta_ada_w', 'delta_ada_b', 'delta_norm_mix', 'delta_norm_mlp', 'delta_w_in_ab', 'delta_conv_w', 'delta_hg_norm', 'delta_lb_logits', 'delta_w_out_ab', 'delta_w_qkv', 'delta_q_norm', 'delta_k_norm', 'delta_w_out_c', 'delta_mlp_w1', 'delta_mlp_w2', 'new_m_ada_w', 'new_m_ada_b', 'new_m_norm_mix', 'new_m_norm_mlp', 'new_m_w_in_ab', 'new_m_conv_w', 'new_m_hg_norm', 'new_m_lb_logits', 'new_m_w_out_ab', 'new_m_w_qkv', 'new_m_q_norm', 'new_m_k_norm', 'new_m_w_out_c', 'new_m_mlp_w1', 'new_m_mlp_w2', 'new_v_ada_w', 'new_v_ada_b', 'new_v_norm_mix', 'new_v_norm_mlp', 'new_v_w_in_ab', 'new_v_conv_w', 'new_v_hg_norm', 'new_v_lb_logits', 'new_v_w_out_ab', 'new_v_w_qkv', 'new_v_q_norm', 'new_v_k_norm', 'new_v_w_out_c', 'new_v_mlp_w1', 'new_v_mlp_w2']
TWIN_LEAF_KINDS = {'loss': 'loss', 'grad_x': 'grad_x', 'grad_ada_w': 'grad_w', 'grad_ada_b': 'grad_w', 'grad_norm_mix': 'grad_w', 'grad_norm_mlp': 'grad_w', 'grad_w_in_ab': 'grad_w', 'grad_conv_w': 'grad_w', 'grad_hg_norm': 'grad_w', 'grad_lb_logits': 'grad_w', 'grad_w_out_ab': 'grad_w', 'grad_w_qkv': 'grad_w', 'grad_q_norm': 'grad_w', 'grad_k_norm': 'grad_w', 'grad_w_out_c': 'grad_w', 'grad_mlp_w1': 'grad_w', 'grad_mlp_w2': 'grad_w', 'delta_ada_w': 'delta_w', 'delta_ada_b': 'delta_w', 'delta_norm_mix': 'delta_w', 'delta_norm_mlp': 'delta_w', 'delta_w_in_ab': 'delta_w', 'delta_conv_w': 'delta_w', 'delta_hg_norm': 'delta_w', 'delta_lb_logits': 'delta_w', 'delta_w_out_ab': 'delta_w', 'delta_w_qkv': 'delta_w', 'delta_q_norm': 'delta_w', 'delta_k_norm': 'delta_w', 'delta_w_out_c': 'delta_w', 'delta_mlp_w1': 'delta_w', 'delta_mlp_w2': 'delta_w', 'new_m_ada_w': 'new_m', 'new_m_ada_b': 'new_m', 'new_m_norm_mix': 'new_m', 'new_m_norm_mlp': 'new_m', 'new_m_w_in_ab': 'new_m', 'new_m_conv_w': 'new_m', 'new_m_hg_norm': 'new_m', 'new_m_lb_logits': 'new_m', 'new_m_w_out_ab': 'new_m', 'new_m_w_qkv': 'new_m', 'new_m_q_norm': 'new_m', 'new_m_k_norm': 'new_m', 'new_m_w_out_c': 'new_m', 'new_m_mlp_w1': 'new_m', 'new_m_mlp_w2': 'new_m', 'new_v_ada_w': 'new_v', 'new_v_ada_b': 'new_v', 'new_v_norm_mix': 'new_v', 'new_v_norm_mlp': 'new_v', 'new_v_w_in_ab': 'new_v', 'new_v_conv_w': 'new_v', 'new_v_hg_norm': 'new_v', 'new_v_lb_logits': 'new_v', 'new_v_w_out_ab': 'new_v', 'new_v_w_qkv': 'new_v', 'new_v_q_norm': 'new_v', 'new_v_k_norm': 'new_v', 'new_v_w_out_c': 'new_v', 'new_v_mlp_w1': 'new_v', 'new_v_mlp_w2': 'new_v'}


def _forward(args):
    return _fwd_reference(*[args[k] for k in FWD_PARAMS])


def _output_shape():
    out = _jax.eval_shape(lambda: _forward(_fwd_setup_inputs(0)))
    return out.shape, out.dtype

N_MICROBATCH = 1
ADAM_LR = 0.001
ADAM_B1 = 0.9
ADAM_B2 = 0.999
ADAM_EPS = 1e-08
ADAM_WD = 0.01
ADAM_STEP = 10
PER_EXAMPLE_BATCH_AXIS = {'x': 0, 'c': 0, 'loss_target': 0}
SHARED_INPUTS = []
_WEIGHT_DTYPES = {'ada_w': _jnp.float32, 'ada_b': _jnp.float32, 'norm_mix': _jnp.float32, 'norm_mlp': _jnp.float32, 'w_in_ab': _jnp.float32, 'conv_w': _jnp.float32, 'hg_norm': _jnp.float32, 'lb_logits': _jnp.float32, 'w_out_ab': _jnp.float32, 'w_qkv': _jnp.float32, 'q_norm': _jnp.float32, 'k_norm': _jnp.float32, 'w_out_c': _jnp.float32, 'mlp_w1': _jnp.float32, 'mlp_w2': _jnp.float32}
MOMENT_SCALE = {'ada_w': 3.845427e+00, 'ada_b': 8.119758e+00, 'norm_mix': 6.091785e+00, 'norm_mlp': 1.269733e+01, 'w_in_ab': 2.124512e-01, 'conv_w': 3.015619e+00, 'hg_norm': 1.358256e+00, 'lb_logits': 2.609565e-02, 'w_out_ab': 1.773820e-01, 'w_qkv': 3.822402e-01, 'q_norm': 3.121517e+00, 'k_norm': 3.107790e+00, 'w_out_c': 6.368866e-01, 'mlp_w1': 4.562637e-01, 'mlp_w2': 1.690484e+00}


def _to_microbatches(a, axis):
    t = _jnp.moveaxis(a, axis, 0)
    t = t.reshape((N_MICROBATCH, t.shape[0] // N_MICROBATCH) + t.shape[1:])
    return _jnp.moveaxis(t, 1, axis + 1)


def setup_inputs(seed: int = 0) -> dict:
    inp = _fwd_setup_inputs(seed)
    key = _jax.random.fold_in(_jax.random.key(seed), 7919)
    shape, _ = _output_shape()
    out = dict(inp)
    out["loss_target"] = _jax.random.normal(_jax.random.fold_in(key, 0), shape, _jnp.float32)
    for i, name in enumerate(TWIN_WEIGHTS):
        w = inp[name].astype(_jnp.float32)
        if MOMENT_SCALE is None:
            s = _jnp.sqrt(_jnp.mean(_jnp.square(w)) + 1e-30)
        else:
            s = MOMENT_SCALE[name]
        km, kv = _jax.random.split(_jax.random.fold_in(key, i + 1))
        out[name] = w
        out["m_" + name] = s * _jax.random.normal(km, w.shape, _jnp.float32)
        out["v_" + name] = (s * s) * _jax.random.uniform(kv, w.shape, _jnp.float32, 0.5, 1.5)
    if N_MICROBATCH > 1:
        for name, axis in PER_EXAMPLE_BATCH_AXIS.items():
            out[name] = _to_microbatches(out[name], axis)
    return {'x': out['x'], 'c': out['c'], 'ada_w': out['ada_w'], 'ada_b': out['ada_b'], 'norm_mix': out['norm_mix'], 'norm_mlp': out['norm_mlp'], 'w_in_ab': out['w_in_ab'], 'conv_w': out['conv_w'], 'hg_norm': out['hg_norm'], 'lb_logits': out['lb_logits'], 'w_out_ab': out['w_out_ab'], 'w_qkv': out['w_qkv'], 'q_norm': out['q_norm'], 'k_norm': out['k_norm'], 'w_out_c': out['w_out_c'], 'mlp_w1': out['mlp_w1'], 'mlp_w2': out['mlp_w2'], 'loss_target': out['loss_target'], 'm_ada_w': out['m_ada_w'], 'm_ada_b': out['m_ada_b'], 'm_norm_mix': out['m_norm_mix'], 'm_norm_mlp': out['m_norm_mlp'], 'm_w_in_ab': out['m_w_in_ab'], 'm_conv_w': out['m_conv_w'], 'm_hg_norm': out['m_hg_norm'], 'm_lb_logits': out['m_lb_logits'], 'm_w_out_ab': out['m_w_out_ab'], 'm_w_qkv': out['m_w_qkv'], 'm_q_norm': out['m_q_norm'], 'm_k_norm': out['m_k_norm'], 'm_w_out_c': out['m_w_out_c'], 'm_mlp_w1': out['m_mlp_w1'], 'm_mlp_w2': out['m_mlp_w2'], 'v_ada_w': out['v_ada_w'], 'v_ada_b': out['v_ada_b'], 'v_norm_mix': out['v_norm_mix'], 'v_norm_mlp': out['v_norm_mlp'], 'v_w_in_ab': out['v_w_in_ab'], 'v_conv_w': out['v_conv_w'], 'v_hg_norm': out['v_hg_norm'], 'v_lb_logits': out['v_lb_logits'], 'v_w_out_ab': out['v_w_out_ab'], 'v_w_qkv': out['v_w_qkv'], 'v_q_norm': out['v_q_norm'], 'v_k_norm': out['v_k_norm'], 'v_w_out_c': out['v_w_out_c'], 'v_mlp_w1': out['v_mlp_w1'], 'v_mlp_w2': out['v_mlp_w2']}


def _loss(weights, diff, rest, loss_target):
    with _jax.named_scope("forward"):
        args = {**rest, TWIN_DIFF_INPUT: diff, **{k: w.astype(_WEIGHT_DTYPES[k]) for k, w in weights.items()}}
        y = _forward(args)
    with _jax.named_scope("loss_head"):
        err = _jnp.square(y.astype(_jnp.float32) - loss_target)
        return 0.5 * _jnp.sum(_jnp.mean(err, axis=-1)) if err.ndim else 0.5 * err


def _adamw(w, g, m, v):
    m = ADAM_B1 * m + (1.0 - ADAM_B1) * g
    v = ADAM_B2 * v + (1.0 - ADAM_B2) * _jnp.square(g)
    m_hat = m / (1.0 - ADAM_B1 ** ADAM_STEP)
    v_hat = v / (1.0 - ADAM_B2 ** ADAM_STEP)
    delta = -ADAM_LR * (m_hat / (_jnp.sqrt(v_hat) + ADAM_EPS) + ADAM_WD * w)
    return delta, m, v


def reference(x, c, ada_w, ada_b, norm_mix, norm_mlp, w_in_ab, conv_w, hg_norm, lb_logits, w_out_ab, w_qkv, q_norm, k_norm, w_out_c, mlp_w1, mlp_w2, loss_target, m_ada_w, m_ada_b, m_norm_mix, m_norm_mlp, m_w_in_ab, m_conv_w, m_hg_norm, m_lb_logits, m_w_out_ab, m_w_qkv, m_q_norm, m_k_norm, m_w_out_c, m_mlp_w1, m_mlp_w2, v_ada_w, v_ada_b, v_norm_mix, v_norm_mlp, v_w_in_ab, v_conv_w, v_hg_norm, v_lb_logits, v_w_out_ab, v_w_qkv, v_q_norm, v_k_norm, v_w_out_c, v_mlp_w1, v_mlp_w2):
    given = dict(x=x, c=c, ada_w=ada_w, ada_b=ada_b, norm_mix=norm_mix, norm_mlp=norm_mlp, w_in_ab=w_in_ab, conv_w=conv_w, hg_norm=hg_norm, lb_logits=lb_logits, w_out_ab=w_out_ab, w_qkv=w_qkv, q_norm=q_norm, k_norm=k_norm, w_out_c=w_out_c, mlp_w1=mlp_w1, mlp_w2=mlp_w2, loss_target=loss_target, m_ada_w=m_ada_w, m_ada_b=m_ada_b, m_norm_mix=m_norm_mix, m_norm_mlp=m_norm_mlp, m_w_in_ab=m_w_in_ab, m_conv_w=m_conv_w, m_hg_norm=m_hg_norm, m_lb_logits=m_lb_logits, m_w_out_ab=m_w_out_ab, m_w_qkv=m_w_qkv, m_q_norm=m_q_norm, m_k_norm=m_k_norm, m_w_out_c=m_w_out_c, m_mlp_w1=m_mlp_w1, m_mlp_w2=m_mlp_w2, v_ada_w=v_ada_w, v_ada_b=v_ada_b, v_norm_mix=v_norm_mix, v_norm_mlp=v_norm_mlp, v_w_in_ab=v_w_in_ab, v_conv_w=v_conv_w, v_hg_norm=v_hg_norm, v_lb_logits=v_lb_logits, v_w_out_ab=v_w_out_ab, v_w_qkv=v_w_qkv, v_q_norm=v_q_norm, v_k_norm=v_k_norm, v_w_out_c=v_w_out_c, v_mlp_w1=v_mlp_w1, v_mlp_w2=v_mlp_w2)
    weights = {n: given[n] for n in TWIN_WEIGHTS}
    shared = {n: given[n] for n in SHARED_INPUTS}
    per_example = {n: given[n] for n in ['x', 'c']}
    grad_fn = _jax.value_and_grad(_loss, argnums=(0, 1))

    def one_microbatch(ex, loss_target):
        ex = dict(ex)
        diff = ex.pop(TWIN_DIFF_INPUT)
        return grad_fn(weights, diff, {**shared, **ex}, loss_target)

    if N_MICROBATCH == 1:
        loss, (grad_w, grad_x) = one_microbatch(per_example, given["loss_target"])
    else:
        def body(carry, xs):
            loss_sum, grad_sum = carry
            l_k, (gw_k, gx_k) = one_microbatch(xs[0], xs[1])
            with _jax.named_scope("update"):
                return (loss_sum + l_k, _jax.tree.map(_jnp.add, grad_sum, gw_k)), gx_k

        init = (_jnp.zeros((), _jnp.float32), _jax.tree.map(_jnp.zeros_like, weights))
        (loss, grad_w), grad_x = _jax.lax.scan(body, init, (per_example, given["loss_target"]))
    with _jax.named_scope("update"):
        delta_w, new_m, new_v = {}, {}, {}
        for n in TWIN_WEIGHTS:
            delta_w[n], new_m[n], new_v[n] = _adamw(weights[n], grad_w[n], given["m_" + n], given["v_" + n])
    return (loss, grad_x, *[grad_w[n] for n in TWIN_WEIGHTS], *[delta_w[n] for n in TWIN_WEIGHTS],
            *[new_m[n] for n in TWIN_WEIGHTS], *[new_v[n] for n in TWIN_WEIGHTS])
```

```python
import functools

import jax
import jax.numpy as jnp
from jax import lax
from jax.experimental import pallas as pl
from jax.experimental.pallas import tpu as pltpu

F32 = jnp.float32
BF16 = jnp.bfloat16
HIGHEST = lax.Precision.HIGHEST
MESH = pl.DeviceIdType.MESH

D_MODEL = 1024
D_FF = 4096
CHUNK = 64
HEAD_TILE = 128
SB_HEAD_DIM = 64
CONV_DIM = 512
HG_DIM = 512
AB_IN = 3584
N_CHIPS = 4
N_DEV = 8
EPS = 1e-6
ATT_BLOCK = 128
HG_SLAB = 512

ADAM_LR = 0.001
ADAM_B1 = 0.9
ADAM_B2 = 0.999
ADAM_EPS = 1e-08
ADAM_WD = 0.01
ADAM_STEP = 10

NN = (((1,), (0,)), ((), ()))
NT = (((1,), (1,)), ((), ()))
TN = (((0,), (0,)), ((), ()))


def _cp(*dims):
    return pltpu.CompilerParams(dimension_semantics=dims) if dims else pltpu.CompilerParams()


def _dot(a, b, dn, precision=None):
    return lax.dot_general(a, b, dn, preferred_element_type=F32, precision=precision)


def _sigmoid(z):
    return 1.0 / (1.0 + jnp.exp(-z))


def _matmul(a, b, *, dn, grid, a_spec, b_spec, acc_shape, epilogue, extras=(), extra_specs=(),
            out_shapes, out_specs, name):
    nk = grid[2]
    n_extra = len(extras)
    n_out = len(out_shapes)

    def body(*refs):
        a_ref, b_ref = refs[0], refs[1]
        extra_refs = refs[2:2 + n_extra]
        out_refs = refs[2 + n_extra:2 + n_extra + n_out]
        acc_ref = refs[-1]
        k = pl.program_id(2)
        part = _dot(a_ref[...], b_ref[...], dn)

        if nk == 1:
            epilogue(part, extra_refs, out_refs)
        else:
            @pl.when(k == 0)
            def _():
                acc_ref[...] = part

            @pl.when(k > 0)
            def _():
                acc_ref[...] += part

            @pl.when(k == nk - 1)
            def _():
                epilogue(acc_ref[...], extra_refs, out_refs)

    return pl.pallas_call(
        body, grid=grid, in_specs=[a_spec, b_spec, *extra_specs], out_specs=out_specs, out_shape=out_shapes,
        scratch_shapes=[pltpu.VMEM(acc_shape, F32)], name=name,
        compiler_params=_cp("parallel", "parallel", "arbitrary"))(a, b, *extras)


def _store(dtype):
    def epilogue(acc, extra_refs, out_refs):
        out_refs[0][...] = acc.astype(dtype)
    return epilogue


def _tok_tile(s):
    return min(512, s)


def matmul_nn_chunked(a, w, out_dtype, name, epilogue=None, extras=(), extra_specs=(), out_shapes=None,
                      out_specs=None):
    s, k = a.shape
    _, _, n4 = w.shape
    tm = _tok_tile(s)
    if out_shapes is None:
        out_shapes = [jax.ShapeDtypeStruct((s, N_CHIPS * n4), out_dtype)]
        out_specs = [pl.BlockSpec((tm, n4), lambda i, j, kk: (i, j))]
        epilogue = _store(out_dtype)
    return _matmul(
        a, w, dn=NN, grid=(s // tm, N_CHIPS, 1),
        a_spec=pl.BlockSpec((tm, k), lambda i, j, kk: (i, 0)),
        b_spec=pl.BlockSpec((None, k, n4), lambda i, j, kk: (j, 0, 0)),
        acc_shape=(tm, n4), epilogue=epilogue, extras=extras, extra_specs=extra_specs,
        out_shapes=out_shapes, out_specs=out_specs, name=name)


def matmul_nt_chunked(dy, w, name):
    s = dy.shape[0]
    _, k, n4 = w.shape
    tm = _tok_tile(s)
    return _matmul(
        dy, w, dn=NT, grid=(s // tm, 1, N_CHIPS),
        a_spec=pl.BlockSpec((tm, n4), lambda i, j, kk: (i, kk)),
        b_spec=pl.BlockSpec((None, k, n4), lambda i, j, kk: (kk, 0, 0)),
        acc_shape=(tm, k), epilogue=_store(F32),
        out_shapes=[jax.ShapeDtypeStruct((s, k), F32)], out_specs=[pl.BlockSpec((tm, k), lambda i, j, kk: (i, 0))],
        name=name)[0]


def matmul_tn_chunked(x, dy, name):
    s, k = x.shape
    n4 = dy.shape[1] // N_CHIPS
    ts = _tok_tile(s)
    return _matmul(
        x, dy, dn=TN, grid=(1, N_CHIPS, s // ts),
        a_spec=pl.BlockSpec((ts, k), lambda i, j, kk: (kk, 0)),
        b_spec=pl.BlockSpec((ts, n4), lambda i, j, kk: (kk, j)),
        acc_shape=(k, n4), epilogue=_store(F32),
        out_shapes=[jax.ShapeDtypeStruct((N_CHIPS, k, n4), F32)],
        out_specs=[pl.BlockSpec((None, k, n4), lambda i, j, kk: (j, 0, 0))], name=name)[0]


def matmul_nn_plain(a, w, name, epilogue, extras, extra_specs, out_shapes, out_specs, tk=1024):
    s, k = a.shape
    n = w.shape[1]
    tm = _tok_tile(s)
    return _matmul(
        a, w, dn=NN, grid=(s // tm, 1, k // tk),
        a_spec=pl.BlockSpec((tm, tk), lambda i, j, kk: (i, kk)),
        b_spec=pl.BlockSpec((tk, n), lambda i, j, kk: (kk, 0)),
        acc_shape=(tm, n), epilogue=epilogue, extras=extras, extra_specs=extra_specs,
        out_shapes=out_shapes, out_specs=out_specs, name=name)


def matmul_nt_plain(dy, w, name, epilogue=None, extras=(), extra_specs=(), out_dtype=F32, tn=1024):
    s, n = dy.shape
    k = w.shape[0]
    tm = _tok_tile(s)
    return _matmul(
        dy, w, dn=NT, grid=(s // tm, k // tn, 1),
        a_spec=pl.BlockSpec((tm, n), lambda i, j, kk: (i, 0)),
        b_spec=pl.BlockSpec((tn, n), lambda i, j, kk: (j, 0)),
        acc_shape=(tm, tn), epilogue=epilogue or _store(out_dtype), extras=extras, extra_specs=extra_specs,
        out_shapes=[jax.ShapeDtypeStruct((s, k), out_dtype)], out_specs=[pl.BlockSpec((tm, tn), lambda i, j, kk: (i, j))],
        name=name)[0]


def matmul_tn_plain(x, dy, name, tk=1024):
    s, k = x.shape
    n = dy.shape[1]
    ts = _tok_tile(s)
    return _matmul(
        x, dy, dn=TN, grid=(k // tk, 1, s // ts),
        a_spec=pl.BlockSpec((ts, tk), lambda i, j, kk: (kk, i)),
        b_spec=pl.BlockSpec((ts, n), lambda i, j, kk: (kk, 0)),
        acc_shape=(tk, n), epilogue=_store(F32),
        out_shapes=[jax.ShapeDtypeStruct((k, n), F32)], out_specs=[pl.BlockSpec((tk, n), lambda i, j, kk: (i, 0))],
        name=name)[0]


def _row_spec(n):
    return pl.BlockSpec((1, n), lambda i: (0, 0))


def norm_mod_fwd(x, g, scale, shift, name):
    s, d = x.shape
    tm = _tok_tile(s)

    def body(x_ref, g_ref, sc_ref, sh_ref, h_ref):
        xv = x_ref[...]
        r = lax.rsqrt(jnp.mean(xv * xv, axis=-1, keepdims=True) + EPS)
        h_ref[...] = ((xv * r * g_ref[...]) * (1.0 + sc_ref[...]) + sh_ref[...]).astype(BF16)

    tile = pl.BlockSpec((tm, d), lambda i: (i, 0))
    return pl.pallas_call(
        body, grid=(s // tm,), in_specs=[tile, _row_spec(d), _row_spec(d), _row_spec(d)], out_specs=tile,
        out_shape=jax.ShapeDtypeStruct((s, d), BF16), name=name, compiler_params=_cp("parallel"))(x, g, scale, shift)


def norm_mod_bwd(dh, x, g, scale, dres, name):
    s, d = x.shape
    tm = _tok_tile(s)

    def body(dh_ref, x_ref, g_ref, sc_ref, dres_ref, dx_ref, dg_ref, dsc_ref, dsh_ref):
        @pl.when(pl.program_id(0) == 0)
        def _():
            dg_ref[...] = jnp.zeros_like(dg_ref)
            dsc_ref[...] = jnp.zeros_like(dsc_ref)
            dsh_ref[...] = jnp.zeros_like(dsh_ref)

        xv = x_ref[...]
        dhv = dh_ref[...]
        r = lax.rsqrt(jnp.mean(xv * xv, axis=-1, keepdims=True) + EPS)
        xn = xv * r
        gv = g_ref[...]
        s1 = 1.0 + sc_ref[...]
        dsh_ref[...] += jnp.sum(dhv, axis=0, keepdims=True)
        dsc_ref[...] += jnp.sum(dhv * xn * gv, axis=0, keepdims=True)
        dg_ref[...] += jnp.sum(dhv * xn * s1, axis=0, keepdims=True)
        dxn = dhv * gv * s1
        dx_ref[...] = dres_ref[...] + r * (dxn - xn * jnp.mean(dxn * xn, axis=-1, keepdims=True))

    tile = pl.BlockSpec((tm, d), lambda i: (i, 0))
    row = jax.ShapeDtypeStruct((1, d), F32)
    return pl.pallas_call(
        body, grid=(s // tm,), in_specs=[tile, tile, _row_spec(d), _row_spec(d), tile],
        out_specs=[tile, _row_spec(d), _row_spec(d), _row_spec(d)],
        out_shape=[jax.ShapeDtypeStruct((s, d), F32), row, row, row], name=name,
        compiler_params=_cp("arbitrary"))(dh, x, g, scale, dres)


def gate_bwd(dx, y, gate, name):
    s, d = dx.shape
    tm = _tok_tile(s)

    def body(dx_ref, y_ref, gate_ref, dy_ref, dgate_ref):
        @pl.when(pl.program_id(0) == 0)
        def _():
            dgate_ref[...] = jnp.zeros_like(dgate_ref)

        dxv = dx_ref[...]
        dy_ref[...] = (gate_ref[...] * dxv).astype(BF16)
        dgate_ref[...] += jnp.sum(dxv * y_ref[...], axis=0, keepdims=True)

    tile = pl.BlockSpec((tm, d), lambda i: (i, 0))
    return pl.pallas_call(
        body, grid=(s // tm,), in_specs=[tile, tile, _row_spec(d)], out_specs=[tile, _row_spec(d)],
        out_shape=[jax.ShapeDtypeStruct((s, d), BF16), jax.ShapeDtypeStruct((1, d), F32)], name=name,
        compiler_params=_cp("arbitrary"))(dx, y, gate)


def loss_and_grad(y, target):
    s, d = y.shape
    tm = _tok_tile(s)

    def body(y_ref, t_ref, dy_ref, loss_ref):
        @pl.when(pl.program_id(0) == 0)
        def _():
            loss_ref[...] = jnp.zeros_like(loss_ref)

        err = y_ref[...] - t_ref[...]
        dy_ref[...] = err * (1.0 / d)
        loss_ref[...] += jnp.sum(err * err) * (0.5 / d)

    tile = pl.BlockSpec((tm, d), lambda i: (i, 0))
    return pl.pallas_call(
        body, grid=(s // tm,), in_specs=[tile, tile], out_specs=[tile, _row_spec(128)],
        out_shape=[jax.ShapeDtypeStruct((s, d), F32), jax.ShapeDtypeStruct((1, 128), F32)], name="loss_and_grad",
        compiler_params=_cp("arbitrary"))(y, target)


def _proj_residual(acc, extra_refs, out_refs):
    x_ref, gate_ref = extra_refs
    out_refs[0][...] = acc
    out_refs[1][...] = x_ref[...] + gate_ref[...] * acc


def proj_residual(a, w, x, gate, name):
    s, d = x.shape
    tm = _tok_tile(s)
    tile = pl.BlockSpec((tm, d), lambda i, j, kk: (i, 0))
    shape = jax.ShapeDtypeStruct((s, d), F32)
    return matmul_nn_plain(
        a, w, name, _proj_residual, extras=(x, gate),
        extra_specs=(tile, pl.BlockSpec((1, d), lambda i, j, kk: (0, 0))),
        out_shapes=[shape, shape], out_specs=[tile, tile])


def _mlp_up(acc, extra_refs, out_refs):
    r = jnp.maximum(acc, 0.0)
    out_refs[0][...] = (r * r).astype(BF16)
    out_refs[1][...] = r.astype(BF16)


def mlp_up(h, w1, name):
    s = h.shape[0]
    n4 = w1.shape[2]
    tm = _tok_tile(s)
    shape = jax.ShapeDtypeStruct((s, N_CHIPS * n4), BF16)
    spec = pl.BlockSpec((tm, n4), lambda i, j, kk: (i, j))
    return matmul_nn_chunked(h, w1, BF16, name, epilogue=_mlp_up, out_shapes=[shape, shape], out_specs=[spec, spec])


def _dact(acc, extra_refs, out_refs):
    out_refs[0][...] = (acc * (2.0 * extra_refs[0][...].astype(F32))).astype(BF16)


def mlp_down_bwd(dy, w2, r, name):
    s = dy.shape[0]
    tm = _tok_tile(s)
    return matmul_nt_plain(dy, w2, name, epilogue=_dact, extras=(r,),
                           extra_specs=(pl.BlockSpec((tm, 1024), lambda i, j, kk: (i, j)),), out_dtype=BF16)


def _shift_down(p, n, row):
    return jnp.where(row >= n, pltpu.roll(p, n, 0), 0.0)


def _shift_up(p, n, row):
    rows = p.shape[0]
    return jnp.where(row < rows - n, pltpu.roll(p, rows - n, 0), 0.0)


def _u_col(block):
    return lambda i: (0, block + i)


def conv_mixer_fwd(u, conv_w):
    s = u.shape[0]
    nb = CONV_DIM // HEAD_TILE

    def body(ab_ref, ac_ref, ah_ref, w_ref, y_ref):
        row = lax.broadcasted_iota(jnp.int32, (s, HEAD_TILE), 0)
        p = ac_ref[...] * ah_ref[...]
        w = w_ref[...]
        conv = w[0:1] * _shift_down(p, 2, row) + w[1:2] * _shift_down(p, 1, row) + w[2:3] * p
        y_ref[...] = (ab_ref[...] * conv).astype(BF16)

    col = lambda b: pl.BlockSpec((s, HEAD_TILE), _u_col(b * nb))
    return pl.pallas_call(
        body, grid=(nb,), in_specs=[col(0), col(1), col(2), pl.BlockSpec((3, HEAD_TILE), lambda i: (0, i))],
        out_specs=pl.BlockSpec((s, HEAD_TILE), lambda i: (0, i)),
        out_shape=jax.ShapeDtypeStruct((s, CONV_DIM), BF16), name="conv_mixer_fwd",
        compiler_params=_cp("parallel"))(u, u, u, conv_w)


def conv_mixer_bwd(dmix, u, conv_w):
    s = u.shape[0]
    nb = CONV_DIM // HEAD_TILE

    def body(dy_ref, ab_ref, ac_ref, ah_ref, w_ref, dab_ref, dac_ref, dah_ref, dw_ref):
        row = lax.broadcasted_iota(jnp.int32, (s, HEAD_TILE), 0)
        ac = ac_ref[...]
        ah = ah_ref[...]
        p = ac * ah
        w = w_ref[...]
        p1 = _shift_down(p, 1, row)
        p2 = _shift_down(p, 2, row)
        conv = w[0:1] * p2 + w[1:2] * p1 + w[2:3] * p
        dy = dy_ref[...]
        dab_ref[...] = (dy * conv).astype(BF16)
        dconv = dy * ab_ref[...]
        dp = w[0:1] * _shift_up(dconv, 2, row) + w[1:2] * _shift_up(dconv, 1, row) + w[2:3] * dconv
        dac_ref[...] = (dp * ah).astype(BF16)
        dah_ref[...] = (dp * ac).astype(BF16)
        dw_ref[...] = jnp.concatenate(
            [jnp.sum(dconv * p2, axis=0, keepdims=True), jnp.sum(dconv * p1, axis=0, keepdims=True),
             jnp.sum(dconv * p, axis=0, keepdims=True)], axis=0)

    col = lambda b: pl.BlockSpec((s, HEAD_TILE), _u_col(b * nb))
    out = pl.BlockSpec((s, HEAD_TILE), lambda i: (0, i))
    wspec = pl.BlockSpec((3, HEAD_TILE), lambda i: (0, i))
    shape = jax.ShapeDtypeStruct((s, CONV_DIM), BF16)
    return pl.pallas_call(
        body, grid=(nb,), in_specs=[out, col(0), col(1), col(2), wspec], out_specs=[out, out, out, wspec],
        out_shape=[shape, shape, shape, jax.ShapeDtypeStruct((3, CONV_DIM), F32)], name="conv_mixer_bwd",
        compiler_params=_cp("parallel"))(dmix, u, u, u, conv_w)


def _chunk_cumsum(g, pos):
    for sh in (1, 2, 4, 8, 16, 32):
        g = g + jnp.where(pos >= sh, pltpu.roll(g, sh, 0), 0.0)
    return g


def _chunk_rev_cumsum(g, pos):
    rows = g.shape[0]
    for sh in (1, 2, 4, 8, 16, 32):
        g = g + jnp.where(pos < CHUNK - sh, pltpu.roll(g, rows - sh, 0), 0.0)
    return g


def _lower_bound(lb_ref):
    logits = lb_ref[...]
    e = jnp.exp(logits - jnp.max(logits, axis=0, keepdims=True))
    p = e / jnp.sum(e, axis=0, keepdims=True)
    return p[0:1], p


def _hg_gates(hf, lb):
    sg = _sigmoid(hf)
    f = lb + (1.0 - lb) * sg
    return sg, f, jnp.log(f), 1.0 - f


def _hg_specs(s, slab, order):
    n = s // slab
    col = lambda b: pl.BlockSpec((slab, HEAD_TILE), lambda h, i: (order(i, n), b + h))
    return n, col


def hgrn_fwd(u, lb_logits, hg_norm):
    s = u.shape[0]
    slab = min(HG_SLAB, s)
    cps = slab // CHUNK
    n, col = _hg_specs(s, slab, lambda i, n_: i)
    heads = HG_DIM // HEAD_TILE

    def body(q_ref, f_ref, i_ref, g_ref, lb_ref, nw_ref, y_ref, o_ref, st_ref, at_ref):
        @pl.when(pl.program_id(1) == 0)
        def _():
            at_ref[...] = jnp.zeros_like(at_ref)

        pos = lax.broadcasted_iota(jnp.int32, (slab, HEAD_TILE), 0) & (CHUNK - 1)
        lb, _ = _lower_bound(lb_ref)
        q = q_ref[...]
        v = i_ref[...]
        _, _, g, kk = _hg_gates(f_ref[...], lb)
        b = _chunk_cumsum(g, pos)

        def diag(d, o):
            lam = jnp.exp(jnp.where(pos >= d, b - pltpu.roll(b, d, 0), -jnp.inf))
            sc = jnp.sum(q * pltpu.roll(kk, d, 0) * lam, axis=-1, keepdims=True)
            return o + sc * pltpu.roll(v, d, 0)

        o_ref[...] = lax.fori_loop(0, CHUNK, diag, jnp.zeros((slab, HEAD_TILE), F32))

        qhat = q * jnp.exp(b)
        for c in range(cps):
            rows = slice(c * CHUNK, (c + 1) * CHUNK)
            at = at_ref[...]
            st_ref[c] = at
            o_ref[rows, :] += _dot(qhat[rows], at, NT, HIGHEST)
            bc = b[(c + 1) * CHUNK - 1:(c + 1) * CHUNK]
            khat = kk[rows] * jnp.exp(bc - b[rows])
            at_ref[...] = at * jnp.exp(bc) + _dot(v[rows], khat, TN, HIGHEST)

        o = o_ref[...]
        r = lax.rsqrt(jnp.mean(o * o, axis=-1, keepdims=True) + EPS)
        hg = g_ref[...]
        y_ref[...] = (o * r * nw_ref[...] * (hg * _sigmoid(hg))).astype(BF16)

    out = pl.BlockSpec((slab, HEAD_TILE), lambda h, i: (i, h))
    par = lambda rows: pl.BlockSpec((rows, HEAD_TILE), lambda h, i: (0, h))
    return pl.pallas_call(
        body, grid=(heads, n), in_specs=[col(12), col(16), col(20), col(24), par(3), par(1)],
        out_specs=[out, out, pl.BlockSpec((None, cps, HEAD_TILE, HEAD_TILE), lambda h, i: (h, i, 0, 0))],
        out_shape=[jax.ShapeDtypeStruct((s, HG_DIM), BF16), jax.ShapeDtypeStruct((s, HG_DIM), F32),
                   jax.ShapeDtypeStruct((heads, s // CHUNK, HEAD_TILE, HEAD_TILE), F32)],
        scratch_shapes=[pltpu.VMEM((HEAD_TILE, HEAD_TILE), F32)], name="hgrn_fwd",
        compiler_params=_cp("parallel", "arbitrary"))(u, u, u, u, lb_logits, hg_norm)


def hgrn_bwd(dmix, u, o_raw, states, lb_logits, hg_norm):
    s = u.shape[0]
    slab = min(HG_SLAB, s)
    cps = slab // CHUNK
    n, col = _hg_specs(s, slab, lambda i, n_: n_ - 1 - i)
    heads = HG_DIM // HEAD_TILE

    def body(dy_ref, q_ref, f_ref, i_ref, g_ref, o_ref, st_ref, lb_ref, nw_ref,
             dq_ref, df_ref, di_ref, dg_ref, dnw_ref, dlb_ref, dat_ref, dlbacc_ref, dqs_ref, dks_ref, dvs_ref, dbc_ref):
        step = pl.program_id(1)

        @pl.when(step == 0)
        def _():
            dat_ref[...] = jnp.zeros_like(dat_ref)
            dlbacc_ref[...] = jnp.zeros_like(dlbacc_ref)
            dnw_ref[...] = jnp.zeros_like(dnw_ref)

        pos = lax.broadcasted_iota(jnp.int32, (slab, HEAD_TILE), 0) & (CHUNK - 1)
        lb, probs = _lower_bound(lb_ref)
        q = q_ref[...]
        v = i_ref[...]
        sg_f, f, g, kk = _hg_gates(f_ref[...], lb)
        b = _chunk_cumsum(g, pos)

        o = o_ref[...]
        nw = nw_ref[...]
        r = lax.rsqrt(jnp.mean(o * o, axis=-1, keepdims=True) + EPS)
        hg = g_ref[...]
        sg = _sigmoid(hg)
        dy = dy_ref[...]
        d_on = dy * (hg * sg)
        dg_ref[...] = (dy * (o * r * nw) * (sg * (1.0 + hg * (1.0 - sg)))).astype(BF16)
        dnw_ref[...] += jnp.sum(d_on * o * r, axis=0, keepdims=True)
        t1 = d_on * nw
        do = r * t1 - o * (r * r * r) * jnp.mean(t1 * o, axis=-1, keepdims=True)

        eb = jnp.exp(b)
        qhat = q * eb
        dbc_ref[...] = jnp.zeros_like(dbc_ref)
        for c in reversed(range(cps)):
            rows = slice(c * CHUNK, (c + 1) * CHUNK)
            last = (c + 1) * CHUNK - 1
            at = st_ref[c]
            dat = dat_ref[...]
            bc = b[last:last + 1]
            ebc = jnp.exp(bc)
            dec = jnp.exp(bc - b[rows])
            khat = kk[rows] * dec
            at_next = at * ebc + _dot(v[rows], khat, TN, HIGHEST)
            dbc_ref[last:last + 1, :] = jnp.sum(dat * at_next, axis=0, keepdims=True)
            dqs_ref[rows, :] = eb[rows] * _dot(do[rows], at, NN, HIGHEST)
            dks_ref[rows, :] = dec * _dot(v[rows], dat, NN, HIGHEST)
            dvs_ref[rows, :] = _dot(khat, dat, NT, HIGHEST)
            dat_ref[...] = dat * ebc + _dot(do[rows], qhat[rows], TN, HIGHEST)

        def diag(d, carry):
            dq, dk, dv = carry
            lam = jnp.exp(jnp.where(pos >= d, b - pltpu.roll(b, d, 0), -jnp.inf))
            kd = pltpu.roll(kk, d, 0)
            vd = pltpu.roll(v, d, 0)
            sc = jnp.sum(q * kd * lam, axis=-1, keepdims=True)
            pd = jnp.sum(do * vd, axis=-1, keepdims=True)
            dq = dq + pd * kd * lam
            back = jnp.where(d == 0, 0, slab - d)
            dk = dk + pltpu.roll(pd * q * lam, back, 0)
            dv = dv + pltpu.roll(sc * do, back, 0)
            return dq, dk, dv

        dq, dk, dv = lax.fori_loop(0, CHUNK, diag, (dqs_ref[...], dks_ref[...], dvs_ref[...]))

        db = q * dq - kk * dk + dbc_ref[...]
        dgl = _chunk_rev_cumsum(db, pos)
        dfv = dgl / f - dk
        dq_ref[...] = dq.astype(BF16)
        di_ref[...] = dv.astype(BF16)
        df_ref[...] = (dfv * (1.0 - lb) * sg_f * (1.0 - sg_f)).astype(BF16)
        dlbacc_ref[...] += jnp.sum(dfv * (1.0 - sg_f), axis=0, keepdims=True)

        @pl.when(step == n - 1)
        def _():
            dlb = dlbacc_ref[...]
            sel = (lax.broadcasted_iota(jnp.int32, (3, HEAD_TILE), 0) == 0).astype(F32)
            dlb_ref[...] = dlb * probs[0:1] * (sel - probs)

    out = pl.BlockSpec((slab, HEAD_TILE), lambda h, i: (n - 1 - i, h))
    par = lambda rows: pl.BlockSpec((rows, HEAD_TILE), lambda h, i: (0, h))
    dyspec = pl.BlockSpec((slab, HEAD_TILE), lambda h, i: (n - 1 - i, CONV_DIM // HEAD_TILE + h))
    shape = jax.ShapeDtypeStruct((s, HG_DIM), BF16)
    slab_f32 = pltpu.VMEM((slab, HEAD_TILE), F32)
    return pl.pallas_call(
        body, grid=(heads, n),
        in_specs=[dyspec, col(12), col(16), col(20), col(24), out,
                  pl.BlockSpec((None, cps, HEAD_TILE, HEAD_TILE), lambda h, i: (h, n - 1 - i, 0, 0)), par(3), par(1)],
        out_specs=[out, out, out, out, par(1), par(3)],
        out_shape=[shape, shape, shape, shape, jax.ShapeDtypeStruct((1, HG_DIM), F32),
                   jax.ShapeDtypeStruct((3, HG_DIM), F32)],
        scratch_shapes=[pltpu.VMEM((HEAD_TILE, HEAD_TILE), F32), pltpu.VMEM((1, HEAD_TILE), F32),
                        slab_f32, slab_f32, slab_f32, slab_f32],
        name="hgrn_bwd", compiler_params=_cp("parallel", "arbitrary"))(
            dmix, u, u, u, u, o_raw, states, lb_logits, hg_norm)


def _pair_rstd(x, lo):
    x2 = x * x
    s_lo = jnp.sum(jnp.where(lo, x2, 0.0), axis=-1, keepdims=True)
    s_hi = jnp.sum(jnp.where(lo, 0.0, x2), axis=-1, keepdims=True)
    inv = 1.0 / SB_HEAD_DIM
    return jnp.where(lo, lax.rsqrt(s_lo * inv + EPS), lax.rsqrt(s_hi * inv + EPS))


def _pair_mean(x, lo):
    s_lo = jnp.sum(jnp.where(lo, x, 0.0), axis=-1, keepdims=True)
    s_hi = jnp.sum(jnp.where(lo, 0.0, x), axis=-1, keepdims=True)
    return jnp.where(lo, s_lo, s_hi) * (1.0 / SB_HEAD_DIM)


def qk_norm_fwd(qkv, qg, kg):
    s = qkv.shape[0]
    tm = _tok_tile(s)
    nt = D_MODEL // HEAD_TILE

    def body(q_ref, k_ref, v_ref, qg_ref, kg_ref, qn_ref, kn_ref, vb_ref):
        lo = lax.broadcasted_iota(jnp.int32, (tm, HEAD_TILE), 1) < SB_HEAD_DIM
        qv = q_ref[...]
        kv = k_ref[...]
        qn_ref[...] = (qv * _pair_rstd(qv, lo) * qg_ref[...]).astype(BF16)
        kn_ref[...] = (kv * _pair_rstd(kv, lo) * kg_ref[...]).astype(BF16)
        vb_ref[...] = v_ref[...].astype(BF16)

    col = lambda b: pl.BlockSpec((tm, HEAD_TILE), lambda i, j: (i, b * nt + j))
    gain = pl.BlockSpec((1, HEAD_TILE), lambda i, j: (0, 0))
    out = pl.BlockSpec((tm, HEAD_TILE), lambda i, j: (i, j))
    shape = jax.ShapeDtypeStruct((s, D_MODEL), BF16)
    return pl.pallas_call(
        body, grid=(s // tm, nt), in_specs=[col(0), col(1), col(2), gain, gain], out_specs=[out, out, out],
        out_shape=[shape, shape, shape], name="qk_norm_fwd", compiler_params=_cp("parallel", "parallel"))(
            qkv, qkv, qkv, qg, kg)


def qk_norm_bwd(dqn, dkn, dv, qkv, qg, kg):
    s = qkv.shape[0]
    tm = _tok_tile(s)
    nt = D_MODEL // HEAD_TILE

    def body(dqn_ref, dkn_ref, dv_ref, q_ref, k_ref, qg_ref, kg_ref, dq_ref, dk_ref, dvb_ref, dqg_ref, dkg_ref):
        @pl.when((pl.program_id(0) == 0) & (pl.program_id(1) == 0))
        def _():
            dqg_ref[...] = jnp.zeros_like(dqg_ref)
            dkg_ref[...] = jnp.zeros_like(dkg_ref)

        lo = lax.broadcasted_iota(jnp.int32, (tm, HEAD_TILE), 1) < SB_HEAD_DIM

        def one(x_ref, g_ref, dn_ref, dx_ref, dgain_ref):
            xv = x_ref[...]
            r = _pair_rstd(xv, lo)
            xn = xv * r
            dn = dn_ref[...]
            dgain_ref[...] += jnp.sum(dn * xn, axis=0, keepdims=True)
            t1 = dn * g_ref[...]
            dx_ref[...] = (r * (t1 - xn * _pair_mean(t1 * xn, lo))).astype(BF16)

        one(q_ref, qg_ref, dqn_ref, dq_ref, dqg_ref)
        one(k_ref, kg_ref, dkn_ref, dk_ref, dkg_ref)
        dvb_ref[...] = dv_ref[...].astype(BF16)

    col = lambda b: pl.BlockSpec((tm, HEAD_TILE), lambda i, j: (i, b * nt + j))
    gain = pl.BlockSpec((1, HEAD_TILE), lambda i, j: (0, 0))
    out = pl.BlockSpec((tm, HEAD_TILE), lambda i, j: (i, j))
    shape = jax.ShapeDtypeStruct((s, D_MODEL), BF16)
    grow = jax.ShapeDtypeStruct((1, HEAD_TILE), F32)
    return pl.pallas_call(
        body, grid=(s // tm, nt), in_specs=[out, out, out, col(0), col(1), gain, gain],
        out_specs=[out, out, out, gain, gain], out_shape=[shape, shape, shape, grow, grow], name="qk_norm_bwd",
        compiler_params=_cp("arbitrary", "arbitrary"))(dqn, dkn, dv, qkv, qkv, qg, kg)


def _split_dot(x, u):
    hi = x.astype(BF16)
    lo = (x - hi.astype(F32)).astype(BF16)
    return _dot(hi, u, NN) + _dot(lo, u, NN)


def _sb_block(qh, kb, carry, causal, suffix, diag):
    z = _dot(qh, kb, NT) * (SB_HEAD_DIM ** -0.5)
    lb = jnp.minimum(z, 0.0) - jnp.log1p(jnp.exp(-jnp.abs(z)))
    lom = lb - z
    if diag:
        lom = jnp.where(causal, lom, 0.0)
    incl = _split_dot(lom, suffix)
    w = jnp.exp(lb + (incl - lom) + carry)
    if diag:
        w = jnp.where(causal, w, 0.0)
    return lb, lom, incl, w


def _sb_consts():
    row = lax.broadcasted_iota(jnp.int32, (ATT_BLOCK, ATT_BLOCK), 0)
    col = lax.broadcasted_iota(jnp.int32, (ATT_BLOCK, ATT_BLOCK), 1)
    suffix = (row >= col).astype(BF16)
    causal = col < row
    lo = col < SB_HEAD_DIM
    return suffix, causal, lo


def _rows(i):
    return pl.ds(pl.multiple_of(i * ATT_BLOCK, ATT_BLOCK), ATT_BLOCK)


def sb_attn_fwd(qn, kn, vb):
    s = qn.shape[0]
    nq = s // ATT_BLOCK
    nt = D_MODEL // HEAD_TILE

    def body(q_ref, k_ref, v_ref, o_ref, ob_ref):
        suffix, causal, lo = _sb_consts()

        def qblock(qi, _):
            qb = q_ref[_rows(qi), :].astype(F32)
            heads = []
            for hh in range(2):
                qh = jnp.where(lo if hh == 0 else ~lo, qb, 0.0).astype(BF16)

                def step(kj, carry, acc, diag):
                    _, lom, incl, w = _sb_block(qh, k_ref[_rows(kj), :], carry, causal, suffix, diag)
                    acc = acc + _split_dot(w, v_ref[_rows(kj), :])
                    return carry + jnp.sum(lom, axis=-1, keepdims=True), acc

                carry, acc = step(qi, jnp.zeros((ATT_BLOCK, 1), F32), jnp.zeros((ATT_BLOCK, HEAD_TILE), F32), True)
                _, acc = lax.fori_loop(0, qi, lambda jj, c: step(qi - 1 - jj, c[0], c[1], False), (carry, acc))
                heads.append(acc)
            o = jnp.where(lo, heads[0], heads[1])
            o_ref[_rows(qi), :] = o
            ob_ref[_rows(qi), :] = o.astype(BF16)
            return 0

        lax.fori_loop(0, nq, qblock, 0)

    spec = pl.BlockSpec((s, HEAD_TILE), lambda p: (0, p))
    return pl.pallas_call(
        body, grid=(nt,), in_specs=[spec, spec, spec], out_specs=[spec, spec],
        out_shape=[jax.ShapeDtypeStruct((s, D_MODEL), F32), jax.ShapeDtypeStruct((s, D_MODEL), BF16)],
        name="sb_attn_fwd", compiler_params=_cp("parallel"))(qn, kn, vb)


def sb_attn_bwd(qn, kn, vb, o, do):
    s = qn.shape[0]
    nq = s // ATT_BLOCK
    nt = D_MODEL // HEAD_TILE
    scale = SB_HEAD_DIM ** -0.5

    def body(q_ref, k_ref, v_ref, o_ref, do_ref, dq_ref, dk_ref, dv_ref):
        suffix, causal, lo = _sb_consts()
        dk_ref[...] = jnp.zeros_like(dk_ref)
        dv_ref[...] = jnp.zeros_like(dv_ref)

        def qblock(qi, _):
            qb = q_ref[_rows(qi), :].astype(F32)
            dob = do_ref[_rows(qi), :]
            prod = dob.astype(BF16).astype(F32) * o_ref[_rows(qi), :]
            dq_heads = []
            for hh in range(2):
                mask = lo if hh == 0 else ~lo
                qh = jnp.where(mask, qb, 0.0).astype(BF16)
                doh = jnp.where(mask, dob, 0.0).astype(BF16)
                total = jnp.sum(jnp.where(mask, prod, 0.0), axis=-1, keepdims=True)

                def step(kj, carry, carry_e, dq, diag):
                    kb = k_ref[_rows(kj), :]
                    lb, lom, incl, w = _sb_block(qh, kb, carry, causal, suffix, diag)
                    e = _dot(doh, v_ref[_rows(kj), :], NT) * w
                    incl_e = _split_dot(e, suffix)
                    before = total - carry_e - incl_e
                    dz = e * jnp.exp(lom) - before * jnp.exp(lb)
                    if diag:
                        dz = jnp.where(causal, dz, 0.0)
                    dzs = (dz * scale).astype(BF16)
                    dq = dq + _dot(dzs, kb, NN)
                    dk_ref[_rows(kj), :] += _dot(dzs, qh, TN)
                    dv_ref[_rows(kj), :] += _dot(w.astype(BF16), doh, TN)
                    return (carry + jnp.sum(lom, axis=-1, keepdims=True),
                            carry_e + jnp.sum(e, axis=-1, keepdims=True), dq)

                zero = jnp.zeros((ATT_BLOCK, 1), F32)
                state = step(qi, zero, zero, jnp.zeros((ATT_BLOCK, HEAD_TILE), F32), True)
                state = lax.fori_loop(0, qi, lambda jj, c: step(qi - 1 - jj, c[0], c[1], c[2], False), state)
                dq_heads.append(state[2])
            dq_ref[_rows(qi), :] = jnp.where(lo, dq_heads[0], dq_heads[1])
            return 0

        lax.fori_loop(0, nq, qblock, 0)

    spec = pl.BlockSpec((s, HEAD_TILE), lambda p: (0, p))
    shape = jax.ShapeDtypeStruct((s, D_MODEL), F32)
    return pl.pallas_call(
        body, grid=(nt,), in_specs=[spec] * 5, out_specs=[spec] * 3, out_shape=[shape] * 3,
        name="sb_attn_bwd", compiler_params=_cp("parallel"))(qn, kn, vb, o, do)


def ada_fwd(c_all, ada_w, ada_b_shard):
    layers, d, cols = ada_w.shape
    tn = 512

    def body(c_ref, w_ref, b_ref, out_ref):
        cv = c_ref[...]
        act = (cv * _sigmoid(cv)).astype(BF16)
        out_ref[...] = _dot(act, w_ref[...].astype(BF16), NN) + b_ref[...]

    return pl.pallas_call(
        body, grid=(layers, cols // tn),
        in_specs=[pl.BlockSpec((N_DEV, d), lambda l, j: (0, 0)), pl.BlockSpec((None, d, tn), lambda l, j: (l, 0, j)),
                  pl.BlockSpec((None, 1, tn), lambda l, j: (l, 0, j))],
        out_specs=pl.BlockSpec((None, N_DEV, tn), lambda l, j: (l, 0, j)),
        out_shape=jax.ShapeDtypeStruct((layers, N_DEV, cols), F32), name="ada_fwd",
        compiler_params=_cp("parallel", "parallel"))(c_all, ada_w, ada_b_shard)


def ada_w_grad(c_all, dmod):
    layers, _, cols = dmod.shape
    d = c_all.shape[1]
    tn = 512

    def body(c_ref, dm_ref, out_ref):
        cv = c_ref[...]
        out_ref[...] = _dot(cv * _sigmoid(cv), dm_ref[...], TN, HIGHEST)

    return pl.pallas_call(
        body, grid=(layers, cols // tn),
        in_specs=[pl.BlockSpec((N_DEV, d), lambda l, j: (0, 0)), pl.BlockSpec((None, N_DEV, tn), lambda l, j: (l, 0, j))],
        out_specs=pl.BlockSpec((None, d, tn), lambda l, j: (l, 0, j)),
        out_shape=jax.ShapeDtypeStruct((layers, d, cols), F32), name="ada_w_grad",
        compiler_params=_cp("parallel", "parallel"))(c_all, dmod)


def _row_tile(r, c, elems):
    best = 8
    for t in range(8, r + 1, 8):
        if r % t == 0 and t * c <= elems:
            best = t
    return best


def sum_rows(x, name):
    n, r, c = x.shape
    tr = _row_tile(r, c, 1 << 17)

    def body(x_ref, out_ref):
        acc = x_ref[0]
        for i in range(1, n):
            acc = acc + x_ref[i]
        out_ref[...] = acc

    return pl.pallas_call(
        body, grid=(r // tr,), in_specs=[pl.BlockSpec((n, tr, c), lambda i: (0, i, 0))],
        out_specs=pl.BlockSpec((tr, c), lambda i: (i, 0)), out_shape=jax.ShapeDtypeStruct((r, c), F32), name=name,
        compiler_params=_cp("parallel"))(x)


def add2(a, b, name):
    r, c = a.shape
    tr = _row_tile(r, c, 1 << 18)

    def body(a_ref, b_ref, out_ref):
        out_ref[...] = a_ref[...] + b_ref[...]

    spec = pl.BlockSpec((tr, c), lambda i: (i, 0))
    return pl.pallas_call(
        body, grid=(r // tr,), in_specs=[spec, spec], out_specs=spec, out_shape=jax.ShapeDtypeStruct((r, c), F32),
        name=name, compiler_params=_cp("parallel"))(a, b)


def adamw(w, g, m, v, name):
    r, c = w.shape
    tr = _row_tile(r, c, 1 << 17)
    c1 = 1.0 - ADAM_B1 ** ADAM_STEP
    c2 = 1.0 - ADAM_B2 ** ADAM_STEP

    def body(w_ref, g_ref, m_ref, v_ref, d_ref, nm_ref, nv_ref):
        gv = g_ref[...]
        nm = ADAM_B1 * m_ref[...] + (1.0 - ADAM_B1) * gv
        nv = ADAM_B2 * v_ref[...] + (1.0 - ADAM_B2) * (gv * gv)
        d_ref[...] = -ADAM_LR * ((nm / c1) / (jnp.sqrt(nv / c2) + ADAM_EPS) + ADAM_WD * w_ref[...])
        nm_ref[...] = nm
        nv_ref[...] = nv

    spec = pl.BlockSpec((tr, c), lambda i: (i, 0))
    shape = jax.ShapeDtypeStruct((r, c), F32)
    return pl.pallas_call(
        body, grid=(r // tr,), in_specs=[spec] * 4, out_specs=[spec] * 3, out_shape=[shape] * 3, name=name,
        compiler_params=_cp("parallel"))(w, g, m, v)


def _me():
    return lax.axis_index("x"), lax.axis_index("y"), lax.axis_index("c")


def _flip(v, bit):
    return 1 - v if bit else v


HBM = pl.BlockSpec(memory_space=pl.ANY)
VMEM = pl.BlockSpec(memory_space=pltpu.VMEM)


def all_gather_rows(v, name):
    n = v.shape[1]

    def body(v_ref, out_ref, send_sems, recv_sems):
        x, y, c = _me()
        me = 4 * x + 2 * y + c
        out_ref[pl.ds(me, 1), :] = v_ref[...]
        copies = []
        for j in range(1, N_DEV):
            peer = (_flip(x, j & 4), _flip(y, j & 2), _flip(c, j & 1))
            copies.append(pltpu.make_async_remote_copy(
                src_ref=v_ref, dst_ref=out_ref.at[pl.ds(me, 1), :], send_sem=send_sems.at[j - 1],
                recv_sem=recv_sems.at[j - 1], device_id=peer, device_id_type=MESH))
        for cp in copies:
            cp.start()
        for cp in copies:
            cp.wait()

    return pl.pallas_call(
        body, in_specs=[VMEM], out_specs=VMEM, out_shape=jax.ShapeDtypeStruct((N_DEV, n), F32),
        scratch_shapes=[pltpu.SemaphoreType.DMA((N_DEV - 1,)), pltpu.SemaphoreType.DMA((N_DEV - 1,))], name=name)(v)


def _chip_peers(x, y):
    return [((1 - x, y), 2 * (1 - x) + y), ((x, 1 - y), 2 * x + (1 - y)), ((1 - x, 1 - y), 2 * (1 - x) + (1 - y))]


def chip_all_gather(shards, name):
    n = len(shards)

    def body(*refs):
        ins, outs = refs[:n], refs[n:2 * n]
        local_sem, send_sems, recv_sems = refs[2 * n:]
        x, y, c = _me()
        mine = 2 * x + y
        started = []
        for a in range(n):
            cp = pltpu.make_async_copy(ins[a], outs[a].at[mine], local_sem.at[a])
            cp.start()
            started.append(cp)
            for p, (chip, _) in enumerate(_chip_peers(x, y)):
                cp = pltpu.make_async_remote_copy(
                    src_ref=ins[a], dst_ref=outs[a].at[mine], send_sem=send_sems.at[a, p], recv_sem=recv_sems.at[a, p],
                    device_id=(*chip, c), device_id_type=MESH)
                cp.start()
                started.append(cp)
        for cp in started:
            cp.wait()

    return pl.pallas_call(
        body, in_specs=[HBM] * n, out_specs=[HBM] * n,
        out_shape=[jax.ShapeDtypeStruct((N_CHIPS, *s.shape), s.dtype) for s in shards],
        scratch_shapes=[pltpu.SemaphoreType.DMA((n,)), pltpu.SemaphoreType.DMA((n, 3)), pltpu.SemaphoreType.DMA((n, 3))],
        name=name)(*shards)


def sibling_split(grads, name):
    n = len(grads)

    def body(*refs):
        ins, own, got = refs[:n], refs[n:2 * n], refs[2 * n:3 * n]
        local_sem, send_sems, recv_sems = refs[3 * n:]
        x, y, c = _me()
        started = []
        for a in range(n):
            half = ins[a].shape[1] // 2
            keep = pltpu.make_async_copy(ins[a].at[:, pl.ds(c * half, half), :], own[a], local_sem.at[a])
            give = pltpu.make_async_remote_copy(
                src_ref=ins[a].at[:, pl.ds((1 - c) * half, half), :], dst_ref=got[a], send_sem=send_sems.at[a],
                recv_sem=recv_sems.at[a], device_id=(x, y, 1 - c), device_id_type=MESH)
            keep.start()
            give.start()
            started += [keep, give]
        for cp in started:
            cp.wait()

    halves = [jax.ShapeDtypeStruct((g.shape[0], g.shape[1] // 2, g.shape[2]), g.dtype) for g in grads]
    return pl.pallas_call(
        body, in_specs=[HBM] * n, out_specs=[HBM] * (2 * n), out_shape=halves + halves,
        scratch_shapes=[pltpu.SemaphoreType.DMA((n,)), pltpu.SemaphoreType.DMA((n,)), pltpu.SemaphoreType.DMA((n,))],
        name=name)(*grads)


def chip_scatter(parts, name):
    n = len(parts)

    def body(*refs):
        ins, outs = refs[:n], refs[n:2 * n]
        local_sem, send_sems, recv_sems = refs[2 * n:]
        x, y, c = _me()
        mine = 2 * x + y
        started = []
        for a in range(n):
            cp = pltpu.make_async_copy(ins[a].at[mine], outs[a].at[mine], local_sem.at[a])
            cp.start()
            started.append(cp)
            for p, (chip, slot) in enumerate(_chip_peers(x, y)):
                cp = pltpu.make_async_remote_copy(
                    src_ref=ins[a].at[slot], dst_ref=outs[a].at[mine], send_sem=send_sems.at[a, p],
                    recv_sem=recv_sems.at[a, p], device_id=(*chip, c), device_id_type=MESH)
                cp.start()
                started.append(cp)
        for cp in started:
            cp.wait()

    return pl.pallas_call(
        body, in_specs=[HBM] * n, out_specs=[HBM] * n,
        out_shape=[jax.ShapeDtypeStruct(p.shape, p.dtype) for p in parts],
        scratch_shapes=[pltpu.SemaphoreType.DMA((n,)), pltpu.SemaphoreType.DMA((n, 3)), pltpu.SemaphoreType.DMA((n, 3))],
        name=name)(*parts)


def sibling_join(halves, name):
    n = len(halves)

    def body(*refs):
        ins, outs = refs[:n], refs[n:2 * n]
        local_sem, send_sems, recv_sems = refs[2 * n:]
        x, y, c = _me()
        started = []
        for a in range(n):
            half = ins[a].shape[0]
            rows = outs[a].at[pl.ds(c * half, half), :]
            keep = pltpu.make_async_copy(ins[a], rows, local_sem.at[a])
            give = pltpu.make_async_remote_copy(
                src_ref=ins[a], dst_ref=rows, send_sem=send_sems.at[a], recv_sem=recv_sems.at[a],
                device_id=(x, y, 1 - c), device_id_type=MESH)
            keep.start()
            give.start()
            started += [keep, give]
        for cp in started:
            cp.wait()

    return pl.pallas_call(
        body, in_specs=[HBM] * n, out_specs=[HBM] * n,
        out_shape=[jax.ShapeDtypeStruct((2 * h.shape[0], h.shape[1]), h.dtype) for h in halves],
        scratch_shapes=[pltpu.SemaphoreType.DMA((n,)), pltpu.SemaphoreType.DMA((n,)), pltpu.SemaphoreType.DMA((n,))],
        name=name)(*halves)


def reduce_to_owner(grads):
    own, got = (lambda r: (r[:len(grads)], r[len(grads):]))(sibling_split(grads, "grad_sibling_split"))
    parts = [add2(o.reshape(-1, o.shape[2]), g.reshape(-1, g.shape[2]), f"grad_pair_sum_{a}").reshape(o.shape)
             for a, (o, g) in enumerate(zip(own, got))]
    landed = chip_scatter(parts, "grad_chip_scatter")
    halves = [sum_rows(l, f"grad_chip_sum_{a}") for a, l in enumerate(landed)]
    return sibling_join(halves, "grad_sibling_join")


def _pad_row(v, n):
    return jnp.pad(v.reshape(1, -1), ((0, 0), (0, n - v.size)))


def local_step(x, target, mod, wts):
    d = D_MODEL
    row = lambda v: v.reshape(1, -1)
    mods = [[row(mod[l, i * d:(i + 1) * d]) for i in range(6)] for l in range(2)]
    saved = []
    for l in range(2):
        shift1, scale1, gate1, shift2, scale2, gate2 = mods[l]
        g_mix, g_mlp = row(wts["norm_mix"][l]), row(wts["norm_mlp"][l])
        h = norm_mod_fwd(x, g_mix, scale1, shift1, f"norm_mix_fwd_{l}")
        if l == 0:
            u = matmul_nn_chunked(h, wts["w_in"], F32, "in_proj_ab")[0]
            y_a = conv_mixer_fwd(u, wts["conv_w"])
            y_b, o_raw, states = hgrn_fwd(u, wts["lb_logits"], wts["hg_norm"])
            mix = jnp.concatenate([y_a, y_b], axis=1)
            y, x1 = proj_residual(mix, wts["w_out_ab"], x, gate1, "out_proj_ab")
            ctx = (u, o_raw, states)
        else:
            qkv = matmul_nn_chunked(h, wts["w_qkv"], F32, "in_proj_c")[0]
            qn, kn, vb = qk_norm_fwd(qkv, wts["qg"], wts["kg"])
            o, mix = sb_attn_fwd(qn, kn, vb)
            y, x1 = proj_residual(mix, wts["w_out_c"], x, gate1, "out_proj_c")
            ctx = (qkv, qn, kn, vb, o)
        h2 = norm_mod_fwd(x1, g_mlp, scale2, shift2, f"norm_mlp_fwd_{l}")
        act, r = mlp_up(h2, wts["w1"][l], f"mlp_up_{l}")
        y2, x2 = proj_residual(act, wts["w2"][l], x1, gate2, f"mlp_down_{l}")
        saved.append((x, h, mix, y, x1, h2, act, r, y2, ctx))
        x = x2

    dx, loss_row = loss_and_grad(x, target)
    small, big = {}, {}
    dmod = [None, None]
    d_norm_mix, d_norm_mlp = [None, None], [None, None]
    for l in (1, 0):
        shift1, scale1, gate1, shift2, scale2, gate2 = mods[l]
        g_mix, g_mlp = row(wts["norm_mix"][l]), row(wts["norm_mlp"][l])
        x0, h, mix, y, x1, h2, act, r, y2, ctx = saved[l]
        dy2, dgate2 = gate_bwd(dx, y2, gate2, f"mlp_gate_bwd_{l}")
        dz = mlp_down_bwd(dy2, wts["w2"][l], r, f"mlp_down_bwd_{l}")
        big[f"w2_{l}"] = matmul_tn_plain(act, dy2, f"mlp_w2_grad_{l}")
        dh2 = matmul_nt_chunked(dz, wts["w1"][l], f"mlp_up_bwd_{l}")
        big[f"w1_{l}"] = matmul_tn_chunked(h2, dz, f"mlp_w1_grad_{l}")
        dx1, d_norm_mlp[l], dscale2, dshift2 = norm_mod_bwd(dh2, x1, g_mlp, scale2, dx, f"norm_mlp_bwd_{l}")
        dy, dgate1 = gate_bwd(dx1, y, gate1, f"mix_gate_bwd_{l}")
        if l == 0:
            u, o_raw, states = ctx
            dmix = matmul_nt_plain(dy, wts["w_out_ab"], "out_proj_ab_bwd")
            big["w_out_ab"] = matmul_tn_plain(mix, dy, "w_out_ab_grad")
            dab, dac, dah, small["conv_w"] = conv_mixer_bwd(dmix, u, wts["conv_w"])
            dhq, dhf, dhi, dhg, small["hg_norm"], small["lb_logits"] = hgrn_bwd(
                dmix, u, o_raw, states, wts["lb_logits"], wts["hg_norm"])
            du = jnp.concatenate([dab, dac, dah, dhq, dhf, dhi, dhg], axis=1)
            dh = matmul_nt_chunked(du, wts["w_in"], "in_proj_ab_bwd")
            big["w_in"] = matmul_tn_chunked(h, du, "w_in_grad")
        else:
            qkv, qn, kn, vb, o = ctx
            do = matmul_nt_plain(dy, wts["w_out_c"], "out_proj_c_bwd")
            big["w_out_c"] = matmul_tn_plain(mix, dy, "w_out_c_grad")
            dqn, dkn, dv = sb_attn_bwd(qn, kn, vb, o, do)
            dq, dk, dvb, dqg, dkg = qk_norm_bwd(dqn, dkn, dv, qkv, wts["qg"], wts["kg"])
            small["q_norm"] = dqg[:, :SB_HEAD_DIM] + dqg[:, SB_HEAD_DIM:]
            small["k_norm"] = dkg[:, :SB_HEAD_DIM] + dkg[:, SB_HEAD_DIM:]
            dqkv = jnp.concatenate([dq, dk, dvb], axis=1)
            dh = matmul_nt_chunked(dqkv, wts["w_qkv"], "in_proj_c_bwd")
            big["w_qkv"] = matmul_tn_chunked(h, dqkv, "w_qkv_grad")
        dx, d_norm_mix[l], dscale1, dshift1 = norm_mod_bwd(dh, x0, g_mix, scale1, dx1, f"norm_mix_bwd_{l}")
        dmod[l] = jnp.concatenate([dshift1, dscale1, dgate1, dshift2, dscale2, dgate2], axis=1)
    small["mod"] = jnp.concatenate(dmod, axis=0)
    small["norm_mix"] = jnp.concatenate(d_norm_mix, axis=0)
    small["norm_mlp"] = jnp.concatenate(d_norm_mlp, axis=0)
    return loss_row, dx, small, big


SMALL_ORDER = ("mod", "norm_mix", "norm_mlp", "conv_w", "hg_norm", "lb_logits", "q_norm", "k_norm")


def kernel(x, c, ada_w, ada_b, norm_mix, norm_mlp, w_in_ab, conv_w, hg_norm, lb_logits, w_out_ab, w_qkv, q_norm, k_norm, w_out_c, mlp_w1, mlp_w2, loss_target, m_ada_w, m_ada_b, m_norm_mix, m_norm_mlp, m_w_in_ab, m_conv_w, m_hg_norm, m_lb_logits, m_w_out_ab, m_w_qkv, m_q_norm, m_k_norm, m_w_out_c, m_mlp_w1, m_mlp_w2, v_ada_w, v_ada_b, v_norm_mix, v_norm_mlp, v_w_in_ab, v_conv_w, v_hg_norm, v_lb_logits, v_w_out_ab, v_w_qkv, v_q_norm, v_k_norm, v_w_out_c, v_mlp_w1, v_mlp_w2):
    d = D_MODEL
    ax, ay, ac = _me()
    chip = 2 * ax + ay
    dev = 2 * chip + ac
    cols = ada_w.shape[2]

    first = all_gather_rows(_pad_row(jnp.concatenate([c.reshape(-1), conv_w.reshape(-1)]), 1536), "gather_cond")
    c_all = first[:, :d]
    conv_full = first[::2, d:d + 3 * HEAD_TILE].reshape(N_CHIPS, 3, HEAD_TILE).transpose(1, 0, 2).reshape(3, CONV_DIM)
    ada_b_shard = lax.dynamic_slice(ada_b, (0, chip * cols), (2, cols)).reshape(2, 1, cols)
    mod_cols = ada_fwd(c_all, ada_w, ada_b_shard)
    mod_all = all_gather_rows(mod_cols.reshape(1, -1), "gather_mod").reshape(N_DEV, 2, N_DEV, cols)
    mod = lax.dynamic_index_in_dim(mod_all[::2], dev, axis=2, keepdims=False).transpose(1, 0, 2).reshape(2, 6 * d)

    shards = [w_in_ab[0], w_out_ab[0], w_qkv[0], w_out_c[0], mlp_w1[0], mlp_w1[1], mlp_w2[0], mlp_w2[1]]
    g_in, g_out_ab, g_qkv, g_out_c, g_w1a, g_w1b, g_w2a, g_w2b = chip_all_gather(
        [s.astype(BF16) for s in shards], "gather_weights")
    wts = dict(
        norm_mix=norm_mix, norm_mlp=norm_mlp, w_in=g_in, conv_w=conv_full, hg_norm=hg_norm, lb_logits=lb_logits,
        w_out_ab=g_out_ab.reshape(d, d), w_qkv=g_qkv, qg=jnp.tile(q_norm, (1, 2)), kg=jnp.tile(k_norm, (1, 2)),
        w_out_c=g_out_c.reshape(d, d), w1=[g_w1a, g_w1b], w2=[g_w2a.reshape(D_FF, d), g_w2b.reshape(D_FF, d)])

    loss_row, grad_x, small, big = local_step(x[0], loss_target[0], mod, wts)

    flat = jnp.concatenate([small[k].reshape(-1) for k in SMALL_ORDER] + [loss_row[0, :1]])
    n_small = -(-flat.size // 1024) * 1024
    gathered = all_gather_rows(_pad_row(flat, n_small), "gather_small")
    total = sum_rows(gathered.reshape(N_DEV, 8, n_small // 8), "small_sum").reshape(-1)
    sizes = [small[k].size for k in SMALL_ORDER]
    offs = [sum(sizes[:i]) for i in range(len(sizes) + 1)]
    tot = {k: total[offs[i]:offs[i + 1]].reshape(small[k].shape) for i, k in enumerate(SMALL_ORDER)}
    loss = total[offs[-1]]
    mod_rows = gathered[:, :2 * 6 * d].reshape(N_DEV, 2, 6 * d)
    dmod_cols = lax.dynamic_slice(mod_rows, (0, 0, chip * cols), (N_DEV, 2, cols)).transpose(1, 0, 2)
    g_ada_w = ada_w_grad(c_all, dmod_cols)

    as_chunks = lambda g: g.reshape(N_CHIPS, g.shape[0] // N_CHIPS, g.shape[1])
    names = ["w_in", "w_out_ab", "w_qkv", "w_out_c", "w1_0", "w1_1", "w2_0", "w2_1"]
    chunked = [big[k] if big[k].ndim == 3 else as_chunks(big[k]) for k in names]
    r_in, r_out_ab, r_qkv, r_out_c, r_w1a, r_w1b, r_w2a, r_w2b = reduce_to_owner(chunked)

    grads = dict(
        ada_w=g_ada_w, ada_b=tot["mod"], norm_mix=tot["norm_mix"], norm_mlp=tot["norm_mlp"], w_in_ab=r_in[None],
        conv_w=lax.dynamic_slice(tot["conv_w"], (0, chip * HEAD_TILE), (3, HEAD_TILE))[None], hg_norm=tot["hg_norm"],
        lb_logits=tot["lb_logits"], w_out_ab=r_out_ab[None], w_qkv=r_qkv[None], q_norm=tot["q_norm"],
        k_norm=tot["k_norm"], w_out_c=r_out_c[None], mlp_w1=jnp.stack([r_w1a, r_w1b]), mlp_w2=jnp.stack([r_w2a, r_w2b]))
    weights = dict(ada_w=ada_w, ada_b=ada_b, norm_mix=norm_mix, norm_mlp=norm_mlp, w_in_ab=w_in_ab, conv_w=conv_w,
                   hg_norm=hg_norm, lb_logits=lb_logits, w_out_ab=w_out_ab, w_qkv=w_qkv, q_norm=q_norm, k_norm=k_norm,
                   w_out_c=w_out_c, mlp_w1=mlp_w1, mlp_w2=mlp_w2)
    m_in = dict(ada_w=m_ada_w, ada_b=m_ada_b, norm_mix=m_norm_mix, norm_mlp=m_norm_mlp, w_in_ab=m_w_in_ab,
                conv_w=m_conv_w, hg_norm=m_hg_norm, lb_logits=m_lb_logits, w_out_ab=m_w_out_ab, w_qkv=m_w_qkv,
                q_norm=m_q_norm, k_norm=m_k_norm, w_out_c=m_w_out_c, mlp_w1=m_mlp_w1, mlp_w2=m_mlp_w2)
    v_in = dict(ada_w=v_ada_w, ada_b=v_ada_b, norm_mix=v_norm_mix, norm_mlp=v_norm_mlp, w_in_ab=v_w_in_ab,
                conv_w=v_conv_w, hg_norm=v_hg_norm, lb_logits=v_lb_logits, w_out_ab=v_w_out_ab, w_qkv=v_w_qkv,
                q_norm=v_q_norm, k_norm=v_k_norm, w_out_c=v_w_out_c, mlp_w1=v_mlp_w1, mlp_w2=v_mlp_w2)
    order = list(weights)
    large = ("ada_w", "w_in_ab", "w_out_ab", "w_qkv", "w_out_c", "mlp_w1", "mlp_w2")
    delta, new_m, new_v = {}, {}, {}
    for k in large:
        shape = weights[k].shape
        flat2 = lambda a: a.reshape(-1, shape[-1])
        dl, nm, nv = adamw(flat2(weights[k]), flat2(grads[k]), flat2(m_in[k]), flat2(v_in[k]), f"adamw_{k}")
        delta[k], new_m[k], new_v[k] = dl.reshape(shape), nm.reshape(shape), nv.reshape(shape)
    rest = [k for k in order if k not in large]
    n_rest = -(-sum(weights[k].size for k in rest) // 1024) * 1024
    pack = lambda tree: _pad_row(jnp.concatenate([tree[k].reshape(-1) for k in rest]), n_rest).reshape(8, n_rest // 8)
    dl, nm, nv = adamw(pack(weights), pack(grads), pack(m_in), pack(v_in), "adamw_small")
    off = 0
    for k in rest:
        size, shape = weights[k].size, weights[k].shape
        delta[k], new_m[k], new_v[k] = (a.reshape(-1)[off:off + size].reshape(shape) for a in (dl, nm, nv))
        off += size
    grads = {k: grads[k].reshape(weights[k].shape) for k in order}
    return (loss, grad_x[None], *[grads[k] for k in order], *[delta[k] for k in order],
            *[new_m[k] for k in order], *[new_v[k] for k in order])
```

```python
import functools

import jax
import jax.numpy as jnp
from jax import lax
from jax.experimental import pallas as pl
from jax.experimental.pallas import tpu as pltpu

F32 = jnp.float32
BF16 = jnp.bfloat16
HIGHEST = lax.Precision.HIGHEST
MESH = pl.DeviceIdType.MESH

D_MODEL = 1024
D_FF = 4096
CHUNK = 64
HEAD_TILE = 128
SB_HEAD_DIM = 64
CONV_DIM = 512
HG_DIM = 512
AB_IN = 3584
N_CHIPS = 4
N_DEV = 8
EPS = 1e-6
ATT_BLOCK = 128
ATT_TILE = 256
HG_SLAB = 512

ADAM_LR = 0.001
ADAM_B1 = 0.9
ADAM_B2 = 0.999
ADAM_EPS = 1e-08
ADAM_WD = 0.01
ADAM_STEP = 10

NN = (((1,), (0,)), ((), ()))
NT = (((1,), (1,)), ((), ()))
TN = (((0,), (0,)), ((), ()))


def _cp(*dims):
    return pltpu.CompilerParams(dimension_semantics=dims) if dims else pltpu.CompilerParams()


def _dot(a, b, dn, precision=None):
    return lax.dot_general(a, b, dn, preferred_element_type=F32, precision=precision)


def _sigmoid(z):
    return 1.0 / (1.0 + jnp.exp(-z))


def _matmul(a, b, *, dn, grid, a_spec, b_spec, acc_shape, epilogue, extras=(), extra_specs=(),
            out_shapes, out_specs, name):
    nk = grid[2]
    n_extra = len(extras)
    n_out = len(out_shapes)

    def body(*refs):
        a_ref, b_ref = refs[0], refs[1]
        extra_refs = refs[2:2 + n_extra]
        out_refs = refs[2 + n_extra:2 + n_extra + n_out]
        acc_ref = refs[-1]
        k = pl.program_id(2)
        part = _dot(a_ref[...], b_ref[...], dn)

        if nk == 1:
            epilogue(part, extra_refs, out_refs)
        else:
            @pl.when(k == 0)
            def _():
                acc_ref[...] = part

            @pl.when(k > 0)
            def _():
                acc_ref[...] += part

            @pl.when(k == nk - 1)
            def _():
                epilogue(acc_ref[...], extra_refs, out_refs)

    return pl.pallas_call(
        body, grid=grid, in_specs=[a_spec, b_spec, *extra_specs], out_specs=out_specs, out_shape=out_shapes,
        scratch_shapes=[pltpu.VMEM(acc_shape, F32)], name=name,
        compiler_params=_cp("parallel", "parallel", "arbitrary"))(a, b, *extras)


def _store(dtype):
    def epilogue(acc, extra_refs, out_refs):
        out_refs[0][...] = acc.astype(dtype)
    return epilogue


def _tok_tile(s):
    return min(512, s)


def matmul_nn_chunked(a, w, out_dtype, name, epilogue=None, extras=(), extra_specs=(), out_shapes=None,
                      out_specs=None):
    s, k = a.shape
    _, _, n4 = w.shape
    tm = _tok_tile(s)
    if out_shapes is None:
        out_shapes = [jax.ShapeDtypeStruct((s, N_CHIPS * n4), out_dtype)]
        out_specs = [pl.BlockSpec((tm, n4), lambda i, j, kk: (i, j))]
        epilogue = _store(out_dtype)
    return _matmul(
        a, w, dn=NN, grid=(s // tm, N_CHIPS, 1),
        a_spec=pl.BlockSpec((tm, k), lambda i, j, kk: (i, 0)),
        b_spec=pl.BlockSpec((None, k, n4), lambda i, j, kk: (j, 0, 0)),
        acc_shape=(tm, n4), epilogue=epilogue, extras=extras, extra_specs=extra_specs,
        out_shapes=out_shapes, out_specs=out_specs, name=name)


def matmul_nt_chunked(dy, w, name):
    s = dy.shape[0]
    _, k, n4 = w.shape
    tm = _tok_tile(s)
    return _matmul(
        dy, w, dn=NT, grid=(s // tm, 1, N_CHIPS),
        a_spec=pl.BlockSpec((tm, n4), lambda i, j, kk: (i, kk)),
        b_spec=pl.BlockSpec((None, k, n4), lambda i, j, kk: (kk, 0, 0)),
        acc_shape=(tm, k), epilogue=_store(F32),
        out_shapes=[jax.ShapeDtypeStruct((s, k), F32)], out_specs=[pl.BlockSpec((tm, k), lambda i, j, kk: (i, 0))],
        name=name)[0]


def matmul_tn_chunked(x, dy, name):
    s, k = x.shape
    n4 = dy.shape[1] // N_CHIPS
    ts = _tok_tile(s)
    return _matmul(
        x, dy, dn=TN, grid=(1, N_CHIPS, s // ts),
        a_spec=pl.BlockSpec((ts, k), lambda i, j, kk: (kk, 0)),
        b_spec=pl.BlockSpec((ts, n4), lambda i, j, kk: (kk, j)),
        acc_shape=(k, n4), epilogue=_store(F32),
        out_shapes=[jax.ShapeDtypeStruct((N_CHIPS, k, n4), F32)],
        out_specs=[pl.BlockSpec((None, k, n4), lambda i, j, kk: (j, 0, 0))], name=name)[0]


def matmul_nn_plain(a, w, name, epilogue, extras, extra_specs, out_shapes, out_specs, tk=1024):
    s, k = a.shape
    n = w.shape[1]
    tm = _tok_tile(s)
    return _matmul(
        a, w, dn=NN, grid=(s // tm, 1, k // tk),
        a_spec=pl.BlockSpec((tm, tk), lambda i, j, kk: (i, kk)),
        b_spec=pl.BlockSpec((tk, n), lambda i, j, kk: (kk, 0)),
        acc_shape=(tm, n), epilogue=epilogue, extras=extras, extra_specs=extra_specs,
        out_shapes=out_shapes, out_specs=out_specs, name=name)


def matmul_nt_plain(dy, w, name, epilogue=None, extras=(), extra_specs=(), out_dtype=F32, tn=1024):
    s, n = dy.shape
    k = w.shape[0]
    tm = _tok_tile(s)
    return _matmul(
        dy, w, dn=NT, grid=(s // tm, k // tn, 1),
        a_spec=pl.BlockSpec((tm, n), lambda i, j, kk: (i, 0)),
        b_spec=pl.BlockSpec((tn, n), lambda i, j, kk: (j, 0)),
        acc_shape=(tm, tn), epilogue=epilogue or _store(out_dtype), extras=extras, extra_specs=extra_specs,
        out_shapes=[jax.ShapeDtypeStruct((s, k), out_dtype)], out_specs=[pl.BlockSpec((tm, tn), lambda i, j, kk: (i, j))],
        name=name)[0]


def matmul_tn_plain(x, dy, name, tk=1024):
    s, k = x.shape
    n = dy.shape[1]
    ts = _tok_tile(s)
    return _matmul(
        x, dy, dn=TN, grid=(k // tk, 1, s // ts),
        a_spec=pl.BlockSpec((ts, tk), lambda i, j, kk: (kk, i)),
        b_spec=pl.BlockSpec((ts, n), lambda i, j, kk: (kk, 0)),
        acc_shape=(tk, n), epilogue=_store(F32),
        out_shapes=[jax.ShapeDtypeStruct((k, n), F32)], out_specs=[pl.BlockSpec((tk, n), lambda i, j, kk: (i, 0))],
        name=name)[0]


def _row_spec(n):
    return pl.BlockSpec((1, n), lambda i: (0, 0))


def norm_mod_fwd(x, g, scale, shift, name):
    s, d = x.shape
    tm = _tok_tile(s)

    def body(x_ref, g_ref, sc_ref, sh_ref, h_ref):
        xv = x_ref[...]
        r = lax.rsqrt(jnp.mean(xv * xv, axis=-1, keepdims=True) + EPS)
        h_ref[...] = ((xv * r * g_ref[...]) * (1.0 + sc_ref[...]) + sh_ref[...]).astype(BF16)

    tile = pl.BlockSpec((tm, d), lambda i: (i, 0))
    return pl.pallas_call(
        body, grid=(s // tm,), in_specs=[tile, _row_spec(d), _row_spec(d), _row_spec(d)], out_specs=tile,
        out_shape=jax.ShapeDtypeStruct((s, d), BF16), name=name, compiler_params=_cp("parallel"))(x, g, scale, shift)


def norm_mod_bwd(dh, x, g, scale, dres, name):
    s, d = x.shape
    tm = _tok_tile(s)

    def body(dh_ref, x_ref, g_ref, sc_ref, dres_ref, dx_ref, dg_ref, dsc_ref, dsh_ref):
        @pl.when(pl.program_id(0) == 0)
        def _():
            dg_ref[...] = jnp.zeros_like(dg_ref)
            dsc_ref[...] = jnp.zeros_like(dsc_ref)
            dsh_ref[...] = jnp.zeros_like(dsh_ref)

        xv = x_ref[...]
        dhv = dh_ref[...]
        r = lax.rsqrt(jnp.mean(xv * xv, axis=-1, keepdims=True) + EPS)
        xn = xv * r
        gv = g_ref[...]
        s1 = 1.0 + sc_ref[...]
        dsh_ref[...] += jnp.sum(dhv, axis=0, keepdims=True)
        dsc_ref[...] += jnp.sum(dhv * xn * gv, axis=0, keepdims=True)
        dg_ref[...] += jnp.sum(dhv * xn * s1, axis=0, keepdims=True)
        dxn = dhv * gv * s1
        dx_ref[...] = dres_ref[...] + r * (dxn - xn * jnp.mean(dxn * xn, axis=-1, keepdims=True))

    tile = pl.BlockSpec((tm, d), lambda i: (i, 0))
    row = jax.ShapeDtypeStruct((1, d), F32)
    return pl.pallas_call(
        body, grid=(s // tm,), in_specs=[tile, tile, _row_spec(d), _row_spec(d), tile],
        out_specs=[tile, _row_spec(d), _row_spec(d), _row_spec(d)],
        out_shape=[jax.ShapeDtypeStruct((s, d), F32), row, row, row], name=name,
        compiler_params=_cp("arbitrary"))(dh, x, g, scale, dres)


def gate_bwd(dx, y, gate, name):
    s, d = dx.shape
    tm = _tok_tile(s)

    def body(dx_ref, y_ref, gate_ref, dy_ref, dgate_ref):
        @pl.when(pl.program_id(0) == 0)
        def _():
            dgate_ref[...] = jnp.zeros_like(dgate_ref)

        dxv = dx_ref[...]
        dy_ref[...] = (gate_ref[...] * dxv).astype(BF16)
        dgate_ref[...] += jnp.sum(dxv * y_ref[...], axis=0, keepdims=True)

    tile = pl.BlockSpec((tm, d), lambda i: (i, 0))
    return pl.pallas_call(
        body, grid=(s // tm,), in_specs=[tile, tile, _row_spec(d)], out_specs=[tile, _row_spec(d)],
        out_shape=[jax.ShapeDtypeStruct((s, d), BF16), jax.ShapeDtypeStruct((1, d), F32)], name=name,
        compiler_params=_cp("arbitrary"))(dx, y, gate)


def loss_and_grad(y, target):
    s, d = y.shape
    tm = _tok_tile(s)

    def body(y_ref, t_ref, dy_ref, loss_ref):
        @pl.when(pl.program_id(0) == 0)
        def _():
            loss_ref[...] = jnp.zeros_like(loss_ref)

        err = y_ref[...] - t_ref[...]
        dy_ref[...] = err * (1.0 / d)
        loss_ref[...] += jnp.sum(err * err) * (0.5 / d)

    tile = pl.BlockSpec((tm, d), lambda i: (i, 0))
    return pl.pallas_call(
        body, grid=(s // tm,), in_specs=[tile, tile], out_specs=[tile, _row_spec(128)],
        out_shape=[jax.ShapeDtypeStruct((s, d), F32), jax.ShapeDtypeStruct((1, 128), F32)], name="loss_and_grad",
        compiler_params=_cp("arbitrary"))(y, target)


def _proj_residual(acc, extra_refs, out_refs):
    x_ref, gate_ref = extra_refs
    out_refs[0][...] = acc
    out_refs[1][...] = x_ref[...] + gate_ref[...] * acc


def proj_residual(a, w, x, gate, name):
    s, d = x.shape
    tm = _tok_tile(s)
    tile = pl.BlockSpec((tm, d), lambda i, j, kk: (i, 0))
    shape = jax.ShapeDtypeStruct((s, d), F32)
    return matmul_nn_plain(
        a, w, name, _proj_residual, extras=(x, gate),
        extra_specs=(tile, pl.BlockSpec((1, d), lambda i, j, kk: (0, 0))),
        out_shapes=[shape, shape], out_specs=[tile, tile])


def _mlp_up(acc, extra_refs, out_refs):
    r = jnp.maximum(acc, 0.0)
    out_refs[0][...] = (r * r).astype(BF16)
    out_refs[1][...] = r.astype(BF16)


def mlp_up(h, w1, name):
    s = h.shape[0]
    n4 = w1.shape[2]
    tm = _tok_tile(s)
    shape = jax.ShapeDtypeStruct((s, N_CHIPS * n4), BF16)
    spec = pl.BlockSpec((tm, n4), lambda i, j, kk: (i, j))
    return matmul_nn_chunked(h, w1, BF16, name, epilogue=_mlp_up, out_shapes=[shape, shape], out_specs=[spec, spec])


def _dact(acc, extra_refs, out_refs):
    out_refs[0][...] = (acc * (2.0 * extra_refs[0][...].astype(F32))).astype(BF16)


def mlp_down_bwd(dy, w2, r, name):
    s = dy.shape[0]
    tm = _tok_tile(s)
    return matmul_nt_plain(dy, w2, name, epilogue=_dact, extras=(r,),
                           extra_specs=(pl.BlockSpec((tm, 1024), lambda i, j, kk: (i, j)),), out_dtype=BF16)


def _shift_down(p, n, row):
    return jnp.where(row >= n, pltpu.roll(p, n, 0), 0.0)


def _shift_up(p, n, row):
    rows = p.shape[0]
    return jnp.where(row < rows - n, pltpu.roll(p, rows - n, 0), 0.0)


def _u_col(block):
    return lambda i: (0, block + i)


def conv_mixer_fwd(u, conv_w):
    s = u.shape[0]
    nb = CONV_DIM // HEAD_TILE

    def body(ab_ref, ac_ref, ah_ref, w_ref, y_ref):
        row = lax.broadcasted_iota(jnp.int32, (s, HEAD_TILE), 0)
        p = ac_ref[...] * ah_ref[...]
        w = w_ref[...]
        conv = w[0:1] * _shift_down(p, 2, row) + w[1:2] * _shift_down(p, 1, row) + w[2:3] * p
        y_ref[...] = (ab_ref[...] * conv).astype(BF16)

    col = lambda b: pl.BlockSpec((s, HEAD_TILE), _u_col(b * nb))
    return pl.pallas_call(
        body, grid=(nb,), in_specs=[col(0), col(1), col(2), pl.BlockSpec((3, HEAD_TILE), lambda i: (0, i))],
        out_specs=pl.BlockSpec((s, HEAD_TILE), lambda i: (0, i)),
        out_shape=jax.ShapeDtypeStruct((s, CONV_DIM), BF16), name="conv_mixer_fwd",
        compiler_params=_cp("parallel"))(u, u, u, conv_w)


def conv_mixer_bwd(dmix, u, conv_w):
    s = u.shape[0]
    nb = CONV_DIM // HEAD_TILE

    def body(dy_ref, ab_ref, ac_ref, ah_ref, w_ref, dab_ref, dac_ref, dah_ref, dw_ref):
        row = lax.broadcasted_iota(jnp.int32, (s, HEAD_TILE), 0)
        ac = ac_ref[...]
        ah = ah_ref[...]
        p = ac * ah
        w = w_ref[...]
        p1 = _shift_down(p, 1, row)
        p2 = _shift_down(p, 2, row)
        conv = w[0:1] * p2 + w[1:2] * p1 + w[2:3] * p
        dy = dy_ref[...]
        dab_ref[...] = (dy * conv).astype(BF16)
        dconv = dy * ab_ref[...]
        dp = w[0:1] * _shift_up(dconv, 2, row) + w[1:2] * _shift_up(dconv, 1, row) + w[2:3] * dconv
        dac_ref[...] = (dp * ah).astype(BF16)
        dah_ref[...] = (dp * ac).astype(BF16)
        dw_ref[...] = jnp.concatenate(
            [jnp.sum(dconv * p2, axis=0, keepdims=True), jnp.sum(dconv * p1, axis=0, keepdims=True),
             jnp.sum(dconv * p, axis=0, keepdims=True)], axis=0)

    col = lambda b: pl.BlockSpec((s, HEAD_TILE), _u_col(b * nb))
    out = pl.BlockSpec((s, HEAD_TILE), lambda i: (0, i))
    wspec = pl.BlockSpec((3, HEAD_TILE), lambda i: (0, i))
    shape = jax.ShapeDtypeStruct((s, CONV_DIM), BF16)
    return pl.pallas_call(
        body, grid=(nb,), in_specs=[out, col(0), col(1), col(2), wspec], out_specs=[out, out, out, wspec],
        out_shape=[shape, shape, shape, jax.ShapeDtypeStruct((3, CONV_DIM), F32)], name="conv_mixer_bwd",
        compiler_params=_cp("parallel"))(dmix, u, u, u, conv_w)


def _chunk_cumsum(g, pos):
    for sh in (1, 2, 4, 8, 16, 32):
        g = g + jnp.where(pos >= sh, pltpu.roll(g, sh, 0), 0.0)
    return g


def _chunk_rev_cumsum(g, pos):
    rows = g.shape[0]
    for sh in (1, 2, 4, 8, 16, 32):
        g = g + jnp.where(pos < CHUNK - sh, pltpu.roll(g, rows - sh, 0), 0.0)
    return g


def _lower_bound(lb_ref):
    logits = lb_ref[...]
    e = jnp.exp(logits - jnp.max(logits, axis=0, keepdims=True))
    p = e / jnp.sum(e, axis=0, keepdims=True)
    return p[0:1], p


def _hg_gates(hf, lb):
    sg = _sigmoid(hf)
    f = lb + (1.0 - lb) * sg
    return sg, f, jnp.log(f), 1.0 - f


def _hg_specs(s, slab, order):
    n = s // slab
    col = lambda b: pl.BlockSpec((slab, HEAD_TILE), lambda h, i: (order(i, n), b + h))
    return n, col


def hgrn_fwd(u, lb_logits, hg_norm):
    s = u.shape[0]
    slab = min(HG_SLAB, s)
    cps = slab // CHUNK
    n, col = _hg_specs(s, slab, lambda i, n_: i)
    heads = HG_DIM // HEAD_TILE

    def body(q_ref, f_ref, i_ref, g_ref, lb_ref, nw_ref, y_ref, o_ref, st_ref, at_ref):
        @pl.when(pl.program_id(1) == 0)
        def _():
            at_ref[...] = jnp.zeros_like(at_ref)

        pos = lax.broadcasted_iota(jnp.int32, (slab, HEAD_TILE), 0) & (CHUNK - 1)
        lb, _ = _lower_bound(lb_ref)
        q = q_ref[...]
        v = i_ref[...]
        _, _, g, kk = _hg_gates(f_ref[...], lb)
        b = _chunk_cumsum(g, pos)

        def diag(d, o):
            lam = jnp.exp(jnp.where(pos >= d, b - pltpu.roll(b, d, 0), -jnp.inf))
            sc = jnp.sum(q * pltpu.roll(kk, d, 0) * lam, axis=-1, keepdims=True)
            return o + sc * pltpu.roll(v, d, 0)

        o_ref[...] = lax.fori_loop(0, CHUNK, diag, jnp.zeros((slab, HEAD_TILE), F32))

        qhat = q * jnp.exp(b)
        for c in range(cps):
            rows = slice(c * CHUNK, (c + 1) * CHUNK)
            at = at_ref[...]
            st_ref[c] = at
            o_ref[rows, :] += _dot(qhat[rows], at, NT, HIGHEST)
            bc = b[(c + 1) * CHUNK - 1:(c + 1) * CHUNK]
            khat = kk[rows] * jnp.exp(bc - b[rows])
            at_ref[...] = at * jnp.exp(bc) + _dot(v[rows], khat, TN, HIGHEST)

        o = o_ref[...]
        r = lax.rsqrt(jnp.mean(o * o, axis=-1, keepdims=True) + EPS)
        hg = g_ref[...]
        y_ref[...] = (o * r * nw_ref[...] * (hg * _sigmoid(hg))).astype(BF16)

    out = pl.BlockSpec((slab, HEAD_TILE), lambda h, i: (i, h))
    par = lambda rows: pl.BlockSpec((rows, HEAD_TILE), lambda h, i: (0, h))
    return pl.pallas_call(
        body, grid=(heads, n), in_specs=[col(12), col(16), col(20), col(24), par(3), par(1)],
        out_specs=[out, out, pl.BlockSpec((None, cps, HEAD_TILE, HEAD_TILE), lambda h, i: (h, i, 0, 0))],
        out_shape=[jax.ShapeDtypeStruct((s, HG_DIM), BF16), jax.ShapeDtypeStruct((s, HG_DIM), F32),
                   jax.ShapeDtypeStruct((heads, s // CHUNK, HEAD_TILE, HEAD_TILE), F32)],
        scratch_shapes=[pltpu.VMEM((HEAD_TILE, HEAD_TILE), F32)], name="hgrn_fwd",
        compiler_params=_cp("parallel", "arbitrary"))(u, u, u, u, lb_logits, hg_norm)


def hgrn_bwd(dmix, u, o_raw, states, lb_logits, hg_norm):
    s = u.shape[0]
    slab = min(HG_SLAB, s)
    cps = slab // CHUNK
    n, col = _hg_specs(s, slab, lambda i, n_: n_ - 1 - i)
    heads = HG_DIM // HEAD_TILE

    def body(dy_ref, q_ref, f_ref, i_ref, g_ref, o_ref, st_ref, lb_ref, nw_ref,
             dq_ref, df_ref, di_ref, dg_ref, dnw_ref, dlb_ref, dat_ref, dlbacc_ref, dqs_ref, dks_ref, dvs_ref, dbc_ref):
        step = pl.program_id(1)

        @pl.when(step == 0)
        def _():
            dat_ref[...] = jnp.zeros_like(dat_ref)
            dlbacc_ref[...] = jnp.zeros_like(dlbacc_ref)
            dnw_ref[...] = jnp.zeros_like(dnw_ref)

        pos = lax.broadcasted_iota(jnp.int32, (slab, HEAD_TILE), 0) & (CHUNK - 1)
        lb, probs = _lower_bound(lb_ref)
        q = q_ref[...]
        v = i_ref[...]
        sg_f, f, g, kk = _hg_gates(f_ref[...], lb)
        b = _chunk_cumsum(g, pos)

        o = o_ref[...]
        nw = nw_ref[...]
        r = lax.rsqrt(jnp.mean(o * o, axis=-1, keepdims=True) + EPS)
        hg = g_ref[...]
        sg = _sigmoid(hg)
        dy = dy_ref[...]
        d_on = dy * (hg * sg)
        dg_ref[...] = (dy * (o * r * nw) * (sg * (1.0 + hg * (1.0 - sg)))).astype(BF16)
        dnw_ref[...] += jnp.sum(d_on * o * r, axis=0, keepdims=True)
        t1 = d_on * nw
        do = r * t1 - o * (r * r * r) * jnp.mean(t1 * o, axis=-1, keepdims=True)

        eb = jnp.exp(b)
        qhat = q * eb
        dbc_ref[...] = jnp.zeros_like(dbc_ref)
        for c in reversed(range(cps)):
            rows = slice(c * CHUNK, (c + 1) * CHUNK)
            last = (c + 1) * CHUNK - 1
            at = st_ref[c]
            dat = dat_ref[...]
            bc = b[last:last + 1]
            ebc = jnp.exp(bc)
            dec = jnp.exp(bc - b[rows])
            khat = kk[rows] * dec
            at_next = at * ebc + _dot(v[rows], khat, TN, HIGHEST)
            dbc_ref[last:last + 1, :] = jnp.sum(dat * at_next, axis=0, keepdims=True)
            dqs_ref[rows, :] = eb[rows] * _dot(do[rows], at, NN, HIGHEST)
            dks_ref[rows, :] = dec * _dot(v[rows], dat, NN, HIGHEST)
            dvs_ref[rows, :] = _dot(khat, dat, NT, HIGHEST)
            dat_ref[...] = dat * ebc + _dot(do[rows], qhat[rows], TN, HIGHEST)

        def diag(d, carry):
            dq, dk, dv = carry
            lam = jnp.exp(jnp.where(pos >= d, b - pltpu.roll(b, d, 0), -jnp.inf))
            kd = pltpu.roll(kk, d, 0)
            vd = pltpu.roll(v, d, 0)
            sc = jnp.sum(q * kd * lam, axis=-1, keepdims=True)
            pd = jnp.sum(do * vd, axis=-1, keepdims=True)
            dq = dq + pd * kd * lam
            back = jnp.where(d == 0, 0, slab - d)
            dk = dk + pltpu.roll(pd * q * lam, back, 0)
            dv = dv + pltpu.roll(sc * do, back, 0)
            return dq, dk, dv

        dq, dk, dv = lax.fori_loop(0, CHUNK, diag, (dqs_ref[...], dks_ref[...], dvs_ref[...]))

        db = q * dq - kk * dk + dbc_ref[...]
        dgl = _chunk_rev_cumsum(db, pos)
        dfv = dgl / f - dk
        dq_ref[...] = dq.astype(BF16)
        di_ref[...] = dv.astype(BF16)
        df_ref[...] = (dfv * (1.0 - lb) * sg_f * (1.0 - sg_f)).astype(BF16)
        dlbacc_ref[...] += jnp.sum(dfv * (1.0 - sg_f), axis=0, keepdims=True)

        @pl.when(step == n - 1)
        def _():
            dlb = dlbacc_ref[...]
            sel = (lax.broadcasted_iota(jnp.int32, (3, HEAD_TILE), 0) == 0).astype(F32)
            dlb_ref[...] = dlb * probs[0:1] * (sel - probs)

    out = pl.BlockSpec((slab, HEAD_TILE), lambda h, i: (n - 1 - i, h))
    par = lambda rows: pl.BlockSpec((rows, HEAD_TILE), lambda h, i: (0, h))
    dyspec = pl.BlockSpec((slab, HEAD_TILE), lambda h, i: (n - 1 - i, CONV_DIM // HEAD_TILE + h))
    shape = jax.ShapeDtypeStruct((s, HG_DIM), BF16)
    slab_f32 = pltpu.VMEM((slab, HEAD_TILE), F32)
    return pl.pallas_call(
        body, grid=(heads, n),
        in_specs=[dyspec, col(12), col(16), col(20), col(24), out,
                  pl.BlockSpec((None, cps, HEAD_TILE, HEAD_TILE), lambda h, i: (h, n - 1 - i, 0, 0)), par(3), par(1)],
        out_specs=[out, out, out, out, par(1), par(3)],
        out_shape=[shape, shape, shape, shape, jax.ShapeDtypeStruct((1, HG_DIM), F32),
                   jax.ShapeDtypeStruct((3, HG_DIM), F32)],
        scratch_shapes=[pltpu.VMEM((HEAD_TILE, HEAD_TILE), F32), pltpu.VMEM((1, HEAD_TILE), F32),
                        slab_f32, slab_f32, slab_f32, slab_f32],
        name="hgrn_bwd", compiler_params=_cp("parallel", "arbitrary"))(
            dmix, u, u, u, u, o_raw, states, lb_logits, hg_norm)


def _pair_rstd(x, lo):
    x2 = x * x
    s_lo = jnp.sum(jnp.where(lo, x2, 0.0), axis=-1, keepdims=True)
    s_hi = jnp.sum(jnp.where(lo, 0.0, x2), axis=-1, keepdims=True)
    inv = 1.0 / SB_HEAD_DIM
    return jnp.where(lo, lax.rsqrt(s_lo * inv + EPS), lax.rsqrt(s_hi * inv + EPS))


def _pair_mean(x, lo):
    s_lo = jnp.sum(jnp.where(lo, x, 0.0), axis=-1, keepdims=True)
    s_hi = jnp.sum(jnp.where(lo, 0.0, x), axis=-1, keepdims=True)
    return jnp.where(lo, s_lo, s_hi) * (1.0 / SB_HEAD_DIM)


def qk_norm_fwd(qkv, qg, kg):
    s = qkv.shape[0]
    tm = _tok_tile(s)
    nt = D_MODEL // HEAD_TILE

    def body(q_ref, k_ref, v_ref, qg_ref, kg_ref, qn_ref, kn_ref, vb_ref):
        lo = lax.broadcasted_iota(jnp.int32, (tm, HEAD_TILE), 1) < SB_HEAD_DIM
        qv = q_ref[...]
        kv = k_ref[...]
        qn_ref[...] = (qv * _pair_rstd(qv, lo) * qg_ref[...]).astype(BF16)
        kn_ref[...] = (kv * _pair_rstd(kv, lo) * kg_ref[...]).astype(BF16)
        vb_ref[...] = v_ref[...].astype(BF16)

    col = lambda b: pl.BlockSpec((tm, HEAD_TILE), lambda i, j: (i, b * nt + j))
    gain = pl.BlockSpec((1, HEAD_TILE), lambda i, j: (0, 0))
    out = pl.BlockSpec((tm, HEAD_TILE), lambda i, j: (i, j))
    shape = jax.ShapeDtypeStruct((s, D_MODEL), BF16)
    return pl.pallas_call(
        body, grid=(s // tm, nt), in_specs=[col(0), col(1), col(2), gain, gain], out_specs=[out, out, out],
        out_shape=[shape, shape, shape], name="qk_norm_fwd", compiler_params=_cp("parallel", "parallel"))(
            qkv, qkv, qkv, qg, kg)


def qk_norm_bwd(dqn, dkn, dv, qkv, qg, kg):
    s = qkv.shape[0]
    tm = _tok_tile(s)
    nt = D_MODEL // HEAD_TILE

    def body(dqn_ref, dkn_ref, dv_ref, q_ref, k_ref, qg_ref, kg_ref, dq_ref, dk_ref, dvb_ref, dqg_ref, dkg_ref):
        @pl.when((pl.program_id(0) == 0) & (pl.program_id(1) == 0))
        def _():
            dqg_ref[...] = jnp.zeros_like(dqg_ref)
            dkg_ref[...] = jnp.zeros_like(dkg_ref)

        lo = lax.broadcasted_iota(jnp.int32, (tm, HEAD_TILE), 1) < SB_HEAD_DIM

        def one(x_ref, g_ref, dn_ref, dx_ref, dgain_ref):
            xv = x_ref[...]
            r = _pair_rstd(xv, lo)
            xn = xv * r
            dn = dn_ref[...]
            dgain_ref[...] += jnp.sum(dn * xn, axis=0, keepdims=True)
            t1 = dn * g_ref[...]
            dx_ref[...] = (r * (t1 - xn * _pair_mean(t1 * xn, lo))).astype(BF16)

        one(q_ref, qg_ref, dqn_ref, dq_ref, dqg_ref)
        one(k_ref, kg_ref, dkn_ref, dk_ref, dkg_ref)
        dvb_ref[...] = dv_ref[...].astype(BF16)

    col = lambda b: pl.BlockSpec((tm, HEAD_TILE), lambda i, j: (i, b * nt + j))
    gain = pl.BlockSpec((1, HEAD_TILE), lambda i, j: (0, 0))
    out = pl.BlockSpec((tm, HEAD_TILE), lambda i, j: (i, j))
    shape = jax.ShapeDtypeStruct((s, D_MODEL), BF16)
    grow = jax.ShapeDtypeStruct((1, HEAD_TILE), F32)
    return pl.pallas_call(
        body, grid=(s // tm, nt), in_specs=[out, out, out, col(0), col(1), gain, gain],
        out_specs=[out, out, out, gain, gain], out_shape=[shape, shape, shape, grow, grow], name="qk_norm_bwd",
        compiler_params=_cp("arbitrary", "arbitrary"))(dqn, dkn, dv, qkv, qkv, qg, kg)


def _split_dot(x, u):
    hi = x.astype(BF16)
    lo = (x - hi.astype(F32)).astype(BF16)
    return _dot(hi, u, NN) + _dot(lo, u, NN)


def _sb_tile(qs, kb, carry, causal, suffix, diag):
    z = _dot(qs, kb, NT)
    lb = jnp.minimum(z, 0.0) - jnp.log(1.0 + jnp.exp(-jnp.abs(z)))
    lom = lb - z
    if diag:
        lom = jnp.where(causal, lom, 0.0)
    ws = []
    for j in reversed(range(ATT_TILE // ATT_BLOCK)):
        cols = slice(j * ATT_BLOCK, (j + 1) * ATT_BLOCK)
        ws.append(jnp.exp(z[:, cols] + _split_dot(lom[:, cols], suffix) + carry))
        carry = carry + jnp.sum(lom[:, cols], axis=-1, keepdims=True)
    w = jnp.concatenate(ws[::-1], axis=1)
    if diag:
        w = jnp.where(causal, w, 0.0)
    return lb, w, carry


def _sb_consts():
    row = lax.broadcasted_iota(jnp.int32, (ATT_BLOCK, ATT_BLOCK), 0)
    col = lax.broadcasted_iota(jnp.int32, (ATT_BLOCK, ATT_BLOCK), 1)
    suffix = (row >= col).astype(BF16)
    trow = lax.broadcasted_iota(jnp.int32, (ATT_TILE, ATT_TILE), 0)
    tcol = lax.broadcasted_iota(jnp.int32, (ATT_TILE, ATT_TILE), 1)
    causal = tcol < trow
    lo = lax.broadcasted_iota(jnp.int32, (ATT_TILE, HEAD_TILE), 1) < SB_HEAD_DIM
    return suffix, causal, lo


def _rows(i):
    return pl.ds(pl.multiple_of(i * ATT_TILE, ATT_TILE), ATT_TILE)


def _head_query(qb, mask):
    return (jnp.where(mask, qb, 0.0) * (SB_HEAD_DIM ** -0.5)).astype(BF16)


def sb_attn_fwd(qn, kn, vb):
    s = qn.shape[0]
    nq = s // ATT_TILE
    nt = D_MODEL // HEAD_TILE

    def body(q_ref, k_ref, v_ref, o_ref, ob_ref):
        suffix, causal, lo = _sb_consts()

        def qtile(qi, _):
            qb = q_ref[_rows(qi), :].astype(F32)
            qs = [_head_query(qb, lo), _head_query(qb, ~lo)]

            def step(kj, state, diag):
                kb = k_ref[_rows(kj), :]
                vt = v_ref[_rows(kj), :]
                out = []
                for hh in range(2):
                    carry, acc = state[hh]
                    _, w, carry = _sb_tile(qs[hh], kb, carry, causal, suffix, diag)
                    out.append((carry, acc + _dot(w.astype(BF16), vt, NN)))
                return tuple(out)

            start = (jnp.zeros((ATT_TILE, 1), F32), jnp.zeros((ATT_TILE, HEAD_TILE), F32))
            state = step(qi, (start, start), True)
            state = lax.fori_loop(0, qi, lambda jj, st: step(qi - 1 - jj, st, False), state)
            o = jnp.where(lo, state[0][1], state[1][1])
            o_ref[_rows(qi), :] = o
            ob_ref[_rows(qi), :] = o.astype(BF16)
            return 0

        lax.fori_loop(0, nq, qtile, 0)

    spec = pl.BlockSpec((s, HEAD_TILE), lambda p: (0, p))
    return pl.pallas_call(
        body, grid=(nt,), in_specs=[spec, spec, spec], out_specs=[spec, spec],
        out_shape=[jax.ShapeDtypeStruct((s, D_MODEL), F32), jax.ShapeDtypeStruct((s, D_MODEL), BF16)],
        name="sb_attn_fwd", compiler_params=_cp("parallel"))(qn, kn, vb)


def sb_attn_bwd(qn, kn, vb, o, do):
    s = qn.shape[0]
    nq = s // ATT_TILE
    nt = D_MODEL // HEAD_TILE

    def body(q_ref, k_ref, v_ref, o_ref, do_ref, dq_ref, dk_ref, dv_ref):
        suffix, causal, lo = _sb_consts()
        dk_ref[...] = jnp.zeros_like(dk_ref)
        dv_ref[...] = jnp.zeros_like(dv_ref)

        def qtile(qi, _):
            qb = q_ref[_rows(qi), :].astype(F32)
            dob = do_ref[_rows(qi), :].astype(BF16).astype(F32)
            prod = dob * o_ref[_rows(qi), :]
            masks = [lo, ~lo]
            qs = [_head_query(qb, m) for m in masks]
            dos = [jnp.where(m, dob, 0.0).astype(BF16) for m in masks]
            totals = [jnp.sum(jnp.where(m, prod, 0.0), axis=-1, keepdims=True) for m in masks]

            def step(kj, state, diag):
                kb = k_ref[_rows(kj), :]
                vt = v_ref[_rows(kj), :]
                out = []
                dk = jnp.zeros((ATT_TILE, HEAD_TILE), F32)
                dv = jnp.zeros((ATT_TILE, HEAD_TILE), F32)
                for hh in range(2):
                    carry, carry_e, dq = state[hh]
                    lb, w, carry = _sb_tile(qs[hh], kb, carry, causal, suffix, diag)
                    wb = w.astype(BF16)
                    e = _dot(dos[hh], vt, NT) * wb.astype(F32)
                    befores = []
                    for j in reversed(range(ATT_TILE // ATT_BLOCK)):
                        cols = slice(j * ATT_BLOCK, (j + 1) * ATT_BLOCK)
                        befores.append(totals[hh] - carry_e - _split_dot(e[:, cols], suffix))
                        carry_e = carry_e + jnp.sum(e[:, cols], axis=-1, keepdims=True)
                    before = jnp.concatenate(befores[::-1], axis=1)
                    beta = jnp.exp(lb)
                    dz = e * (1.0 - beta) - before * beta
                    if diag:
                        dz = jnp.where(causal, dz, 0.0)
                    dzb = dz.astype(BF16)
                    dq = dq + _dot(dzb, kb, NN)
                    dk = dk + _dot(dzb, qs[hh], TN)
                    dv = dv + _dot(wb, dos[hh], TN)
                    out.append((carry, carry_e, dq))
                dk_ref[_rows(kj), :] += dk
                dv_ref[_rows(kj), :] += dv
                return tuple(out)

            zero = jnp.zeros((ATT_TILE, 1), F32)
            start = (zero, zero, jnp.zeros((ATT_TILE, HEAD_TILE), F32))
            state = step(qi, (start, start), True)
            state = lax.fori_loop(0, qi, lambda jj, st: step(qi - 1 - jj, st, False), state)
            dq_ref[_rows(qi), :] = jnp.where(lo, state[0][2], state[1][2]) * (SB_HEAD_DIM ** -0.5)
            return 0

        lax.fori_loop(0, nq, qtile, 0)

    spec = pl.BlockSpec((s, HEAD_TILE), lambda p: (0, p))
    shape = jax.ShapeDtypeStruct((s, D_MODEL), F32)
    return pl.pallas_call(
        body, grid=(nt,), in_specs=[spec] * 5, out_specs=[spec] * 3, out_shape=[shape] * 3,
        name="sb_attn_bwd", compiler_params=_cp("parallel"))(qn, kn, vb, o, do)


def ada_fwd(c_all, ada_w, ada_b_shard):
    layers, d, cols = ada_w.shape
    tn = 512

    def body(c_ref, w_ref, b_ref, out_ref):
        cv = c_ref[...]
        act = (cv * _sigmoid(cv)).astype(BF16)
        out_ref[...] = _dot(act, w_ref[...].astype(BF16), NN) + b_ref[...]

    return pl.pallas_call(
        body, grid=(layers, cols // tn),
        in_specs=[pl.BlockSpec((N_DEV, d), lambda l, j: (0, 0)), pl.BlockSpec((None, d, tn), lambda l, j: (l, 0, j)),
                  pl.BlockSpec((None, 1, tn), lambda l, j: (l, 0, j))],
        out_specs=pl.BlockSpec((None, N_DEV, tn), lambda l, j: (l, 0, j)),
        out_shape=jax.ShapeDtypeStruct((layers, N_DEV, cols), F32), name="ada_fwd",
        compiler_params=_cp("parallel", "parallel"))(c_all, ada_w, ada_b_shard)


def ada_w_grad(c_all, dmod):
    layers, _, cols = dmod.shape
    d = c_all.shape[1]
    tn = 512

    def body(c_ref, dm_ref, out_ref):
        cv = c_ref[...]
        out_ref[...] = _dot(cv * _sigmoid(cv), dm_ref[...], TN, HIGHEST)

    return pl.pallas_call(
        body, grid=(layers, cols // tn),
        in_specs=[pl.BlockSpec((N_DEV, d), lambda l, j: (0, 0)), pl.BlockSpec((None, N_DEV, tn), lambda l, j: (l, 0, j))],
        out_specs=pl.BlockSpec((None, d, tn), lambda l, j: (l, 0, j)),
        out_shape=jax.ShapeDtypeStruct((layers, d, cols), F32), name="ada_w_grad",
        compiler_params=_cp("parallel", "parallel"))(c_all, dmod)


def _row_tile(r, c, elems):
    best = 8
    for t in range(8, r + 1, 8):
        if r % t == 0 and t * c <= elems:
            best = t
    return best


def sum_rows(x, name):
    n, r, c = x.shape
    tr = _row_tile(r, c, 1 << 17)

    def body(x_ref, out_ref):
        acc = x_ref[0]
        for i in range(1, n):
            acc = acc + x_ref[i]
        out_ref[...] = acc

    return pl.pallas_call(
        body, grid=(r // tr,), in_specs=[pl.BlockSpec((n, tr, c), lambda i: (0, i, 0))],
        out_specs=pl.BlockSpec((tr, c), lambda i: (i, 0)), out_shape=jax.ShapeDtypeStruct((r, c), F32), name=name,
        compiler_params=_cp("parallel"))(x)


def add2(a, b, name):
    r, c = a.shape
    tr = _row_tile(r, c, 1 << 18)

    def body(a_ref, b_ref, out_ref):
        out_ref[...] = a_ref[...] + b_ref[...]

    spec = pl.BlockSpec((tr, c), lambda i: (i, 0))
    return pl.pallas_call(
        body, grid=(r // tr,), in_specs=[spec, spec], out_specs=spec, out_shape=jax.ShapeDtypeStruct((r, c), F32),
        name=name, compiler_params=_cp("parallel"))(a, b)


def adamw(w, g, m, v, name):
    r, c = w.shape
    tr = _row_tile(r, c, 1 << 17)
    c1 = 1.0 - ADAM_B1 ** ADAM_STEP
    c2 = 1.0 - ADAM_B2 ** ADAM_STEP

    def body(w_ref, g_ref, m_ref, v_ref, d_ref, nm_ref, nv_ref):
        gv = g_ref[...]
        nm = ADAM_B1 * m_ref[...] + (1.0 - ADAM_B1) * gv
        nv = ADAM_B2 * v_ref[...] + (1.0 - ADAM_B2) * (gv * gv)
        d_ref[...] = -ADAM_LR * ((nm / c1) / (jnp.sqrt(nv / c2) + ADAM_EPS) + ADAM_WD * w_ref[...])
        nm_ref[...] = nm
        nv_ref[...] = nv

    spec = pl.BlockSpec((tr, c), lambda i: (i, 0))
    shape = jax.ShapeDtypeStruct((r, c), F32)
    return pl.pallas_call(
        body, grid=(r // tr,), in_specs=[spec] * 4, out_specs=[spec] * 3, out_shape=[shape] * 3, name=name,
        compiler_params=_cp("parallel"))(w, g, m, v)


def _me():
    return lax.axis_index("x"), lax.axis_index("y"), lax.axis_index("c")


def _flip(v, bit):
    return 1 - v if bit else v


HBM = pl.BlockSpec(memory_space=pl.ANY)
VMEM = pl.BlockSpec(memory_space=pltpu.VMEM)


def all_gather_rows(v, name):
    n = v.shape[1]

    def body(v_ref, out_ref, send_sems, recv_sems):
        x, y, c = _me()
        me = 4 * x + 2 * y + c
        out_ref[pl.ds(me, 1), :] = v_ref[...]
        copies = []
        for j in range(1, N_DEV):
            peer = (_flip(x, j & 4), _flip(y, j & 2), _flip(c, j & 1))
            copies.append(pltpu.make_async_remote_copy(
                src_ref=v_ref, dst_ref=out_ref.at[pl.ds(me, 1), :], send_sem=send_sems.at[j - 1],
                recv_sem=recv_sems.at[j - 1], device_id=peer, device_id_type=MESH))
        for cp in copies:
            cp.start()
        for cp in copies:
            cp.wait()

    return pl.pallas_call(
        body, in_specs=[VMEM], out_specs=VMEM, out_shape=jax.ShapeDtypeStruct((N_DEV, n), F32),
        scratch_shapes=[pltpu.SemaphoreType.DMA((N_DEV - 1,)), pltpu.SemaphoreType.DMA((N_DEV - 1,))], name=name)(v)


def _chip_peers(x, y):
    return [((1 - x, y), 2 * (1 - x) + y), ((x, 1 - y), 2 * x + (1 - y)), ((1 - x, 1 - y), 2 * (1 - x) + (1 - y))]


def chip_all_gather(shards, name):
    n = len(shards)

    def body(*refs):
        ins, outs = refs[:n], refs[n:2 * n]
        local_sem, send_sems, recv_sems = refs[2 * n:]
        x, y, c = _me()
        mine = 2 * x + y
        started = []
        for a in range(n):
            cp = pltpu.make_async_copy(ins[a], outs[a].at[mine], local_sem.at[a])
            cp.start()
            started.append(cp)
            for p, (chip, _) in enumerate(_chip_peers(x, y)):
                cp = pltpu.make_async_remote_copy(
                    src_ref=ins[a], dst_ref=outs[a].at[mine], send_sem=send_sems.at[a, p], recv_sem=recv_sems.at[a, p],
                    device_id=(*chip, c), device_id_type=MESH)
                cp.start()
                started.append(cp)
        for cp in started:
            cp.wait()

    return pl.pallas_call(
        body, in_specs=[HBM] * n, out_specs=[HBM] * n,
        out_shape=[jax.ShapeDtypeStruct((N_CHIPS, *s.shape), s.dtype) for s in shards],
        scratch_shapes=[pltpu.SemaphoreType.DMA((n,)), pltpu.SemaphoreType.DMA((n, 3)), pltpu.SemaphoreType.DMA((n, 3))],
        name=name)(*shards)


def sibling_split(grads, name):
    n = len(grads)

    def body(*refs):
        ins, own, got = refs[:n], refs[n:2 * n], refs[2 * n:3 * n]
        local_sem, send_sems, recv_sems = refs[3 * n:]
        x, y, c = _me()
        started = []
        for a in range(n):
            half = ins[a].shape[1] // 2
            keep = pltpu.make_async_copy(ins[a].at[:, pl.ds(c * half, half), :], own[a], local_sem.at[a])
            give = pltpu.make_async_remote_copy(
                src_ref=ins[a].at[:, pl.ds((1 - c) * half, half), :], dst_ref=got[a], send_sem=send_sems.at[a],
                recv_sem=recv_sems.at[a], device_id=(x, y, 1 - c), device_id_type=MESH)
            keep.start()
            give.start()
            started += [keep, give]
        for cp in started:
            cp.wait()

    halves = [jax.ShapeDtypeStruct((g.shape[0], g.shape[1] // 2, g.shape[2]), g.dtype) for g in grads]
    return pl.pallas_call(
        body, in_specs=[HBM] * n, out_specs=[HBM] * (2 * n), out_shape=halves + halves,
        scratch_shapes=[pltpu.SemaphoreType.DMA((n,)), pltpu.SemaphoreType.DMA((n,)), pltpu.SemaphoreType.DMA((n,))],
        name=name)(*grads)


def chip_scatter(parts, name):
    n = len(parts)

    def body(*refs):
        ins, outs = refs[:n], refs[n:2 * n]
        local_sem, send_sems, recv_sems = refs[2 * n:]
        x, y, c = _me()
        mine = 2 * x + y
        started = []
        for a in range(n):
            cp = pltpu.make_async_copy(ins[a].at[mine], outs[a].at[mine], local_sem.at[a])
            cp.start()
            started.append(cp)
            for p, (chip, slot) in enumerate(_chip_peers(x, y)):
                cp = pltpu.make_async_remote_copy(
                    src_ref=ins[a].at[slot], dst_ref=outs[a].at[mine], send_sem=send_sems.at[a, p],
                    recv_sem=recv_sems.at[a, p], device_id=(*chip, c), device_id_type=MESH)
                cp.start()
                started.append(cp)
        for cp in started:
            cp.wait()

    return pl.pallas_call(
        body, in_specs=[HBM] * n, out_specs=[HBM] * n,
        out_shape=[jax.ShapeDtypeStruct(p.shape, p.dtype) for p in parts],
        scratch_shapes=[pltpu.SemaphoreType.DMA((n,)), pltpu.SemaphoreType.DMA((n, 3)), pltpu.SemaphoreType.DMA((n, 3))],
        name=name)(*parts)


def sibling_join(halves, name):
    n = len(halves)

    def body(*refs):
        ins, outs = refs[:n], refs[n:2 * n]
        local_sem, send_sems, recv_sems = refs[2 * n:]
        x, y, c = _me()
        started = []
        for a in range(n):
            half = ins[a].shape[0]
            rows = outs[a].at[pl.ds(c * half, half), :]
            keep = pltpu.make_async_copy(ins[a], rows, local_sem.at[a])
            give = pltpu.make_async_remote_copy(
                src_ref=ins[a], dst_ref=rows, send_sem=send_sems.at[a], recv_sem=recv_sems.at[a],
                device_id=(x, y, 1 - c), device_id_type=MESH)
            keep.start()
            give.start()
            started += [keep, give]
        for cp in started:
            cp.wait()

    return pl.pallas_call(
        body, in_specs=[HBM] * n, out_specs=[HBM] * n,
        out_shape=[jax.ShapeDtypeStruct((2 * h.shape[0], h.shape[1]), h.dtype) for h in halves],
        scratch_shapes=[pltpu.SemaphoreType.DMA((n,)), pltpu.SemaphoreType.DMA((n,)), pltpu.SemaphoreType.DMA((n,))],
        name=name)(*halves)


def reduce_to_owner(grads):
    own, got = (lambda r: (r[:len(grads)], r[len(grads):]))(sibling_split(grads, "grad_sibling_split"))
    parts = [add2(o.reshape(-1, o.shape[2]), g.reshape(-1, g.shape[2]), f"grad_pair_sum_{a}").reshape(o.shape)
             for a, (o, g) in enumerate(zip(own, got))]
    landed = chip_scatter(parts, "grad_chip_scatter")
    halves = [sum_rows(l, f"grad_chip_sum_{a}") for a, l in enumerate(landed)]
    return sibling_join(halves, "grad_sibling_join")


def _pad_row(v, n):
    return jnp.pad(v.reshape(1, -1), ((0, 0), (0, n - v.size)))


def local_step(x, target, mod, wts):
    d = D_MODEL
    row = lambda v: v.reshape(1, -1)
    mods = [[row(mod[l, i * d:(i + 1) * d]) for i in range(6)] for l in range(2)]
    saved = []
    for l in range(2):
        shift1, scale1, gate1, shift2, scale2, gate2 = mods[l]
        g_mix, g_mlp = row(wts["norm_mix"][l]), row(wts["norm_mlp"][l])
        h = norm_mod_fwd(x, g_mix, scale1, shift1, f"norm_mix_fwd_{l}")
        if l == 0:
            u = matmul_nn_chunked(h, wts["w_in"], F32, "in_proj_ab")[0]
            y_a = conv_mixer_fwd(u, wts["conv_w"])
            y_b, o_raw, states = hgrn_fwd(u, wts["lb_logits"], wts["hg_norm"])
            mix = jnp.concatenate([y_a, y_b], axis=1)
            y, x1 = proj_residual(mix, wts["w_out_ab"], x, gate1, "out_proj_ab")
            ctx = (u, o_raw, states)
        else:
            qkv = matmul_nn_chunked(h, wts["w_qkv"], F32, "in_proj_c")[0]
            qn, kn, vb = qk_norm_fwd(qkv, wts["qg"], wts["kg"])
            o, mix = sb_attn_fwd(qn, kn, vb)
            y, x1 = proj_residual(mix, wts["w_out_c"], x, gate1, "out_proj_c")
            ctx = (qkv, qn, kn, vb, o)
        h2 = norm_mod_fwd(x1, g_mlp, scale2, shift2, f"norm_mlp_fwd_{l}")
        act, r = mlp_up(h2, wts["w1"][l], f"mlp_up_{l}")
        y2, x2 = proj_residual(act, wts["w2"][l], x1, gate2, f"mlp_down_{l}")
        saved.append((x, h, mix, y, x1, h2, act, r, y2, ctx))
        x = x2

    dx, loss_row = loss_and_grad(x, target)
    small, big = {}, {}
    dmod = [None, None]
    d_norm_mix, d_norm_mlp = [None, None], [None, None]
    for l in (1, 0):
        shift1, scale1, gate1, shift2, scale2, gate2 = mods[l]
        g_mix, g_mlp = row(wts["norm_mix"][l]), row(wts["norm_mlp"][l])
        x0, h, mix, y, x1, h2, act, r, y2, ctx = saved[l]
        dy2, dgate2 = gate_bwd(dx, y2, gate2, f"mlp_gate_bwd_{l}")
        dz = mlp_down_bwd(dy2, wts["w2"][l], r, f"mlp_down_bwd_{l}")
        big[f"w2_{l}"] = matmul_tn_plain(act, dy2, f"mlp_w2_grad_{l}")
        dh2 = matmul_nt_chunked(dz, wts["w1"][l], f"mlp_up_bwd_{l}")
        big[f"w1_{l}"] = matmul_tn_chunked(h2, dz, f"mlp_w1_grad_{l}")
        dx1, d_norm_mlp[l], dscale2, dshift2 = norm_mod_bwd(dh2, x1, g_mlp, scale2, dx, f"norm_mlp_bwd_{l}")
        dy, dgate1 = gate_bwd(dx1, y, gate1, f"mix_gate_bwd_{l}")
        if l == 0:
            u, o_raw, states = ctx
            dmix = matmul_nt_plain(dy, wts["w_out_ab"], "out_proj_ab_bwd")
            big["w_out_ab"] = matmul_tn_plain(mix, dy, "w_out_ab_grad")
            dab, dac, dah, small["conv_w"] = conv_mixer_bwd(dmix, u, wts["conv_w"])
            dhq, dhf, dhi, dhg, small["hg_norm"], small["lb_logits"] = hgrn_bwd(
                dmix, u, o_raw, states, wts["lb_logits"], wts["hg_norm"])
            du = jnp.concatenate([dab, dac, dah, dhq, dhf, dhi, dhg], axis=1)
            dh = matmul_nt_chunked(du, wts["w_in"], "in_proj_ab_bwd")
            big["w_in"] = matmul_tn_chunked(h, du, "w_in_grad")
        else:
            qkv, qn, kn, vb, o = ctx
            do = matmul_nt_plain(dy, wts["w_out_c"], "out_proj_c_bwd")
            big["w_out_c"] = matmul_tn_plain(mix, dy, "w_out_c_grad")
            dqn, dkn, dv = sb_attn_bwd(qn, kn, vb, o, do)
            dq, dk, dvb, dqg, dkg = qk_norm_bwd(dqn, dkn, dv, qkv, wts["qg"], wts["kg"])
            small["q_norm"] = dqg[:, :SB_HEAD_DIM] + dqg[:, SB_HEAD_DIM:]
            small["k_norm"] = dkg[:, :SB_HEAD_DIM] + dkg[:, SB_HEAD_DIM:]
            dqkv = jnp.concatenate([dq, dk, dvb], axis=1)
            dh = matmul_nt_chunked(dqkv, wts["w_qkv"], "in_proj_c_bwd")
            big["w_qkv"] = matmul_tn_chunked(h, dqkv, "w_qkv_grad")
        dx, d_norm_mix[l], dscale1, dshift1 = norm_mod_bwd(dh, x0, g_mix, scale1, dx1, f"norm_mix_bwd_{l}")
        dmod[l] = jnp.concatenate([dshift1, dscale1, dgate1, dshift2, dscale2, dgate2], axis=1)
    small["mod"] = jnp.concatenate(dmod, axis=0)
    small["norm_mix"] = jnp.concatenate(d_norm_mix, axis=0)
    small["norm_mlp"] = jnp.concatenate(d_norm_mlp, axis=0)
    return loss_row, dx, small, big


SMALL_ORDER = ("mod", "norm_mix", "norm_mlp", "conv_w", "hg_norm", "lb_logits", "q_norm", "k_norm")


def kernel(x, c, ada_w, ada_b, norm_mix, norm_mlp, w_in_ab, conv_w, hg_norm, lb_logits, w_out_ab, w_qkv, q_norm, k_norm, w_out_c, mlp_w1, mlp_w2, loss_target, m_ada_w, m_ada_b, m_norm_mix, m_norm_mlp, m_w_in_ab, m_conv_w, m_hg_norm, m_lb_logits, m_w_out_ab, m_w_qkv, m_q_norm, m_k_norm, m_w_out_c, m_mlp_w1, m_mlp_w2, v_ada_w, v_ada_b, v_norm_mix, v_norm_mlp, v_w_in_ab, v_conv_w, v_hg_norm, v_lb_logits, v_w_out_ab, v_w_qkv, v_q_norm, v_k_norm, v_w_out_c, v_mlp_w1, v_mlp_w2):
    d = D_MODEL
    ax, ay, ac = _me()
    chip = 2 * ax + ay
    dev = 2 * chip + ac
    cols = ada_w.shape[2]

    first = all_gather_rows(_pad_row(jnp.concatenate([c.reshape(-1), conv_w.reshape(-1)]), 1536), "gather_cond")
    c_all = first[:, :d]
    conv_full = first[::2, d:d + 3 * HEAD_TILE].reshape(N_CHIPS, 3, HEAD_TILE).transpose(1, 0, 2).reshape(3, CONV_DIM)
    ada_b_shard = lax.dynamic_slice(ada_b, (0, chip * cols), (2, cols)).reshape(2, 1, cols)
    mod_cols = ada_fwd(c_all, ada_w, ada_b_shard)
    mod_all = all_gather_rows(mod_cols.reshape(1, -1), "gather_mod").reshape(N_DEV, 2, N_DEV, cols)
    mod = lax.dynamic_index_in_dim(mod_all[::2], dev, axis=2, keepdims=False).transpose(1, 0, 2).reshape(2, 6 * d)

    shards = [w_in_ab[0], w_out_ab[0], w_qkv[0], w_out_c[0], mlp_w1[0], mlp_w1[1], mlp_w2[0], mlp_w2[1]]
    g_in, g_out_ab, g_qkv, g_out_c, g_w1a, g_w1b, g_w2a, g_w2b = chip_all_gather(
        [s.astype(BF16) for s in shards], "gather_weights")
    wts = dict(
        norm_mix=norm_mix, norm_mlp=norm_mlp, w_in=g_in, conv_w=conv_full, hg_norm=hg_norm, lb_logits=lb_logits,
        w_out_ab=g_out_ab.reshape(d, d), w_qkv=g_qkv, qg=jnp.tile(q_norm, (1, 2)), kg=jnp.tile(k_norm, (1, 2)),
        w_out_c=g_out_c.reshape(d, d), w1=[g_w1a, g_w1b], w2=[g_w2a.reshape(D_FF, d), g_w2b.reshape(D_FF, d)])

    loss_row, grad_x, small, big = local_step(x[0], loss_target[0], mod, wts)

    flat = jnp.concatenate([small[k].reshape(-1) for k in SMALL_ORDER] + [loss_row[0, :1]])
    n_small = -(-flat.size // 1024) * 1024
    gathered = all_gather_rows(_pad_row(flat, n_small), "gather_small")
    total = sum_rows(gathered.reshape(N_DEV, 8, n_small // 8), "small_sum").reshape(-1)
    sizes = [small[k].size for k in SMALL_ORDER]
    offs = [sum(sizes[:i]) for i in range(len(sizes) + 1)]
    tot = {k: total[offs[i]:offs[i + 1]].reshape(small[k].shape) for i, k in enumerate(SMALL_ORDER)}
    loss = total[offs[-1]]
    mod_rows = gathered[:, :2 * 6 * d].reshape(N_DEV, 2, 6 * d)
    dmod_cols = lax.dynamic_slice(mod_rows, (0, 0, chip * cols), (N_DEV, 2, cols)).transpose(1, 0, 2)
    g_ada_w = ada_w_grad(c_all, dmod_cols)

    as_chunks = lambda g: g.reshape(N_CHIPS, g.shape[0] // N_CHIPS, g.shape[1])
    names = ["w_in", "w_out_ab", "w_qkv", "w_out_c", "w1_0", "w1_1", "w2_0", "w2_1"]
    chunked = [big[k] if big[k].ndim == 3 else as_chunks(big[k]) for k in names]
    r_in, r_out_ab, r_qkv, r_out_c, r_w1a, r_w1b, r_w2a, r_w2b = reduce_to_owner(chunked)

    grads = dict(
        ada_w=g_ada_w, ada_b=tot["mod"], norm_mix=tot["norm_mix"], norm_mlp=tot["norm_mlp"], w_in_ab=r_in[None],
        conv_w=lax.dynamic_slice(tot["conv_w"], (0, chip * HEAD_TILE), (3, HEAD_TILE))[None], hg_norm=tot["hg_norm"],
        lb_logits=tot["lb_logits"], w_out_ab=r_out_ab[None], w_qkv=r_qkv[None], q_norm=tot["q_norm"],
        k_norm=tot["k_norm"], w_out_c=r_out_c[None], mlp_w1=jnp.stack([r_w1a, r_w1b]), mlp_w2=jnp.stack([r_w2a, r_w2b]))
    weights = dict(ada_w=ada_w, ada_b=ada_b, norm_mix=norm_mix, norm_mlp=norm_mlp, w_in_ab=w_in_ab, conv_w=conv_w,
                   hg_norm=hg_norm, lb_logits=lb_logits, w_out_ab=w_out_ab, w_qkv=w_qkv, q_norm=q_norm, k_norm=k_norm,
                   w_out_c=w_out_c, mlp_w1=mlp_w1, mlp_w2=mlp_w2)
    m_in = dict(ada_w=m_ada_w, ada_b=m_ada_b, norm_mix=m_norm_mix, norm_mlp=m_norm_mlp, w_in_ab=m_w_in_ab,
                conv_w=m_conv_w, hg_norm=m_hg_norm, lb_logits=m_lb_logits, w_out_ab=m_w_out_ab, w_qkv=m_w_qkv,
                q_norm=m_q_norm, k_norm=m_k_norm, w_out_c=m_w_out_c, mlp_w1=m_mlp_w1, mlp_w2=m_mlp_w2)
    v_in = dict(ada_w=v_ada_w, ada_b=v_ada_b, norm_mix=v_norm_mix, norm_mlp=v_norm_mlp, w_in_ab=v_w_in_ab,
                conv_w=v_conv_w, hg_norm=v_hg_norm, lb_logits=v_lb_logits, w_out_ab=v_w_out_ab, w_qkv=v_w_qkv,
                q_norm=v_q_norm, k_norm=v_k_norm, w_out_c=v_w_out_c, mlp_w1=v_mlp_w1, mlp_w2=v_mlp_w2)
    order = list(weights)
    large = ("ada_w", "w_in_ab", "w_out_ab", "w_qkv", "w_out_c", "mlp_w1", "mlp_w2")
    delta, new_m, new_v = {}, {}, {}
    for k in large:
        shape = weights[k].shape
        flat2 = lambda a: a.reshape(-1, shape[-1])
        dl, nm, nv = adamw(flat2(weights[k]), flat2(grads[k]), flat2(m_in[k]), flat2(v_in[k]), f"adamw_{k}")
        delta[k], new_m[k], new_v[k] = dl.reshape(shape), nm.reshape(shape), nv.reshape(shape)
    rest = [k for k in order if k not in large]
    n_rest = -(-sum(weights[k].size for k in rest) // 1024) * 1024
    pack = lambda tree: _pad_row(jnp.concatenate([tree[k].reshape(-1) for k in rest]), n_rest).reshape(8, n_rest // 8)
    dl, nm, nv = adamw(pack(weights), pack(grads), pack(m_in), pack(v_in), "adamw_small")
    off = 0
    for k in rest:
        size, shape = weights[k].size, weights[k].shape
        delta[k], new_m[k], new_v[k] = (a.reshape(-1)[off:off + size].reshape(shape) for a in (dl, nm, nv))
        off += size
    grads = {k: grads[k].reshape(weights[k].shape) for k in order}
    return (loss, grad_x[None], *[grads[k] for k in order], *[delta[k] for k in order],
            *[new_m[k] for k in order], *[new_v[k] for k in order])
```

```python
import functools

import jax
import jax.numpy as jnp
from jax import lax
from jax.experimental import pallas as pl
from jax.experimental.pallas import tpu as pltpu

F32 = jnp.float32
BF16 = jnp.bfloat16
HIGHEST = lax.Precision.HIGHEST
MESH = pl.DeviceIdType.MESH

D_MODEL = 1024
D_FF = 4096
CHUNK = 64
HEAD_TILE = 128
SB_HEAD_DIM = 64
CONV_DIM = 512
HG_DIM = 512
AB_IN = 3584
N_CHIPS = 4
N_DEV = 8
EPS = 1e-6
ATT_BLOCK = 128
ATT_TILE = 256
HG_SLAB = 512

ADAM_LR = 0.001
ADAM_B1 = 0.9
ADAM_B2 = 0.999
ADAM_EPS = 1e-08
ADAM_WD = 0.01
ADAM_STEP = 10

NN = (((1,), (0,)), ((), ()))
NT = (((1,), (1,)), ((), ()))
TN = (((0,), (0,)), ((), ()))


def _cp(*dims):
    return pltpu.CompilerParams(dimension_semantics=dims) if dims else pltpu.CompilerParams()


def _dot(a, b, dn, precision=None):
    return lax.dot_general(a, b, dn, preferred_element_type=F32, precision=precision)


def _sigmoid(z):
    return 1.0 / (1.0 + jnp.exp(-z))


def _matmul(a, b, *, dn, grid, a_spec, b_spec, acc_shape, epilogue, extras=(), extra_specs=(),
            out_shapes, out_specs, name):
    nk = grid[2]
    n_extra = len(extras)
    n_out = len(out_shapes)

    def body(*refs):
        a_ref, b_ref = refs[0], refs[1]
        extra_refs = refs[2:2 + n_extra]
        out_refs = refs[2 + n_extra:2 + n_extra + n_out]
        acc_ref = refs[-1]
        k = pl.program_id(2)
        part = _dot(a_ref[...], b_ref[...], dn)

        if nk == 1:
            epilogue(part, extra_refs, out_refs)
        else:
            @pl.when(k == 0)
            def _():
                acc_ref[...] = part

            @pl.when(k > 0)
            def _():
                acc_ref[...] += part

            @pl.when(k == nk - 1)
            def _():
                epilogue(acc_ref[...], extra_refs, out_refs)

    return pl.pallas_call(
        body, grid=grid, in_specs=[a_spec, b_spec, *extra_specs], out_specs=out_specs, out_shape=out_shapes,
        scratch_shapes=[pltpu.VMEM(acc_shape, F32)], name=name,
        compiler_params=_cp("parallel", "parallel", "arbitrary"))(a, b, *extras)


def _store(dtype):
    def epilogue(acc, extra_refs, out_refs):
        out_refs[0][...] = acc.astype(dtype)
    return epilogue


def _tok_tile(s):
    return min(512, s)


def matmul_nn_chunked(a, w, out_dtype, name, epilogue=None, extras=(), extra_specs=(), out_shapes=None,
                      out_specs=None):
    s, k = a.shape
    _, _, n4 = w.shape
    tm = _tok_tile(s)
    if out_shapes is None:
        out_shapes = [jax.ShapeDtypeStruct((s, N_CHIPS * n4), out_dtype)]
        out_specs = [pl.BlockSpec((tm, n4), lambda i, j, kk: (i, j))]
        epilogue = _store(out_dtype)
    return _matmul(
        a, w, dn=NN, grid=(s // tm, N_CHIPS, 1),
        a_spec=pl.BlockSpec((tm, k), lambda i, j, kk: (i, 0)),
        b_spec=pl.BlockSpec((None, k, n4), lambda i, j, kk: (j, 0, 0)),
        acc_shape=(tm, n4), epilogue=epilogue, extras=extras, extra_specs=extra_specs,
        out_shapes=out_shapes, out_specs=out_specs, name=name)


def matmul_nt_chunked(dy, w, name):
    s = dy.shape[0]
    _, k, n4 = w.shape
    tm = _tok_tile(s)
    return _matmul(
        dy, w, dn=NT, grid=(s // tm, 1, N_CHIPS),
        a_spec=pl.BlockSpec((tm, n4), lambda i, j, kk: (i, kk)),
        b_spec=pl.BlockSpec((None, k, n4), lambda i, j, kk: (kk, 0, 0)),
        acc_shape=(tm, k), epilogue=_store(F32),
        out_shapes=[jax.ShapeDtypeStruct((s, k), F32)], out_specs=[pl.BlockSpec((tm, k), lambda i, j, kk: (i, 0))],
        name=name)[0]


def matmul_tn_chunked(x, dy, name):
    s, k = x.shape
    n4 = dy.shape[1] // N_CHIPS
    ts = _tok_tile(s)
    return _matmul(
        x, dy, dn=TN, grid=(1, N_CHIPS, s // ts),
        a_spec=pl.BlockSpec((ts, k), lambda i, j, kk: (kk, 0)),
        b_spec=pl.BlockSpec((ts, n4), lambda i, j, kk: (kk, j)),
        acc_shape=(k, n4), epilogue=_store(F32),
        out_shapes=[jax.ShapeDtypeStruct((N_CHIPS, k, n4), F32)],
        out_specs=[pl.BlockSpec((None, k, n4), lambda i, j, kk: (j, 0, 0))], name=name)[0]


def matmul_nn_plain(a, w, name, epilogue, extras, extra_specs, out_shapes, out_specs, tk=1024):
    s, k = a.shape
    n = w.shape[1]
    tm = _tok_tile(s)
    return _matmul(
        a, w, dn=NN, grid=(s // tm, 1, k // tk),
        a_spec=pl.BlockSpec((tm, tk), lambda i, j, kk: (i, kk)),
        b_spec=pl.BlockSpec((tk, n), lambda i, j, kk: (kk, 0)),
        acc_shape=(tm, n), epilogue=epilogue, extras=extras, extra_specs=extra_specs,
        out_shapes=out_shapes, out_specs=out_specs, name=name)


def matmul_nt_plain(dy, w, name, epilogue=None, extras=(), extra_specs=(), out_dtype=F32, tn=1024):
    s, n = dy.shape
    k = w.shape[0]
    tm = _tok_tile(s)
    return _matmul(
        dy, w, dn=NT, grid=(s // tm, k // tn, 1),
        a_spec=pl.BlockSpec((tm, n), lambda i, j, kk: (i, 0)),
        b_spec=pl.BlockSpec((tn, n), lambda i, j, kk: (j, 0)),
        acc_shape=(tm, tn), epilogue=epilogue or _store(out_dtype), extras=extras, extra_specs=extra_specs,
        out_shapes=[jax.ShapeDtypeStruct((s, k), out_dtype)], out_specs=[pl.BlockSpec((tm, tn), lambda i, j, kk: (i, j))],
        name=name)[0]


def matmul_tn_plain(x, dy, name, tk=1024):
    s, k = x.shape
    n = dy.shape[1]
    ts = _tok_tile(s)
    return _matmul(
        x, dy, dn=TN, grid=(k // tk, 1, s // ts),
        a_spec=pl.BlockSpec((ts, tk), lambda i, j, kk: (kk, i)),
        b_spec=pl.BlockSpec((ts, n), lambda i, j, kk: (kk, 0)),
        acc_shape=(tk, n), epilogue=_store(F32),
        out_shapes=[jax.ShapeDtypeStruct((k, n), F32)], out_specs=[pl.BlockSpec((tk, n), lambda i, j, kk: (i, 0))],
        name=name)[0]


def _row_spec(n):
    return pl.BlockSpec((1, n), lambda i: (0, 0))


def norm_mod_fwd(x, g, scale, shift, name):
    s, d = x.shape
    tm = _tok_tile(s)

    def body(x_ref, g_ref, sc_ref, sh_ref, h_ref):
        xv = x_ref[...]
        r = lax.rsqrt(jnp.mean(xv * xv, axis=-1, keepdims=True) + EPS)
        h_ref[...] = ((xv * r * g_ref[...]) * (1.0 + sc_ref[...]) + sh_ref[...]).astype(BF16)

    tile = pl.BlockSpec((tm, d), lambda i: (i, 0))
    return pl.pallas_call(
        body, grid=(s // tm,), in_specs=[tile, _row_spec(d), _row_spec(d), _row_spec(d)], out_specs=tile,
        out_shape=jax.ShapeDtypeStruct((s, d), BF16), name=name, compiler_params=_cp("parallel"))(x, g, scale, shift)


def norm_mod_bwd(dh, x, g, scale, dres, name):
    s, d = x.shape
    tm = _tok_tile(s)

    def body(dh_ref, x_ref, g_ref, sc_ref, dres_ref, dx_ref, dg_ref, dsc_ref, dsh_ref):
        @pl.when(pl.program_id(0) == 0)
        def _():
            dg_ref[...] = jnp.zeros_like(dg_ref)
            dsc_ref[...] = jnp.zeros_like(dsc_ref)
            dsh_ref[...] = jnp.zeros_like(dsh_ref)

        xv = x_ref[...]
        dhv = dh_ref[...]
        r = lax.rsqrt(jnp.mean(xv * xv, axis=-1, keepdims=True) + EPS)
        xn = xv * r
        gv = g_ref[...]
        s1 = 1.0 + sc_ref[...]
        dsh_ref[...] += jnp.sum(dhv, axis=0, keepdims=True)
        dsc_ref[...] += jnp.sum(dhv * xn * gv, axis=0, keepdims=True)
        dg_ref[...] += jnp.sum(dhv * xn * s1, axis=0, keepdims=True)
        dxn = dhv * gv * s1
        dx_ref[...] = dres_ref[...] + r * (dxn - xn * jnp.mean(dxn * xn, axis=-1, keepdims=True))

    tile = pl.BlockSpec((tm, d), lambda i: (i, 0))
    row = jax.ShapeDtypeStruct((1, d), F32)
    return pl.pallas_call(
        body, grid=(s // tm,), in_specs=[tile, tile, _row_spec(d), _row_spec(d), tile],
        out_specs=[tile, _row_spec(d), _row_spec(d), _row_spec(d)],
        out_shape=[jax.ShapeDtypeStruct((s, d), F32), row, row, row], name=name,
        compiler_params=_cp("arbitrary"))(dh, x, g, scale, dres)


def gate_bwd(dx, y, gate, name):
    s, d = dx.shape
    tm = _tok_tile(s)

    def body(dx_ref, y_ref, gate_ref, dy_ref, dgate_ref):
        @pl.when(pl.program_id(0) == 0)
        def _():
            dgate_ref[...] = jnp.zeros_like(dgate_ref)

        dxv = dx_ref[...]
        dy_ref[...] = (gate_ref[...] * dxv).astype(BF16)
        dgate_ref[...] += jnp.sum(dxv * y_ref[...], axis=0, keepdims=True)

    tile = pl.BlockSpec((tm, d), lambda i: (i, 0))
    return pl.pallas_call(
        body, grid=(s // tm,), in_specs=[tile, tile, _row_spec(d)], out_specs=[tile, _row_spec(d)],
        out_shape=[jax.ShapeDtypeStruct((s, d), BF16), jax.ShapeDtypeStruct((1, d), F32)], name=name,
        compiler_params=_cp("arbitrary"))(dx, y, gate)


def loss_and_grad(y, target):
    s, d = y.shape
    tm = _tok_tile(s)

    def body(y_ref, t_ref, dy_ref, loss_ref):
        @pl.when(pl.program_id(0) == 0)
        def _():
            loss_ref[...] = jnp.zeros_like(loss_ref)

        err = y_ref[...] - t_ref[...]
        dy_ref[...] = err * (1.0 / d)
        loss_ref[...] += jnp.sum(err * err) * (0.5 / d)

    tile = pl.BlockSpec((tm, d), lambda i: (i, 0))
    return pl.pallas_call(
        body, grid=(s // tm,), in_specs=[tile, tile], out_specs=[tile, _row_spec(128)],
        out_shape=[jax.ShapeDtypeStruct((s, d), F32), jax.ShapeDtypeStruct((1, 128), F32)], name="loss_and_grad",
        compiler_params=_cp("arbitrary"))(y, target)


def _proj_residual(acc, extra_refs, out_refs):
    x_ref, gate_ref = extra_refs
    out_refs[0][...] = acc
    out_refs[1][...] = x_ref[...] + gate_ref[...] * acc


def proj_residual(a, w, x, gate, name):
    s, d = x.shape
    tm = _tok_tile(s)
    tile = pl.BlockSpec((tm, d), lambda i, j, kk: (i, 0))
    shape = jax.ShapeDtypeStruct((s, d), F32)
    return matmul_nn_plain(
        a, w, name, _proj_residual, extras=(x, gate),
        extra_specs=(tile, pl.BlockSpec((1, d), lambda i, j, kk: (0, 0))),
        out_shapes=[shape, shape], out_specs=[tile, tile])


def _mlp_up(acc, extra_refs, out_refs):
    r = jnp.maximum(acc, 0.0)
    out_refs[0][...] = (r * r).astype(BF16)
    out_refs[1][...] = r.astype(BF16)


def mlp_up(h, w1, name):
    s = h.shape[0]
    n4 = w1.shape[2]
    tm = _tok_tile(s)
    shape = jax.ShapeDtypeStruct((s, N_CHIPS * n4), BF16)
    spec = pl.BlockSpec((tm, n4), lambda i, j, kk: (i, j))
    return matmul_nn_chunked(h, w1, BF16, name, epilogue=_mlp_up, out_shapes=[shape, shape], out_specs=[spec, spec])


def _dact(acc, extra_refs, out_refs):
    out_refs[0][...] = (acc * (2.0 * extra_refs[0][...].astype(F32))).astype(BF16)


def mlp_down_bwd(dy, w2, r, name):
    s = dy.shape[0]
    tm = _tok_tile(s)
    return matmul_nt_plain(dy, w2, name, epilogue=_dact, extras=(r,),
                           extra_specs=(pl.BlockSpec((tm, 1024), lambda i, j, kk: (i, j)),), out_dtype=BF16)


def _shift_down(p, n, row):
    return jnp.where(row >= n, pltpu.roll(p, n, 0), 0.0)


def _shift_up(p, n, row):
    rows = p.shape[0]
    return jnp.where(row < rows - n, pltpu.roll(p, rows - n, 0), 0.0)


def _u_col(block):
    return lambda i: (0, block + i)


def conv_mixer_fwd(u, conv_w):
    s = u.shape[0]
    nb = CONV_DIM // HEAD_TILE

    def body(ab_ref, ac_ref, ah_ref, w_ref, y_ref):
        row = lax.broadcasted_iota(jnp.int32, (s, HEAD_TILE), 0)
        p = ac_ref[...] * ah_ref[...]
        w = w_ref[...]
        conv = w[0:1] * _shift_down(p, 2, row) + w[1:2] * _shift_down(p, 1, row) + w[2:3] * p
        y_ref[...] = (ab_ref[...] * conv).astype(BF16)

    col = lambda b: pl.BlockSpec((s, HEAD_TILE), _u_col(b * nb))
    return pl.pallas_call(
        body, grid=(nb,), in_specs=[col(0), col(1), col(2), pl.BlockSpec((3, HEAD_TILE), lambda i: (0, i))],
        out_specs=pl.BlockSpec((s, HEAD_TILE), lambda i: (0, i)),
        out_shape=jax.ShapeDtypeStruct((s, CONV_DIM), BF16), name="conv_mixer_fwd",
        compiler_params=_cp("parallel"))(u, u, u, conv_w)


def conv_mixer_bwd(dmix, u, conv_w):
    s = u.shape[0]
    nb = CONV_DIM // HEAD_TILE

    def body(dy_ref, ab_ref, ac_ref, ah_ref, w_ref, dab_ref, dac_ref, dah_ref, dw_ref):
        row = lax.broadcasted_iota(jnp.int32, (s, HEAD_TILE), 0)
        ac = ac_ref[...]
        ah = ah_ref[...]
        p = ac * ah
        w = w_ref[...]
        p1 = _shift_down(p, 1, row)
        p2 = _shift_down(p, 2, row)
        conv = w[0:1] * p2 + w[1:2] * p1 + w[2:3] * p
        dy = dy_ref[...]
        dab_ref[...] = (dy * conv).astype(BF16)
        dconv = dy * ab_ref[...]
        dp = w[0:1] * _shift_up(dconv, 2, row) + w[1:2] * _shift_up(dconv, 1, row) + w[2:3] * dconv
        dac_ref[...] = (dp * ah).astype(BF16)
        dah_ref[...] = (dp * ac).astype(BF16)
        dw_ref[...] = jnp.concatenate(
            [jnp.sum(dconv * p2, axis=0, keepdims=True), jnp.sum(dconv * p1, axis=0, keepdims=True),
             jnp.sum(dconv * p, axis=0, keepdims=True)], axis=0)

    col = lambda b: pl.BlockSpec((s, HEAD_TILE), _u_col(b * nb))
    out = pl.BlockSpec((s, HEAD_TILE), lambda i: (0, i))
    wspec = pl.BlockSpec((3, HEAD_TILE), lambda i: (0, i))
    shape = jax.ShapeDtypeStruct((s, CONV_DIM), BF16)
    return pl.pallas_call(
        body, grid=(nb,), in_specs=[out, col(0), col(1), col(2), wspec], out_specs=[out, out, out, wspec],
        out_shape=[shape, shape, shape, jax.ShapeDtypeStruct((3, CONV_DIM), F32)], name="conv_mixer_bwd",
        compiler_params=_cp("parallel"))(dmix, u, u, u, conv_w)


def _chunk_cumsum(g, pos):
    for sh in (1, 2, 4, 8, 16, 32):
        g = g + jnp.where(pos >= sh, pltpu.roll(g, sh, 0), 0.0)
    return g


def _chunk_rev_cumsum(g, pos):
    rows = g.shape[0]
    for sh in (1, 2, 4, 8, 16, 32):
        g = g + jnp.where(pos < CHUNK - sh, pltpu.roll(g, rows - sh, 0), 0.0)
    return g


def _lower_bound(lb_ref):
    logits = lb_ref[...]
    e = jnp.exp(logits - jnp.max(logits, axis=0, keepdims=True))
    p = e / jnp.sum(e, axis=0, keepdims=True)
    return p[0:1], p


def _hg_gates(hf, lb):
    sg = _sigmoid(hf)
    f = lb + (1.0 - lb) * sg
    return sg, f, jnp.log(f), 1.0 - f


def _hg_specs(s, slab, order):
    n = s // slab
    col = lambda b: pl.BlockSpec((slab, HEAD_TILE), lambda h, i: (order(i, n), b + h))
    return n, col


def _pad_front(ref, val):
    ref[0:CHUNK, :] = jnp.zeros((CHUNK, HEAD_TILE), F32)
    ref[CHUNK:, :] = val


def _shifted_down(ref, d8, r, rows):
    if r == 0:
        return ref[pl.ds(pl.multiple_of(CHUNK - d8, 8), rows), :]
    win = ref[pl.ds(pl.multiple_of(CHUNK - 8 - d8, 8), rows + 8), :]
    return pltpu.roll(win, r, 0)[8:, :]


def _pulled_up(ref, val, d8, r, rows):
    if r == 0:
        ref[0:rows, :] = val
    else:
        ref[0:rows, :] = pltpu.roll(val, rows - r, 0)
    return ref[pl.ds(pl.multiple_of(d8, 8), rows), :]


def hgrn_fwd(u, lb_logits, hg_norm):
    s = u.shape[0]
    slab = min(HG_SLAB, s)
    cps = slab // CHUNK
    n, col = _hg_specs(s, slab, lambda i, n_: i)
    heads = HG_DIM // HEAD_TILE

    def body(q_ref, f_ref, i_ref, g_ref, lb_ref, nw_ref, y_ref, o_ref, st_ref, at_ref, bp_ref, kp_ref, vp_ref):
        @pl.when(pl.program_id(1) == 0)
        def _():
            at_ref[...] = jnp.zeros_like(at_ref)

        pos = lax.broadcasted_iota(jnp.int32, (slab, HEAD_TILE), 0) & (CHUNK - 1)
        lb, _ = _lower_bound(lb_ref)
        q = q_ref[...]
        v = i_ref[...]
        _, _, g, kk = _hg_gates(f_ref[...], lb)
        b = _chunk_cumsum(g, pos)
        for ref, val in ((bp_ref, b), (kp_ref, kk), (vp_ref, v)):
            _pad_front(ref, val)

        def diag(d8, r, o):
            d = d8 + r
            lam = jnp.exp(jnp.where(pos >= d, b - _shifted_down(bp_ref, d8, r, slab), -jnp.inf))
            sc = jnp.sum(q * _shifted_down(kp_ref, d8, r, slab) * lam, axis=-1, keepdims=True)
            return o + sc * _shifted_down(vp_ref, d8, r, slab)

        o = jnp.zeros((slab, HEAD_TILE), F32)
        for r in range(8):
            o = lax.fori_loop(0, CHUNK // 8, lambda a, acc: diag(a * 8, r, acc), o)
        o_ref[...] = o

        qhat = q * jnp.exp(b)
        for c in range(cps):
            rows = slice(c * CHUNK, (c + 1) * CHUNK)
            at = at_ref[...]
            st_ref[c] = at
            o_ref[rows, :] += _dot(qhat[rows], at, NT, HIGHEST)
            bc = b[(c + 1) * CHUNK - 1:(c + 1) * CHUNK]
            khat = kk[rows] * jnp.exp(bc - b[rows])
            at_ref[...] = at * jnp.exp(bc) + _dot(v[rows], khat, TN, HIGHEST)

        o = o_ref[...]
        r = lax.rsqrt(jnp.mean(o * o, axis=-1, keepdims=True) + EPS)
        hg = g_ref[...]
        y_ref[...] = (o * r * nw_ref[...] * (hg * _sigmoid(hg))).astype(BF16)

    out = pl.BlockSpec((slab, HEAD_TILE), lambda h, i: (i, h))
    par = lambda rows: pl.BlockSpec((rows, HEAD_TILE), lambda h, i: (0, h))
    return pl.pallas_call(
        body, grid=(heads, n), in_specs=[col(12), col(16), col(20), col(24), par(3), par(1)],
        out_specs=[out, out, pl.BlockSpec((None, cps, HEAD_TILE, HEAD_TILE), lambda h, i: (h, i, 0, 0))],
        out_shape=[jax.ShapeDtypeStruct((s, HG_DIM), BF16), jax.ShapeDtypeStruct((s, HG_DIM), F32),
                   jax.ShapeDtypeStruct((heads, s // CHUNK, HEAD_TILE, HEAD_TILE), F32)],
        scratch_shapes=[pltpu.VMEM((HEAD_TILE, HEAD_TILE), F32)] + [pltpu.VMEM((slab + CHUNK, HEAD_TILE), F32)] * 3,
        name="hgrn_fwd",
        compiler_params=_cp("parallel", "arbitrary"))(u, u, u, u, lb_logits, hg_norm)


def hgrn_bwd(dmix, u, o_raw, states, lb_logits, hg_norm):
    s = u.shape[0]
    slab = min(HG_SLAB, s)
    cps = slab // CHUNK
    n, col = _hg_specs(s, slab, lambda i, n_: n_ - 1 - i)
    heads = HG_DIM // HEAD_TILE

    def body(dy_ref, q_ref, f_ref, i_ref, g_ref, o_ref, st_ref, lb_ref, nw_ref,
             dq_ref, df_ref, di_ref, dg_ref, dnw_ref, dlb_ref, dat_ref, dlbacc_ref, dqs_ref, dks_ref, dvs_ref, dbc_ref,
             bp_ref, kp_ref, vp_ref, up1_ref, up2_ref):
        step = pl.program_id(1)

        @pl.when(step == 0)
        def _():
            dat_ref[...] = jnp.zeros_like(dat_ref)
            dlbacc_ref[...] = jnp.zeros_like(dlbacc_ref)
            dnw_ref[...] = jnp.zeros_like(dnw_ref)

        pos = lax.broadcasted_iota(jnp.int32, (slab, HEAD_TILE), 0) & (CHUNK - 1)
        lb, probs = _lower_bound(lb_ref)
        q = q_ref[...]
        v = i_ref[...]
        sg_f, f, g, kk = _hg_gates(f_ref[...], lb)
        b = _chunk_cumsum(g, pos)

        o = o_ref[...]
        nw = nw_ref[...]
        r = lax.rsqrt(jnp.mean(o * o, axis=-1, keepdims=True) + EPS)
        hg = g_ref[...]
        sg = _sigmoid(hg)
        dy = dy_ref[...]
        d_on = dy * (hg * sg)
        dg_ref[...] = (dy * (o * r * nw) * (sg * (1.0 + hg * (1.0 - sg)))).astype(BF16)
        dnw_ref[...] += jnp.sum(d_on * o * r, axis=0, keepdims=True)
        t1 = d_on * nw
        do = r * t1 - o * (r * r * r) * jnp.mean(t1 * o, axis=-1, keepdims=True)

        eb = jnp.exp(b)
        qhat = q * eb
        dbc_ref[...] = jnp.zeros_like(dbc_ref)
        for c in reversed(range(cps)):
            rows = slice(c * CHUNK, (c + 1) * CHUNK)
            last = (c + 1) * CHUNK - 1
            at = st_ref[c]
            dat = dat_ref[...]
            bc = b[last:last + 1]
            ebc = jnp.exp(bc)
            dec = jnp.exp(bc - b[rows])
            khat = kk[rows] * dec
            at_next = at * ebc + _dot(v[rows], khat, TN, HIGHEST)
            dbc_ref[last:last + 1, :] = jnp.sum(dat * at_next, axis=0, keepdims=True)
            dqs_ref[rows, :] = eb[rows] * _dot(do[rows], at, NN, HIGHEST)
            dks_ref[rows, :] = dec * _dot(v[rows], dat, NN, HIGHEST)
            dvs_ref[rows, :] = _dot(khat, dat, NT, HIGHEST)
            dat_ref[...] = dat * ebc + _dot(do[rows], qhat[rows], TN, HIGHEST)

        for ref, val in ((bp_ref, b), (kp_ref, kk), (vp_ref, v)):
            _pad_front(ref, val)
        for ref in (up1_ref, up2_ref):
            ref[slab:, :] = jnp.zeros((CHUNK, HEAD_TILE), F32)

        def diag(d8, r, carry):
            dq, dk, dv = carry
            d = d8 + r
            lam = jnp.exp(jnp.where(pos >= d, b - _shifted_down(bp_ref, d8, r, slab), -jnp.inf))
            kd = _shifted_down(kp_ref, d8, r, slab)
            vd = _shifted_down(vp_ref, d8, r, slab)
            sc = jnp.sum(q * kd * lam, axis=-1, keepdims=True)
            pd = jnp.sum(do * vd, axis=-1, keepdims=True)
            dq = dq + pd * kd * lam
            dk = dk + _pulled_up(up1_ref, pd * q * lam, d8, r, slab)
            dv = dv + _pulled_up(up2_ref, sc * do, d8, r, slab)
            return dq, dk, dv

        carry = (dqs_ref[...], dks_ref[...], dvs_ref[...])
        for r in range(8):
            carry = lax.fori_loop(0, CHUNK // 8, lambda a, cr: diag(a * 8, r, cr), carry)
        dq, dk, dv = carry

        db = q * dq - kk * dk + dbc_ref[...]
        dgl = _chunk_rev_cumsum(db, pos)
        dfv = dgl / f - dk
        dq_ref[...] = dq.astype(BF16)
        di_ref[...] = dv.astype(BF16)
        df_ref[...] = (dfv * (1.0 - lb) * sg_f * (1.0 - sg_f)).astype(BF16)
        dlbacc_ref[...] += jnp.sum(dfv * (1.0 - sg_f), axis=0, keepdims=True)

        @pl.when(step == n - 1)
        def _():
            dlb = dlbacc_ref[...]
            sel = (lax.broadcasted_iota(jnp.int32, (3, HEAD_TILE), 0) == 0).astype(F32)
            dlb_ref[...] = dlb * probs[0:1] * (sel - probs)

    out = pl.BlockSpec((slab, HEAD_TILE), lambda h, i: (n - 1 - i, h))
    par = lambda rows: pl.BlockSpec((rows, HEAD_TILE), lambda h, i: (0, h))
    dyspec = pl.BlockSpec((slab, HEAD_TILE), lambda h, i: (n - 1 - i, CONV_DIM // HEAD_TILE + h))
    shape = jax.ShapeDtypeStruct((s, HG_DIM), BF16)
    slab_f32 = pltpu.VMEM((slab, HEAD_TILE), F32)
    return pl.pallas_call(
        body, grid=(heads, n),
        in_specs=[dyspec, col(12), col(16), col(20), col(24), out,
                  pl.BlockSpec((None, cps, HEAD_TILE, HEAD_TILE), lambda h, i: (h, n - 1 - i, 0, 0)), par(3), par(1)],
        out_specs=[out, out, out, out, par(1), par(3)],
        out_shape=[shape, shape, shape, shape, jax.ShapeDtypeStruct((1, HG_DIM), F32),
                   jax.ShapeDtypeStruct((3, HG_DIM), F32)],
        scratch_shapes=[pltpu.VMEM((HEAD_TILE, HEAD_TILE), F32), pltpu.VMEM((1, HEAD_TILE), F32),
                        slab_f32, slab_f32, slab_f32, slab_f32] + [pltpu.VMEM((slab + CHUNK, HEAD_TILE), F32)] * 5,
        name="hgrn_bwd", compiler_params=_cp("parallel", "arbitrary"))(
            dmix, u, u, u, u, o_raw, states, lb_logits, hg_norm)


def _pair_rstd(x, lo):
    x2 = x * x
    s_lo = jnp.sum(jnp.where(lo, x2, 0.0), axis=-1, keepdims=True)
    s_hi = jnp.sum(jnp.where(lo, 0.0, x2), axis=-1, keepdims=True)
    inv = 1.0 / SB_HEAD_DIM
    return jnp.where(lo, lax.rsqrt(s_lo * inv + EPS), lax.rsqrt(s_hi * inv + EPS))


def _pair_mean(x, lo):
    s_lo = jnp.sum(jnp.where(lo, x, 0.0), axis=-1, keepdims=True)
    s_hi = jnp.sum(jnp.where(lo, 0.0, x), axis=-1, keepdims=True)
    return jnp.where(lo, s_lo, s_hi) * (1.0 / SB_HEAD_DIM)


def qk_norm_fwd(qkv, qg, kg):
    s = qkv.shape[0]
    tm = _tok_tile(s)
    nt = D_MODEL // HEAD_TILE

    def body(q_ref, k_ref, v_ref, qg_ref, kg_ref, qn_ref, kn_ref, vb_ref):
        lo = lax.broadcasted_iota(jnp.int32, (tm, HEAD_TILE), 1) < SB_HEAD_DIM
        qv = q_ref[...]
        kv = k_ref[...]
        qn_ref[...] = (qv * _pair_rstd(qv, lo) * qg_ref[...]).astype(BF16)
        kn_ref[...] = (kv * _pair_rstd(kv, lo) * kg_ref[...]).astype(BF16)
        vb_ref[...] = v_ref[...].astype(BF16)

    col = lambda b: pl.BlockSpec((tm, HEAD_TILE), lambda i, j: (i, b * nt + j))
    gain = pl.BlockSpec((1, HEAD_TILE), lambda i, j: (0, 0))
    out = pl.BlockSpec((tm, HEAD_TILE), lambda i, j: (i, j))
    shape = jax.ShapeDtypeStruct((s, D_MODEL), BF16)
    return pl.pallas_call(
        body, grid=(s // tm, nt), in_specs=[col(0), col(1), col(2), gain, gain], out_specs=[out, out, out],
        out_shape=[shape, shape, shape], name="qk_norm_fwd", compiler_params=_cp("parallel", "parallel"))(
            qkv, qkv, qkv, qg, kg)


def qk_norm_bwd(dqn, dkn, dv, qkv, qg, kg):
    s = qkv.shape[0]
    tm = _tok_tile(s)
    nt = D_MODEL // HEAD_TILE

    def body(dqn_ref, dkn_ref, dv_ref, q_ref, k_ref, qg_ref, kg_ref, dq_ref, dk_ref, dvb_ref, dqg_ref, dkg_ref):
        @pl.when((pl.program_id(0) == 0) & (pl.program_id(1) == 0))
        def _():
            dqg_ref[...] = jnp.zeros_like(dqg_ref)
            dkg_ref[...] = jnp.zeros_like(dkg_ref)

        lo = lax.broadcasted_iota(jnp.int32, (tm, HEAD_TILE), 1) < SB_HEAD_DIM

        def one(x_ref, g_ref, dn_ref, dx_ref, dgain_ref):
            xv = x_ref[...]
            r = _pair_rstd(xv, lo)
            xn = xv * r
            dn = dn_ref[...]
            dgain_ref[...] += jnp.sum(dn * xn, axis=0, keepdims=True)
            t1 = dn * g_ref[...]
            dx_ref[...] = (r * (t1 - xn * _pair_mean(t1 * xn, lo))).astype(BF16)

        one(q_ref, qg_ref, dqn_ref, dq_ref, dqg_ref)
        one(k_ref, kg_ref, dkn_ref, dk_ref, dkg_ref)
        dvb_ref[...] = dv_ref[...].astype(BF16)

    col = lambda b: pl.BlockSpec((tm, HEAD_TILE), lambda i, j: (i, b * nt + j))
    gain = pl.BlockSpec((1, HEAD_TILE), lambda i, j: (0, 0))
    out = pl.BlockSpec((tm, HEAD_TILE), lambda i, j: (i, j))
    shape = jax.ShapeDtypeStruct((s, D_MODEL), BF16)
    grow = jax.ShapeDtypeStruct((1, HEAD_TILE), F32)
    return pl.pallas_call(
        body, grid=(s // tm, nt), in_specs=[out, out, out, col(0), col(1), gain, gain],
        out_specs=[out, out, out, gain, gain], out_shape=[shape, shape, shape, grow, grow], name="qk_norm_bwd",
        compiler_params=_cp("arbitrary", "arbitrary"))(dqn, dkn, dv, qkv, qkv, qg, kg)


def _split_dot(x, u):
    hi = x.astype(BF16)
    lo = (x - hi.astype(F32)).astype(BF16)
    return _dot(hi, u, NN) + _dot(lo, u, NN)


def _sb_tile(qs, kb, carry, causal, suffix, diag):
    z = _dot(qs, kb, NT)
    lb = jnp.minimum(z, 0.0) - jnp.log(1.0 + jnp.exp(-jnp.abs(z)))
    lom = lb - z
    if diag:
        lom = jnp.where(causal, lom, 0.0)
    ws = []
    for j in reversed(range(ATT_TILE // ATT_BLOCK)):
        cols = slice(j * ATT_BLOCK, (j + 1) * ATT_BLOCK)
        ws.append(jnp.exp(z[:, cols] + _split_dot(lom[:, cols], suffix) + carry))
        carry = carry + jnp.sum(lom[:, cols], axis=-1, keepdims=True)
    w = jnp.concatenate(ws[::-1], axis=1)
    if diag:
        w = jnp.where(causal, w, 0.0)
    return lb, w, carry


def _sb_consts():
    row = lax.broadcasted_iota(jnp.int32, (ATT_BLOCK, ATT_BLOCK), 0)
    col = lax.broadcasted_iota(jnp.int32, (ATT_BLOCK, ATT_BLOCK), 1)
    suffix = (row >= col).astype(BF16)
    trow = lax.broadcasted_iota(jnp.int32, (ATT_TILE, ATT_TILE), 0)
    tcol = lax.broadcasted_iota(jnp.int32, (ATT_TILE, ATT_TILE), 1)
    causal = tcol < trow
    lo = lax.broadcasted_iota(jnp.int32, (ATT_TILE, HEAD_TILE), 1) < SB_HEAD_DIM
    return suffix, causal, lo


def _rows(i):
    return pl.ds(pl.multiple_of(i * ATT_TILE, ATT_TILE), ATT_TILE)


def _head_query(qb, mask):
    return (jnp.where(mask, qb, 0.0) * (SB_HEAD_DIM ** -0.5)).astype(BF16)


def sb_attn_fwd(qn, kn, vb):
    s = qn.shape[0]
    nq = s // ATT_TILE
    nt = D_MODEL // HEAD_TILE

    def body(q_ref, k_ref, v_ref, o_ref, ob_ref):
        suffix, causal, lo = _sb_consts()

        def qtile(qi, _):
            qb = q_ref[_rows(qi), :].astype(F32)
            qs = [_head_query(qb, lo), _head_query(qb, ~lo)]

            def step(kj, state, diag):
                kb = k_ref[_rows(kj), :]
                vt = v_ref[_rows(kj), :]
                out = []
                for hh in range(2):
                    carry, acc = state[hh]
                    _, w, carry = _sb_tile(qs[hh], kb, carry, causal, suffix, diag)
                    out.append((carry, acc + _dot(w.astype(BF16), vt, NN)))
                return tuple(out)

            start = (jnp.zeros((ATT_TILE, 1), F32), jnp.zeros((ATT_TILE, HEAD_TILE), F32))
            state = step(qi, (start, start), True)
            state = lax.fori_loop(0, qi, lambda jj, st: step(qi - 1 - jj, st, False), state)
            o = jnp.where(lo, state[0][1], state[1][1])
            o_ref[_rows(qi), :] = o
            ob_ref[_rows(qi), :] = o.astype(BF16)
            return 0

        lax.fori_loop(0, nq, qtile, 0)

    spec = pl.BlockSpec((s, HEAD_TILE), lambda p: (0, p))
    return pl.pallas_call(
        body, grid=(nt,), in_specs=[spec, spec, spec], out_specs=[spec, spec],
        out_shape=[jax.ShapeDtypeStruct((s, D_MODEL), F32), jax.ShapeDtypeStruct((s, D_MODEL), BF16)],
        name="sb_attn_fwd", compiler_params=_cp("parallel"))(qn, kn, vb)


def sb_attn_bwd(qn, kn, vb, o, do):
    s = qn.shape[0]
    nq = s // ATT_TILE
    nt = D_MODEL // HEAD_TILE

    def body(q_ref, k_ref, v_ref, o_ref, do_ref, dq_ref, dk_ref, dv_ref):
        suffix, causal, lo = _sb_consts()
        dk_ref[...] = jnp.zeros_like(dk_ref)
        dv_ref[...] = jnp.zeros_like(dv_ref)

        def qtile(qi, _):
            qb = q_ref[_rows(qi), :].astype(F32)
            dob = do_ref[_rows(qi), :].astype(BF16).astype(F32)
            prod = dob * o_ref[_rows(qi), :]
            masks = [lo, ~lo]
            qs = [_head_query(qb, m) for m in masks]
            dos = [jnp.where(m, dob, 0.0).astype(BF16) for m in masks]
            totals = [jnp.sum(jnp.where(m, prod, 0.0), axis=-1, keepdims=True) for m in masks]

            def step(kj, state, diag):
                kb = k_ref[_rows(kj), :]
                vt = v_ref[_rows(kj), :]
                out = []
                dk = jnp.zeros((ATT_TILE, HEAD_TILE), F32)
                dv = jnp.zeros((ATT_TILE, HEAD_TILE), F32)
                for hh in range(2):
                    carry, carry_e, dq = state[hh]
                    lb, w, carry = _sb_tile(qs[hh], kb, carry, causal, suffix, diag)
                    wb = w.astype(BF16)
                    e = _dot(dos[hh], vt, NT) * wb.astype(F32)
                    befores = []
                    for j in reversed(range(ATT_TILE // ATT_BLOCK)):
                        cols = slice(j * ATT_BLOCK, (j + 1) * ATT_BLOCK)
                        befores.append(totals[hh] - carry_e - _split_dot(e[:, cols], suffix))
                        carry_e = carry_e + jnp.sum(e[:, cols], axis=-1, keepdims=True)
                    before = jnp.concatenate(befores[::-1], axis=1)
                    beta = jnp.exp(lb)
                    dz = e * (1.0 - beta) - before * beta
                    if diag:
                        dz = jnp.where(causal, dz, 0.0)
                    dzb = dz.astype(BF16)
                    dq = dq + _dot(dzb, kb, NN)
                    dk = dk + _dot(dzb, qs[hh], TN)
                    dv = dv + _dot(wb, dos[hh], TN)
                    out.append((carry, carry_e, dq))
                dk_ref[_rows(kj), :] += dk
                dv_ref[_rows(kj), :] += dv
                return tuple(out)

            zero = jnp.zeros((ATT_TILE, 1), F32)
            start = (zero, zero, jnp.zeros((ATT_TILE, HEAD_TILE), F32))
            state = step(qi, (start, start), True)
            state = lax.fori_loop(0, qi, lambda jj, st: step(qi - 1 - jj, st, False), state)
            dq_ref[_rows(qi), :] = jnp.where(lo, state[0][2], state[1][2]) * (SB_HEAD_DIM ** -0.5)
            return 0

        lax.fori_loop(0, nq, qtile, 0)

    spec = pl.BlockSpec((s, HEAD_TILE), lambda p: (0, p))
    shape = jax.ShapeDtypeStruct((s, D_MODEL), F32)
    return pl.pallas_call(
        body, grid=(nt,), in_specs=[spec] * 5, out_specs=[spec] * 3, out_shape=[shape] * 3,
        name="sb_attn_bwd", compiler_params=_cp("parallel"))(qn, kn, vb, o, do)


def ada_fwd(c_all, ada_w, ada_b_shard):
    layers, d, cols = ada_w.shape
    tn = 512

    def body(c_ref, w_ref, b_ref, out_ref):
        cv = c_ref[...]
        act = (cv * _sigmoid(cv)).astype(BF16)
        out_ref[...] = _dot(act, w_ref[...].astype(BF16), NN) + b_ref[...]

    return pl.pallas_call(
        body, grid=(layers, cols // tn),
        in_specs=[pl.BlockSpec((N_DEV, d), lambda l, j: (0, 0)), pl.BlockSpec((None, d, tn), lambda l, j: (l, 0, j)),
                  pl.BlockSpec((None, 1, tn), lambda l, j: (l, 0, j))],
        out_specs=pl.BlockSpec((None, N_DEV, tn), lambda l, j: (l, 0, j)),
        out_shape=jax.ShapeDtypeStruct((layers, N_DEV, cols), F32), name="ada_fwd",
        compiler_params=_cp("parallel", "parallel"))(c_all, ada_w, ada_b_shard)


def ada_w_grad(c_all, dmod):
    layers, _, cols = dmod.shape
    d = c_all.shape[1]
    tn = 512

    def body(c_ref, dm_ref, out_ref):
        cv = c_ref[...]
        out_ref[...] = _dot(cv * _sigmoid(cv), dm_ref[...], TN, HIGHEST)

    return pl.pallas_call(
        body, grid=(layers, cols // tn),
        in_specs=[pl.BlockSpec((N_DEV, d), lambda l, j: (0, 0)), pl.BlockSpec((None, N_DEV, tn), lambda l, j: (l, 0, j))],
        out_specs=pl.BlockSpec((None, d, tn), lambda l, j: (l, 0, j)),
        out_shape=jax.ShapeDtypeStruct((layers, d, cols), F32), name="ada_w_grad",
        compiler_params=_cp("parallel", "parallel"))(c_all, dmod)


def _row_tile(r, c, elems):
    best = 8
    for t in range(8, r + 1, 8):
        if r % t == 0 and t * c <= elems:
            best = t
    return best


def sum_rows(x, name):
    n, r, c = x.shape
    tr = _row_tile(r, c, 1 << 17)

    def body(x_ref, out_ref):
        acc = x_ref[0]
        for i in range(1, n):
            acc = acc + x_ref[i]
        out_ref[...] = acc

    return pl.pallas_call(
        body, grid=(r // tr,), in_specs=[pl.BlockSpec((n, tr, c), lambda i: (0, i, 0))],
        out_specs=pl.BlockSpec((tr, c), lambda i: (i, 0)), out_shape=jax.ShapeDtypeStruct((r, c), F32), name=name,
        compiler_params=_cp("parallel"))(x)


def adamw(w, g, m, v, name):
    r, c = w.shape
    tr = _row_tile(r, c, 1 << 17)
    c1 = 1.0 - ADAM_B1 ** ADAM_STEP
    c2 = 1.0 - ADAM_B2 ** ADAM_STEP

    def body(w_ref, g_ref, m_ref, v_ref, d_ref, nm_ref, nv_ref):
        gv = g_ref[...]
        nm = ADAM_B1 * m_ref[...] + (1.0 - ADAM_B1) * gv
        nv = ADAM_B2 * v_ref[...] + (1.0 - ADAM_B2) * (gv * gv)
        d_ref[...] = -ADAM_LR * ((nm / c1) / (jnp.sqrt(nv / c2) + ADAM_EPS) + ADAM_WD * w_ref[...])
        nm_ref[...] = nm
        nv_ref[...] = nv

    spec = pl.BlockSpec((tr, c), lambda i: (i, 0))
    shape = jax.ShapeDtypeStruct((r, c), F32)
    return pl.pallas_call(
        body, grid=(r // tr,), in_specs=[spec] * 4, out_specs=[spec] * 3, out_shape=[shape] * 3, name=name,
        compiler_params=_cp("parallel"))(w, g, m, v)


def _me():
    return lax.axis_index("x"), lax.axis_index("y"), lax.axis_index("c")


def _flip(v, bit):
    return 1 - v if bit else v


HBM = pl.BlockSpec(memory_space=pl.ANY)
VMEM = pl.BlockSpec(memory_space=pltpu.VMEM)


def all_gather_rows(v, name):
    n = v.shape[1]

    def body(v_ref, out_ref, send_sems, recv_sems):
        x, y, c = _me()
        me = 4 * x + 2 * y + c
        out_ref[pl.ds(me, 1), :] = v_ref[...]
        copies = []
        for j in range(1, N_DEV):
            peer = (_flip(x, j & 4), _flip(y, j & 2), _flip(c, j & 1))
            copies.append(pltpu.make_async_remote_copy(
                src_ref=v_ref, dst_ref=out_ref.at[pl.ds(me, 1), :], send_sem=send_sems.at[j - 1],
                recv_sem=recv_sems.at[j - 1], device_id=peer, device_id_type=MESH))
        for cp in copies:
            cp.start()
        for cp in copies:
            cp.wait()

    return pl.pallas_call(
        body, in_specs=[VMEM], out_specs=VMEM, out_shape=jax.ShapeDtypeStruct((N_DEV, n), F32),
        scratch_shapes=[pltpu.SemaphoreType.DMA((N_DEV - 1,)), pltpu.SemaphoreType.DMA((N_DEV - 1,))], name=name)(v)


def _chip_peers(x, y):
    return [((1 - x, y), 2 * (1 - x) + y), ((x, 1 - y), 2 * x + (1 - y)), ((1 - x, 1 - y), 2 * (1 - x) + (1 - y))]


def chip_all_gather(shards, name):
    n = len(shards)

    def body(*refs):
        ins, outs = refs[:n], refs[n:2 * n]
        local_sem, send_sems, recv_sems = refs[2 * n:]
        x, y, c = _me()
        mine = 2 * x + y
        started = []
        for a in range(n):
            cp = pltpu.make_async_copy(ins[a], outs[a].at[mine], local_sem.at[a])
            cp.start()
            started.append(cp)
            for p, (chip, _) in enumerate(_chip_peers(x, y)):
                cp = pltpu.make_async_remote_copy(
                    src_ref=ins[a], dst_ref=outs[a].at[mine], send_sem=send_sems.at[a, p], recv_sem=recv_sems.at[a, p],
                    device_id=(*chip, c), device_id_type=MESH)
                cp.start()
                started.append(cp)
        for cp in started:
            cp.wait()

    return pl.pallas_call(
        body, in_specs=[HBM] * n, out_specs=[HBM] * n,
        out_shape=[jax.ShapeDtypeStruct((N_CHIPS, *s.shape), s.dtype) for s in shards],
        scratch_shapes=[pltpu.SemaphoreType.DMA((n,)), pltpu.SemaphoreType.DMA((n, 3)), pltpu.SemaphoreType.DMA((n, 3))],
        name=name)(*shards)


def sibling_split(grads, name):
    n = len(grads)

    def body(*refs):
        ins, got = refs[:n], refs[n:2 * n]
        send_sems, recv_sems = refs[2 * n:]
        x, y, c = _me()
        started = []
        for a in range(n):
            half = ins[a].shape[1] // 2
            give = pltpu.make_async_remote_copy(
                src_ref=ins[a].at[:, pl.ds((1 - c) * half, half), :], dst_ref=got[a], send_sem=send_sems.at[a],
                recv_sem=recv_sems.at[a], device_id=(x, y, 1 - c), device_id_type=MESH)
            give.start()
            started.append(give)
        for cp in started:
            cp.wait()

    return pl.pallas_call(
        body, in_specs=[HBM] * n, out_specs=[HBM] * n,
        out_shape=[jax.ShapeDtypeStruct((g.shape[0], g.shape[1] // 2, g.shape[2]), g.dtype) for g in grads],
        scratch_shapes=[pltpu.SemaphoreType.DMA((n,)), pltpu.SemaphoreType.DMA((n,))], name=name)(*grads)


def pair_sum(g, got, name):
    k, half, c = got.shape
    tr = _row_tile(half, c, 1 << 18)
    nb = half // tr

    def body(g_ref, got_ref, out_ref):
        out_ref[...] = g_ref[...] + got_ref[...]

    spec = pl.BlockSpec((None, tr, c), lambda j, i: (j, i, 0))
    return pl.pallas_call(
        body, grid=(k, nb),
        in_specs=[pl.BlockSpec((None, tr, c), lambda j, i: (j, lax.axis_index("c") * nb + i, 0)), spec],
        out_specs=spec, out_shape=jax.ShapeDtypeStruct(got.shape, F32), name=name,
        compiler_params=_cp("parallel", "parallel"))(g, got)


def chip_scatter(parts, name):
    n = len(parts)

    def body(*refs):
        ins, outs = refs[:n], refs[n:2 * n]
        send_sems, recv_sems = refs[2 * n:]
        x, y, c = _me()
        started = []
        for a in range(n):
            for p, (chip, slot) in enumerate(_chip_peers(x, y)):
                cp = pltpu.make_async_remote_copy(
                    src_ref=ins[a].at[slot], dst_ref=outs[a].at[p], send_sem=send_sems.at[a, p],
                    recv_sem=recv_sems.at[a, p], device_id=(*chip, c), device_id_type=MESH)
                cp.start()
                started.append(cp)
        for cp in started:
            cp.wait()

    return pl.pallas_call(
        body, in_specs=[HBM] * n, out_specs=[HBM] * n,
        out_shape=[jax.ShapeDtypeStruct((3, *p.shape[1:]), p.dtype) for p in parts],
        scratch_shapes=[pltpu.SemaphoreType.DMA((n, 3)), pltpu.SemaphoreType.DMA((n, 3))], name=name)(*parts)


def chip_sum(part, landed, name):
    _, half, c = landed.shape
    tr = _row_tile(half, c, 1 << 17)
    nb = half // tr

    def body(part_ref, landed_ref, out_ref):
        out_ref[...] = ((part_ref[...] + landed_ref[0]) + landed_ref[1]) + landed_ref[2]

    return pl.pallas_call(
        body, grid=(nb,),
        in_specs=[pl.BlockSpec((None, tr, c), lambda i: (2 * lax.axis_index("x") + lax.axis_index("y"), i, 0)),
                  pl.BlockSpec((3, tr, c), lambda i: (0, i, 0))],
        out_specs=pl.BlockSpec((tr, c), lambda i: (lax.axis_index("c") * nb + i, 0)),
        out_shape=jax.ShapeDtypeStruct((2 * half, c), F32), name=name, compiler_params=_cp("parallel"))(part, landed)


def sibling_join(arrays, name):
    n = len(arrays)

    def body(*refs):
        ins, outs = refs[:n], refs[n:2 * n]
        send_sems, recv_sems = refs[2 * n:]
        x, y, c = _me()
        started = []
        for a in range(n):
            half = ins[a].shape[0] // 2
            give = pltpu.make_async_remote_copy(
                src_ref=ins[a].at[pl.ds(c * half, half), :], dst_ref=outs[a].at[pl.ds(c * half, half), :],
                send_sem=send_sems.at[a], recv_sem=recv_sems.at[a], device_id=(x, y, 1 - c), device_id_type=MESH)
            give.start()
            started.append(give)
        for cp in started:
            cp.wait()

    return pl.pallas_call(
        body, in_specs=[HBM] * n, out_specs=[HBM] * n,
        out_shape=[jax.ShapeDtypeStruct(h.shape, h.dtype) for h in arrays],
        input_output_aliases={a: a for a in range(n)},
        scratch_shapes=[pltpu.SemaphoreType.DMA((n,)), pltpu.SemaphoreType.DMA((n,))], name=name)(*arrays)


def reduce_to_owner(grads):
    got = sibling_split(grads, "grad_sibling_split")
    parts = [pair_sum(g, h, f"grad_pair_sum_{a}") for a, (g, h) in enumerate(zip(grads, got))]
    landed = chip_scatter(parts, "grad_chip_scatter")
    halves = [chip_sum(p, l, f"grad_chip_sum_{a}") for a, (p, l) in enumerate(zip(parts, landed))]
    return sibling_join(halves, "grad_sibling_join")


def _pad_row(v, n):
    return jnp.pad(v.reshape(1, -1), ((0, 0), (0, n - v.size)))


def local_step(x, target, mod, wts):
    d = D_MODEL
    row = lambda v: v.reshape(1, -1)
    mods = [[row(mod[l, i * d:(i + 1) * d]) for i in range(6)] for l in range(2)]
    saved = []
    for l in range(2):
        shift1, scale1, gate1, shift2, scale2, gate2 = mods[l]
        g_mix, g_mlp = row(wts["norm_mix"][l]), row(wts["norm_mlp"][l])
        h = norm_mod_fwd(x, g_mix, scale1, shift1, f"norm_mix_fwd_{l}")
        if l == 0:
            u = matmul_nn_chunked(h, wts["w_in"], F32, "in_proj_ab")[0]
            y_a = conv_mixer_fwd(u, wts["conv_w"])
            y_b, o_raw, states = hgrn_fwd(u, wts["lb_logits"], wts["hg_norm"])
            mix = jnp.concatenate([y_a, y_b], axis=1)
            y, x1 = proj_residual(mix, wts["w_out_ab"], x, gate1, "out_proj_ab")
            ctx = (u, o_raw, states)
        else:
            qkv = matmul_nn_chunked(h, wts["w_qkv"], F32, "in_proj_c")[0]
            qn, kn, vb = qk_norm_fwd(qkv, wts["qg"], wts["kg"])
            o, mix = sb_attn_fwd(qn, kn, vb)
            y, x1 = proj_residual(mix, wts["w_out_c"], x, gate1, "out_proj_c")
            ctx = (qkv, qn, kn, vb, o)
        h2 = norm_mod_fwd(x1, g_mlp, scale2, shift2, f"norm_mlp_fwd_{l}")
        act, r = mlp_up(h2, wts["w1"][l], f"mlp_up_{l}")
        y2, x2 = proj_residual(act, wts["w2"][l], x1, gate2, f"mlp_down_{l}")
        saved.append((x, h, mix, y, x1, h2, act, r, y2, ctx))
        x = x2

    dx, loss_row = loss_and_grad(x, target)
    small, big = {}, {}
    dmod = [None, None]
    d_norm_mix, d_norm_mlp = [None, None], [None, None]
    for l in (1, 0):
        shift1, scale1, gate1, shift2, scale2, gate2 = mods[l]
        g_mix, g_mlp = row(wts["norm_mix"][l]), row(wts["norm_mlp"][l])
        x0, h, mix, y, x1, h2, act, r, y2, ctx = saved[l]
        dy2, dgate2 = gate_bwd(dx, y2, gate2, f"mlp_gate_bwd_{l}")
        dz = mlp_down_bwd(dy2, wts["w2"][l], r, f"mlp_down_bwd_{l}")
        big[f"w2_{l}"] = matmul_tn_plain(act, dy2, f"mlp_w2_grad_{l}")
        dh2 = matmul_nt_chunked(dz, wts["w1"][l], f"mlp_up_bwd_{l}")
        big[f"w1_{l}"] = matmul_tn_chunked(h2, dz, f"mlp_w1_grad_{l}")
        dx1, d_norm_mlp[l], dscale2, dshift2 = norm_mod_bwd(dh2, x1, g_mlp, scale2, dx, f"norm_mlp_bwd_{l}")
        dy, dgate1 = gate_bwd(dx1, y, gate1, f"mix_gate_bwd_{l}")
        if l == 0:
            u, o_raw, states = ctx
            dmix = matmul_nt_plain(dy, wts["w_out_ab"], "out_proj_ab_bwd")
            big["w_out_ab"] = matmul_tn_plain(mix, dy, "w_out_ab_grad")
            dab, dac, dah, small["conv_w"] = conv_mixer_bwd(dmix, u, wts["conv_w"])
            dhq, dhf, dhi, dhg, small["hg_norm"], small["lb_logits"] = hgrn_bwd(
                dmix, u, o_raw, states, wts["lb_logits"], wts["hg_norm"])
            du = jnp.concatenate([dab, dac, dah, dhq, dhf, dhi, dhg], axis=1)
            dh = matmul_nt_chunked(du, wts["w_in"], "in_proj_ab_bwd")
            big["w_in"] = matmul_tn_chunked(h, du, "w_in_grad")
        else:
            qkv, qn, kn, vb, o = ctx
            do = matmul_nt_plain(dy, wts["w_out_c"], "out_proj_c_bwd")
            big["w_out_c"] = matmul_tn_plain(mix, dy, "w_out_c_grad")
            dqn, dkn, dv = sb_attn_bwd(qn, kn, vb, o, do)
            dq, dk, dvb, dqg, dkg = qk_norm_bwd(dqn, dkn, dv, qkv, wts["qg"], wts["kg"])
            small["q_norm"] = dqg[:, :SB_HEAD_DIM] + dqg[:, SB_HEAD_DIM:]
            small["k_norm"] = dkg[:, :SB_HEAD_DIM] + dkg[:, SB_HEAD_DIM:]
            dqkv = jnp.concatenate([dq, dk, dvb], axis=1)
            dh = matmul_nt_chunked(dqkv, wts["w_qkv"], "in_proj_c_bwd")
            big["w_qkv"] = matmul_tn_chunked(h, dqkv, "w_qkv_grad")
        dx, d_norm_mix[l], dscale1, dshift1 = norm_mod_bwd(dh, x0, g_mix, scale1, dx1, f"norm_mix_bwd_{l}")
        dmod[l] = jnp.concatenate([dshift1, dscale1, dgate1, dshift2, dscale2, dgate2], axis=1)
    small["mod"] = jnp.concatenate(dmod, axis=0)
    small["norm_mix"] = jnp.concatenate(d_norm_mix, axis=0)
    small["norm_mlp"] = jnp.concatenate(d_norm_mlp, axis=0)
    return loss_row, dx, small, big


SMALL_ORDER = ("mod", "norm_mix", "norm_mlp", "conv_w", "hg_norm", "lb_logits", "q_norm", "k_norm")


def kernel(x, c, ada_w, ada_b, norm_mix, norm_mlp, w_in_ab, conv_w, hg_norm, lb_logits, w_out_ab, w_qkv, q_norm, k_norm, w_out_c, mlp_w1, mlp_w2, loss_target, m_ada_w, m_ada_b, m_norm_mix, m_norm_mlp, m_w_in_ab, m_conv_w, m_hg_norm, m_lb_logits, m_w_out_ab, m_w_qkv, m_q_norm, m_k_norm, m_w_out_c, m_mlp_w1, m_mlp_w2, v_ada_w, v_ada_b, v_norm_mix, v_norm_mlp, v_w_in_ab, v_conv_w, v_hg_norm, v_lb_logits, v_w_out_ab, v_w_qkv, v_q_norm, v_k_norm, v_w_out_c, v_mlp_w1, v_mlp_w2):
    d = D_MODEL
    ax, ay, ac = _me()
    chip = 2 * ax + ay
    dev = 2 * chip + ac
    cols = ada_w.shape[2]

    first = all_gather_rows(_pad_row(jnp.concatenate([c.reshape(-1), conv_w.reshape(-1)]), 1536), "gather_cond")
    c_all = first[:, :d]
    conv_full = first[::2, d:d + 3 * HEAD_TILE].reshape(N_CHIPS, 3, HEAD_TILE).transpose(1, 0, 2).reshape(3, CONV_DIM)
    ada_b_shard = lax.dynamic_slice(ada_b, (0, chip * cols), (2, cols)).reshape(2, 1, cols)
    mod_cols = ada_fwd(c_all, ada_w, ada_b_shard)
    mod_all = all_gather_rows(mod_cols.reshape(1, -1), "gather_mod").reshape(N_DEV, 2, N_DEV, cols)
    mod = lax.dynamic_index_in_dim(mod_all[::2], dev, axis=2, keepdims=False).transpose(1, 0, 2).reshape(2, 6 * d)

    shards = [w_in_ab[0], w_out_ab[0], w_qkv[0], w_out_c[0], mlp_w1[0], mlp_w1[1], mlp_w2[0], mlp_w2[1]]
    g_in, g_out_ab, g_qkv, g_out_c, g_w1a, g_w1b, g_w2a, g_w2b = chip_all_gather(
        [s.astype(BF16) for s in shards], "gather_weights")
    wts = dict(
        norm_mix=norm_mix, norm_mlp=norm_mlp, w_in=g_in, conv_w=conv_full, hg_norm=hg_norm, lb_logits=lb_logits,
        w_out_ab=g_out_ab.reshape(d, d), w_qkv=g_qkv, qg=jnp.tile(q_norm, (1, 2)), kg=jnp.tile(k_norm, (1, 2)),
        w_out_c=g_out_c.reshape(d, d), w1=[g_w1a, g_w1b], w2=[g_w2a.reshape(D_FF, d), g_w2b.reshape(D_FF, d)])

    loss_row, grad_x, small, big = local_step(x[0], loss_target[0], mod, wts)

    flat = jnp.concatenate([small[k].reshape(-1) for k in SMALL_ORDER] + [loss_row[0, :1]])
    n_small = -(-flat.size // 1024) * 1024
    gathered = all_gather_rows(_pad_row(flat, n_small), "gather_small")
    total = sum_rows(gathered.reshape(N_DEV, 8, n_small // 8), "small_sum").reshape(-1)
    sizes = [small[k].size for k in SMALL_ORDER]
    offs = [sum(sizes[:i]) for i in range(len(sizes) + 1)]
    tot = {k: total[offs[i]:offs[i + 1]].reshape(small[k].shape) for i, k in enumerate(SMALL_ORDER)}
    loss = total[offs[-1]]
    mod_rows = gathered[:, :2 * 6 * d].reshape(N_DEV, 2, 6 * d)
    dmod_cols = lax.dynamic_slice(mod_rows, (0, 0, chip * cols), (N_DEV, 2, cols)).transpose(1, 0, 2)
    g_ada_w = ada_w_grad(c_all, dmod_cols)

    as_chunks = lambda g: g.reshape(N_CHIPS, g.shape[0] // N_CHIPS, g.shape[1])
    names = ["w_in", "w_out_ab", "w_qkv", "w_out_c", "w1_0", "w1_1", "w2_0", "w2_1"]
    chunked = [big[k] if big[k].ndim == 3 else as_chunks(big[k]) for k in names]
    r_in, r_out_ab, r_qkv, r_out_c, r_w1a, r_w1b, r_w2a, r_w2b = reduce_to_owner(chunked)

    grads = dict(
        ada_w=g_ada_w, ada_b=tot["mod"], norm_mix=tot["norm_mix"], norm_mlp=tot["norm_mlp"], w_in_ab=r_in[None],
        conv_w=lax.dynamic_slice(tot["conv_w"], (0, chip * HEAD_TILE), (3, HEAD_TILE))[None], hg_norm=tot["hg_norm"],
        lb_logits=tot["lb_logits"], w_out_ab=r_out_ab[None], w_qkv=r_qkv[None], q_norm=tot["q_norm"],
        k_norm=tot["k_norm"], w_out_c=r_out_c[None], mlp_w1=jnp.stack([r_w1a, r_w1b]), mlp_w2=jnp.stack([r_w2a, r_w2b]))
    weights = dict(ada_w=ada_w, ada_b=ada_b, norm_mix=norm_mix, norm_mlp=norm_mlp, w_in_ab=w_in_ab, conv_w=conv_w,
                   hg_norm=hg_norm, lb_logits=lb_logits, w_out_ab=w_out_ab, w_qkv=w_qkv, q_norm=q_norm, k_norm=k_norm,
                   w_out_c=w_out_c, mlp_w1=mlp_w1, mlp_w2=mlp_w2)
    m_in = dict(ada_w=m_ada_w, ada_b=m_ada_b, norm_mix=m_norm_mix, norm_mlp=m_norm_mlp, w_in_ab=m_w_in_ab,
                conv_w=m_conv_w, hg_norm=m_hg_norm, lb_logits=m_lb_logits, w_out_ab=m_w_out_ab, w_qkv=m_w_qkv,
                q_norm=m_q_norm, k_norm=m_k_norm, w_out_c=m_w_out_c, mlp_w1=m_mlp_w1, mlp_w2=m_mlp_w2)
    v_in = dict(ada_w=v_ada_w, ada_b=v_ada_b, norm_mix=v_norm_mix, norm_mlp=v_norm_mlp, w_in_ab=v_w_in_ab,
                conv_w=v_conv_w, hg_norm=v_hg_norm, lb_logits=v_lb_logits, w_out_ab=v_w_out_ab, w_qkv=v_w_qkv,
                q_norm=v_q_norm, k_norm=v_k_norm, w_out_c=v_w_out_c, mlp_w1=v_mlp_w1, mlp_w2=v_mlp_w2)
    order = list(weights)
    large = ("ada_w", "w_in_ab", "w_out_ab", "w_qkv", "w_out_c", "mlp_w1", "mlp_w2")
    delta, new_m, new_v = {}, {}, {}
    for k in large:
        shape = weights[k].shape
        flat2 = lambda a: a.reshape(-1, shape[-1])
        dl, nm, nv = adamw(flat2(weights[k]), flat2(grads[k]), flat2(m_in[k]), flat2(v_in[k]), f"adamw_{k}")
        delta[k], new_m[k], new_v[k] = dl.reshape(shape), nm.reshape(shape), nv.reshape(shape)
    rest = [k for k in order if k not in large]
    n_rest = -(-sum(weights[k].size for k in rest) // 1024) * 1024
    pack = lambda tree: _pad_row(jnp.concatenate([tree[k].reshape(-1) for k in rest]), n_rest).reshape(8, n_rest // 8)
    dl, nm, nv = adamw(pack(weights), pack(grads), pack(m_in), pack(v_in), "adamw_small")
    off = 0
    for k in rest:
        size, shape = weights[k].size, weights[k].shape
        delta[k], new_m[k], new_v[k] = (a.reshape(-1)[off:off + size].reshape(shape) for a in (dl, nm, nv))
        off += size
    grads = {k: grads[k].reshape(weights[k].shape) for k in order}
    return (loss, grad_x[None], *[grads[k] for k in order], *[delta[k] for k in order],
            *[new_m[k] for k in order], *[new_v[k] for k in order])
```

```python
import functools

import jax
import jax.numpy as jnp
from jax import lax
from jax.experimental import pallas as pl
from jax.experimental.pallas import tpu as pltpu

F32 = jnp.float32
BF16 = jnp.bfloat16
HIGHEST = lax.Precision.HIGHEST
MESH = pl.DeviceIdType.MESH

D_MODEL = 1024
D_FF = 4096
CHUNK = 64
HEAD_TILE = 128
SB_HEAD_DIM = 64
CONV_DIM = 512
HG_DIM = 512
AB_IN = 3584
N_CHIPS = 4
N_DEV = 8
EPS = 1e-6
ATT_BLOCK = 128
ATT_TILE = 512
HG_FWD = (512, 1)
HG_BWD = (128, 4)

ADAM_LR = 0.001
ADAM_B1 = 0.9
ADAM_B2 = 0.999
ADAM_EPS = 1e-08
ADAM_WD = 0.01
ADAM_STEP = 10

NN = (((1,), (0,)), ((), ()))
NT = (((1,), (1,)), ((), ()))
TN = (((0,), (0,)), ((), ()))


def _cp(*dims):
    return pltpu.CompilerParams(dimension_semantics=dims) if dims else pltpu.CompilerParams()


def _dot(a, b, dn, precision=None):
    return lax.dot_general(a, b, dn, preferred_element_type=F32, precision=precision)


def _sigmoid(z):
    return 1.0 / (1.0 + jnp.exp(-z))


def _matmul(a, b, *, dn, grid, a_spec, b_spec, acc_shape, epilogue, extras=(), extra_specs=(),
            out_shapes, out_specs, name):
    nk = grid[2]
    n_extra = len(extras)
    n_out = len(out_shapes)

    def body(*refs):
        a_ref, b_ref = refs[0], refs[1]
        extra_refs = refs[2:2 + n_extra]
        out_refs = refs[2 + n_extra:2 + n_extra + n_out]
        acc_ref = refs[-1]
        k = pl.program_id(2)
        part = _dot(a_ref[...], b_ref[...], dn)

        if nk == 1:
            epilogue(part, extra_refs, out_refs)
        else:
            @pl.when(k == 0)
            def _():
                acc_ref[...] = part

            @pl.when(k > 0)
            def _():
                acc_ref[...] += part

            @pl.when(k == nk - 1)
            def _():
                epilogue(acc_ref[...], extra_refs, out_refs)

    return pl.pallas_call(
        body, grid=grid, in_specs=[a_spec, b_spec, *extra_specs], out_specs=out_specs, out_shape=out_shapes,
        scratch_shapes=[pltpu.VMEM(acc_shape, F32)], name=name,
        compiler_params=_cp("parallel", "parallel", "arbitrary"))(a, b, *extras)


def _store(dtype):
    def epilogue(acc, extra_refs, out_refs):
        out_refs[0][...] = acc.astype(dtype)
    return epilogue


def _tok_tile(s):
    return min(512, s)


def matmul_nn_chunked(a, w, out_dtype, name, epilogue=None, extras=(), extra_specs=(), out_shapes=None,
                      out_specs=None):
    s, k = a.shape
    _, _, n4 = w.shape
    tm = _tok_tile(s)
    if out_shapes is None:
        out_shapes = [jax.ShapeDtypeStruct((s, N_CHIPS * n4), out_dtype)]
        out_specs = [pl.BlockSpec((tm, n4), lambda i, j, kk: (i, j))]
        epilogue = _store(out_dtype)
    return _matmul(
        a, w, dn=NN, grid=(s // tm, N_CHIPS, 1),
        a_spec=pl.BlockSpec((tm, k), lambda i, j, kk: (i, 0)),
        b_spec=pl.BlockSpec((None, k, n4), lambda i, j, kk: (j, 0, 0)),
        acc_shape=(tm, n4), epilogue=epilogue, extras=extras, extra_specs=extra_specs,
        out_shapes=out_shapes, out_specs=out_specs, name=name)


def matmul_nt_chunked(dy, w, name):
    s = dy.shape[0]
    _, k, n4 = w.shape
    tm = _tok_tile(s)
    return _matmul(
        dy, w, dn=NT, grid=(s // tm, 1, N_CHIPS),
        a_spec=pl.BlockSpec((tm, n4), lambda i, j, kk: (i, kk)),
        b_spec=pl.BlockSpec((None, k, n4), lambda i, j, kk: (kk, 0, 0)),
        acc_shape=(tm, k), epilogue=_store(F32),
        out_shapes=[jax.ShapeDtypeStruct((s, k), F32)], out_specs=[pl.BlockSpec((tm, k), lambda i, j, kk: (i, 0))],
        name=name)[0]


def matmul_tn_chunked(x, dy, name):
    s, k = x.shape
    n4 = dy.shape[1] // N_CHIPS
    ts = _tok_tile(s)
    return _matmul(
        x, dy, dn=TN, grid=(1, N_CHIPS, s // ts),
        a_spec=pl.BlockSpec((ts, k), lambda i, j, kk: (kk, 0)),
        b_spec=pl.BlockSpec((ts, n4), lambda i, j, kk: (kk, j)),
        acc_shape=(k, n4), epilogue=_store(F32),
        out_shapes=[jax.ShapeDtypeStruct((N_CHIPS, k, n4), F32)],
        out_specs=[pl.BlockSpec((None, k, n4), lambda i, j, kk: (j, 0, 0))], name=name)[0]


def matmul_nn_plain(a, w, name, epilogue, extras, extra_specs, out_shapes, out_specs, tk=1024):
    s, k = a.shape
    n = w.shape[1]
    tm = _tok_tile(s)
    return _matmul(
        a, w, dn=NN, grid=(s // tm, 1, k // tk),
        a_spec=pl.BlockSpec((tm, tk), lambda i, j, kk: (i, kk)),
        b_spec=pl.BlockSpec((tk, n), lambda i, j, kk: (kk, 0)),
        acc_shape=(tm, n), epilogue=epilogue, extras=extras, extra_specs=extra_specs,
        out_shapes=out_shapes, out_specs=out_specs, name=name)


def matmul_nt_plain(dy, w, name, epilogue=None, extras=(), extra_specs=(), out_dtype=F32, tn=1024):
    s, n = dy.shape
    k = w.shape[0]
    tm = _tok_tile(s)
    return _matmul(
        dy, w, dn=NT, grid=(s // tm, k // tn, 1),
        a_spec=pl.BlockSpec((tm, n), lambda i, j, kk: (i, 0)),
        b_spec=pl.BlockSpec((tn, n), lambda i, j, kk: (j, 0)),
        acc_shape=(tm, tn), epilogue=epilogue or _store(out_dtype), extras=extras, extra_specs=extra_specs,
        out_shapes=[jax.ShapeDtypeStruct((s, k), out_dtype)], out_specs=[pl.BlockSpec((tm, tn), lambda i, j, kk: (i, j))],
        name=name)[0]


def matmul_tn_plain(x, dy, name, tk=1024):
    s, k = x.shape
    n = dy.shape[1]
    ts = _tok_tile(s)
    return _matmul(
        x, dy, dn=TN, grid=(k // tk, 1, s // ts),
        a_spec=pl.BlockSpec((ts, tk), lambda i, j, kk: (kk, i)),
        b_spec=pl.BlockSpec((ts, n), lambda i, j, kk: (kk, 0)),
        acc_shape=(tk, n), epilogue=_store(F32),
        out_shapes=[jax.ShapeDtypeStruct((k, n), F32)], out_specs=[pl.BlockSpec((tk, n), lambda i, j, kk: (i, 0))],
        name=name)[0]


def _row_spec(n):
    return pl.BlockSpec((1, n), lambda i: (0, 0))


def norm_mod_fwd(x, g, scale, shift, name):
    s, d = x.shape
    tm = _tok_tile(s)

    def body(x_ref, g_ref, sc_ref, sh_ref, h_ref):
        xv = x_ref[...]
        r = lax.rsqrt(jnp.mean(xv * xv, axis=-1, keepdims=True) + EPS)
        h_ref[...] = ((xv * r * g_ref[...]) * (1.0 + sc_ref[...]) + sh_ref[...]).astype(BF16)

    tile = pl.BlockSpec((tm, d), lambda i: (i, 0))
    return pl.pallas_call(
        body, grid=(s // tm,), in_specs=[tile, _row_spec(d), _row_spec(d), _row_spec(d)], out_specs=tile,
        out_shape=jax.ShapeDtypeStruct((s, d), BF16), name=name, compiler_params=_cp("parallel"))(x, g, scale, shift)


def norm_mod_bwd(dh, x, g, scale, dres, name):
    s, d = x.shape
    tm = _tok_tile(s)

    def body(dh_ref, x_ref, g_ref, sc_ref, dres_ref, dx_ref, dg_ref, dsc_ref, dsh_ref):
        @pl.when(pl.program_id(0) == 0)
        def _():
            dg_ref[...] = jnp.zeros_like(dg_ref)
            dsc_ref[...] = jnp.zeros_like(dsc_ref)
            dsh_ref[...] = jnp.zeros_like(dsh_ref)

        xv = x_ref[...]
        dhv = dh_ref[...]
        r = lax.rsqrt(jnp.mean(xv * xv, axis=-1, keepdims=True) + EPS)
        xn = xv * r
        gv = g_ref[...]
        s1 = 1.0 + sc_ref[...]
        dsh_ref[...] += jnp.sum(dhv, axis=0, keepdims=True)
        dsc_ref[...] += jnp.sum(dhv * xn * gv, axis=0, keepdims=True)
        dg_ref[...] += jnp.sum(dhv * xn * s1, axis=0, keepdims=True)
        dxn = dhv * gv * s1
        dx_ref[...] = dres_ref[...] + r * (dxn - xn * jnp.mean(dxn * xn, axis=-1, keepdims=True))

    tile = pl.BlockSpec((tm, d), lambda i: (i, 0))
    row = jax.ShapeDtypeStruct((1, d), F32)
    return pl.pallas_call(
        body, grid=(s // tm,), in_specs=[tile, tile, _row_spec(d), _row_spec(d), tile],
        out_specs=[tile, _row_spec(d), _row_spec(d), _row_spec(d)],
        out_shape=[jax.ShapeDtypeStruct((s, d), F32), row, row, row], name=name,
        compiler_params=_cp("arbitrary"))(dh, x, g, scale, dres)


def gate_bwd(dx, y, gate, name):
    s, d = dx.shape
    tm = _tok_tile(s)

    def body(dx_ref, y_ref, gate_ref, dy_ref, dgate_ref):
        @pl.when(pl.program_id(0) == 0)
        def _():
            dgate_ref[...] = jnp.zeros_like(dgate_ref)

        dxv = dx_ref[...]
        dy_ref[...] = (gate_ref[...] * dxv).astype(BF16)
        dgate_ref[...] += jnp.sum(dxv * y_ref[...], axis=0, keepdims=True)

    tile = pl.BlockSpec((tm, d), lambda i: (i, 0))
    return pl.pallas_call(
        body, grid=(s // tm,), in_specs=[tile, tile, _row_spec(d)], out_specs=[tile, _row_spec(d)],
        out_shape=[jax.ShapeDtypeStruct((s, d), BF16), jax.ShapeDtypeStruct((1, d), F32)], name=name,
        compiler_params=_cp("arbitrary"))(dx, y, gate)


def loss_and_grad(y, target):
    s, d = y.shape
    tm = _tok_tile(s)

    def body(y_ref, t_ref, dy_ref, loss_ref):
        @pl.when(pl.program_id(0) == 0)
        def _():
            loss_ref[...] = jnp.zeros_like(loss_ref)

        err = y_ref[...] - t_ref[...]
        dy_ref[...] = err * (1.0 / d)
        loss_ref[...] += jnp.sum(err * err) * (0.5 / d)

    tile = pl.BlockSpec((tm, d), lambda i: (i, 0))
    return pl.pallas_call(
        body, grid=(s // tm,), in_specs=[tile, tile], out_specs=[tile, _row_spec(128)],
        out_shape=[jax.ShapeDtypeStruct((s, d), F32), jax.ShapeDtypeStruct((1, 128), F32)], name="loss_and_grad",
        compiler_params=_cp("arbitrary"))(y, target)


def _proj_residual(acc, extra_refs, out_refs):
    x_ref, gate_ref = extra_refs
    out_refs[0][...] = acc
    out_refs[1][...] = x_ref[...] + gate_ref[...] * acc


def proj_residual(a, w, x, gate, name):
    s, d = x.shape
    tm = _tok_tile(s)
    tile = pl.BlockSpec((tm, d), lambda i, j, kk: (i, 0))
    shape = jax.ShapeDtypeStruct((s, d), F32)
    return matmul_nn_plain(
        a, w, name, _proj_residual, extras=(x, gate),
        extra_specs=(tile, pl.BlockSpec((1, d), lambda i, j, kk: (0, 0))),
        out_shapes=[shape, shape], out_specs=[tile, tile])


def _mlp_up(acc, extra_refs, out_refs):
    r = jnp.maximum(acc, 0.0)
    out_refs[0][...] = (r * r).astype(BF16)
    out_refs[1][...] = r.astype(BF16)


def mlp_up(h, w1, name):
    s = h.shape[0]
    n4 = w1.shape[2]
    tm = _tok_tile(s)
    shape = jax.ShapeDtypeStruct((s, N_CHIPS * n4), BF16)
    spec = pl.BlockSpec((tm, n4), lambda i, j, kk: (i, j))
    return matmul_nn_chunked(h, w1, BF16, name, epilogue=_mlp_up, out_shapes=[shape, shape], out_specs=[spec, spec])


def _dact(acc, extra_refs, out_refs):
    out_refs[0][...] = (acc * (2.0 * extra_refs[0][...].astype(F32))).astype(BF16)


def mlp_down_bwd(dy, w2, r, name):
    s = dy.shape[0]
    tm = _tok_tile(s)
    return matmul_nt_plain(dy, w2, name, epilogue=_dact, extras=(r,),
                           extra_specs=(pl.BlockSpec((tm, 1024), lambda i, j, kk: (i, j)),), out_dtype=BF16)


def _shift_down(p, n, row):
    return jnp.where(row >= n, pltpu.roll(p, n, 0), 0.0)


def _shift_up(p, n, row):
    rows = p.shape[0]
    return jnp.where(row < rows - n, pltpu.roll(p, rows - n, 0), 0.0)


def _u_col(block):
    return lambda i: (0, block + i)


def conv_mixer_fwd(u, conv_w):
    s = u.shape[0]
    nb = CONV_DIM // HEAD_TILE

    def body(ab_ref, ac_ref, ah_ref, w_ref, y_ref):
        row = lax.broadcasted_iota(jnp.int32, (s, HEAD_TILE), 0)
        p = ac_ref[...] * ah_ref[...]
        w = w_ref[...]
        conv = w[0:1] * _shift_down(p, 2, row) + w[1:2] * _shift_down(p, 1, row) + w[2:3] * p
        y_ref[...] = (ab_ref[...] * conv).astype(BF16)

    col = lambda b: pl.BlockSpec((s, HEAD_TILE), _u_col(b * nb))
    return pl.pallas_call(
        body, grid=(nb,), in_specs=[col(0), col(1), col(2), pl.BlockSpec((3, HEAD_TILE), lambda i: (0, i))],
        out_specs=pl.BlockSpec((s, HEAD_TILE), lambda i: (0, i)),
        out_shape=jax.ShapeDtypeStruct((s, CONV_DIM), BF16), name="conv_mixer_fwd",
        compiler_params=_cp("parallel"))(u, u, u, conv_w)


def conv_mixer_bwd(dmix, u, conv_w):
    s = u.shape[0]
    nb = CONV_DIM // HEAD_TILE

    def body(dy_ref, ab_ref, ac_ref, ah_ref, w_ref, dab_ref, dac_ref, dah_ref, dw_ref):
        row = lax.broadcasted_iota(jnp.int32, (s, HEAD_TILE), 0)
        ac = ac_ref[...]
        ah = ah_ref[...]
        p = ac * ah
        w = w_ref[...]
        p1 = _shift_down(p, 1, row)
        p2 = _shift_down(p, 2, row)
        conv = w[0:1] * p2 + w[1:2] * p1 + w[2:3] * p
        dy = dy_ref[...]
        dab_ref[...] = (dy * conv).astype(BF16)
        dconv = dy * ab_ref[...]
        dp = w[0:1] * _shift_up(dconv, 2, row) + w[1:2] * _shift_up(dconv, 1, row) + w[2:3] * dconv
        dac_ref[...] = (dp * ah).astype(BF16)
        dah_ref[...] = (dp * ac).astype(BF16)
        dw_ref[...] = jnp.concatenate(
            [jnp.sum(dconv * p2, axis=0, keepdims=True), jnp.sum(dconv * p1, axis=0, keepdims=True),
             jnp.sum(dconv * p, axis=0, keepdims=True)], axis=0)

    col = lambda b: pl.BlockSpec((s, HEAD_TILE), _u_col(b * nb))
    out = pl.BlockSpec((s, HEAD_TILE), lambda i: (0, i))
    wspec = pl.BlockSpec((3, HEAD_TILE), lambda i: (0, i))
    shape = jax.ShapeDtypeStruct((s, CONV_DIM), BF16)
    return pl.pallas_call(
        body, grid=(nb,), in_specs=[out, col(0), col(1), col(2), wspec], out_specs=[out, out, out, wspec],
        out_shape=[shape, shape, shape, jax.ShapeDtypeStruct((3, CONV_DIM), F32)], name="conv_mixer_bwd",
        compiler_params=_cp("parallel"))(dmix, u, u, u, conv_w)


def _chunk_cumsum(g, pos):
    for sh in (1, 2, 4, 8, 16, 32):
        g = g + jnp.where(pos >= sh, pltpu.roll(g, sh, 0), 0.0)
    return g


def _chunk_rev_cumsum(g, pos):
    rows = g.shape[0]
    for sh in (1, 2, 4, 8, 16, 32):
        g = g + jnp.where(pos < CHUNK - sh, pltpu.roll(g, rows - sh, 0), 0.0)
    return g


def _lower_bound(lb_ref):
    logits = lb_ref[...]
    e = jnp.exp(logits - jnp.max(logits, axis=0, keepdims=True))
    p = e / jnp.sum(e, axis=0, keepdims=True)
    return p[0:1], p


def _hg_gates(hf, lb):
    sg = _sigmoid(hf)
    f = lb + (1.0 - lb) * sg
    return sg, f, jnp.log(f), 1.0 - f


def _hg_specs(s, slab, order):
    n = s // slab
    col = lambda b: pl.BlockSpec((slab, HEAD_TILE), lambda h, i: (order(i, n), b + h))
    return n, col


def _aligned(start):
    return start if isinstance(start, int) else pl.multiple_of(start, 8)


def _each_distance8(step, carry, unroll):
    groups = CHUNK // 8 // unroll

    def group(i, carry):
        for j in range(unroll):
            carry = step((i * unroll + j) * 8, carry)
        return carry

    return group(0, carry) if groups == 1 else lax.fori_loop(0, groups, group, carry)


def _pad_front(ref, val):
    ref[0:CHUNK, :] = jnp.zeros((CHUNK, HEAD_TILE), F32)
    ref[CHUNK:, :] = val


def _shifted_down(ref, d8, r, rows):
    if r == 0:
        return ref[pl.ds(_aligned(CHUNK - d8), rows), :]
    win = ref[pl.ds(_aligned(CHUNK - 8 - d8), rows + 8), :]
    return pltpu.roll(win, r, 0)[8:, :]


def _pulled_up(ref, val, d8, r, rows):
    if r == 0:
        ref[0:rows, :] = val
    else:
        ref[0:rows, :] = pltpu.roll(val, rows - r, 0)
    return ref[pl.ds(_aligned(d8), rows), :]


def hgrn_fwd(u, lb_logits, hg_norm):
    s = u.shape[0]
    slab = min(HG_FWD[0], s)
    cps = slab // CHUNK
    n, col = _hg_specs(s, slab, lambda i, n_: i)
    heads = HG_DIM // HEAD_TILE

    def body(q_ref, f_ref, i_ref, g_ref, lb_ref, nw_ref, y_ref, o_ref, st_ref, at_ref, bp_ref, kp_ref, vp_ref):
        @pl.when(pl.program_id(1) == 0)
        def _():
            at_ref[...] = jnp.zeros_like(at_ref)

        pos = lax.broadcasted_iota(jnp.int32, (slab, HEAD_TILE), 0) & (CHUNK - 1)
        lb, _ = _lower_bound(lb_ref)
        q = q_ref[...]
        v = i_ref[...]
        _, _, g, kk = _hg_gates(f_ref[...], lb)
        b = _chunk_cumsum(g, pos)
        for ref, val in ((bp_ref, b), (kp_ref, kk), (vp_ref, v)):
            _pad_front(ref, val)

        def diag(d8, r, o):
            d = d8 + r
            lam = jnp.exp(jnp.where(pos >= d, b - _shifted_down(bp_ref, d8, r, slab), -jnp.inf))
            sc = jnp.sum(q * _shifted_down(kp_ref, d8, r, slab) * lam, axis=-1, keepdims=True)
            return o + sc * _shifted_down(vp_ref, d8, r, slab)

        o = jnp.zeros((slab, HEAD_TILE), F32)
        for r in range(8):
            o = _each_distance8(lambda d8, acc: diag(d8, r, acc), o, HG_FWD[1])
        o_ref[...] = o

        qhat = q * jnp.exp(b)
        for c in range(cps):
            rows = slice(c * CHUNK, (c + 1) * CHUNK)
            at = at_ref[...]
            st_ref[c] = at
            o_ref[rows, :] += _dot(qhat[rows], at, NT, HIGHEST)
            bc = b[(c + 1) * CHUNK - 1:(c + 1) * CHUNK]
            khat = kk[rows] * jnp.exp(bc - b[rows])
            at_ref[...] = at * jnp.exp(bc) + _dot(v[rows], khat, TN, HIGHEST)

        o = o_ref[...]
        r = lax.rsqrt(jnp.mean(o * o, axis=-1, keepdims=True) + EPS)
        hg = g_ref[...]
        y_ref[...] = (o * r * nw_ref[...] * (hg * _sigmoid(hg))).astype(BF16)

    out = pl.BlockSpec((slab, HEAD_TILE), lambda h, i: (i, h))
    par = lambda rows: pl.BlockSpec((rows, HEAD_TILE), lambda h, i: (0, h))
    return pl.pallas_call(
        body, grid=(heads, n), in_specs=[col(12), col(16), col(20), col(24), par(3), par(1)],
        out_specs=[out, out, pl.BlockSpec((None, cps, HEAD_TILE, HEAD_TILE), lambda h, i: (h, i, 0, 0))],
        out_shape=[jax.ShapeDtypeStruct((s, HG_DIM), BF16), jax.ShapeDtypeStruct((s, HG_DIM), F32),
                   jax.ShapeDtypeStruct((heads, s // CHUNK, HEAD_TILE, HEAD_TILE), F32)],
        scratch_shapes=[pltpu.VMEM((HEAD_TILE, HEAD_TILE), F32)] + [pltpu.VMEM((slab + CHUNK, HEAD_TILE), F32)] * 3,
        name="hgrn_fwd",
        compiler_params=_cp("parallel", "arbitrary"))(u, u, u, u, lb_logits, hg_norm)


def hgrn_bwd(dmix, u, o_raw, states, lb_logits, hg_norm):
    s = u.shape[0]
    slab = min(HG_BWD[0], s)
    cps = slab // CHUNK
    n, col = _hg_specs(s, slab, lambda i, n_: n_ - 1 - i)
    heads = HG_DIM // HEAD_TILE

    def body(dy_ref, q_ref, f_ref, i_ref, g_ref, o_ref, st_ref, lb_ref, nw_ref,
             dq_ref, df_ref, di_ref, dg_ref, dnw_ref, dlb_ref, dat_ref, dlbacc_ref, dqs_ref, dks_ref, dvs_ref, dbc_ref,
             bp_ref, kp_ref, vp_ref, up1_ref, up2_ref):
        step = pl.program_id(1)

        @pl.when(step == 0)
        def _():
            dat_ref[...] = jnp.zeros_like(dat_ref)
            dlbacc_ref[...] = jnp.zeros_like(dlbacc_ref)
            dnw_ref[...] = jnp.zeros_like(dnw_ref)

        pos = lax.broadcasted_iota(jnp.int32, (slab, HEAD_TILE), 0) & (CHUNK - 1)
        lb, probs = _lower_bound(lb_ref)
        q = q_ref[...]
        v = i_ref[...]
        sg_f, f, g, kk = _hg_gates(f_ref[...], lb)
        b = _chunk_cumsum(g, pos)

        o = o_ref[...]
        nw = nw_ref[...]
        r = lax.rsqrt(jnp.mean(o * o, axis=-1, keepdims=True) + EPS)
        hg = g_ref[...]
        sg = _sigmoid(hg)
        dy = dy_ref[...]
        d_on = dy * (hg * sg)
        dg_ref[...] = (dy * (o * r * nw) * (sg * (1.0 + hg * (1.0 - sg)))).astype(BF16)
        dnw_ref[...] += jnp.sum(d_on * o * r, axis=0, keepdims=True)
        t1 = d_on * nw
        do = r * t1 - o * (r * r * r) * jnp.mean(t1 * o, axis=-1, keepdims=True)

        eb = jnp.exp(b)
        qhat = q * eb
        dbc_ref[...] = jnp.zeros_like(dbc_ref)
        for c in reversed(range(cps)):
            rows = slice(c * CHUNK, (c + 1) * CHUNK)
            last = (c + 1) * CHUNK - 1
            at = st_ref[c]
            dat = dat_ref[...]
            bc = b[last:last + 1]
            ebc = jnp.exp(bc)
            dec = jnp.exp(bc - b[rows])
            khat = kk[rows] * dec
            at_next = at * ebc + _dot(v[rows], khat, TN, HIGHEST)
            dbc_ref[last:last + 1, :] = jnp.sum(dat * at_next, axis=0, keepdims=True)
            dqs_ref[rows, :] = eb[rows] * _dot(do[rows], at, NN, HIGHEST)
            dks_ref[rows, :] = dec * _dot(v[rows], dat, NN, HIGHEST)
            dvs_ref[rows, :] = _dot(khat, dat, NT, HIGHEST)
            dat_ref[...] = dat * ebc + _dot(do[rows], qhat[rows], TN, HIGHEST)

        for ref, val in ((bp_ref, b), (kp_ref, kk), (vp_ref, v)):
            _pad_front(ref, val)
        for ref in (up1_ref, up2_ref):
            ref[slab:, :] = jnp.zeros((CHUNK, HEAD_TILE), F32)

        def diag(d8, r, carry):
            dq, dk, dv = carry
            d = d8 + r
            lam = jnp.exp(jnp.where(pos >= d, b - _shifted_down(bp_ref, d8, r, slab), -jnp.inf))
            kd = _shifted_down(kp_ref, d8, r, slab)
            vd = _shifted_down(vp_ref, d8, r, slab)
            sc = jnp.sum(q * kd * lam, axis=-1, keepdims=True)
            pd = jnp.sum(do * vd, axis=-1, keepdims=True)
            dq = dq + pd * kd * lam
            dk = dk + _pulled_up(up1_ref, pd * q * lam, d8, r, slab)
            dv = dv + _pulled_up(up2_ref, sc * do, d8, r, slab)
            return dq, dk, dv

        carry = (dqs_ref[...], dks_ref[...], dvs_ref[...])
        for r in range(8):
            carry = _each_distance8(lambda d8, cr: diag(d8, r, cr), carry, HG_BWD[1])
        dq, dk, dv = carry

        db = q * dq - kk * dk + dbc_ref[...]
        dgl = _chunk_rev_cumsum(db, pos)
        dfv = dgl / f - dk
        dq_ref[...] = dq.astype(BF16)
        di_ref[...] = dv.astype(BF16)
        df_ref[...] = (dfv * (1.0 - lb) * sg_f * (1.0 - sg_f)).astype(BF16)
        dlbacc_ref[...] += jnp.sum(dfv * (1.0 - sg_f), axis=0, keepdims=True)

        @pl.when(step == n - 1)
        def _():
            dlb = dlbacc_ref[...]
            sel = (lax.broadcasted_iota(jnp.int32, (3, HEAD_TILE), 0) == 0).astype(F32)
            dlb_ref[...] = dlb * probs[0:1] * (sel - probs)

    out = pl.BlockSpec((slab, HEAD_TILE), lambda h, i: (n - 1 - i, h))
    par = lambda rows: pl.BlockSpec((rows, HEAD_TILE), lambda h, i: (0, h))
    dyspec = pl.BlockSpec((slab, HEAD_TILE), lambda h, i: (n - 1 - i, CONV_DIM // HEAD_TILE + h))
    shape = jax.ShapeDtypeStruct((s, HG_DIM), BF16)
    slab_f32 = pltpu.VMEM((slab, HEAD_TILE), F32)
    return pl.pallas_call(
        body, grid=(heads, n),
        in_specs=[dyspec, col(12), col(16), col(20), col(24), out,
                  pl.BlockSpec((None, cps, HEAD_TILE, HEAD_TILE), lambda h, i: (h, n - 1 - i, 0, 0)), par(3), par(1)],
        out_specs=[out, out, out, out, par(1), par(3)],
        out_shape=[shape, shape, shape, shape, jax.ShapeDtypeStruct((1, HG_DIM), F32),
                   jax.ShapeDtypeStruct((3, HG_DIM), F32)],
        scratch_shapes=[pltpu.VMEM((HEAD_TILE, HEAD_TILE), F32), pltpu.VMEM((1, HEAD_TILE), F32),
                        slab_f32, slab_f32, slab_f32, slab_f32] + [pltpu.VMEM((slab + CHUNK, HEAD_TILE), F32)] * 5,
        name="hgrn_bwd", compiler_params=_cp("parallel", "arbitrary"))(
            dmix, u, u, u, u, o_raw, states, lb_logits, hg_norm)


def _pair_rstd(x, lo):
    x2 = x * x
    s_lo = jnp.sum(jnp.where(lo, x2, 0.0), axis=-1, keepdims=True)
    s_hi = jnp.sum(jnp.where(lo, 0.0, x2), axis=-1, keepdims=True)
    inv = 1.0 / SB_HEAD_DIM
    return jnp.where(lo, lax.rsqrt(s_lo * inv + EPS), lax.rsqrt(s_hi * inv + EPS))


def _pair_mean(x, lo):
    s_lo = jnp.sum(jnp.where(lo, x, 0.0), axis=-1, keepdims=True)
    s_hi = jnp.sum(jnp.where(lo, 0.0, x), axis=-1, keepdims=True)
    return jnp.where(lo, s_lo, s_hi) * (1.0 / SB_HEAD_DIM)


def qk_norm_fwd(qkv, qg, kg):
    s = qkv.shape[0]
    tm = _tok_tile(s)
    nt = D_MODEL // HEAD_TILE

    def body(q_ref, k_ref, v_ref, qg_ref, kg_ref, qn_ref, kn_ref, vb_ref):
        lo = lax.broadcasted_iota(jnp.int32, (tm, HEAD_TILE), 1) < SB_HEAD_DIM
        qv = q_ref[...]
        kv = k_ref[...]
        qn_ref[...] = (qv * _pair_rstd(qv, lo) * qg_ref[...]).astype(BF16)
        kn_ref[...] = (kv * _pair_rstd(kv, lo) * kg_ref[...]).astype(BF16)
        vb_ref[...] = v_ref[...].astype(BF16)

    col = lambda b: pl.BlockSpec((tm, HEAD_TILE), lambda i, j: (i, b * nt + j))
    gain = pl.BlockSpec((1, HEAD_TILE), lambda i, j: (0, 0))
    out = pl.BlockSpec((tm, HEAD_TILE), lambda i, j: (i, j))
    shape = jax.ShapeDtypeStruct((s, D_MODEL), BF16)
    return pl.pallas_call(
        body, grid=(s // tm, nt), in_specs=[col(0), col(1), col(2), gain, gain], out_specs=[out, out, out],
        out_shape=[shape, shape, shape], name="qk_norm_fwd", compiler_params=_cp("parallel", "parallel"))(
            qkv, qkv, qkv, qg, kg)


def qk_norm_bwd(dqn, dkn, dv, qkv, qg, kg):
    s = qkv.shape[0]
    tm = _tok_tile(s)
    nt = D_MODEL // HEAD_TILE

    def body(dqn_ref, dkn_ref, dv_ref, q_ref, k_ref, qg_ref, kg_ref, dq_ref, dk_ref, dvb_ref, dqg_ref, dkg_ref):
        @pl.when((pl.program_id(0) == 0) & (pl.program_id(1) == 0))
        def _():
            dqg_ref[...] = jnp.zeros_like(dqg_ref)
            dkg_ref[...] = jnp.zeros_like(dkg_ref)

        lo = lax.broadcasted_iota(jnp.int32, (tm, HEAD_TILE), 1) < SB_HEAD_DIM

        def one(x_ref, g_ref, dn_ref, dx_ref, dgain_ref):
            xv = x_ref[...]
            r = _pair_rstd(xv, lo)
            xn = xv * r
            dn = dn_ref[...]
            dgain_ref[...] += jnp.sum(dn * xn, axis=0, keepdims=True)
            t1 = dn * g_ref[...]
            dx_ref[...] = (r * (t1 - xn * _pair_mean(t1 * xn, lo))).astype(BF16)

        one(q_ref, qg_ref, dqn_ref, dq_ref, dqg_ref)
        one(k_ref, kg_ref, dkn_ref, dk_ref, dkg_ref)
        dvb_ref[...] = dv_ref[...].astype(BF16)

    col = lambda b: pl.BlockSpec((tm, HEAD_TILE), lambda i, j: (i, b * nt + j))
    gain = pl.BlockSpec((1, HEAD_TILE), lambda i, j: (0, 0))
    out = pl.BlockSpec((tm, HEAD_TILE), lambda i, j: (i, j))
    shape = jax.ShapeDtypeStruct((s, D_MODEL), BF16)
    grow = jax.ShapeDtypeStruct((1, HEAD_TILE), F32)
    return pl.pallas_call(
        body, grid=(s // tm, nt), in_specs=[out, out, out, col(0), col(1), gain, gain],
        out_specs=[out, out, out, gain, gain], out_shape=[shape, shape, shape, grow, grow], name="qk_norm_bwd",
        compiler_params=_cp("arbitrary", "arbitrary"))(dqn, dkn, dv, qkv, qkv, qg, kg)


def _split_dot(x, u):
    hi = x.astype(BF16)
    lo = (x - hi.astype(F32)).astype(BF16)
    return _dot(hi, u, NN) + _dot(lo, u, NN)


def _sb_tile(qs, kb, carry, causal, suffix, diag):
    z = _dot(qs, kb, NT)
    lb = jnp.minimum(z, 0.0) - jnp.log(1.0 + jnp.exp(-jnp.abs(z)))
    lom = lb - z
    if diag:
        lom = jnp.where(causal, lom, 0.0)
    ws = []
    for j in reversed(range(ATT_TILE // ATT_BLOCK)):
        cols = slice(j * ATT_BLOCK, (j + 1) * ATT_BLOCK)
        ws.append(jnp.exp(z[:, cols] + _split_dot(lom[:, cols], suffix) + carry))
        carry = carry + jnp.sum(lom[:, cols], axis=-1, keepdims=True)
    w = jnp.concatenate(ws[::-1], axis=1)
    if diag:
        w = jnp.where(causal, w, 0.0)
    return lb, w, carry


def _sb_consts():
    row = lax.broadcasted_iota(jnp.int32, (ATT_BLOCK, ATT_BLOCK), 0)
    col = lax.broadcasted_iota(jnp.int32, (ATT_BLOCK, ATT_BLOCK), 1)
    suffix = (row >= col).astype(BF16)
    trow = lax.broadcasted_iota(jnp.int32, (ATT_TILE, ATT_TILE), 0)
    tcol = lax.broadcasted_iota(jnp.int32, (ATT_TILE, ATT_TILE), 1)
    causal = tcol < trow
    lo = lax.broadcasted_iota(jnp.int32, (ATT_TILE, HEAD_TILE), 1) < SB_HEAD_DIM
    return suffix, causal, lo


def _rows(i):
    return pl.ds(pl.multiple_of(i * ATT_TILE, ATT_TILE), ATT_TILE)


def _head_query(qb, mask):
    return (jnp.where(mask, qb, 0.0) * (SB_HEAD_DIM ** -0.5)).astype(BF16)


def sb_attn_fwd(qn, kn, vb):
    s = qn.shape[0]
    nq = s // ATT_TILE
    nt = D_MODEL // HEAD_TILE

    def body(q_ref, k_ref, v_ref, o_ref, ob_ref):
        suffix, causal, lo = _sb_consts()

        def qtile(qi, _):
            qb = q_ref[_rows(qi), :].astype(F32)
            qs = [_head_query(qb, lo), _head_query(qb, ~lo)]

            def step(kj, state, diag):
                kb = k_ref[_rows(kj), :]
                vt = v_ref[_rows(kj), :]
                out = []
                for hh in range(2):
                    carry, acc = state[hh]
                    _, w, carry = _sb_tile(qs[hh], kb, carry, causal, suffix, diag)
                    out.append((carry, acc + _dot(w.astype(BF16), vt, NN)))
                return tuple(out)

            start = (jnp.zeros((ATT_TILE, 1), F32), jnp.zeros((ATT_TILE, HEAD_TILE), F32))
            state = step(qi, (start, start), True)
            state = lax.fori_loop(0, qi, lambda jj, st: step(qi - 1 - jj, st, False), state)
            o = jnp.where(lo, state[0][1], state[1][1])
            o_ref[_rows(qi), :] = o
            ob_ref[_rows(qi), :] = o.astype(BF16)
            return 0

        lax.fori_loop(0, nq, qtile, 0)

    spec = pl.BlockSpec((s, HEAD_TILE), lambda p: (0, p))
    return pl.pallas_call(
        body, grid=(nt,), in_specs=[spec, spec, spec], out_specs=[spec, spec],
        out_shape=[jax.ShapeDtypeStruct((s, D_MODEL), F32), jax.ShapeDtypeStruct((s, D_MODEL), BF16)],
        name="sb_attn_fwd", compiler_params=_cp("parallel"))(qn, kn, vb)


def sb_attn_bwd(qn, kn, vb, o, do):
    s = qn.shape[0]
    nq = s // ATT_TILE
    nt = D_MODEL // HEAD_TILE

    def body(q_ref, k_ref, v_ref, o_ref, do_ref, dq_ref, dk_ref, dv_ref):
        suffix, causal, lo = _sb_consts()
        dk_ref[...] = jnp.zeros_like(dk_ref)
        dv_ref[...] = jnp.zeros_like(dv_ref)

        def qtile(qi, _):
            qb = q_ref[_rows(qi), :].astype(F32)
            dob = do_ref[_rows(qi), :].astype(BF16).astype(F32)
            prod = dob * o_ref[_rows(qi), :]
            masks = [lo, ~lo]
            qs = [_head_query(qb, m) for m in masks]
            dos = [jnp.where(m, dob, 0.0).astype(BF16) for m in masks]
            totals = [jnp.sum(jnp.where(m, prod, 0.0), axis=-1, keepdims=True) for m in masks]

            def step(kj, state, diag):
                kb = k_ref[_rows(kj), :]
                vt = v_ref[_rows(kj), :]
                out = []
                dk = jnp.zeros((ATT_TILE, HEAD_TILE), F32)
                dv = jnp.zeros((ATT_TILE, HEAD_TILE), F32)
                for hh in range(2):
                    carry, carry_e, dq = state[hh]
                    lb, w, carry = _sb_tile(qs[hh], kb, carry, causal, suffix, diag)
                    wb = w.astype(BF16)
                    e = _dot(dos[hh], vt, NT) * wb.astype(F32)
                    befores = []
                    for j in reversed(range(ATT_TILE // ATT_BLOCK)):
                        cols = slice(j * ATT_BLOCK, (j + 1) * ATT_BLOCK)
                        befores.append(totals[hh] - carry_e - _split_dot(e[:, cols], suffix))
                        carry_e = carry_e + jnp.sum(e[:, cols], axis=-1, keepdims=True)
                    before = jnp.concatenate(befores[::-1], axis=1)
                    beta = jnp.exp(lb)
                    dz = e * (1.0 - beta) - before * beta
                    if diag:
                        dz = jnp.where(causal, dz, 0.0)
                    dzb = dz.astype(BF16)
                    dq = dq + _dot(dzb, kb, NN)
                    dk = dk + _dot(dzb, qs[hh], TN)
                    dv = dv + _dot(wb, dos[hh], TN)
                    out.append((carry, carry_e, dq))
                dk_ref[_rows(kj), :] += dk
                dv_ref[_rows(kj), :] += dv
                return tuple(out)

            zero = jnp.zeros((ATT_TILE, 1), F32)
            start = (zero, zero, jnp.zeros((ATT_TILE, HEAD_TILE), F32))
            state = step(qi, (start, start), True)
            state = lax.fori_loop(0, qi, lambda jj, st: step(qi - 1 - jj, st, False), state)
            dq_ref[_rows(qi), :] = jnp.where(lo, state[0][2], state[1][2]) * (SB_HEAD_DIM ** -0.5)
            return 0

        lax.fori_loop(0, nq, qtile, 0)

    spec = pl.BlockSpec((s, HEAD_TILE), lambda p: (0, p))
    shape = jax.ShapeDtypeStruct((s, D_MODEL), F32)
    return pl.pallas_call(
        body, grid=(nt,), in_specs=[spec] * 5, out_specs=[spec] * 3, out_shape=[shape] * 3,
        name="sb_attn_bwd", compiler_params=_cp("parallel"))(qn, kn, vb, o, do)


def ada_fwd(c_all, ada_w, ada_b_shard):
    layers, d, cols = ada_w.shape
    tn = 512

    def body(c_ref, w_ref, b_ref, out_ref):
        cv = c_ref[...]
        act = (cv * _sigmoid(cv)).astype(BF16)
        out_ref[...] = _dot(act, w_ref[...].astype(BF16), NN) + b_ref[...]

    return pl.pallas_call(
        body, grid=(layers, cols // tn),
        in_specs=[pl.BlockSpec((N_DEV, d), lambda l, j: (0, 0)), pl.BlockSpec((None, d, tn), lambda l, j: (l, 0, j)),
                  pl.BlockSpec((None, 1, tn), lambda l, j: (l, 0, j))],
        out_specs=pl.BlockSpec((None, N_DEV, tn), lambda l, j: (l, 0, j)),
        out_shape=jax.ShapeDtypeStruct((layers, N_DEV, cols), F32), name="ada_fwd",
        compiler_params=_cp("parallel", "parallel"))(c_all, ada_w, ada_b_shard)


def ada_w_grad(c_all, dmod):
    layers, _, cols = dmod.shape
    d = c_all.shape[1]
    tn = 512

    def body(c_ref, dm_ref, out_ref):
        cv = c_ref[...]
        out_ref[...] = _dot(cv * _sigmoid(cv), dm_ref[...], TN, HIGHEST)

    return pl.pallas_call(
        body, grid=(layers, cols // tn),
        in_specs=[pl.BlockSpec((N_DEV, d), lambda l, j: (0, 0)), pl.BlockSpec((None, N_DEV, tn), lambda l, j: (l, 0, j))],
        out_specs=pl.BlockSpec((None, d, tn), lambda l, j: (l, 0, j)),
        out_shape=jax.ShapeDtypeStruct((layers, d, cols), F32), name="ada_w_grad",
        compiler_params=_cp("parallel", "parallel"))(c_all, dmod)


def _row_tile(r, c, elems):
    best = 8
    for t in range(8, r + 1, 8):
        if r % t == 0 and t * c <= elems:
            best = t
    return best


def sum_rows(x, name):
    n, r, c = x.shape
    tr = _row_tile(r, c, 1 << 17)

    def body(x_ref, out_ref):
        acc = x_ref[0]
        for i in range(1, n):
            acc = acc + x_ref[i]
        out_ref[...] = acc

    return pl.pallas_call(
        body, grid=(r // tr,), in_specs=[pl.BlockSpec((n, tr, c), lambda i: (0, i, 0))],
        out_specs=pl.BlockSpec((tr, c), lambda i: (i, 0)), out_shape=jax.ShapeDtypeStruct((r, c), F32), name=name,
        compiler_params=_cp("parallel"))(x)


def adamw(w, g, m, v, name):
    r, c = w.shape
    tr = _row_tile(r, c, 1 << 17)
    c1 = 1.0 - ADAM_B1 ** ADAM_STEP
    c2 = 1.0 - ADAM_B2 ** ADAM_STEP

    def body(w_ref, g_ref, m_ref, v_ref, d_ref, nm_ref, nv_ref):
        gv = g_ref[...]
        nm = ADAM_B1 * m_ref[...] + (1.0 - ADAM_B1) * gv
        nv = ADAM_B2 * v_ref[...] + (1.0 - ADAM_B2) * (gv * gv)
        d_ref[...] = -ADAM_LR * ((nm / c1) / (jnp.sqrt(nv / c2) + ADAM_EPS) + ADAM_WD * w_ref[...])
        nm_ref[...] = nm
        nv_ref[...] = nv

    spec = pl.BlockSpec((tr, c), lambda i: (i, 0))
    shape = jax.ShapeDtypeStruct((r, c), F32)
    return pl.pallas_call(
        body, grid=(r // tr,), in_specs=[spec] * 4, out_specs=[spec] * 3, out_shape=[shape] * 3, name=name,
        compiler_params=_cp("parallel"))(w, g, m, v)


def _me():
    return lax.axis_index("x"), lax.axis_index("y"), lax.axis_index("c")


def _flip(v, bit):
    return 1 - v if bit else v


HBM = pl.BlockSpec(memory_space=pl.ANY)
VMEM = pl.BlockSpec(memory_space=pltpu.VMEM)


def all_gather_rows(v, name):
    n = v.shape[1]

    def body(v_ref, out_ref, send_sems, recv_sems):
        x, y, c = _me()
        me = 4 * x + 2 * y + c
        out_ref[pl.ds(me, 1), :] = v_ref[...]
        copies = []
        for j in range(1, N_DEV):
            peer = (_flip(x, j & 4), _flip(y, j & 2), _flip(c, j & 1))
            copies.append(pltpu.make_async_remote_copy(
                src_ref=v_ref, dst_ref=out_ref.at[pl.ds(me, 1), :], send_sem=send_sems.at[j - 1],
                recv_sem=recv_sems.at[j - 1], device_id=peer, device_id_type=MESH))
        for cp in copies:
            cp.start()
        for cp in copies:
            cp.wait()

    return pl.pallas_call(
        body, in_specs=[VMEM], out_specs=VMEM, out_shape=jax.ShapeDtypeStruct((N_DEV, n), F32),
        scratch_shapes=[pltpu.SemaphoreType.DMA((N_DEV - 1,)), pltpu.SemaphoreType.DMA((N_DEV - 1,))], name=name)(v)


def _chip_peers(x, y):
    return [((1 - x, y), 2 * (1 - x) + y), ((x, 1 - y), 2 * x + (1 - y)), ((1 - x, 1 - y), 2 * (1 - x) + (1 - y))]


def chip_all_gather(shards, name):
    n = len(shards)

    def body(*refs):
        ins, outs = refs[:n], refs[n:2 * n]
        local_sem, send_sems, recv_sems = refs[2 * n:]
        x, y, c = _me()
        mine = 2 * x + y
        started = []
        for a in range(n):
            cp = pltpu.make_async_copy(ins[a], outs[a].at[mine], local_sem.at[a])
            cp.start()
            started.append(cp)
            for p, (chip, _) in enumerate(_chip_peers(x, y)):
                cp = pltpu.make_async_remote_copy(
                    src_ref=ins[a], dst_ref=outs[a].at[mine], send_sem=send_sems.at[a, p], recv_sem=recv_sems.at[a, p],
                    device_id=(*chip, c), device_id_type=MESH)
                cp.start()
                started.append(cp)
        for cp in started:
            cp.wait()

    return pl.pallas_call(
        body, in_specs=[HBM] * n, out_specs=[HBM] * n,
        out_shape=[jax.ShapeDtypeStruct((N_CHIPS, *s.shape), s.dtype) for s in shards],
        scratch_shapes=[pltpu.SemaphoreType.DMA((n,)), pltpu.SemaphoreType.DMA((n, 3)), pltpu.SemaphoreType.DMA((n, 3))],
        name=name)(*shards)


def sibling_split(grads, name):
    n = len(grads)

    def body(*refs):
        ins, got = refs[:n], refs[n:2 * n]
        send_sems, recv_sems = refs[2 * n:]
        x, y, c = _me()
        started = []
        for a in range(n):
            half = ins[a].shape[1] // 2
            give = pltpu.make_async_remote_copy(
                src_ref=ins[a].at[:, pl.ds((1 - c) * half, half), :], dst_ref=got[a], send_sem=send_sems.at[a],
                recv_sem=recv_sems.at[a], device_id=(x, y, 1 - c), device_id_type=MESH)
            give.start()
            started.append(give)
        for cp in started:
            cp.wait()

    return pl.pallas_call(
        body, in_specs=[HBM] * n, out_specs=[HBM] * n,
        out_shape=[jax.ShapeDtypeStruct((g.shape[0], g.shape[1] // 2, g.shape[2]), g.dtype) for g in grads],
        scratch_shapes=[pltpu.SemaphoreType.DMA((n,)), pltpu.SemaphoreType.DMA((n,))], name=name)(*grads)


def pair_sum(g, got, name):
    k, half, c = got.shape
    tr = _row_tile(half, c, 1 << 18)
    nb = half // tr

    def body(g_ref, got_ref, out_ref):
        out_ref[...] = (g_ref[...] + got_ref[...]).astype(BF16)

    spec = pl.BlockSpec((None, tr, c), lambda j, i: (j, i, 0))
    return pl.pallas_call(
        body, grid=(k, nb),
        in_specs=[pl.BlockSpec((None, tr, c), lambda j, i: (j, lax.axis_index("c") * nb + i, 0)), spec],
        out_specs=spec, out_shape=jax.ShapeDtypeStruct(got.shape, BF16), name=name,
        compiler_params=_cp("parallel", "parallel"))(g, got)


def chip_scatter(parts, name):
    n = len(parts)

    def body(*refs):
        ins, outs = refs[:n], refs[n:2 * n]
        send_sems, recv_sems = refs[2 * n:]
        x, y, c = _me()
        started = []
        for a in range(n):
            for p, (chip, slot) in enumerate(_chip_peers(x, y)):
                cp = pltpu.make_async_remote_copy(
                    src_ref=ins[a].at[slot], dst_ref=outs[a].at[p], send_sem=send_sems.at[a, p],
                    recv_sem=recv_sems.at[a, p], device_id=(*chip, c), device_id_type=MESH)
                cp.start()
                started.append(cp)
        for cp in started:
            cp.wait()

    return pl.pallas_call(
        body, in_specs=[HBM] * n, out_specs=[HBM] * n,
        out_shape=[jax.ShapeDtypeStruct((3, *p.shape[1:]), p.dtype) for p in parts],
        scratch_shapes=[pltpu.SemaphoreType.DMA((n, 3)), pltpu.SemaphoreType.DMA((n, 3))], name=name)(*parts)


def chip_sum(part, landed, name):
    _, half, c = landed.shape
    tr = _row_tile(half, c, 1 << 17)
    nb = half // tr

    def body(part_ref, landed_ref, out_ref):
        up = lambda v: v.astype(F32)
        out_ref[...] = ((up(part_ref[...]) + up(landed_ref[0])) + up(landed_ref[1])) + up(landed_ref[2])

    return pl.pallas_call(
        body, grid=(nb,),
        in_specs=[pl.BlockSpec((None, tr, c), lambda i: (2 * lax.axis_index("x") + lax.axis_index("y"), i, 0)),
                  pl.BlockSpec((3, tr, c), lambda i: (0, i, 0))],
        out_specs=pl.BlockSpec((tr, c), lambda i: (lax.axis_index("c") * nb + i, 0)),
        out_shape=jax.ShapeDtypeStruct((2 * half, c), F32), name=name, compiler_params=_cp("parallel"))(part, landed)


def sibling_join(arrays, name):
    n = len(arrays)

    def body(*refs):
        ins, outs = refs[:n], refs[n:2 * n]
        send_sems, recv_sems = refs[2 * n:]
        x, y, c = _me()
        started = []
        for a in range(n):
            half = ins[a].shape[0] // 2
            give = pltpu.make_async_remote_copy(
                src_ref=ins[a].at[pl.ds(c * half, half), :], dst_ref=outs[a].at[pl.ds(c * half, half), :],
                send_sem=send_sems.at[a], recv_sem=recv_sems.at[a], device_id=(x, y, 1 - c), device_id_type=MESH)
            give.start()
            started.append(give)
        for cp in started:
            cp.wait()

    return pl.pallas_call(
        body, in_specs=[HBM] * n, out_specs=[HBM] * n,
        out_shape=[jax.ShapeDtypeStruct(h.shape, h.dtype) for h in arrays],
        input_output_aliases={a: a for a in range(n)},
        scratch_shapes=[pltpu.SemaphoreType.DMA((n,)), pltpu.SemaphoreType.DMA((n,))], name=name)(*arrays)


def reduce_to_owner(grads):
    got = sibling_split(grads, "grad_sibling_split")
    parts = [pair_sum(g, h, f"grad_pair_sum_{a}") for a, (g, h) in enumerate(zip(grads, got))]
    landed = chip_scatter(parts, "grad_chip_scatter")
    halves = [chip_sum(p, l, f"grad_chip_sum_{a}") for a, (p, l) in enumerate(zip(parts, landed))]
    return sibling_join(halves, "grad_sibling_join")


def _pad_row(v, n):
    return jnp.pad(v.reshape(1, -1), ((0, 0), (0, n - v.size)))


def local_step(x, target, mod, wts):
    d = D_MODEL
    row = lambda v: v.reshape(1, -1)
    mods = [[row(mod[l, i * d:(i + 1) * d]) for i in range(6)] for l in range(2)]
    saved = []
    for l in range(2):
        shift1, scale1, gate1, shift2, scale2, gate2 = mods[l]
        g_mix, g_mlp = row(wts["norm_mix"][l]), row(wts["norm_mlp"][l])
        h = norm_mod_fwd(x, g_mix, scale1, shift1, f"norm_mix_fwd_{l}")
        if l == 0:
            u = matmul_nn_chunked(h, wts["w_in"], F32, "in_proj_ab")[0]
            y_a = conv_mixer_fwd(u, wts["conv_w"])
            y_b, o_raw, states = hgrn_fwd(u, wts["lb_logits"], wts["hg_norm"])
            mix = jnp.concatenate([y_a, y_b], axis=1)
            y, x1 = proj_residual(mix, wts["w_out_ab"], x, gate1, "out_proj_ab")
            ctx = (u, o_raw, states)
        else:
            qkv = matmul_nn_chunked(h, wts["w_qkv"], F32, "in_proj_c")[0]
            qn, kn, vb = qk_norm_fwd(qkv, wts["qg"], wts["kg"])
            o, mix = sb_attn_fwd(qn, kn, vb)
            y, x1 = proj_residual(mix, wts["w_out_c"], x, gate1, "out_proj_c")
            ctx = (qkv, qn, kn, vb, o)
        h2 = norm_mod_fwd(x1, g_mlp, scale2, shift2, f"norm_mlp_fwd_{l}")
        act, r = mlp_up(h2, wts["w1"][l], f"mlp_up_{l}")
        y2, x2 = proj_residual(act, wts["w2"][l], x1, gate2, f"mlp_down_{l}")
        saved.append((x, h, mix, y, x1, h2, act, r, y2, ctx))
        x = x2

    dx, loss_row = loss_and_grad(x, target)
    small, big = {}, {}
    dmod = [None, None]
    d_norm_mix, d_norm_mlp = [None, None], [None, None]
    for l in (1, 0):
        shift1, scale1, gate1, shift2, scale2, gate2 = mods[l]
        g_mix, g_mlp = row(wts["norm_mix"][l]), row(wts["norm_mlp"][l])
        x0, h, mix, y, x1, h2, act, r, y2, ctx = saved[l]
        dy2, dgate2 = gate_bwd(dx, y2, gate2, f"mlp_gate_bwd_{l}")
        dz = mlp_down_bwd(dy2, wts["w2"][l], r, f"mlp_down_bwd_{l}")
        big[f"w2_{l}"] = matmul_tn_plain(act, dy2, f"mlp_w2_grad_{l}")
        dh2 = matmul_nt_chunked(dz, wts["w1"][l], f"mlp_up_bwd_{l}")
        big[f"w1_{l}"] = matmul_tn_chunked(h2, dz, f"mlp_w1_grad_{l}")
        dx1, d_norm_mlp[l], dscale2, dshift2 = norm_mod_bwd(dh2, x1, g_mlp, scale2, dx, f"norm_mlp_bwd_{l}")
        dy, dgate1 = gate_bwd(dx1, y, gate1, f"mix_gate_bwd_{l}")
        if l == 0:
            u, o_raw, states = ctx
            dmix = matmul_nt_plain(dy, wts["w_out_ab"], "out_proj_ab_bwd")
            big["w_out_ab"] = matmul_tn_plain(mix, dy, "w_out_ab_grad")
            dab, dac, dah, small["conv_w"] = conv_mixer_bwd(dmix, u, wts["conv_w"])
            dhq, dhf, dhi, dhg, small["hg_norm"], small["lb_logits"] = hgrn_bwd(
                dmix, u, o_raw, states, wts["lb_logits"], wts["hg_norm"])
            du = jnp.concatenate([dab, dac, dah, dhq, dhf, dhi, dhg], axis=1)
            dh = matmul_nt_chunked(du, wts["w_in"], "in_proj_ab_bwd")
            big["w_in"] = matmul_tn_chunked(h, du, "w_in_grad")
        else:
            qkv, qn, kn, vb, o = ctx
            do = matmul_nt_plain(dy, wts["w_out_c"], "out_proj_c_bwd")
            big["w_out_c"] = matmul_tn_plain(mix, dy, "w_out_c_grad")
            dqn, dkn, dv = sb_attn_bwd(qn, kn, vb, o, do)
            dq, dk, dvb, dqg, dkg = qk_norm_bwd(dqn, dkn, dv, qkv, wts["qg"], wts["kg"])
            small["q_norm"] = dqg[:, :SB_HEAD_DIM] + dqg[:, SB_HEAD_DIM:]
            small["k_norm"] = dkg[:, :SB_HEAD_DIM] + dkg[:, SB_HEAD_DIM:]
            dqkv = jnp.concatenate([dq, dk, dvb], axis=1)
            dh = matmul_nt_chunked(dqkv, wts["w_qkv"], "in_proj_c_bwd")
            big["w_qkv"] = matmul_tn_chunked(h, dqkv, "w_qkv_grad")
        dx, d_norm_mix[l], dscale1, dshift1 = norm_mod_bwd(dh, x0, g_mix, scale1, dx1, f"norm_mix_bwd_{l}")
        dmod[l] = jnp.concatenate([dshift1, dscale1, dgate1, dshift2, dscale2, dgate2], axis=1)
    small["mod"] = jnp.concatenate(dmod, axis=0)
    small["norm_mix"] = jnp.concatenate(d_norm_mix, axis=0)
    small["norm_mlp"] = jnp.concatenate(d_norm_mlp, axis=0)
    return loss_row, dx, small, big


SMALL_ORDER = ("mod", "norm_mix", "norm_mlp", "conv_w", "hg_norm", "lb_logits", "q_norm", "k_norm")


def kernel(x, c, ada_w, ada_b, norm_mix, norm_mlp, w_in_ab, conv_w, hg_norm, lb_logits, w_out_ab, w_qkv, q_norm, k_norm, w_out_c, mlp_w1, mlp_w2, loss_target, m_ada_w, m_ada_b, m_norm_mix, m_norm_mlp, m_w_in_ab, m_conv_w, m_hg_norm, m_lb_logits, m_w_out_ab, m_w_qkv, m_q_norm, m_k_norm, m_w_out_c, m_mlp_w1, m_mlp_w2, v_ada_w, v_ada_b, v_norm_mix, v_norm_mlp, v_w_in_ab, v_conv_w, v_hg_norm, v_lb_logits, v_w_out_ab, v_w_qkv, v_q_norm, v_k_norm, v_w_out_c, v_mlp_w1, v_mlp_w2):
    d = D_MODEL
    ax, ay, ac = _me()
    chip = 2 * ax + ay
    dev = 2 * chip + ac
    cols = ada_w.shape[2]

    first = all_gather_rows(_pad_row(jnp.concatenate([c.reshape(-1), conv_w.reshape(-1)]), 1536), "gather_cond")
    c_all = first[:, :d]
    conv_full = first[::2, d:d + 3 * HEAD_TILE].reshape(N_CHIPS, 3, HEAD_TILE).transpose(1, 0, 2).reshape(3, CONV_DIM)
    ada_b_shard = lax.dynamic_slice(ada_b, (0, chip * cols), (2, cols)).reshape(2, 1, cols)
    mod_cols = ada_fwd(c_all, ada_w, ada_b_shard)
    mod_all = all_gather_rows(mod_cols.reshape(1, -1), "gather_mod").reshape(N_DEV, 2, N_DEV, cols)
    mod = lax.dynamic_index_in_dim(mod_all[::2], dev, axis=2, keepdims=False).transpose(1, 0, 2).reshape(2, 6 * d)

    shards = [w_in_ab[0], w_out_ab[0], w_qkv[0], w_out_c[0], mlp_w1[0], mlp_w1[1], mlp_w2[0], mlp_w2[1]]
    g_in, g_out_ab, g_qkv, g_out_c, g_w1a, g_w1b, g_w2a, g_w2b = chip_all_gather(
        [s.astype(BF16) for s in shards], "gather_weights")
    wts = dict(
        norm_mix=norm_mix, norm_mlp=norm_mlp, w_in=g_in, conv_w=conv_full, hg_norm=hg_norm, lb_logits=lb_logits,
        w_out_ab=g_out_ab.reshape(d, d), w_qkv=g_qkv, qg=jnp.tile(q_norm, (1, 2)), kg=jnp.tile(k_norm, (1, 2)),
        w_out_c=g_out_c.reshape(d, d), w1=[g_w1a, g_w1b], w2=[g_w2a.reshape(D_FF, d), g_w2b.reshape(D_FF, d)])

    loss_row, grad_x, small, big = local_step(x[0], loss_target[0], mod, wts)

    flat = jnp.concatenate([small[k].reshape(-1) for k in SMALL_ORDER] + [loss_row[0, :1]])
    n_small = -(-flat.size // 1024) * 1024
    gathered = all_gather_rows(_pad_row(flat, n_small), "gather_small")
    total = sum_rows(gathered.reshape(N_DEV, 8, n_small // 8), "small_sum").reshape(-1)
    sizes = [small[k].size for k in SMALL_ORDER]
    offs = [sum(sizes[:i]) for i in range(len(sizes) + 1)]
    tot = {k: total[offs[i]:offs[i + 1]].reshape(small[k].shape) for i, k in enumerate(SMALL_ORDER)}
    loss = total[offs[-1]]
    mod_rows = gathered[:, :2 * 6 * d].reshape(N_DEV, 2, 6 * d)
    dmod_cols = lax.dynamic_slice(mod_rows, (0, 0, chip * cols), (N_DEV, 2, cols)).transpose(1, 0, 2)
    g_ada_w = ada_w_grad(c_all, dmod_cols)

    as_chunks = lambda g: g.reshape(N_CHIPS, g.shape[0] // N_CHIPS, g.shape[1])
    names = ["w_in", "w_out_ab", "w_qkv", "w_out_c", "w1_0", "w1_1", "w2_0", "w2_1"]
    chunked = [big[k] if big[k].ndim == 3 else as_chunks(big[k]) for k in names]
    r_in, r_out_ab, r_qkv, r_out_c, r_w1a, r_w1b, r_w2a, r_w2b = reduce_to_owner(chunked)

    grads = dict(
        ada_w=g_ada_w, ada_b=tot["mod"], norm_mix=tot["norm_mix"], norm_mlp=tot["norm_mlp"], w_in_ab=r_in[None],
        conv_w=lax.dynamic_slice(tot["conv_w"], (0, chip * HEAD_TILE), (3, HEAD_TILE))[None], hg_norm=tot["hg_norm"],
        lb_logits=tot["lb_logits"], w_out_ab=r_out_ab[None], w_qkv=r_qkv[None], q_norm=tot["q_norm"],
        k_norm=tot["k_norm"], w_out_c=r_out_c[None], mlp_w1=jnp.stack([r_w1a, r_w1b]), mlp_w2=jnp.stack([r_w2a, r_w2b]))
    weights = dict(ada_w=ada_w, ada_b=ada_b, norm_mix=norm_mix, norm_mlp=norm_mlp, w_in_ab=w_in_ab, conv_w=conv_w,
                   hg_norm=hg_norm, lb_logits=lb_logits, w_out_ab=w_out_ab, w_qkv=w_qkv, q_norm=q_norm, k_norm=k_norm,
                   w_out_c=w_out_c, mlp_w1=mlp_w1, mlp_w2=mlp_w2)
    m_in = dict(ada_w=m_ada_w, ada_b=m_ada_b, norm_mix=m_norm_mix, norm_mlp=m_norm_mlp, w_in_ab=m_w_in_ab,
                conv_w=m_conv_w, hg_norm=m_hg_norm, lb_logits=m_lb_logits, w_out_ab=m_w_out_ab, w_qkv=m_w_qkv,
                q_norm=m_q_norm, k_norm=m_k_norm, w_out_c=m_w_out_c, mlp_w1=m_mlp_w1, mlp_w2=m_mlp_w2)
    v_in = dict(ada_w=v_ada_w, ada_b=v_ada_b, norm_mix=v_norm_mix, norm_mlp=v_norm_mlp, w_in_ab=v_w_in_ab,
                conv_w=v_conv_w, hg_norm=v_hg_norm, lb_logits=v_lb_logits, w_out_ab=v_w_out_ab, w_qkv=v_w_qkv,
                q_norm=v_q_norm, k_norm=v_k_norm, w_out_c=v_w_out_c, mlp_w1=v_mlp_w1, mlp_w2=v_mlp_w2)
    order = list(weights)
    large = ("ada_w", "w_in_ab", "w_out_ab", "w_qkv", "w_out_c", "mlp_w1", "mlp_w2")
    delta, new_m, new_v = {}, {}, {}
    for k in large:
        shape = weights[k].shape
        flat2 = lambda a: a.reshape(-1, shape[-1])
        dl, nm, nv = adamw(flat2(weights[k]), flat2(grads[k]), flat2(m_in[k]), flat2(v_in[k]), f"adamw_{k}")
        delta[k], new_m[k], new_v[k] = dl.reshape(shape), nm.reshape(shape), nv.reshape(shape)
    rest = [k for k in order if k not in large]
    n_rest = -(-sum(weights[k].size for k in rest) // 1024) * 1024
    pack = lambda tree: _pad_row(jnp.concatenate([tree[k].reshape(-1) for k in rest]), n_rest).reshape(8, n_rest // 8)
    dl, nm, nv = adamw(pack(weights), pack(grads), pack(m_in), pack(v_in), "adamw_small")
    off = 0
    for k in rest:
        size, shape = weights[k].size, weights[k].shape
        delta[k], new_m[k], new_v[k] = (a.reshape(-1)[off:off + size].reshape(shape) for a in (dl, nm, nv))
        off += size
    grads = {k: grads[k].reshape(weights[k].shape) for k in order}
    return (loss, grad_x[None], *[grads[k] for k in order], *[delta[k] for k in order],
            *[new_m[k] for k in order], *[new_v[k] for k in order])
```

```python
import functools

import jax
import jax.numpy as jnp
from jax import lax
from jax.experimental import pallas as pl
from jax.experimental.pallas import tpu as pltpu

F32 = jnp.float32
BF16 = jnp.bfloat16
HIGHEST = lax.Precision.HIGHEST
MESH = pl.DeviceIdType.MESH

D_MODEL = 1024
D_FF = 4096
CHUNK = 64
HEAD_TILE = 128
SB_HEAD_DIM = 64
CONV_DIM = 512
HG_DIM = 512
AB_IN = 3584
N_CHIPS = 4
N_DEV = 8
EPS = 1e-6
ATT_BLOCK = 128
ATT_TILE = 512
HG_FWD = (512, 1)
HG_BWD = (128, 4)

ADAM_LR = 0.001
ADAM_B1 = 0.9
ADAM_B2 = 0.999
ADAM_EPS = 1e-08
ADAM_WD = 0.01
ADAM_STEP = 10

NN = (((1,), (0,)), ((), ()))
NT = (((1,), (1,)), ((), ()))
TN = (((0,), (0,)), ((), ()))


def _cp(*dims):
    return pltpu.CompilerParams(dimension_semantics=dims) if dims else pltpu.CompilerParams()


def _dot(a, b, dn, precision=None):
    return lax.dot_general(a, b, dn, preferred_element_type=F32, precision=precision)


def _sigmoid(z):
    return 1.0 / (1.0 + jnp.exp(-z))


def _matmul(a, b, *, dn, grid, a_spec, b_spec, acc_shape, epilogue, extras=(), extra_specs=(),
            out_shapes, out_specs, name):
    nk = grid[2]
    n_extra = len(extras)
    n_out = len(out_shapes)

    def body(*refs):
        a_ref, b_ref = refs[0], refs[1]
        extra_refs = refs[2:2 + n_extra]
        out_refs = refs[2 + n_extra:2 + n_extra + n_out]
        acc_ref = refs[-1]
        k = pl.program_id(2)
        part = _dot(a_ref[...], b_ref[...], dn)

        if nk == 1:
            epilogue(part, None, extra_refs, out_refs)
        else:
            @pl.when(k == 0)
            def _():
                acc_ref[...] = part

            @pl.when(k > 0)
            def _():
                acc_ref[...] += part

            @pl.when(k == nk - 1)
            def _():
                epilogue(acc_ref[...], None, extra_refs, out_refs)

    return pl.pallas_call(
        body, grid=grid, in_specs=[a_spec, b_spec, *extra_specs], out_specs=out_specs, out_shape=out_shapes,
        scratch_shapes=[pltpu.VMEM(acc_shape, F32)], name=name,
        compiler_params=_cp("parallel", "parallel", "arbitrary"))(a, b, *extras)


def _cols(ref, cols):
    return ref.at[:, cols] if cols is not None else ref


def _store(dtype):
    def epilogue(acc, cols, extra_refs, out_refs):
        _cols(out_refs[0], cols)[...] = acc.astype(dtype)
    return epilogue


def _tok_tile(s):
    return min(512, s)


def _matmul_resident(a, w, *, dn, blocks, accumulate, epilogue, extras=(), extra_specs=(), out_shapes, out_specs,
                     name):
    s, ka = a.shape
    tm = _tok_tile(s)
    n_extra = len(extras)

    def body(*refs):
        a_ref, w_ref = refs[0], refs[1]
        extra_refs = refs[2:2 + n_extra]
        out_refs = refs[2 + n_extra:]
        acc = None
        for w_index, a_cols, out_cols in blocks:
            part = _dot(_cols(a_ref, a_cols)[...], w_ref[w_index], dn)
            if accumulate:
                acc = part if acc is None else acc + part
            else:
                epilogue(part, out_cols, extra_refs, out_refs)
        if accumulate:
            epilogue(acc, None, extra_refs, out_refs)

    return pl.pallas_call(
        body, grid=(s // tm,),
        in_specs=[pl.BlockSpec((tm, ka), lambda i: (i, 0)), pl.BlockSpec(w.shape, lambda i: (0,) * w.ndim),
                  *extra_specs],
        out_specs=out_specs, out_shape=out_shapes, name=name, compiler_params=_cp("parallel"))(a, w, *extras)


def _col_blocks(n_blocks, width):
    return [slice(j * width, (j + 1) * width) for j in range(n_blocks)]


def matmul_nn_chunked(a, w, out_dtype, name, epilogue=None, out_shapes=None):
    s = a.shape[0]
    n4 = w.shape[2]
    tm = _tok_tile(s)
    if out_shapes is None:
        out_shapes = [jax.ShapeDtypeStruct((s, N_CHIPS * n4), out_dtype)]
        epilogue = _store(out_dtype)
    return _matmul_resident(
        a, w, dn=NN, blocks=[(j, None, cols) for j, cols in enumerate(_col_blocks(N_CHIPS, n4))], accumulate=False,
        epilogue=epilogue, out_shapes=out_shapes,
        out_specs=[pl.BlockSpec((tm, N_CHIPS * n4), lambda i: (i, 0))] * len(out_shapes), name=name)


def matmul_nt_chunked(dy, w, name):
    s = dy.shape[0]
    _, k, n4 = w.shape
    tm = _tok_tile(s)
    return _matmul_resident(
        dy, w, dn=NT, blocks=[(j, cols, None) for j, cols in enumerate(_col_blocks(N_CHIPS, n4))], accumulate=True,
        epilogue=_store(F32), out_shapes=[jax.ShapeDtypeStruct((s, k), F32)],
        out_specs=[pl.BlockSpec((tm, k), lambda i: (i, 0))], name=name)[0]


def matmul_tn_chunked(x, dy, name):
    s, k = x.shape
    n = dy.shape[1]
    n4 = n // N_CHIPS
    ts = _tok_tile(s)

    def body(x_ref, dy_ref, out_ref):
        xt = x_ref[...].T
        first = pl.program_id(0) == 0
        for j, cols in enumerate(_col_blocks(N_CHIPS, n4)):
            part = _dot(xt, dy_ref[:, cols], NN)

            @pl.when(first)
            def _():
                out_ref[j] = part

            @pl.when(jnp.logical_not(first))
            def _():
                out_ref[j] += part

    return pl.pallas_call(
        body, grid=(s // ts,),
        in_specs=[pl.BlockSpec((ts, k), lambda i: (i, 0)), pl.BlockSpec((ts, n), lambda i: (i, 0))],
        out_specs=pl.BlockSpec((N_CHIPS, k, n4), lambda i: (0, 0, 0)),
        out_shape=jax.ShapeDtypeStruct((N_CHIPS, k, n4), F32), name=name, compiler_params=_cp("arbitrary"))(x, dy)


def matmul_nn_plain(a, w, name, epilogue, extras, extra_specs, out_shapes, out_specs):
    return _matmul_resident(
        a, w, dn=NN, blocks=[((slice(None), slice(None)), None, None)], accumulate=True, epilogue=epilogue,
        extras=extras, extra_specs=extra_specs, out_shapes=out_shapes, out_specs=out_specs, name=name)


def matmul_nt_plain(dy, w, name, epilogue=None, extras=(), extra_specs=(), out_dtype=F32, tn=1024):
    s = dy.shape[0]
    k = w.shape[0]
    tm = _tok_tile(s)
    return _matmul_resident(
        dy, w, dn=NT, blocks=[((cols, slice(None)), None, cols) for cols in _col_blocks(k // tn, tn)],
        accumulate=False, epilogue=epilogue or _store(out_dtype), extras=extras, extra_specs=extra_specs,
        out_shapes=[jax.ShapeDtypeStruct((s, k), out_dtype)], out_specs=[pl.BlockSpec((tm, k), lambda i: (i, 0))],
        name=name)[0]


def matmul_tn_plain(x, dy, name, tk=1024):
    s, k = x.shape
    n = dy.shape[1]
    ts = _tok_tile(s)
    return _matmul(
        x, dy, dn=TN, grid=(k // tk, 1, s // ts),
        a_spec=pl.BlockSpec((ts, tk), lambda i, j, kk: (kk, i)),
        b_spec=pl.BlockSpec((ts, n), lambda i, j, kk: (kk, 0)),
        acc_shape=(tk, n), epilogue=_store(F32),
        out_shapes=[jax.ShapeDtypeStruct((k, n), F32)], out_specs=[pl.BlockSpec((tk, n), lambda i, j, kk: (i, 0))],
        name=name)[0]


def _row_spec(n):
    return pl.BlockSpec((1, n), lambda i: (0, 0))


def norm_mod_fwd(x, g, scale, shift, name):
    s, d = x.shape
    tm = _tok_tile(s)

    def body(x_ref, g_ref, sc_ref, sh_ref, h_ref):
        xv = x_ref[...]
        r = lax.rsqrt(jnp.mean(xv * xv, axis=-1, keepdims=True) + EPS)
        h_ref[...] = ((xv * r * g_ref[...]) * (1.0 + sc_ref[...]) + sh_ref[...]).astype(BF16)

    tile = pl.BlockSpec((tm, d), lambda i: (i, 0))
    return pl.pallas_call(
        body, grid=(s // tm,), in_specs=[tile, _row_spec(d), _row_spec(d), _row_spec(d)], out_specs=tile,
        out_shape=jax.ShapeDtypeStruct((s, d), BF16), name=name, compiler_params=_cp("parallel"))(x, g, scale, shift)


def norm_mod_bwd(dh, x, g, scale, dres, name):
    s, d = x.shape
    tm = _tok_tile(s)

    def body(dh_ref, x_ref, g_ref, sc_ref, dres_ref, dx_ref, dg_ref, dsc_ref, dsh_ref):
        @pl.when(pl.program_id(0) == 0)
        def _():
            dg_ref[...] = jnp.zeros_like(dg_ref)
            dsc_ref[...] = jnp.zeros_like(dsc_ref)
            dsh_ref[...] = jnp.zeros_like(dsh_ref)

        xv = x_ref[...]
        dhv = dh_ref[...]
        r = lax.rsqrt(jnp.mean(xv * xv, axis=-1, keepdims=True) + EPS)
        xn = xv * r
        gv = g_ref[...]
        s1 = 1.0 + sc_ref[...]
        dsh_ref[...] += jnp.sum(dhv, axis=0, keepdims=True)
        dsc_ref[...] += jnp.sum(dhv * xn * gv, axis=0, keepdims=True)
        dg_ref[...] += jnp.sum(dhv * xn * s1, axis=0, keepdims=True)
        dxn = dhv * gv * s1
        dx_ref[...] = dres_ref[...] + r * (dxn - xn * jnp.mean(dxn * xn, axis=-1, keepdims=True))

    tile = pl.BlockSpec((tm, d), lambda i: (i, 0))
    row = jax.ShapeDtypeStruct((1, d), F32)
    return pl.pallas_call(
        body, grid=(s // tm,), in_specs=[tile, tile, _row_spec(d), _row_spec(d), tile],
        out_specs=[tile, _row_spec(d), _row_spec(d), _row_spec(d)],
        out_shape=[jax.ShapeDtypeStruct((s, d), F32), row, row, row], name=name,
        compiler_params=_cp("arbitrary"))(dh, x, g, scale, dres)


def gate_bwd(dx, y, gate, name):
    s, d = dx.shape
    tm = _tok_tile(s)

    def body(dx_ref, y_ref, gate_ref, dy_ref, dgate_ref):
        @pl.when(pl.program_id(0) == 0)
        def _():
            dgate_ref[...] = jnp.zeros_like(dgate_ref)

        dxv = dx_ref[...]
        dy_ref[...] = (gate_ref[...] * dxv).astype(BF16)
        dgate_ref[...] += jnp.sum(dxv * y_ref[...], axis=0, keepdims=True)

    tile = pl.BlockSpec((tm, d), lambda i: (i, 0))
    return pl.pallas_call(
        body, grid=(s // tm,), in_specs=[tile, tile, _row_spec(d)], out_specs=[tile, _row_spec(d)],
        out_shape=[jax.ShapeDtypeStruct((s, d), BF16), jax.ShapeDtypeStruct((1, d), F32)], name=name,
        compiler_params=_cp("arbitrary"))(dx, y, gate)


def loss_and_grad(y, target):
    s, d = y.shape
    tm = _tok_tile(s)

    def body(y_ref, t_ref, dy_ref, loss_ref):
        @pl.when(pl.program_id(0) == 0)
        def _():
            loss_ref[...] = jnp.zeros_like(loss_ref)

        err = y_ref[...] - t_ref[...]
        dy_ref[...] = err * (1.0 / d)
        loss_ref[...] += jnp.sum(err * err) * (0.5 / d)

    tile = pl.BlockSpec((tm, d), lambda i: (i, 0))
    return pl.pallas_call(
        body, grid=(s // tm,), in_specs=[tile, tile], out_specs=[tile, _row_spec(128)],
        out_shape=[jax.ShapeDtypeStruct((s, d), F32), jax.ShapeDtypeStruct((1, 128), F32)], name="loss_and_grad",
        compiler_params=_cp("arbitrary"))(y, target)


def _proj_residual(acc, cols, extra_refs, out_refs):
    x_ref, gate_ref = extra_refs
    out_refs[0][...] = acc
    out_refs[1][...] = x_ref[...] + gate_ref[...] * acc


def proj_residual(a, w, x, gate, name):
    s, d = x.shape
    tm = _tok_tile(s)
    tile = pl.BlockSpec((tm, d), lambda i: (i, 0))
    shape = jax.ShapeDtypeStruct((s, d), F32)
    return matmul_nn_plain(
        a, w, name, _proj_residual, extras=(x, gate), extra_specs=(tile, _row_spec(d)),
        out_shapes=[shape, shape], out_specs=[tile, tile])


def _mlp_up(acc, cols, extra_refs, out_refs):
    r = jnp.maximum(acc, 0.0)
    _cols(out_refs[0], cols)[...] = (r * r).astype(BF16)
    _cols(out_refs[1], cols)[...] = r.astype(BF16)


def mlp_up(h, w1, name):
    shape = jax.ShapeDtypeStruct((h.shape[0], N_CHIPS * w1.shape[2]), BF16)
    return matmul_nn_chunked(h, w1, BF16, name, epilogue=_mlp_up, out_shapes=[shape, shape])


def _dact(acc, cols, extra_refs, out_refs):
    _cols(out_refs[0], cols)[...] = (acc * (2.0 * _cols(extra_refs[0], cols)[...].astype(F32))).astype(BF16)


def mlp_down_bwd(dy, w2, r, name):
    s = dy.shape[0]
    tm = _tok_tile(s)
    return matmul_nt_plain(dy, w2, name, epilogue=_dact, extras=(r,),
                           extra_specs=(pl.BlockSpec((tm, r.shape[1]), lambda i: (i, 0)),), out_dtype=BF16)


def _shift_down(p, n, row):
    return jnp.where(row >= n, pltpu.roll(p, n, 0), 0.0)


def _shift_up(p, n, row):
    rows = p.shape[0]
    return jnp.where(row < rows - n, pltpu.roll(p, rows - n, 0), 0.0)


def _u_col(block):
    return lambda i: (0, block + i)


def conv_mixer_fwd(u, conv_w):
    s = u.shape[0]
    nb = CONV_DIM // HEAD_TILE

    def body(ab_ref, ac_ref, ah_ref, w_ref, y_ref):
        row = lax.broadcasted_iota(jnp.int32, (s, HEAD_TILE), 0)
        p = ac_ref[...] * ah_ref[...]
        w = w_ref[...]
        conv = w[0:1] * _shift_down(p, 2, row) + w[1:2] * _shift_down(p, 1, row) + w[2:3] * p
        y_ref[...] = (ab_ref[...] * conv).astype(BF16)

    col = lambda b: pl.BlockSpec((s, HEAD_TILE), _u_col(b * nb))
    return pl.pallas_call(
        body, grid=(nb,), in_specs=[col(0), col(1), col(2), pl.BlockSpec((3, HEAD_TILE), lambda i: (0, i))],
        out_specs=pl.BlockSpec((s, HEAD_TILE), lambda i: (0, i)),
        out_shape=jax.ShapeDtypeStruct((s, CONV_DIM), BF16), name="conv_mixer_fwd",
        compiler_params=_cp("parallel"))(u, u, u, conv_w)


def conv_mixer_bwd(dmix, u, conv_w):
    s = u.shape[0]
    nb = CONV_DIM // HEAD_TILE

    def body(dy_ref, ab_ref, ac_ref, ah_ref, w_ref, dab_ref, dac_ref, dah_ref, dw_ref):
        row = lax.broadcasted_iota(jnp.int32, (s, HEAD_TILE), 0)
        ac = ac_ref[...]
        ah = ah_ref[...]
        p = ac * ah
        w = w_ref[...]
        p1 = _shift_down(p, 1, row)
        p2 = _shift_down(p, 2, row)
        conv = w[0:1] * p2 + w[1:2] * p1 + w[2:3] * p
        dy = dy_ref[...]
        dab_ref[...] = (dy * conv).astype(BF16)
        dconv = dy * ab_ref[...]
        dp = w[0:1] * _shift_up(dconv, 2, row) + w[1:2] * _shift_up(dconv, 1, row) + w[2:3] * dconv
        dac_ref[...] = (dp * ah).astype(BF16)
        dah_ref[...] = (dp * ac).astype(BF16)
        dw_ref[...] = jnp.concatenate(
            [jnp.sum(dconv * p2, axis=0, keepdims=True), jnp.sum(dconv * p1, axis=0, keepdims=True),
             jnp.sum(dconv * p, axis=0, keepdims=True)], axis=0)

    col = lambda b: pl.BlockSpec((s, HEAD_TILE), _u_col(b * nb))
    out = pl.BlockSpec((s, HEAD_TILE), lambda i: (0, i))
    wspec = pl.BlockSpec((3, HEAD_TILE), lambda i: (0, i))
    shape = jax.ShapeDtypeStruct((s, CONV_DIM), BF16)
    return pl.pallas_call(
        body, grid=(nb,), in_specs=[out, col(0), col(1), col(2), wspec], out_specs=[out, out, out, wspec],
        out_shape=[shape, shape, shape, jax.ShapeDtypeStruct((3, CONV_DIM), F32)], name="conv_mixer_bwd",
        compiler_params=_cp("parallel"))(dmix, u, u, u, conv_w)


def _chunk_cumsum(g, pos):
    for sh in (1, 2, 4, 8, 16, 32):
        g = g + jnp.where(pos >= sh, pltpu.roll(g, sh, 0), 0.0)
    return g


def _chunk_rev_cumsum(g, pos):
    rows = g.shape[0]
    for sh in (1, 2, 4, 8, 16, 32):
        g = g + jnp.where(pos < CHUNK - sh, pltpu.roll(g, rows - sh, 0), 0.0)
    return g


def _lower_bound(lb_ref):
    logits = lb_ref[...]
    e = jnp.exp(logits - jnp.max(logits, axis=0, keepdims=True))
    p = e / jnp.sum(e, axis=0, keepdims=True)
    return p[0:1], p


def _hg_gates(hf, lb):
    sg = _sigmoid(hf)
    f = lb + (1.0 - lb) * sg
    return sg, f, jnp.log(f), 1.0 - f


def _hg_specs(s, slab, order):
    n = s // slab
    col = lambda b: pl.BlockSpec((slab, HEAD_TILE), lambda h, i: (order(i, n), b + h))
    return n, col


def _aligned(start):
    return start if isinstance(start, int) else pl.multiple_of(start, 8)


def _each_distance8(step, carry, unroll):
    groups = CHUNK // 8 // unroll

    def group(i, carry):
        for j in range(unroll):
            carry = step((i * unroll + j) * 8, carry)
        return carry

    return group(0, carry) if groups == 1 else lax.fori_loop(0, groups, group, carry)


def _pad_front(ref, val):
    ref[0:CHUNK, :] = jnp.zeros((CHUNK, HEAD_TILE), F32)
    ref[CHUNK:, :] = val


def _shifted_down(ref, d8, r, rows):
    if r == 0:
        return ref[pl.ds(_aligned(CHUNK - d8), rows), :]
    win = ref[pl.ds(_aligned(CHUNK - 8 - d8), rows + 8), :]
    return pltpu.roll(win, r, 0)[8:, :]


def _pulled_up(ref, val, d8, r, rows):
    if r == 0:
        ref[0:rows, :] = val
    else:
        ref[0:rows, :] = pltpu.roll(val, rows - r, 0)
    return ref[pl.ds(_aligned(d8), rows), :]


def hgrn_fwd(u, lb_logits, hg_norm):
    s = u.shape[0]
    slab = min(HG_FWD[0], s)
    cps = slab // CHUNK
    n, col = _hg_specs(s, slab, lambda i, n_: i)
    heads = HG_DIM // HEAD_TILE

    def body(q_ref, f_ref, i_ref, g_ref, lb_ref, nw_ref, y_ref, o_ref, st_ref, at_ref, bp_ref, kp_ref, vp_ref):
        @pl.when(pl.program_id(1) == 0)
        def _():
            at_ref[...] = jnp.zeros_like(at_ref)

        pos = lax.broadcasted_iota(jnp.int32, (slab, HEAD_TILE), 0) & (CHUNK - 1)
        lb, _ = _lower_bound(lb_ref)
        q = q_ref[...]
        v = i_ref[...]
        _, _, g, kk = _hg_gates(f_ref[...], lb)
        b = _chunk_cumsum(g, pos)
        for ref, val in ((bp_ref, b), (kp_ref, kk), (vp_ref, v)):
            _pad_front(ref, val)

        def diag(d8, r, o):
            d = d8 + r
            lam = jnp.exp(jnp.where(pos >= d, b - _shifted_down(bp_ref, d8, r, slab), -jnp.inf))
            sc = jnp.sum(q * _shifted_down(kp_ref, d8, r, slab) * lam, axis=-1, keepdims=True)
            return o + sc * _shifted_down(vp_ref, d8, r, slab)

        o = jnp.zeros((slab, HEAD_TILE), F32)
        for r in range(8):
            o = _each_distance8(lambda d8, acc: diag(d8, r, acc), o, HG_FWD[1])
        o_ref[...] = o

        qhat = q * jnp.exp(b)
        for c in range(cps):
            rows = slice(c * CHUNK, (c + 1) * CHUNK)
            at = at_ref[...]
            st_ref[c] = at
            o_ref[rows, :] += _dot(qhat[rows], at, NT, HIGHEST)
            bc = b[(c + 1) * CHUNK - 1:(c + 1) * CHUNK]
            khat = kk[rows] * jnp.exp(bc - b[rows])
            at_ref[...] = at * jnp.exp(bc) + _dot(v[rows], khat, TN, HIGHEST)

        o = o_ref[...]
        r = lax.rsqrt(jnp.mean(o * o, axis=-1, keepdims=True) + EPS)
        hg = g_ref[...]
        y_ref[...] = (o * r * nw_ref[...] * (hg * _sigmoid(hg))).astype(BF16)

    out = pl.BlockSpec((slab, HEAD_TILE), lambda h, i: (i, h))
    par = lambda rows: pl.BlockSpec((rows, HEAD_TILE), lambda h, i: (0, h))
    return pl.pallas_call(
        body, grid=(heads, n), in_specs=[col(12), col(16), col(20), col(24), par(3), par(1)],
        out_specs=[out, out, pl.BlockSpec((None, cps, HEAD_TILE, HEAD_TILE), lambda h, i: (h, i, 0, 0))],
        out_shape=[jax.ShapeDtypeStruct((s, HG_DIM), BF16), jax.ShapeDtypeStruct((s, HG_DIM), F32),
                   jax.ShapeDtypeStruct((heads, s // CHUNK, HEAD_TILE, HEAD_TILE), F32)],
        scratch_shapes=[pltpu.VMEM((HEAD_TILE, HEAD_TILE), F32)] + [pltpu.VMEM((slab + CHUNK, HEAD_TILE), F32)] * 3,
        name="hgrn_fwd",
        compiler_params=_cp("parallel", "arbitrary"))(u, u, u, u, lb_logits, hg_norm)


def hgrn_bwd(dmix, u, o_raw, states, lb_logits, hg_norm):
    s = u.shape[0]
    slab = min(HG_BWD[0], s)
    cps = slab // CHUNK
    n, col = _hg_specs(s, slab, lambda i, n_: n_ - 1 - i)
    heads = HG_DIM // HEAD_TILE

    def body(dy_ref, q_ref, f_ref, i_ref, g_ref, o_ref, st_ref, lb_ref, nw_ref,
             dq_ref, df_ref, di_ref, dg_ref, dnw_ref, dlb_ref, dat_ref, dlbacc_ref, dqs_ref, dks_ref, dvs_ref, dbc_ref,
             bp_ref, kp_ref, vp_ref, up1_ref, up2_ref):
        step = pl.program_id(1)

        @pl.when(step == 0)
        def _():
            dat_ref[...] = jnp.zeros_like(dat_ref)
            dlbacc_ref[...] = jnp.zeros_like(dlbacc_ref)
            dnw_ref[...] = jnp.zeros_like(dnw_ref)

        pos = lax.broadcasted_iota(jnp.int32, (slab, HEAD_TILE), 0) & (CHUNK - 1)
        lb, probs = _lower_bound(lb_ref)
        q = q_ref[...]
        v = i_ref[...]
        sg_f, f, g, kk = _hg_gates(f_ref[...], lb)
        b = _chunk_cumsum(g, pos)

        o = o_ref[...]
        nw = nw_ref[...]
        r = lax.rsqrt(jnp.mean(o * o, axis=-1, keepdims=True) + EPS)
        hg = g_ref[...]
        sg = _sigmoid(hg)
        dy = dy_ref[...]
        d_on = dy * (hg * sg)
        dg_ref[...] = (dy * (o * r * nw) * (sg * (1.0 + hg * (1.0 - sg)))).astype(BF16)
        dnw_ref[...] += jnp.sum(d_on * o * r, axis=0, keepdims=True)
        t1 = d_on * nw
        do = r * t1 - o * (r * r * r) * jnp.mean(t1 * o, axis=-1, keepdims=True)

        eb = jnp.exp(b)
        qhat = q * eb
        dbc_ref[...] = jnp.zeros_like(dbc_ref)
        for c in reversed(range(cps)):
            rows = slice(c * CHUNK, (c + 1) * CHUNK)
            last = (c + 1) * CHUNK - 1
            at = st_ref[c]
            dat = dat_ref[...]
            bc = b[last:last + 1]
            ebc = jnp.exp(bc)
            dec = jnp.exp(bc - b[rows])
            khat = kk[rows] * dec
            at_next = at * ebc + _dot(v[rows], khat, TN, HIGHEST)
            dbc_ref[last:last + 1, :] = jnp.sum(dat * at_next, axis=0, keepdims=True)
            dqs_ref[rows, :] = eb[rows] * _dot(do[rows], at, NN, HIGHEST)
            dks_ref[rows, :] = dec * _dot(v[rows], dat, NN, HIGHEST)
            dvs_ref[rows, :] = _dot(khat, dat, NT, HIGHEST)
            dat_ref[...] = dat * ebc + _dot(do[rows], qhat[rows], TN, HIGHEST)

        for ref, val in ((bp_ref, b), (kp_ref, kk), (vp_ref, v)):
            _pad_front(ref, val)
        for ref in (up1_ref, up2_ref):
            ref[slab:, :] = jnp.zeros((CHUNK, HEAD_TILE), F32)

        def diag(d8, r, carry):
            dq, dk, dv = carry
            d = d8 + r
            lam = jnp.exp(jnp.where(pos >= d, b - _shifted_down(bp_ref, d8, r, slab), -jnp.inf))
            kd = _shifted_down(kp_ref, d8, r, slab)
            vd = _shifted_down(vp_ref, d8, r, slab)
            sc = jnp.sum(q * kd * lam, axis=-1, keepdims=True)
            pd = jnp.sum(do * vd, axis=-1, keepdims=True)
            dq = dq + pd * kd * lam
            dk = dk + _pulled_up(up1_ref, pd * q * lam, d8, r, slab)
            dv = dv + _pulled_up(up2_ref, sc * do, d8, r, slab)
            return dq, dk, dv

        carry = (dqs_ref[...], dks_ref[...], dvs_ref[...])
        for r in range(8):
            carry = _each_distance8(lambda d8, cr: diag(d8, r, cr), carry, HG_BWD[1])
        dq, dk, dv = carry

        db = q * dq - kk * dk + dbc_ref[...]
        dgl = _chunk_rev_cumsum(db, pos)
        dfv = dgl / f - dk
        dq_ref[...] = dq.astype(BF16)
        di_ref[...] = dv.astype(BF16)
        df_ref[...] = (dfv * (1.0 - lb) * sg_f * (1.0 - sg_f)).astype(BF16)
        dlbacc_ref[...] += jnp.sum(dfv * (1.0 - sg_f), axis=0, keepdims=True)

        @pl.when(step == n - 1)
        def _():
            dlb = dlbacc_ref[...]
            sel = (lax.broadcasted_iota(jnp.int32, (3, HEAD_TILE), 0) == 0).astype(F32)
            dlb_ref[...] = dlb * probs[0:1] * (sel - probs)

    out = pl.BlockSpec((slab, HEAD_TILE), lambda h, i: (n - 1 - i, h))
    par = lambda rows: pl.BlockSpec((rows, HEAD_TILE), lambda h, i: (0, h))
    dyspec = pl.BlockSpec((slab, HEAD_TILE), lambda h, i: (n - 1 - i, CONV_DIM // HEAD_TILE + h))
    shape = jax.ShapeDtypeStruct((s, HG_DIM), BF16)
    slab_f32 = pltpu.VMEM((slab, HEAD_TILE), F32)
    return pl.pallas_call(
        body, grid=(heads, n),
        in_specs=[dyspec, col(12), col(16), col(20), col(24), out,
                  pl.BlockSpec((None, cps, HEAD_TILE, HEAD_TILE), lambda h, i: (h, n - 1 - i, 0, 0)), par(3), par(1)],
        out_specs=[out, out, out, out, par(1), par(3)],
        out_shape=[shape, shape, shape, shape, jax.ShapeDtypeStruct((1, HG_DIM), F32),
                   jax.ShapeDtypeStruct((3, HG_DIM), F32)],
        scratch_shapes=[pltpu.VMEM((HEAD_TILE, HEAD_TILE), F32), pltpu.VMEM((1, HEAD_TILE), F32),
                        slab_f32, slab_f32, slab_f32, slab_f32] + [pltpu.VMEM((slab + CHUNK, HEAD_TILE), F32)] * 5,
        name="hgrn_bwd", compiler_params=_cp("parallel", "arbitrary"))(
            dmix, u, u, u, u, o_raw, states, lb_logits, hg_norm)


def _pair_rstd(x, lo):
    x2 = x * x
    s_lo = jnp.sum(jnp.where(lo, x2, 0.0), axis=-1, keepdims=True)
    s_hi = jnp.sum(jnp.where(lo, 0.0, x2), axis=-1, keepdims=True)
    inv = 1.0 / SB_HEAD_DIM
    return jnp.where(lo, lax.rsqrt(s_lo * inv + EPS), lax.rsqrt(s_hi * inv + EPS))


def _pair_mean(x, lo):
    s_lo = jnp.sum(jnp.where(lo, x, 0.0), axis=-1, keepdims=True)
    s_hi = jnp.sum(jnp.where(lo, 0.0, x), axis=-1, keepdims=True)
    return jnp.where(lo, s_lo, s_hi) * (1.0 / SB_HEAD_DIM)


def qk_norm_fwd(qkv, qg, kg):
    s = qkv.shape[0]
    tm = _tok_tile(s)
    nt = D_MODEL // HEAD_TILE

    def body(q_ref, k_ref, v_ref, qg_ref, kg_ref, qn_ref, kn_ref, vb_ref):
        lo = lax.broadcasted_iota(jnp.int32, (tm, HEAD_TILE), 1) < SB_HEAD_DIM
        qv = q_ref[...]
        kv = k_ref[...]
        qn_ref[...] = (qv * _pair_rstd(qv, lo) * qg_ref[...]).astype(BF16)
        kn_ref[...] = (kv * _pair_rstd(kv, lo) * kg_ref[...]).astype(BF16)
        vb_ref[...] = v_ref[...].astype(BF16)

    col = lambda b: pl.BlockSpec((tm, HEAD_TILE), lambda i, j: (i, b * nt + j))
    gain = pl.BlockSpec((1, HEAD_TILE), lambda i, j: (0, 0))
    out = pl.BlockSpec((tm, HEAD_TILE), lambda i, j: (i, j))
    shape = jax.ShapeDtypeStruct((s, D_MODEL), BF16)
    return pl.pallas_call(
        body, grid=(s // tm, nt), in_specs=[col(0), col(1), col(2), gain, gain], out_specs=[out, out, out],
        out_shape=[shape, shape, shape], name="qk_norm_fwd", compiler_params=_cp("parallel", "parallel"))(
            qkv, qkv, qkv, qg, kg)


def qk_norm_bwd(dqn, dkn, dv, qkv, qg, kg):
    s = qkv.shape[0]
    tm = _tok_tile(s)
    nt = D_MODEL // HEAD_TILE

    def body(dqn_ref, dkn_ref, dv_ref, q_ref, k_ref, qg_ref, kg_ref, dq_ref, dk_ref, dvb_ref, dqg_ref, dkg_ref):
        @pl.when((pl.program_id(0) == 0) & (pl.program_id(1) == 0))
        def _():
            dqg_ref[...] = jnp.zeros_like(dqg_ref)
            dkg_ref[...] = jnp.zeros_like(dkg_ref)

        lo = lax.broadcasted_iota(jnp.int32, (tm, HEAD_TILE), 1) < SB_HEAD_DIM

        def one(x_ref, g_ref, dn_ref, dx_ref, dgain_ref):
            xv = x_ref[...]
            r = _pair_rstd(xv, lo)
            xn = xv * r
            dn = dn_ref[...]
            dgain_ref[...] += jnp.sum(dn * xn, axis=0, keepdims=True)
            t1 = dn * g_ref[...]
            dx_ref[...] = (r * (t1 - xn * _pair_mean(t1 * xn, lo))).astype(BF16)

        one(q_ref, qg_ref, dqn_ref, dq_ref, dqg_ref)
        one(k_ref, kg_ref, dkn_ref, dk_ref, dkg_ref)
        dvb_ref[...] = dv_ref[...].astype(BF16)

    col = lambda b: pl.BlockSpec((tm, HEAD_TILE), lambda i, j: (i, b * nt + j))
    gain = pl.BlockSpec((1, HEAD_TILE), lambda i, j: (0, 0))
    out = pl.BlockSpec((tm, HEAD_TILE), lambda i, j: (i, j))
    shape = jax.ShapeDtypeStruct((s, D_MODEL), BF16)
    grow = jax.ShapeDtypeStruct((1, HEAD_TILE), F32)
    return pl.pallas_call(
        body, grid=(s // tm, nt), in_specs=[out, out, out, col(0), col(1), gain, gain],
        out_specs=[out, out, out, gain, gain], out_shape=[shape, shape, shape, grow, grow], name="qk_norm_bwd",
        compiler_params=_cp("arbitrary", "arbitrary"))(dqn, dkn, dv, qkv, qkv, qg, kg)


def _split_dot(x, u):
    hi = x.astype(BF16)
    lo = (x - hi.astype(F32)).astype(BF16)
    return _dot(hi, u, NN) + _dot(lo, u, NN)


def _sb_tile(qs, kb, carry, causal, suffix, diag):
    z = _dot(qs, kb, NT)
    lb = jnp.minimum(z, 0.0) - jnp.log(1.0 + jnp.exp(-jnp.abs(z)))
    lom = lb - z
    if diag:
        lom = jnp.where(causal, lom, 0.0)
    ws = []
    for j in reversed(range(ATT_TILE // ATT_BLOCK)):
        cols = slice(j * ATT_BLOCK, (j + 1) * ATT_BLOCK)
        ws.append(jnp.exp(z[:, cols] + _split_dot(lom[:, cols], suffix) + carry))
        carry = carry + jnp.sum(lom[:, cols], axis=-1, keepdims=True)
    w = jnp.concatenate(ws[::-1], axis=1)
    if diag:
        w = jnp.where(causal, w, 0.0)
    return lb, w, carry


def _sb_consts():
    row = lax.broadcasted_iota(jnp.int32, (ATT_BLOCK, ATT_BLOCK), 0)
    col = lax.broadcasted_iota(jnp.int32, (ATT_BLOCK, ATT_BLOCK), 1)
    suffix = (row >= col).astype(BF16)
    trow = lax.broadcasted_iota(jnp.int32, (ATT_TILE, ATT_TILE), 0)
    tcol = lax.broadcasted_iota(jnp.int32, (ATT_TILE, ATT_TILE), 1)
    causal = tcol < trow
    lo = lax.broadcasted_iota(jnp.int32, (ATT_TILE, HEAD_TILE), 1) < SB_HEAD_DIM
    return suffix, causal, lo


def _rows(i):
    return pl.ds(pl.multiple_of(i * ATT_TILE, ATT_TILE), ATT_TILE)


def _head_query(qb, mask):
    return (jnp.where(mask, qb, 0.0) * (SB_HEAD_DIM ** -0.5)).astype(BF16)


def sb_attn_fwd(qn, kn, vb):
    s = qn.shape[0]
    nq = s // ATT_TILE
    nt = D_MODEL // HEAD_TILE

    def body(q_ref, k_ref, v_ref, o_ref, ob_ref):
        suffix, causal, lo = _sb_consts()

        def qtile(qi, _):
            qb = q_ref[_rows(qi), :].astype(F32)
            qs = [_head_query(qb, lo), _head_query(qb, ~lo)]

            def step(kj, state, diag):
                kb = k_ref[_rows(kj), :]
                vt = v_ref[_rows(kj), :]
                out = []
                for hh in range(2):
                    carry, acc = state[hh]
                    _, w, carry = _sb_tile(qs[hh], kb, carry, causal, suffix, diag)
                    out.append((carry, acc + _dot(w.astype(BF16), vt, NN)))
                return tuple(out)

            start = (jnp.zeros((ATT_TILE, 1), F32), jnp.zeros((ATT_TILE, HEAD_TILE), F32))
            state = step(qi, (start, start), True)
            state = lax.fori_loop(0, qi, lambda jj, st: step(qi - 1 - jj, st, False), state)
            o = jnp.where(lo, state[0][1], state[1][1])
            o_ref[_rows(qi), :] = o
            ob_ref[_rows(qi), :] = o.astype(BF16)
            return 0

        lax.fori_loop(0, nq, qtile, 0)

    spec = pl.BlockSpec((s, HEAD_TILE), lambda p: (0, p))
    return pl.pallas_call(
        body, grid=(nt,), in_specs=[spec, spec, spec], out_specs=[spec, spec],
        out_shape=[jax.ShapeDtypeStruct((s, D_MODEL), F32), jax.ShapeDtypeStruct((s, D_MODEL), BF16)],
        name="sb_attn_fwd", compiler_params=_cp("parallel"))(qn, kn, vb)


def sb_attn_bwd(qn, kn, vb, o, do):
    s = qn.shape[0]
    nq = s // ATT_TILE
    nt = D_MODEL // HEAD_TILE

    def body(q_ref, k_ref, v_ref, o_ref, do_ref, dq_ref, dk_ref, dv_ref):
        suffix, causal, lo = _sb_consts()
        dk_ref[...] = jnp.zeros_like(dk_ref)
        dv_ref[...] = jnp.zeros_like(dv_ref)

        def qtile(qi, _):
            qb = q_ref[_rows(qi), :].astype(F32)
            dob = do_ref[_rows(qi), :].astype(BF16).astype(F32)
            prod = dob * o_ref[_rows(qi), :]
            masks = [lo, ~lo]
            qs = [_head_query(qb, m) for m in masks]
            dos = [jnp.where(m, dob, 0.0).astype(BF16) for m in masks]
            totals = [jnp.sum(jnp.where(m, prod, 0.0), axis=-1, keepdims=True) for m in masks]

            def step(kj, state, diag):
                kb = k_ref[_rows(kj), :]
                vt = v_ref[_rows(kj), :]
                out = []
                dk = jnp.zeros((ATT_TILE, HEAD_TILE), F32)
                dv = jnp.zeros((ATT_TILE, HEAD_TILE), F32)
                for hh in range(2):
                    carry, carry_e, dq = state[hh]
                    lb, w, carry = _sb_tile(qs[hh], kb, carry, causal, suffix, diag)
                    wb = w.astype(BF16)
                    e = _dot(dos[hh], vt, NT) * wb.astype(F32)
                    befores = []
                    for j in reversed(range(ATT_TILE // ATT_BLOCK)):
                        cols = slice(j * ATT_BLOCK, (j + 1) * ATT_BLOCK)
                        befores.append(totals[hh] - carry_e - _split_dot(e[:, cols], suffix))
                        carry_e = carry_e + jnp.sum(e[:, cols], axis=-1, keepdims=True)
                    before = jnp.concatenate(befores[::-1], axis=1)
                    beta = jnp.exp(lb)
                    dz = e * (1.0 - beta) - before * beta
                    if diag:
                        dz = jnp.where(causal, dz, 0.0)
                    dzb = dz.astype(BF16)
                    dq = dq + _dot(dzb, kb, NN)
                    dk = dk + _dot(dzb, qs[hh], TN)
                    dv = dv + _dot(wb, dos[hh], TN)
                    out.append((carry, carry_e, dq))
                dk_ref[_rows(kj), :] += dk
                dv_ref[_rows(kj), :] += dv
                return tuple(out)

            zero = jnp.zeros((ATT_TILE, 1), F32)
            start = (zero, zero, jnp.zeros((ATT_TILE, HEAD_TILE), F32))
            state = step(qi, (start, start), True)
            state = lax.fori_loop(0, qi, lambda jj, st: step(qi - 1 - jj, st, False), state)
            dq_ref[_rows(qi), :] = jnp.where(lo, state[0][2], state[1][2]) * (SB_HEAD_DIM ** -0.5)
            return 0

        lax.fori_loop(0, nq, qtile, 0)

    spec = pl.BlockSpec((s, HEAD_TILE), lambda p: (0, p))
    shape = jax.ShapeDtypeStruct((s, D_MODEL), F32)
    return pl.pallas_call(
        body, grid=(nt,), in_specs=[spec] * 5, out_specs=[spec] * 3, out_shape=[shape] * 3,
        name="sb_attn_bwd", compiler_params=_cp("parallel"))(qn, kn, vb, o, do)


def ada_fwd(c_all, ada_w, ada_b_shard):
    layers, d, cols = ada_w.shape
    tn = 512

    def body(c_ref, w_ref, b_ref, out_ref):
        cv = c_ref[...]
        act = (cv * _sigmoid(cv)).astype(BF16)
        out_ref[...] = _dot(act, w_ref[...].astype(BF16), NN) + b_ref[...]

    return pl.pallas_call(
        body, grid=(layers, cols // tn),
        in_specs=[pl.BlockSpec((N_DEV, d), lambda l, j: (0, 0)), pl.BlockSpec((None, d, tn), lambda l, j: (l, 0, j)),
                  pl.BlockSpec((None, 1, tn), lambda l, j: (l, 0, j))],
        out_specs=pl.BlockSpec((None, N_DEV, tn), lambda l, j: (l, 0, j)),
        out_shape=jax.ShapeDtypeStruct((layers, N_DEV, cols), F32), name="ada_fwd",
        compiler_params=_cp("parallel", "parallel"))(c_all, ada_w, ada_b_shard)


def ada_w_grad(c_all, dmod):
    layers, _, cols = dmod.shape
    d = c_all.shape[1]
    tn = 512

    def body(c_ref, dm_ref, out_ref):
        cv = c_ref[...]
        out_ref[...] = _dot(cv * _sigmoid(cv), dm_ref[...], TN, HIGHEST)

    return pl.pallas_call(
        body, grid=(layers, cols // tn),
        in_specs=[pl.BlockSpec((N_DEV, d), lambda l, j: (0, 0)), pl.BlockSpec((None, N_DEV, tn), lambda l, j: (l, 0, j))],
        out_specs=pl.BlockSpec((None, d, tn), lambda l, j: (l, 0, j)),
        out_shape=jax.ShapeDtypeStruct((layers, d, cols), F32), name="ada_w_grad",
        compiler_params=_cp("parallel", "parallel"))(c_all, dmod)


def _row_tile(r, c, elems):
    best = 8
    for t in range(8, r + 1, 8):
        if r % t == 0 and t * c <= elems:
            best = t
    return best


def sum_rows(x, name):
    n, r, c = x.shape
    tr = _row_tile(r, c, 1 << 17)

    def body(x_ref, out_ref):
        acc = x_ref[0]
        for i in range(1, n):
            acc = acc + x_ref[i]
        out_ref[...] = acc

    return pl.pallas_call(
        body, grid=(r // tr,), in_specs=[pl.BlockSpec((n, tr, c), lambda i: (0, i, 0))],
        out_specs=pl.BlockSpec((tr, c), lambda i: (i, 0)), out_shape=jax.ShapeDtypeStruct((r, c), F32), name=name,
        compiler_params=_cp("parallel"))(x)


def adamw(w, g, m, v, name):
    r, c = w.shape
    tr = _row_tile(r, c, 1 << 17)
    c1 = 1.0 - ADAM_B1 ** ADAM_STEP
    c2 = 1.0 - ADAM_B2 ** ADAM_STEP

    def body(w_ref, g_ref, m_ref, v_ref, d_ref, nm_ref, nv_ref):
        gv = g_ref[...]
        nm = ADAM_B1 * m_ref[...] + (1.0 - ADAM_B1) * gv
        nv = ADAM_B2 * v_ref[...] + (1.0 - ADAM_B2) * (gv * gv)
        d_ref[...] = -ADAM_LR * ((nm / c1) / (jnp.sqrt(nv / c2) + ADAM_EPS) + ADAM_WD * w_ref[...])
        nm_ref[...] = nm
        nv_ref[...] = nv

    spec = pl.BlockSpec((tr, c), lambda i: (i, 0))
    shape = jax.ShapeDtypeStruct((r, c), F32)
    return pl.pallas_call(
        body, grid=(r // tr,), in_specs=[spec] * 4, out_specs=[spec] * 3, out_shape=[shape] * 3, name=name,
        compiler_params=_cp("parallel"))(w, g, m, v)


def _me():
    return lax.axis_index("x"), lax.axis_index("y"), lax.axis_index("c")


def _flip(v, bit):
    return 1 - v if bit else v


HBM = pl.BlockSpec(memory_space=pl.ANY)
VMEM = pl.BlockSpec(memory_space=pltpu.VMEM)


def all_gather_rows(v, name):
    n = v.shape[1]

    def body(v_ref, out_ref, send_sems, recv_sems):
        x, y, c = _me()
        me = 4 * x + 2 * y + c
        out_ref[pl.ds(me, 1), :] = v_ref[...]
        copies = []
        for j in range(1, N_DEV):
            peer = (_flip(x, j & 4), _flip(y, j & 2), _flip(c, j & 1))
            copies.append(pltpu.make_async_remote_copy(
                src_ref=v_ref, dst_ref=out_ref.at[pl.ds(me, 1), :], send_sem=send_sems.at[j - 1],
                recv_sem=recv_sems.at[j - 1], device_id=peer, device_id_type=MESH))
        for cp in copies:
            cp.start()
        for cp in copies:
            cp.wait()

    return pl.pallas_call(
        body, in_specs=[VMEM], out_specs=VMEM, out_shape=jax.ShapeDtypeStruct((N_DEV, n), F32),
        scratch_shapes=[pltpu.SemaphoreType.DMA((N_DEV - 1,)), pltpu.SemaphoreType.DMA((N_DEV - 1,))], name=name)(v)


def _chip_peers(x, y):
    return [((1 - x, y), 2 * (1 - x) + y), ((x, 1 - y), 2 * x + (1 - y)), ((1 - x, 1 - y), 2 * (1 - x) + (1 - y))]


def cast_to_slot(w, name):
    r, c = w.shape
    tr = _row_tile(r, c, 1 << 18)

    def body(w_ref, out_ref):
        out_ref[...] = w_ref[...].astype(BF16)

    return pl.pallas_call(
        body, grid=(r // tr,), in_specs=[pl.BlockSpec((tr, c), lambda i: (i, 0))],
        out_specs=pl.BlockSpec((None, tr, c), lambda i: (2 * lax.axis_index("x") + lax.axis_index("y"), i, 0)),
        out_shape=jax.ShapeDtypeStruct((N_CHIPS, r, c), BF16), name=name, compiler_params=_cp("parallel"))(w)


def chip_all_gather(slots, name):
    n = len(slots)

    def body(*refs):
        outs = refs[n:2 * n]
        send_sems, recv_sems, pass_send_sems, pass_recv_sems = refs[2 * n:]
        x, y, c = _me()
        mine = 2 * x + y
        peers = _chip_peers(x, y)
        sent = []
        for a in range(n):
            half = outs[a].shape[1] // 2
            rows = outs[a].at[mine, pl.ds(c * half, half), :]
            for p, (chip, _) in enumerate(peers):
                cp = pltpu.make_async_remote_copy(
                    src_ref=rows, dst_ref=rows, send_sem=send_sems.at[a, p], recv_sem=recv_sems.at[a, p],
                    device_id=(*chip, c), device_id_type=MESH)
                cp.start()
                sent.append(cp)
        passed = []
        for a in range(n):
            half = outs[a].shape[1] // 2
            for p, (chip, slot) in enumerate(peers):
                rows = outs[a].at[slot, pl.ds(c * half, half), :]
                pltpu.make_async_remote_copy(
                    src_ref=rows, dst_ref=rows, send_sem=send_sems.at[a, p], recv_sem=recv_sems.at[a, p],
                    device_id=(*chip, c), device_id_type=MESH).wait_recv()
                cp = pltpu.make_async_remote_copy(
                    src_ref=rows, dst_ref=rows, send_sem=pass_send_sems.at[a, p], recv_sem=pass_recv_sems.at[a, p],
                    device_id=(x, y, 1 - c), device_id_type=MESH)
                cp.start()
                passed.append(cp)
        for a in range(n):
            half = outs[a].shape[1] // 2
            for p, (_, slot) in enumerate(peers):
                theirs = outs[a].at[slot, pl.ds((1 - c) * half, half), :]
                pltpu.make_async_remote_copy(
                    src_ref=theirs, dst_ref=theirs, send_sem=pass_send_sems.at[a, p], recv_sem=pass_recv_sems.at[a, p],
                    device_id=(x, y, 1 - c), device_id_type=MESH).wait_recv()
        for cp in sent + passed:
            cp.wait_send()

    sems = pltpu.SemaphoreType.DMA((n, 3))
    return pl.pallas_call(
        body, in_specs=[HBM] * n, out_specs=[HBM] * n,
        out_shape=[jax.ShapeDtypeStruct(s.shape, s.dtype) for s in slots],
        input_output_aliases={a: a for a in range(n)}, scratch_shapes=[sems, sems, sems, sems], name=name)(*slots)


def sibling_split(grads, name):
    n = len(grads)

    def body(*refs):
        ins, got = refs[:n], refs[n:2 * n]
        send_sems, recv_sems = refs[2 * n:]
        x, y, c = _me()
        started = []
        for a in range(n):
            half = ins[a].shape[1] // 2
            give = pltpu.make_async_remote_copy(
                src_ref=ins[a].at[:, pl.ds((1 - c) * half, half), :], dst_ref=got[a], send_sem=send_sems.at[a],
                recv_sem=recv_sems.at[a], device_id=(x, y, 1 - c), device_id_type=MESH)
            give.start()
            started.append(give)
        for cp in started:
            cp.wait()

    return pl.pallas_call(
        body, in_specs=[HBM] * n, out_specs=[HBM] * n,
        out_shape=[jax.ShapeDtypeStruct((g.shape[0], g.shape[1] // 2, g.shape[2]), g.dtype) for g in grads],
        scratch_shapes=[pltpu.SemaphoreType.DMA((n,)), pltpu.SemaphoreType.DMA((n,))], name=name)(*grads)


def pair_sum(g, got, name):
    k, half, c = got.shape
    tr = _row_tile(half, c, 1 << 18)
    nb = half // tr

    def body(g_ref, got_ref, out_ref):
        out_ref[...] = (g_ref[...] + got_ref[...]).astype(BF16)

    spec = pl.BlockSpec((None, tr, c), lambda j, i: (j, i, 0))
    return pl.pallas_call(
        body, grid=(k, nb),
        in_specs=[pl.BlockSpec((None, tr, c), lambda j, i: (j, lax.axis_index("c") * nb + i, 0)), spec],
        out_specs=spec, out_shape=jax.ShapeDtypeStruct(got.shape, BF16), name=name,
        compiler_params=_cp("parallel", "parallel"))(g, got)


def chip_scatter(parts, name):
    n = len(parts)

    def body(*refs):
        ins, outs = refs[:n], refs[n:2 * n]
        send_sems, recv_sems = refs[2 * n:]
        x, y, c = _me()
        started = []
        for a in range(n):
            for p, (chip, slot) in enumerate(_chip_peers(x, y)):
                cp = pltpu.make_async_remote_copy(
                    src_ref=ins[a].at[slot], dst_ref=outs[a].at[p], send_sem=send_sems.at[a, p],
                    recv_sem=recv_sems.at[a, p], device_id=(*chip, c), device_id_type=MESH)
                cp.start()
                started.append(cp)
        for cp in started:
            cp.wait()

    return pl.pallas_call(
        body, in_specs=[HBM] * n, out_specs=[HBM] * n,
        out_shape=[jax.ShapeDtypeStruct((3, *p.shape[1:]), p.dtype) for p in parts],
        scratch_shapes=[pltpu.SemaphoreType.DMA((n, 3)), pltpu.SemaphoreType.DMA((n, 3))], name=name)(*parts)


def chip_sum(part, landed, name):
    _, half, c = landed.shape
    tr = _row_tile(half, c, 1 << 17)
    nb = half // tr

    def body(part_ref, landed_ref, out_ref):
        up = lambda v: v.astype(F32)
        out_ref[...] = ((up(part_ref[...]) + up(landed_ref[0])) + up(landed_ref[1])) + up(landed_ref[2])

    return pl.pallas_call(
        body, grid=(nb,),
        in_specs=[pl.BlockSpec((None, tr, c), lambda i: (2 * lax.axis_index("x") + lax.axis_index("y"), i, 0)),
                  pl.BlockSpec((3, tr, c), lambda i: (0, i, 0))],
        out_specs=pl.BlockSpec((tr, c), lambda i: (lax.axis_index("c") * nb + i, 0)),
        out_shape=jax.ShapeDtypeStruct((2 * half, c), F32), name=name, compiler_params=_cp("parallel"))(part, landed)


def sibling_join(arrays, name):
    n = len(arrays)

    def body(*refs):
        ins, outs = refs[:n], refs[n:2 * n]
        send_sems, recv_sems = refs[2 * n:]
        x, y, c = _me()
        started = []
        for a in range(n):
            half = ins[a].shape[0] // 2
            give = pltpu.make_async_remote_copy(
                src_ref=ins[a].at[pl.ds(c * half, half), :], dst_ref=outs[a].at[pl.ds(c * half, half), :],
                send_sem=send_sems.at[a], recv_sem=recv_sems.at[a], device_id=(x, y, 1 - c), device_id_type=MESH)
            give.start()
            started.append(give)
        for cp in started:
            cp.wait()

    return pl.pallas_call(
        body, in_specs=[HBM] * n, out_specs=[HBM] * n,
        out_shape=[jax.ShapeDtypeStruct(h.shape, h.dtype) for h in arrays],
        input_output_aliases={a: a for a in range(n)},
        scratch_shapes=[pltpu.SemaphoreType.DMA((n,)), pltpu.SemaphoreType.DMA((n,))], name=name)(*arrays)


def reduce_to_owner(grads):
    got = sibling_split(grads, "grad_sibling_split")
    parts = [pair_sum(g, h, f"grad_pair_sum_{a}") for a, (g, h) in enumerate(zip(grads, got))]
    landed = chip_scatter(parts, "grad_chip_scatter")
    halves = [chip_sum(p, l, f"grad_chip_sum_{a}") for a, (p, l) in enumerate(zip(parts, landed))]
    return sibling_join(halves, "grad_sibling_join")


def _pad_row(v, n):
    return jnp.pad(v.reshape(1, -1), ((0, 0), (0, n - v.size)))


def local_step(x, target, mod, wts):
    d = D_MODEL
    row = lambda v: v.reshape(1, -1)
    mods = [[row(mod[l, i * d:(i + 1) * d]) for i in range(6)] for l in range(2)]
    saved = []
    for l in range(2):
        shift1, scale1, gate1, shift2, scale2, gate2 = mods[l]
        g_mix, g_mlp = row(wts["norm_mix"][l]), row(wts["norm_mlp"][l])
        h = norm_mod_fwd(x, g_mix, scale1, shift1, f"norm_mix_fwd_{l}")
        if l == 0:
            u = matmul_nn_chunked(h, wts["w_in"], F32, "in_proj_ab")[0]
            y_a = conv_mixer_fwd(u, wts["conv_w"])
            y_b, o_raw, states = hgrn_fwd(u, wts["lb_logits"], wts["hg_norm"])
            mix = jnp.concatenate([y_a, y_b], axis=1)
            y, x1 = proj_residual(mix, wts["w_out_ab"], x, gate1, "out_proj_ab")
            ctx = (u, o_raw, states)
        else:
            qkv = matmul_nn_chunked(h, wts["w_qkv"], F32, "in_proj_c")[0]
            qn, kn, vb = qk_norm_fwd(qkv, wts["qg"], wts["kg"])
            o, mix = sb_attn_fwd(qn, kn, vb)
            y, x1 = proj_residual(mix, wts["w_out_c"], x, gate1, "out_proj_c")
            ctx = (qkv, qn, kn, vb, o)
        h2 = norm_mod_fwd(x1, g_mlp, scale2, shift2, f"norm_mlp_fwd_{l}")
        act, r = mlp_up(h2, wts["w1"][l], f"mlp_up_{l}")
        y2, x2 = proj_residual(act, wts["w2"][l], x1, gate2, f"mlp_down_{l}")
        saved.append((x, h, mix, y, x1, h2, act, r, y2, ctx))
        x = x2

    dx, loss_row = loss_and_grad(x, target)
    small, big = {}, {}
    dmod = [None, None]
    d_norm_mix, d_norm_mlp = [None, None], [None, None]
    for l in (1, 0):
        shift1, scale1, gate1, shift2, scale2, gate2 = mods[l]
        g_mix, g_mlp = row(wts["norm_mix"][l]), row(wts["norm_mlp"][l])
        x0, h, mix, y, x1, h2, act, r, y2, ctx = saved[l]
        dy2, dgate2 = gate_bwd(dx, y2, gate2, f"mlp_gate_bwd_{l}")
        dz = mlp_down_bwd(dy2, wts["w2"][l], r, f"mlp_down_bwd_{l}")
        big[f"w2_{l}"] = matmul_tn_plain(act, dy2, f"mlp_w2_grad_{l}")
        dh2 = matmul_nt_chunked(dz, wts["w1"][l], f"mlp_up_bwd_{l}")
        big[f"w1_{l}"] = matmul_tn_chunked(h2, dz, f"mlp_w1_grad_{l}")
        dx1, d_norm_mlp[l], dscale2, dshift2 = norm_mod_bwd(dh2, x1, g_mlp, scale2, dx, f"norm_mlp_bwd_{l}")
        dy, dgate1 = gate_bwd(dx1, y, gate1, f"mix_gate_bwd_{l}")
        if l == 0:
            u, o_raw, states = ctx
            dmix = matmul_nt_plain(dy, wts["w_out_ab"], "out_proj_ab_bwd")
            big["w_out_ab"] = matmul_tn_plain(mix, dy, "w_out_ab_grad")
            dab, dac, dah, small["conv_w"] = conv_mixer_bwd(dmix, u, wts["conv_w"])
            dhq, dhf, dhi, dhg, small["hg_norm"], small["lb_logits"] = hgrn_bwd(
                dmix, u, o_raw, states, wts["lb_logits"], wts["hg_norm"])
            du = jnp.concatenate([dab, dac, dah, dhq, dhf, dhi, dhg], axis=1)
            dh = matmul_nt_chunked(du, wts["w_in"], "in_proj_ab_bwd")
            big["w_in"] = matmul_tn_chunked(h, du, "w_in_grad")
        else:
            qkv, qn, kn, vb, o = ctx
            do = matmul_nt_plain(dy, wts["w_out_c"], "out_proj_c_bwd")
            big["w_out_c"] = matmul_tn_plain(mix, dy, "w_out_c_grad")
            dqn, dkn, dv = sb_attn_bwd(qn, kn, vb, o, do)
            dq, dk, dvb, dqg, dkg = qk_norm_bwd(dqn, dkn, dv, qkv, wts["qg"], wts["kg"])
            small["q_norm"] = dqg[:, :SB_HEAD_DIM] + dqg[:, SB_HEAD_DIM:]
            small["k_norm"] = dkg[:, :SB_HEAD_DIM] + dkg[:, SB_HEAD_DIM:]
            dqkv = jnp.concatenate([dq, dk, dvb], axis=1)
            dh = matmul_nt_chunked(dqkv, wts["w_qkv"], "in_proj_c_bwd")
            big["w_qkv"] = matmul_tn_chunked(h, dqkv, "w_qkv_grad")
        dx, d_norm_mix[l], dscale1, dshift1 = norm_mod_bwd(dh, x0, g_mix, scale1, dx1, f"norm_mix_bwd_{l}")
        dmod[l] = jnp.concatenate([dshift1, dscale1, dgate1, dshift2, dscale2, dgate2], axis=1)
    small["mod"] = jnp.concatenate(dmod, axis=0)
    small["norm_mix"] = jnp.concatenate(d_norm_mix, axis=0)
    small["norm_mlp"] = jnp.concatenate(d_norm_mlp, axis=0)
    return loss_row, dx, small, big


SMALL_ORDER = ("mod", "norm_mix", "norm_mlp", "conv_w", "hg_norm", "lb_logits", "q_norm", "k_norm")


def kernel(x, c, ada_w, ada_b, norm_mix, norm_mlp, w_in_ab, conv_w, hg_norm, lb_logits, w_out_ab, w_qkv, q_norm, k_norm, w_out_c, mlp_w1, mlp_w2, loss_target, m_ada_w, m_ada_b, m_norm_mix, m_norm_mlp, m_w_in_ab, m_conv_w, m_hg_norm, m_lb_logits, m_w_out_ab, m_w_qkv, m_q_norm, m_k_norm, m_w_out_c, m_mlp_w1, m_mlp_w2, v_ada_w, v_ada_b, v_norm_mix, v_norm_mlp, v_w_in_ab, v_conv_w, v_hg_norm, v_lb_logits, v_w_out_ab, v_w_qkv, v_q_norm, v_k_norm, v_w_out_c, v_mlp_w1, v_mlp_w2):
    d = D_MODEL
    ax, ay, ac = _me()
    chip = 2 * ax + ay
    dev = 2 * chip + ac
    cols = ada_w.shape[2]

    first = all_gather_rows(_pad_row(jnp.concatenate([c.reshape(-1), conv_w.reshape(-1)]), 1536), "gather_cond")
    c_all = first[:, :d]
    conv_full = first[::2, d:d + 3 * HEAD_TILE].reshape(N_CHIPS, 3, HEAD_TILE).transpose(1, 0, 2).reshape(3, CONV_DIM)
    ada_b_shard = lax.dynamic_slice(ada_b, (0, chip * cols), (2, cols)).reshape(2, 1, cols)
    mod_cols = ada_fwd(c_all, ada_w, ada_b_shard)
    mod_all = all_gather_rows(mod_cols.reshape(1, -1), "gather_mod").reshape(N_DEV, 2, N_DEV, cols)
    mod = lax.dynamic_index_in_dim(mod_all[::2], dev, axis=2, keepdims=False).transpose(1, 0, 2).reshape(2, 6 * d)

    shards = [w_in_ab[0], w_out_ab[0], w_qkv[0], w_out_c[0], mlp_w1[0], mlp_w1[1], mlp_w2[0], mlp_w2[1]]
    g_in, g_out_ab, g_qkv, g_out_c, g_w1a, g_w1b, g_w2a, g_w2b = chip_all_gather(
        [cast_to_slot(s, f"cast_weight_{a}") for a, s in enumerate(shards)], "gather_weights")
    wts = dict(
        norm_mix=norm_mix, norm_mlp=norm_mlp, w_in=g_in, conv_w=conv_full, hg_norm=hg_norm, lb_logits=lb_logits,
        w_out_ab=g_out_ab.reshape(d, d), w_qkv=g_qkv, qg=jnp.tile(q_norm, (1, 2)), kg=jnp.tile(k_norm, (1, 2)),
        w_out_c=g_out_c.reshape(d, d), w1=[g_w1a, g_w1b], w2=[g_w2a.reshape(D_FF, d), g_w2b.reshape(D_FF, d)])

    loss_row, grad_x, small, big = local_step(x[0], loss_target[0], mod, wts)

    flat = jnp.concatenate([small[k].reshape(-1) for k in SMALL_ORDER] + [loss_row[0, :1]])
    n_small = -(-flat.size // 1024) * 1024
    gathered = all_gather_rows(_pad_row(flat, n_small), "gather_small")
    total = sum_rows(gathered.reshape(N_DEV, 8, n_small // 8), "small_sum").reshape(-1)
    sizes = [small[k].size for k in SMALL_ORDER]
    offs = [sum(sizes[:i]) for i in range(len(sizes) + 1)]
    tot = {k: total[offs[i]:offs[i + 1]].reshape(small[k].shape) for i, k in enumerate(SMALL_ORDER)}
    loss = total[offs[-1]]
    mod_rows = gathered[:, :2 * 6 * d].reshape(N_DEV, 2, 6 * d)
    dmod_cols = lax.dynamic_slice(mod_rows, (0, 0, chip * cols), (N_DEV, 2, cols)).transpose(1, 0, 2)
    g_ada_w = ada_w_grad(c_all, dmod_cols)

    as_chunks = lambda g: g.reshape(N_CHIPS, g.shape[0] // N_CHIPS, g.shape[1])
    names = ["w_in", "w_out_ab", "w_qkv", "w_out_c", "w1_0", "w1_1", "w2_0", "w2_1"]
    chunked = [big[k] if big[k].ndim == 3 else as_chunks(big[k]) for k in names]
    r_in, r_out_ab, r_qkv, r_out_c, r_w1a, r_w1b, r_w2a, r_w2b = reduce_to_owner(chunked)

    grads = dict(
        ada_w=g_ada_w, ada_b=tot["mod"], norm_mix=tot["norm_mix"], norm_mlp=tot["norm_mlp"], w_in_ab=r_in[None],
        conv_w=lax.dynamic_slice(tot["conv_w"], (0, chip * HEAD_TILE), (3, HEAD_TILE))[None], hg_norm=tot["hg_norm"],
        lb_logits=tot["lb_logits"], w_out_ab=r_out_ab[None], w_qkv=r_qkv[None], q_norm=tot["q_norm"],
        k_norm=tot["k_norm"], w_out_c=r_out_c[None], mlp_w1=jnp.stack([r_w1a, r_w1b]), mlp_w2=jnp.stack([r_w2a, r_w2b]))
    weights = dict(ada_w=ada_w, ada_b=ada_b, norm_mix=norm_mix, norm_mlp=norm_mlp, w_in_ab=w_in_ab, conv_w=conv_w,
                   hg_norm=hg_norm, lb_logits=lb_logits, w_out_ab=w_out_ab, w_qkv=w_qkv, q_norm=q_norm, k_norm=k_norm,
                   w_out_c=w_out_c, mlp_w1=mlp_w1, mlp_w2=mlp_w2)
    m_in = dict(ada_w=m_ada_w, ada_b=m_ada_b, norm_mix=m_norm_mix, norm_mlp=m_norm_mlp, w_in_ab=m_w_in_ab,
                conv_w=m_conv_w, hg_norm=m_hg_norm, lb_logits=m_lb_logits, w_out_ab=m_w_out_ab, w_qkv=m_w_qkv,
                q_norm=m_q_norm, k_norm=m_k_norm, w_out_c=m_w_out_c, mlp_w1=m_mlp_w1, mlp_w2=m_mlp_w2)
    v_in = dict(ada_w=v_ada_w, ada_b=v_ada_b, norm_mix=v_norm_mix, norm_mlp=v_norm_mlp, w_in_ab=v_w_in_ab,
                conv_w=v_conv_w, hg_norm=v_hg_norm, lb_logits=v_lb_logits, w_out_ab=v_w_out_ab, w_qkv=v_w_qkv,
                q_norm=v_q_norm, k_norm=v_k_norm, w_out_c=v_w_out_c, mlp_w1=v_mlp_w1, mlp_w2=v_mlp_w2)
    order = list(weights)
    large = ("ada_w", "w_in_ab", "w_out_ab", "w_qkv", "w_out_c", "mlp_w1", "mlp_w2")
    delta, new_m, new_v = {}, {}, {}
    for k in large:
        shape = weights[k].shape
        flat2 = lambda a: a.reshape(-1, shape[-1])
        dl, nm, nv = adamw(flat2(weights[k]), flat2(grads[k]), flat2(m_in[k]), flat2(v_in[k]), f"adamw_{k}")
        delta[k], new_m[k], new_v[k] = dl.reshape(shape), nm.reshape(shape), nv.reshape(shape)
    rest = [k for k in order if k not in large]
    n_rest = -(-sum(weights[k].size for k in rest) // 1024) * 1024
    pack = lambda tree: _pad_row(jnp.concatenate([tree[k].reshape(-1) for k in rest]), n_rest).reshape(8, n_rest // 8)
    dl, nm, nv = adamw(pack(weights), pack(grads), pack(m_in), pack(v_in), "adamw_small")
    off = 0
    for k in rest:
        size, shape = weights[k].size, weights[k].shape
        delta[k], new_m[k], new_v[k] = (a.reshape(-1)[off:off + size].reshape(shape) for a in (dl, nm, nv))
        off += size
    grads = {k: grads[k].reshape(weights[k].shape) for k in order}
    return (loss, grad_x[None], *[grads[k] for k in order], *[delta[k] for k in order],
            *[new_m[k] for k in order], *[new_v[k] for k in order])
```

```python
import functools

import jax
import jax.numpy as jnp
from jax import lax
from jax.experimental import pallas as pl
from jax.experimental.pallas import tpu as pltpu

F32 = jnp.float32
BF16 = jnp.bfloat16
HIGHEST = lax.Precision.HIGHEST
MESH = pl.DeviceIdType.MESH

D_MODEL = 1024
D_FF = 4096
CHUNK = 64
HEAD_TILE = 128
SB_HEAD_DIM = 64
CONV_DIM = 512
HG_DIM = 512
AB_IN = 3584
N_CHIPS = 4
N_DEV = 8
EPS = 1e-6
ATT_BLOCK = 128
ATT_TILE = 512
HG_SLAB = 256

ADAM_LR = 0.001
ADAM_B1 = 0.9
ADAM_B2 = 0.999
ADAM_EPS = 1e-08
ADAM_WD = 0.01
ADAM_STEP = 10

NN = (((1,), (0,)), ((), ()))
NT = (((1,), (1,)), ((), ()))
TN = (((0,), (0,)), ((), ()))


def _cp(*dims):
    return pltpu.CompilerParams(dimension_semantics=dims) if dims else pltpu.CompilerParams()


def _dot(a, b, dn, precision=None):
    return lax.dot_general(a, b, dn, preferred_element_type=F32, precision=precision)


def _sigmoid(z):
    return 1.0 / (1.0 + jnp.exp(-z))


def _matmul(a, b, *, dn, grid, a_spec, b_spec, acc_shape, epilogue, extras=(), extra_specs=(),
            out_shapes, out_specs, name):
    nk = grid[2]
    n_extra = len(extras)
    n_out = len(out_shapes)

    def body(*refs):
        a_ref, b_ref = refs[0], refs[1]
        extra_refs = refs[2:2 + n_extra]
        out_refs = refs[2 + n_extra:2 + n_extra + n_out]
        acc_ref = refs[-1]
        k = pl.program_id(2)
        part = _dot(a_ref[...], b_ref[...], dn)

        if nk == 1:
            epilogue(part, None, extra_refs, out_refs)
        else:
            @pl.when(k == 0)
            def _():
                acc_ref[...] = part

            @pl.when(k > 0)
            def _():
                acc_ref[...] += part

            @pl.when(k == nk - 1)
            def _():
                epilogue(acc_ref[...], None, extra_refs, out_refs)

    return pl.pallas_call(
        body, grid=grid, in_specs=[a_spec, b_spec, *extra_specs], out_specs=out_specs, out_shape=out_shapes,
        scratch_shapes=[pltpu.VMEM(acc_shape, F32)], name=name,
        compiler_params=_cp("parallel", "parallel", "arbitrary"))(a, b, *extras)


def _cols(ref, cols):
    return ref.at[:, cols] if cols is not None else ref


def _store(dtype):
    def epilogue(acc, cols, extra_refs, out_refs):
        _cols(out_refs[0], cols)[...] = acc.astype(dtype)
    return epilogue


def _tok_tile(s):
    return min(512, s)


def _matmul_resident(a, w, *, dn, blocks, accumulate, epilogue, extras=(), extra_specs=(), out_shapes, out_specs,
                     name):
    s, ka = a.shape
    tm = _tok_tile(s)
    n_extra = len(extras)

    def body(*refs):
        a_ref, w_ref = refs[0], refs[1]
        extra_refs = refs[2:2 + n_extra]
        out_refs = refs[2 + n_extra:]
        acc = None
        for w_index, a_cols, out_cols in blocks:
            part = _dot(_cols(a_ref, a_cols)[...], w_ref[w_index], dn)
            if accumulate:
                acc = part if acc is None else acc + part
            else:
                epilogue(part, out_cols, extra_refs, out_refs)
        if accumulate:
            epilogue(acc, None, extra_refs, out_refs)

    return pl.pallas_call(
        body, grid=(s // tm,),
        in_specs=[pl.BlockSpec((tm, ka), lambda i: (i, 0)), pl.BlockSpec(w.shape, lambda i: (0,) * w.ndim),
                  *extra_specs],
        out_specs=out_specs, out_shape=out_shapes, name=name, compiler_params=_cp("parallel"))(a, w, *extras)


def _col_blocks(n_blocks, width):
    return [slice(j * width, (j + 1) * width) for j in range(n_blocks)]


def matmul_nn_chunked(a, w, out_dtype, name, epilogue=None, out_shapes=None):
    s = a.shape[0]
    n4 = w.shape[2]
    tm = _tok_tile(s)
    if out_shapes is None:
        out_shapes = [jax.ShapeDtypeStruct((s, N_CHIPS * n4), out_dtype)]
        epilogue = _store(out_dtype)
    return _matmul_resident(
        a, w, dn=NN, blocks=[(j, None, cols) for j, cols in enumerate(_col_blocks(N_CHIPS, n4))], accumulate=False,
        epilogue=epilogue, out_shapes=out_shapes,
        out_specs=[pl.BlockSpec((tm, N_CHIPS * n4), lambda i: (i, 0))] * len(out_shapes), name=name)


def matmul_nt_chunked(dy, w, name):
    s = dy.shape[0]
    _, k, n4 = w.shape
    tm = _tok_tile(s)
    return _matmul_resident(
        dy, w, dn=NT, blocks=[(j, cols, None) for j, cols in enumerate(_col_blocks(N_CHIPS, n4))], accumulate=True,
        epilogue=_store(F32), out_shapes=[jax.ShapeDtypeStruct((s, k), F32)],
        out_specs=[pl.BlockSpec((tm, k), lambda i: (i, 0))], name=name)[0]


def matmul_tn_chunked(x, dy, name):
    s, k = x.shape
    n = dy.shape[1]
    n4 = n // N_CHIPS
    ts = _tok_tile(s)

    def body(x_ref, dy_ref, out_ref):
        xt = x_ref[...].T
        first = pl.program_id(0) == 0
        for j, cols in enumerate(_col_blocks(N_CHIPS, n4)):
            part = _dot(xt, dy_ref[:, cols], NN)

            @pl.when(first)
            def _():
                out_ref[j] = part

            @pl.when(jnp.logical_not(first))
            def _():
                out_ref[j] += part

    return pl.pallas_call(
        body, grid=(s // ts,),
        in_specs=[pl.BlockSpec((ts, k), lambda i: (i, 0)), pl.BlockSpec((ts, n), lambda i: (i, 0))],
        out_specs=pl.BlockSpec((N_CHIPS, k, n4), lambda i: (0, 0, 0)),
        out_shape=jax.ShapeDtypeStruct((N_CHIPS, k, n4), F32), name=name, compiler_params=_cp("arbitrary"))(x, dy)


def matmul_nn_plain(a, w, name, epilogue, extras, extra_specs, out_shapes, out_specs):
    return _matmul_resident(
        a, w, dn=NN, blocks=[((slice(None), slice(None)), None, None)], accumulate=True, epilogue=epilogue,
        extras=extras, extra_specs=extra_specs, out_shapes=out_shapes, out_specs=out_specs, name=name)


def matmul_nt_plain(dy, w, name, epilogue=None, extras=(), extra_specs=(), out_dtype=F32, tn=1024):
    s = dy.shape[0]
    k = w.shape[0]
    tm = _tok_tile(s)
    return _matmul_resident(
        dy, w, dn=NT, blocks=[((cols, slice(None)), None, cols) for cols in _col_blocks(k // tn, tn)],
        accumulate=False, epilogue=epilogue or _store(out_dtype), extras=extras, extra_specs=extra_specs,
        out_shapes=[jax.ShapeDtypeStruct((s, k), out_dtype)], out_specs=[pl.BlockSpec((tm, k), lambda i: (i, 0))],
        name=name)[0]


def matmul_tn_plain(x, dy, name, tk=1024):
    s, k = x.shape
    n = dy.shape[1]
    ts = _tok_tile(s)
    return _matmul(
        x, dy, dn=TN, grid=(k // tk, 1, s // ts),
        a_spec=pl.BlockSpec((ts, tk), lambda i, j, kk: (kk, i)),
        b_spec=pl.BlockSpec((ts, n), lambda i, j, kk: (kk, 0)),
        acc_shape=(tk, n), epilogue=_store(F32),
        out_shapes=[jax.ShapeDtypeStruct((k, n), F32)], out_specs=[pl.BlockSpec((tk, n), lambda i, j, kk: (i, 0))],
        name=name)[0]


def _row_spec(n):
    return pl.BlockSpec((1, n), lambda i: (0, 0))


def norm_mod_fwd(x, g, scale, shift, name):
    s, d = x.shape
    tm = _tok_tile(s)

    def body(x_ref, g_ref, sc_ref, sh_ref, h_ref):
        xv = x_ref[...]
        r = lax.rsqrt(jnp.mean(xv * xv, axis=-1, keepdims=True) + EPS)
        h_ref[...] = ((xv * r * g_ref[...]) * (1.0 + sc_ref[...]) + sh_ref[...]).astype(BF16)

    tile = pl.BlockSpec((tm, d), lambda i: (i, 0))
    return pl.pallas_call(
        body, grid=(s // tm,), in_specs=[tile, _row_spec(d), _row_spec(d), _row_spec(d)], out_specs=tile,
        out_shape=jax.ShapeDtypeStruct((s, d), BF16), name=name, compiler_params=_cp("parallel"))(x, g, scale, shift)


def norm_mod_bwd(dh, x, g, scale, dres, name):
    s, d = x.shape
    tm = _tok_tile(s)

    def body(dh_ref, x_ref, g_ref, sc_ref, dres_ref, dx_ref, dg_ref, dsc_ref, dsh_ref):
        @pl.when(pl.program_id(0) == 0)
        def _():
            dg_ref[...] = jnp.zeros_like(dg_ref)
            dsc_ref[...] = jnp.zeros_like(dsc_ref)
            dsh_ref[...] = jnp.zeros_like(dsh_ref)

        xv = x_ref[...]
        dhv = dh_ref[...]
        r = lax.rsqrt(jnp.mean(xv * xv, axis=-1, keepdims=True) + EPS)
        xn = xv * r
        gv = g_ref[...]
        s1 = 1.0 + sc_ref[...]
        dsh_ref[...] += jnp.sum(dhv, axis=0, keepdims=True)
        dsc_ref[...] += jnp.sum(dhv * xn * gv, axis=0, keepdims=True)
        dg_ref[...] += jnp.sum(dhv * xn * s1, axis=0, keepdims=True)
        dxn = dhv * gv * s1
        dx_ref[...] = dres_ref[...] + r * (dxn - xn * jnp.mean(dxn * xn, axis=-1, keepdims=True))

    tile = pl.BlockSpec((tm, d), lambda i: (i, 0))
    row = jax.ShapeDtypeStruct((1, d), F32)
    return pl.pallas_call(
        body, grid=(s // tm,), in_specs=[tile, tile, _row_spec(d), _row_spec(d), tile],
        out_specs=[tile, _row_spec(d), _row_spec(d), _row_spec(d)],
        out_shape=[jax.ShapeDtypeStruct((s, d), F32), row, row, row], name=name,
        compiler_params=_cp("arbitrary"))(dh, x, g, scale, dres)


def gate_bwd(dx, y, gate, name):
    s, d = dx.shape
    tm = _tok_tile(s)

    def body(dx_ref, y_ref, gate_ref, dy_ref, dgate_ref):
        @pl.when(pl.program_id(0) == 0)
        def _():
            dgate_ref[...] = jnp.zeros_like(dgate_ref)

        dxv = dx_ref[...]
        dy_ref[...] = (gate_ref[...] * dxv).astype(BF16)
        dgate_ref[...] += jnp.sum(dxv * y_ref[...], axis=0, keepdims=True)

    tile = pl.BlockSpec((tm, d), lambda i: (i, 0))
    return pl.pallas_call(
        body, grid=(s // tm,), in_specs=[tile, tile, _row_spec(d)], out_specs=[tile, _row_spec(d)],
        out_shape=[jax.ShapeDtypeStruct((s, d), BF16), jax.ShapeDtypeStruct((1, d), F32)], name=name,
        compiler_params=_cp("arbitrary"))(dx, y, gate)


def loss_and_grad(y, target):
    s, d = y.shape
    tm = _tok_tile(s)

    def body(y_ref, t_ref, dy_ref, loss_ref):
        @pl.when(pl.program_id(0) == 0)
        def _():
            loss_ref[...] = jnp.zeros_like(loss_ref)

        err = y_ref[...] - t_ref[...]
        dy_ref[...] = err * (1.0 / d)
        loss_ref[...] += jnp.sum(err * err) * (0.5 / d)

    tile = pl.BlockSpec((tm, d), lambda i: (i, 0))
    return pl.pallas_call(
        body, grid=(s // tm,), in_specs=[tile, tile], out_specs=[tile, _row_spec(128)],
        out_shape=[jax.ShapeDtypeStruct((s, d), F32), jax.ShapeDtypeStruct((1, 128), F32)], name="loss_and_grad",
        compiler_params=_cp("arbitrary"))(y, target)


def _proj_residual(acc, cols, extra_refs, out_refs):
    x_ref, gate_ref = extra_refs
    out_refs[0][...] = acc
    out_refs[1][...] = x_ref[...] + gate_ref[...] * acc


def proj_residual(a, w, x, gate, name):
    s, d = x.shape
    tm = _tok_tile(s)
    tile = pl.BlockSpec((tm, d), lambda i: (i, 0))
    shape = jax.ShapeDtypeStruct((s, d), F32)
    return matmul_nn_plain(
        a, w, name, _proj_residual, extras=(x, gate), extra_specs=(tile, _row_spec(d)),
        out_shapes=[shape, shape], out_specs=[tile, tile])


def _mlp_up(acc, cols, extra_refs, out_refs):
    r = jnp.maximum(acc, 0.0)
    _cols(out_refs[0], cols)[...] = (r * r).astype(BF16)
    _cols(out_refs[1], cols)[...] = r.astype(BF16)


def mlp_up(h, w1, name):
    shape = jax.ShapeDtypeStruct((h.shape[0], N_CHIPS * w1.shape[2]), BF16)
    return matmul_nn_chunked(h, w1, BF16, name, epilogue=_mlp_up, out_shapes=[shape, shape])


def _dact(acc, cols, extra_refs, out_refs):
    _cols(out_refs[0], cols)[...] = (acc * (2.0 * _cols(extra_refs[0], cols)[...].astype(F32))).astype(BF16)


def mlp_down_bwd(dy, w2, r, name):
    s = dy.shape[0]
    tm = _tok_tile(s)
    return matmul_nt_plain(dy, w2, name, epilogue=_dact, extras=(r,),
                           extra_specs=(pl.BlockSpec((tm, r.shape[1]), lambda i: (i, 0)),), out_dtype=BF16)


def _shift_down(p, n, row):
    return jnp.where(row >= n, pltpu.roll(p, n, 0), 0.0)


def _shift_up(p, n, row):
    rows = p.shape[0]
    return jnp.where(row < rows - n, pltpu.roll(p, rows - n, 0), 0.0)


def _u_col(block):
    return lambda i: (0, block + i)


def conv_mixer_fwd(u, conv_w):
    s = u.shape[0]
    nb = CONV_DIM // HEAD_TILE

    def body(ab_ref, ac_ref, ah_ref, w_ref, y_ref):
        row = lax.broadcasted_iota(jnp.int32, (s, HEAD_TILE), 0)
        p = ac_ref[...] * ah_ref[...]
        w = w_ref[...]
        conv = w[0:1] * _shift_down(p, 2, row) + w[1:2] * _shift_down(p, 1, row) + w[2:3] * p
        y_ref[...] = (ab_ref[...] * conv).astype(BF16)

    col = lambda b: pl.BlockSpec((s, HEAD_TILE), _u_col(b * nb))
    return pl.pallas_call(
        body, grid=(nb,), in_specs=[col(0), col(1), col(2), pl.BlockSpec((3, HEAD_TILE), lambda i: (0, i))],
        out_specs=pl.BlockSpec((s, HEAD_TILE), lambda i: (0, i)),
        out_shape=jax.ShapeDtypeStruct((s, CONV_DIM), BF16), name="conv_mixer_fwd",
        compiler_params=_cp("parallel"))(u, u, u, conv_w)


def conv_mixer_bwd(dmix, u, conv_w):
    s = u.shape[0]
    nb = CONV_DIM // HEAD_TILE

    def body(dy_ref, ab_ref, ac_ref, ah_ref, w_ref, dab_ref, dac_ref, dah_ref, dw_ref):
        row = lax.broadcasted_iota(jnp.int32, (s, HEAD_TILE), 0)
        ac = ac_ref[...]
        ah = ah_ref[...]
        p = ac * ah
        w = w_ref[...]
        p1 = _shift_down(p, 1, row)
        p2 = _shift_down(p, 2, row)
        conv = w[0:1] * p2 + w[1:2] * p1 + w[2:3] * p
        dy = dy_ref[...]
        dab_ref[...] = (dy * conv).astype(BF16)
        dconv = dy * ab_ref[...]
        dp = w[0:1] * _shift_up(dconv, 2, row) + w[1:2] * _shift_up(dconv, 1, row) + w[2:3] * dconv
        dac_ref[...] = (dp * ah).astype(BF16)
        dah_ref[...] = (dp * ac).astype(BF16)
        dw_ref[...] = jnp.concatenate(
            [jnp.sum(dconv * p2, axis=0, keepdims=True), jnp.sum(dconv * p1, axis=0, keepdims=True),
             jnp.sum(dconv * p, axis=0, keepdims=True)], axis=0)

    col = lambda b: pl.BlockSpec((s, HEAD_TILE), _u_col(b * nb))
    out = pl.BlockSpec((s, HEAD_TILE), lambda i: (0, i))
    wspec = pl.BlockSpec((3, HEAD_TILE), lambda i: (0, i))
    shape = jax.ShapeDtypeStruct((s, CONV_DIM), BF16)
    return pl.pallas_call(
        body, grid=(nb,), in_specs=[out, col(0), col(1), col(2), wspec], out_specs=[out, out, out, wspec],
        out_shape=[shape, shape, shape, jax.ShapeDtypeStruct((3, CONV_DIM), F32)], name="conv_mixer_bwd",
        compiler_params=_cp("parallel"))(dmix, u, u, u, conv_w)


def _chunk_cumsum(g, pos):
    for sh in (1, 2, 4, 8, 16, 32):
        g = g + jnp.where(pos >= sh, pltpu.roll(g, sh, 0), 0.0)
    return g


def _chunk_rev_cumsum(g, pos):
    rows = g.shape[0]
    for sh in (1, 2, 4, 8, 16, 32):
        g = g + jnp.where(pos < CHUNK - sh, pltpu.roll(g, rows - sh, 0), 0.0)
    return g


def _lower_bound(lb_ref):
    logits = lb_ref[...]
    e = jnp.exp(logits - jnp.max(logits, axis=0, keepdims=True))
    p = e / jnp.sum(e, axis=0, keepdims=True)
    return p[0:1], p


def _hg_gates(hf, lb):
    sg = _sigmoid(hf)
    f = lb + (1.0 - lb) * sg
    return sg, f, jnp.log(f), 1.0 - f


def _hg_specs(s, slab, order):
    n = s // slab
    col = lambda b: pl.BlockSpec((slab, HEAD_TILE), lambda h, i: (order(i, n), b + h))
    return n, col


def _to_chunks(ref3, val):
    for c in range(ref3.shape[0]):
        ref3[c] = val[c * CHUNK:(c + 1) * CHUNK]


def _from_chunks(ref3):
    return jnp.concatenate([ref3[c] for c in range(ref3.shape[0])], axis=0)


def _rolled(x, r):
    return pltpu.roll(x, r, 1) if r else x


def _decay_window(ba, bs, r, row):
    lam = jnp.exp(ba - _rolled(bs, r))
    return jnp.where(row >= r, lam, 0.0) if r else lam


def hgrn_fwd(u, lb_logits, hg_norm):
    s = u.shape[0]
    slab = min(HG_SLAB, s)
    cps = slab // CHUNK
    n, col = _hg_specs(s, slab, lambda i, n_: i)
    heads = HG_DIM // HEAD_TILE

    def body(q_ref, f_ref, i_ref, g_ref, lb_ref, nw_ref, y_ref, o_ref, st_ref, at_ref, q3, b3, k3, v3, o3):
        @pl.when(pl.program_id(1) == 0)
        def _():
            at_ref[...] = jnp.zeros_like(at_ref)

        pos = lax.broadcasted_iota(jnp.int32, (slab, HEAD_TILE), 0) & (CHUNK - 1)
        lb, _ = _lower_bound(lb_ref)
        q = q_ref[...]
        v = i_ref[...]
        _, _, g, kk = _hg_gates(f_ref[...], lb)
        b = _chunk_cumsum(g, pos)
        for ref, val in ((q3, q), (b3, b), (k3, kk), (v3, v)):
            _to_chunks(ref, val)

        for a in range(CHUNK // 8):
            wn = CHUNK - 8 * a
            qa, ba = q3[:, 8 * a:, :], b3[:, 8 * a:, :]
            bs, ks, vs = b3[:, :wn, :], k3[:, :wn, :], v3[:, :wn, :]
            row = lax.broadcasted_iota(jnp.int32, (cps, wn, HEAD_TILE), 1)
            acc = jnp.zeros((cps, wn, HEAD_TILE), F32)
            for r in range(8):
                lam = _decay_window(ba, bs, r, row)
                sc = jnp.sum(qa * _rolled(ks, r) * lam, axis=-1, keepdims=True)
                acc = acc + sc * _rolled(vs, r)
            if a == 0:
                o3[...] = acc
            else:
                o3[:, 8 * a:, :] += acc

        qhat = q * jnp.exp(b)
        for c in range(cps):
            rows = slice(c * CHUNK, (c + 1) * CHUNK)
            at = at_ref[...]
            st_ref[c] = at
            o_ref[rows, :] = o3[c] + _dot(qhat[rows], at, NT, HIGHEST)
            bc = b[(c + 1) * CHUNK - 1:(c + 1) * CHUNK]
            khat = kk[rows] * jnp.exp(bc - b[rows])
            at_ref[...] = at * jnp.exp(bc) + _dot(v[rows], khat, TN, HIGHEST)

        o = o_ref[...]
        r = lax.rsqrt(jnp.mean(o * o, axis=-1, keepdims=True) + EPS)
        hg = g_ref[...]
        y_ref[...] = (o * r * nw_ref[...] * (hg * _sigmoid(hg))).astype(BF16)

    out = pl.BlockSpec((slab, HEAD_TILE), lambda h, i: (i, h))
    par = lambda rows: pl.BlockSpec((rows, HEAD_TILE), lambda h, i: (0, h))
    return pl.pallas_call(
        body, grid=(heads, n), in_specs=[col(12), col(16), col(20), col(24), par(3), par(1)],
        out_specs=[out, out, pl.BlockSpec((None, cps, HEAD_TILE, HEAD_TILE), lambda h, i: (h, i, 0, 0))],
        out_shape=[jax.ShapeDtypeStruct((s, HG_DIM), BF16), jax.ShapeDtypeStruct((s, HG_DIM), F32),
                   jax.ShapeDtypeStruct((heads, s // CHUNK, HEAD_TILE, HEAD_TILE), F32)],
        scratch_shapes=[pltpu.VMEM((HEAD_TILE, HEAD_TILE), F32)] + [pltpu.VMEM((cps, CHUNK, HEAD_TILE), F32)] * 5,
        name="hgrn_fwd",
        compiler_params=_cp("parallel", "arbitrary"))(u, u, u, u, lb_logits, hg_norm)


def hgrn_bwd(dmix, u, o_raw, states, lb_logits, hg_norm):
    s = u.shape[0]
    slab = min(HG_SLAB, s)
    cps = slab // CHUNK
    n, col = _hg_specs(s, slab, lambda i, n_: n_ - 1 - i)
    heads = HG_DIM // HEAD_TILE

    def body(dy_ref, q_ref, f_ref, i_ref, g_ref, o_ref, st_ref, lb_ref, nw_ref,
             dq_ref, df_ref, di_ref, dg_ref, dnw_ref, dlb_ref, dat_ref, dlbacc_ref, dbc_ref,
             q3, b3, k3, v3, do3, dq3, dk3, dv3):
        step = pl.program_id(1)

        @pl.when(step == 0)
        def _():
            dat_ref[...] = jnp.zeros_like(dat_ref)
            dlbacc_ref[...] = jnp.zeros_like(dlbacc_ref)
            dnw_ref[...] = jnp.zeros_like(dnw_ref)

        pos = lax.broadcasted_iota(jnp.int32, (slab, HEAD_TILE), 0) & (CHUNK - 1)
        lb, probs = _lower_bound(lb_ref)
        q = q_ref[...]
        v = i_ref[...]
        sg_f, f, g, kk = _hg_gates(f_ref[...], lb)
        b = _chunk_cumsum(g, pos)

        o = o_ref[...]
        nw = nw_ref[...]
        r = lax.rsqrt(jnp.mean(o * o, axis=-1, keepdims=True) + EPS)
        hg = g_ref[...]
        sg = _sigmoid(hg)
        dy = dy_ref[...]
        d_on = dy * (hg * sg)
        dg_ref[...] = (dy * (o * r * nw) * (sg * (1.0 + hg * (1.0 - sg)))).astype(BF16)
        dnw_ref[...] += jnp.sum(d_on * o * r, axis=0, keepdims=True)
        t1 = d_on * nw
        do = r * t1 - o * (r * r * r) * jnp.mean(t1 * o, axis=-1, keepdims=True)

        eb = jnp.exp(b)
        qhat = q * eb
        dbc_ref[...] = jnp.zeros_like(dbc_ref)
        for c in reversed(range(cps)):
            rows = slice(c * CHUNK, (c + 1) * CHUNK)
            last = (c + 1) * CHUNK - 1
            at = st_ref[c]
            dat = dat_ref[...]
            bc = b[last:last + 1]
            ebc = jnp.exp(bc)
            dec = jnp.exp(bc - b[rows])
            khat = kk[rows] * dec
            at_next = at * ebc + _dot(v[rows], khat, TN, HIGHEST)
            dbc_ref[last:last + 1, :] = jnp.sum(dat * at_next, axis=0, keepdims=True)
            dq3[c] = eb[rows] * _dot(do[rows], at, NN, HIGHEST)
            dk3[c] = dec * _dot(v[rows], dat, NN, HIGHEST)
            dv3[c] = _dot(khat, dat, NT, HIGHEST)
            dat_ref[...] = dat * ebc + _dot(do[rows], qhat[rows], TN, HIGHEST)

        for ref, val in ((q3, q), (b3, b), (k3, kk), (v3, v), (do3, do)):
            _to_chunks(ref, val)
        for a in range(CHUNK // 8):
            wn = CHUNK - 8 * a
            qa, ba, doa = q3[:, 8 * a:, :], b3[:, 8 * a:, :], do3[:, 8 * a:, :]
            bs, ks, vs = b3[:, :wn, :], k3[:, :wn, :], v3[:, :wn, :]
            row = lax.broadcasted_iota(jnp.int32, (cps, wn, HEAD_TILE), 1)
            zero = jnp.zeros((cps, wn, HEAD_TILE), F32)
            dqa, dka, dva = zero, zero, zero
            for r in range(8):
                lam = _decay_window(ba, bs, r, row)
                kd, vd = _rolled(ks, r), _rolled(vs, r)
                sc = jnp.sum(qa * kd * lam, axis=-1, keepdims=True)
                pd = jnp.sum(doa * vd, axis=-1, keepdims=True)
                dqa = dqa + pd * kd * lam
                dka = dka + _rolled(pd * qa * lam, (wn - r) % wn)
                dva = dva + _rolled(sc * doa, (wn - r) % wn)
            dq3[:, 8 * a:, :] += dqa
            dk3[:, :wn, :] += dka
            dv3[:, :wn, :] += dva
        dq, dk, dv = _from_chunks(dq3), _from_chunks(dk3), _from_chunks(dv3)

        db = q * dq - kk * dk + dbc_ref[...]
        dgl = _chunk_rev_cumsum(db, pos)
        dfv = dgl / f - dk
        dq_ref[...] = dq.astype(BF16)
        di_ref[...] = dv.astype(BF16)
        df_ref[...] = (dfv * (1.0 - lb) * sg_f * (1.0 - sg_f)).astype(BF16)
        dlbacc_ref[...] += jnp.sum(dfv * (1.0 - sg_f), axis=0, keepdims=True)

        @pl.when(step == n - 1)
        def _():
            dlb = dlbacc_ref[...]
            sel = (lax.broadcasted_iota(jnp.int32, (3, HEAD_TILE), 0) == 0).astype(F32)
            dlb_ref[...] = dlb * probs[0:1] * (sel - probs)

    out = pl.BlockSpec((slab, HEAD_TILE), lambda h, i: (n - 1 - i, h))
    par = lambda rows: pl.BlockSpec((rows, HEAD_TILE), lambda h, i: (0, h))
    dyspec = pl.BlockSpec((slab, HEAD_TILE), lambda h, i: (n - 1 - i, CONV_DIM // HEAD_TILE + h))
    shape = jax.ShapeDtypeStruct((s, HG_DIM), BF16)
    slab_f32 = pltpu.VMEM((slab, HEAD_TILE), F32)
    return pl.pallas_call(
        body, grid=(heads, n),
        in_specs=[dyspec, col(12), col(16), col(20), col(24), out,
                  pl.BlockSpec((None, cps, HEAD_TILE, HEAD_TILE), lambda h, i: (h, n - 1 - i, 0, 0)), par(3), par(1)],
        out_specs=[out, out, out, out, par(1), par(3)],
        out_shape=[shape, shape, shape, shape, jax.ShapeDtypeStruct((1, HG_DIM), F32),
                   jax.ShapeDtypeStruct((3, HG_DIM), F32)],
        scratch_shapes=[pltpu.VMEM((HEAD_TILE, HEAD_TILE), F32), pltpu.VMEM((1, HEAD_TILE), F32), slab_f32]
        + [pltpu.VMEM((cps, CHUNK, HEAD_TILE), F32)] * 8,
        name="hgrn_bwd", compiler_params=_cp("parallel", "arbitrary"))(
            dmix, u, u, u, u, o_raw, states, lb_logits, hg_norm)


def _pair_rstd(x, lo):
    x2 = x * x
    s_lo = jnp.sum(jnp.where(lo, x2, 0.0), axis=-1, keepdims=True)
    s_hi = jnp.sum(jnp.where(lo, 0.0, x2), axis=-1, keepdims=True)
    inv = 1.0 / SB_HEAD_DIM
    return jnp.where(lo, lax.rsqrt(s_lo * inv + EPS), lax.rsqrt(s_hi * inv + EPS))


def _pair_mean(x, lo):
    s_lo = jnp.sum(jnp.where(lo, x, 0.0), axis=-1, keepdims=True)
    s_hi = jnp.sum(jnp.where(lo, 0.0, x), axis=-1, keepdims=True)
    return jnp.where(lo, s_lo, s_hi) * (1.0 / SB_HEAD_DIM)


def qk_norm_fwd(qkv, qg, kg):
    s = qkv.shape[0]
    tm = _tok_tile(s)
    nt = D_MODEL // HEAD_TILE

    def body(q_ref, k_ref, v_ref, qg_ref, kg_ref, qn_ref, kn_ref, vb_ref):
        lo = lax.broadcasted_iota(jnp.int32, (tm, HEAD_TILE), 1) < SB_HEAD_DIM
        qv = q_ref[...]
        kv = k_ref[...]
        qn_ref[...] = (qv * _pair_rstd(qv, lo) * qg_ref[...]).astype(BF16)
        kn_ref[...] = (kv * _pair_rstd(kv, lo) * kg_ref[...]).astype(BF16)
        vb_ref[...] = v_ref[...].astype(BF16)

    col = lambda b: pl.BlockSpec((tm, HEAD_TILE), lambda i, j: (i, b * nt + j))
    gain = pl.BlockSpec((1, HEAD_TILE), lambda i, j: (0, 0))
    out = pl.BlockSpec((tm, HEAD_TILE), lambda i, j: (i, j))
    shape = jax.ShapeDtypeStruct((s, D_MODEL), BF16)
    return pl.pallas_call(
        body, grid=(s // tm, nt), in_specs=[col(0), col(1), col(2), gain, gain], out_specs=[out, out, out],
        out_shape=[shape, shape, shape], name="qk_norm_fwd", compiler_params=_cp("parallel", "parallel"))(
            qkv, qkv, qkv, qg, kg)


def qk_norm_bwd(dqn, dkn, dv, qkv, qg, kg):
    s = qkv.shape[0]
    tm = _tok_tile(s)
    nt = D_MODEL // HEAD_TILE

    def body(dqn_ref, dkn_ref, dv_ref, q_ref, k_ref, qg_ref, kg_ref, dq_ref, dk_ref, dvb_ref, dqg_ref, dkg_ref):
        @pl.when((pl.program_id(0) == 0) & (pl.program_id(1) == 0))
        def _():
            dqg_ref[...] = jnp.zeros_like(dqg_ref)
            dkg_ref[...] = jnp.zeros_like(dkg_ref)

        lo = lax.broadcasted_iota(jnp.int32, (tm, HEAD_TILE), 1) < SB_HEAD_DIM

        def one(x_ref, g_ref, dn_ref, dx_ref, dgain_ref):
            xv = x_ref[...]
            r = _pair_rstd(xv, lo)
            xn = xv * r
            dn = dn_ref[...]
            dgain_ref[...] += jnp.sum(dn * xn, axis=0, keepdims=True)
            t1 = dn * g_ref[...]
            dx_ref[...] = (r * (t1 - xn * _pair_mean(t1 * xn, lo))).astype(BF16)

        one(q_ref, qg_ref, dqn_ref, dq_ref, dqg_ref)
        one(k_ref, kg_ref, dkn_ref, dk_ref, dkg_ref)
        dvb_ref[...] = dv_ref[...].astype(BF16)

    col = lambda b: pl.BlockSpec((tm, HEAD_TILE), lambda i, j: (i, b * nt + j))
    gain = pl.BlockSpec((1, HEAD_TILE), lambda i, j: (0, 0))
    out = pl.BlockSpec((tm, HEAD_TILE), lambda i, j: (i, j))
    shape = jax.ShapeDtypeStruct((s, D_MODEL), BF16)
    grow = jax.ShapeDtypeStruct((1, HEAD_TILE), F32)
    return pl.pallas_call(
        body, grid=(s // tm, nt), in_specs=[out, out, out, col(0), col(1), gain, gain],
        out_specs=[out, out, out, gain, gain], out_shape=[shape, shape, shape, grow, grow], name="qk_norm_bwd",
        compiler_params=_cp("arbitrary", "arbitrary"))(dqn, dkn, dv, qkv, qkv, qg, kg)


def _split_dot(x, u):
    hi = x.astype(BF16)
    lo = (x - hi.astype(F32)).astype(BF16)
    return _dot(hi, u, NN) + _dot(lo, u, NN)


def _sb_tile(qs, kb, carry, causal, suffix, diag):
    z = _dot(qs, kb, NT)
    lb = jnp.minimum(z, 0.0) - jnp.log(1.0 + jnp.exp(-jnp.abs(z)))
    lom = lb - z
    if diag:
        lom = jnp.where(causal, lom, 0.0)
    ws = []
    for j in reversed(range(ATT_TILE // ATT_BLOCK)):
        cols = slice(j * ATT_BLOCK, (j + 1) * ATT_BLOCK)
        ws.append(jnp.exp(z[:, cols] + _split_dot(lom[:, cols], suffix) + carry))
        carry = carry + jnp.sum(lom[:, cols], axis=-1, keepdims=True)
    w = jnp.concatenate(ws[::-1], axis=1)
    if diag:
        w = jnp.where(causal, w, 0.0)
    return lb, w, carry


def _sb_consts():
    row = lax.broadcasted_iota(jnp.int32, (ATT_BLOCK, ATT_BLOCK), 0)
    col = lax.broadcasted_iota(jnp.int32, (ATT_BLOCK, ATT_BLOCK), 1)
    suffix = (row >= col).astype(BF16)
    trow = lax.broadcasted_iota(jnp.int32, (ATT_TILE, ATT_TILE), 0)
    tcol = lax.broadcasted_iota(jnp.int32, (ATT_TILE, ATT_TILE), 1)
    causal = tcol < trow
    lo = lax.broadcasted_iota(jnp.int32, (ATT_TILE, HEAD_TILE), 1) < SB_HEAD_DIM
    return suffix, causal, lo


def _rows(i):
    return pl.ds(pl.multiple_of(i * ATT_TILE, ATT_TILE), ATT_TILE)


def _head_query(qb, mask):
    return (jnp.where(mask, qb, 0.0) * (SB_HEAD_DIM ** -0.5)).astype(BF16)


def sb_attn_fwd(qn, kn, vb):
    s = qn.shape[0]
    nq = s // ATT_TILE
    nt = D_MODEL // HEAD_TILE

    def body(q_ref, k_ref, v_ref, o_ref, ob_ref):
        suffix, causal, lo = _sb_consts()

        def qtile(qi, _):
            qb = q_ref[_rows(qi), :].astype(F32)
            qs = [_head_query(qb, lo), _head_query(qb, ~lo)]

            def step(kj, state, diag):
                kb = k_ref[_rows(kj), :]
                vt = v_ref[_rows(kj), :]
                out = []
                for hh in range(2):
                    carry, acc = state[hh]
                    _, w, carry = _sb_tile(qs[hh], kb, carry, causal, suffix, diag)
                    out.append((carry, acc + _dot(w.astype(BF16), vt, NN)))
                return tuple(out)

            start = (jnp.zeros((ATT_TILE, 1), F32), jnp.zeros((ATT_TILE, HEAD_TILE), F32))
            state = step(qi, (start, start), True)
            state = lax.fori_loop(0, qi, lambda jj, st: step(qi - 1 - jj, st, False), state)
            o = jnp.where(lo, state[0][1], state[1][1])
            o_ref[_rows(qi), :] = o
            ob_ref[_rows(qi), :] = o.astype(BF16)
            return 0

        lax.fori_loop(0, nq, qtile, 0)

    spec = pl.BlockSpec((s, HEAD_TILE), lambda p: (0, p))
    return pl.pallas_call(
        body, grid=(nt,), in_specs=[spec, spec, spec], out_specs=[spec, spec],
        out_shape=[jax.ShapeDtypeStruct((s, D_MODEL), F32), jax.ShapeDtypeStruct((s, D_MODEL), BF16)],
        name="sb_attn_fwd", compiler_params=_cp("parallel"))(qn, kn, vb)


def sb_attn_bwd(qn, kn, vb, o, do):
    s = qn.shape[0]
    nq = s // ATT_TILE
    nt = D_MODEL // HEAD_TILE

    def body(q_ref, k_ref, v_ref, o_ref, do_ref, dq_ref, dk_ref, dv_ref):
        suffix, causal, lo = _sb_consts()
        dk_ref[...] = jnp.zeros_like(dk_ref)
        dv_ref[...] = jnp.zeros_like(dv_ref)

        def qtile(qi, _):
            qb = q_ref[_rows(qi), :].astype(F32)
            dob = do_ref[_rows(qi), :].astype(BF16).astype(F32)
            prod = dob * o_ref[_rows(qi), :]
            masks = [lo, ~lo]
            qs = [_head_query(qb, m) for m in masks]
            dos = [jnp.where(m, dob, 0.0).astype(BF16) for m in masks]
            totals = [jnp.sum(jnp.where(m, prod, 0.0), axis=-1, keepdims=True) for m in masks]

            def step(kj, state, diag):
                kb = k_ref[_rows(kj), :]
                vt = v_ref[_rows(kj), :]
                out = []
                dk = jnp.zeros((ATT_TILE, HEAD_TILE), F32)
                dv = jnp.zeros((ATT_TILE, HEAD_TILE), F32)
                for hh in range(2):
                    carry, carry_e, dq = state[hh]
                    lb, w, carry = _sb_tile(qs[hh], kb, carry, causal, suffix, diag)
                    wb = w.astype(BF16)
                    e = _dot(dos[hh], vt, NT) * wb.astype(F32)
                    befores = []
                    for j in reversed(range(ATT_TILE // ATT_BLOCK)):
                        cols = slice(j * ATT_BLOCK, (j + 1) * ATT_BLOCK)
                        befores.append(totals[hh] - carry_e - _split_dot(e[:, cols], suffix))
                        carry_e = carry_e + jnp.sum(e[:, cols], axis=-1, keepdims=True)
                    before = jnp.concatenate(befores[::-1], axis=1)
                    beta = jnp.exp(lb)
                    dz = e * (1.0 - beta) - before * beta
                    if diag:
                        dz = jnp.where(causal, dz, 0.0)
                    dzb = dz.astype(BF16)
                    dq = dq + _dot(dzb, kb, NN)
                    dk = dk + _dot(dzb, qs[hh], TN)
                    dv = dv + _dot(wb, dos[hh], TN)
                    out.append((carry, carry_e, dq))
                dk_ref[_rows(kj), :] += dk
                dv_ref[_rows(kj), :] += dv
                return tuple(out)

            zero = jnp.zeros((ATT_TILE, 1), F32)
            start = (zero, zero, jnp.zeros((ATT_TILE, HEAD_TILE), F32))
            state = step(qi, (start, start), True)
            state = lax.fori_loop(0, qi, lambda jj, st: step(qi - 1 - jj, st, False), state)
            dq_ref[_rows(qi), :] = jnp.where(lo, state[0][2], state[1][2]) * (SB_HEAD_DIM ** -0.5)
            return 0

        lax.fori_loop(0, nq, qtile, 0)

    spec = pl.BlockSpec((s, HEAD_TILE), lambda p: (0, p))
    shape = jax.ShapeDtypeStruct((s, D_MODEL), F32)
    return pl.pallas_call(
        body, grid=(nt,), in_specs=[spec] * 5, out_specs=[spec] * 3, out_shape=[shape] * 3,
        name="sb_attn_bwd", compiler_params=_cp("parallel"))(qn, kn, vb, o, do)


def ada_fwd(c_all, ada_w, ada_b_shard):
    layers, d, cols = ada_w.shape
    tn = 512

    def body(c_ref, w_ref, b_ref, out_ref):
        cv = c_ref[...]
        act = (cv * _sigmoid(cv)).astype(BF16)
        out_ref[...] = _dot(act, w_ref[...].astype(BF16), NN) + b_ref[...]

    return pl.pallas_call(
        body, grid=(layers, cols // tn),
        in_specs=[pl.BlockSpec((N_DEV, d), lambda l, j: (0, 0)), pl.BlockSpec((None, d, tn), lambda l, j: (l, 0, j)),
                  pl.BlockSpec((None, 1, tn), lambda l, j: (l, 0, j))],
        out_specs=pl.BlockSpec((None, N_DEV, tn), lambda l, j: (l, 0, j)),
        out_shape=jax.ShapeDtypeStruct((layers, N_DEV, cols), F32), name="ada_fwd",
        compiler_params=_cp("parallel", "parallel"))(c_all, ada_w, ada_b_shard)


def ada_w_grad(c_all, dmod):
    layers, _, cols = dmod.shape
    d = c_all.shape[1]
    tn = 512

    def body(c_ref, dm_ref, out_ref):
        cv = c_ref[...]
        out_ref[...] = _dot(cv * _sigmoid(cv), dm_ref[...], TN, HIGHEST)

    return pl.pallas_call(
        body, grid=(layers, cols // tn),
        in_specs=[pl.BlockSpec((N_DEV, d), lambda l, j: (0, 0)), pl.BlockSpec((None, N_DEV, tn), lambda l, j: (l, 0, j))],
        out_specs=pl.BlockSpec((None, d, tn), lambda l, j: (l, 0, j)),
        out_shape=jax.ShapeDtypeStruct((layers, d, cols), F32), name="ada_w_grad",
        compiler_params=_cp("parallel", "parallel"))(c_all, dmod)


def _row_tile(r, c, elems):
    best = 8
    for t in range(8, r + 1, 8):
        if r % t == 0 and t * c <= elems:
            best = t
    return best


def sum_rows(x, name):
    n, r, c = x.shape
    tr = _row_tile(r, c, 1 << 17)

    def body(x_ref, out_ref):
        acc = x_ref[0]
        for i in range(1, n):
            acc = acc + x_ref[i]
        out_ref[...] = acc

    return pl.pallas_call(
        body, grid=(r // tr,), in_specs=[pl.BlockSpec((n, tr, c), lambda i: (0, i, 0))],
        out_specs=pl.BlockSpec((tr, c), lambda i: (i, 0)), out_shape=jax.ShapeDtypeStruct((r, c), F32), name=name,
        compiler_params=_cp("parallel"))(x)


def adamw(w, g, m, v, name):
    r, c = w.shape
    tr = _row_tile(r, c, 1 << 17)
    c1 = 1.0 - ADAM_B1 ** ADAM_STEP
    c2 = 1.0 - ADAM_B2 ** ADAM_STEP

    def body(w_ref, g_ref, m_ref, v_ref, d_ref, nm_ref, nv_ref):
        gv = g_ref[...]
        nm = ADAM_B1 * m_ref[...] + (1.0 - ADAM_B1) * gv
        nv = ADAM_B2 * v_ref[...] + (1.0 - ADAM_B2) * (gv * gv)
        d_ref[...] = -ADAM_LR * ((nm / c1) / (jnp.sqrt(nv / c2) + ADAM_EPS) + ADAM_WD * w_ref[...])
        nm_ref[...] = nm
        nv_ref[...] = nv

    spec = pl.BlockSpec((tr, c), lambda i: (i, 0))
    shape = jax.ShapeDtypeStruct((r, c), F32)
    return pl.pallas_call(
        body, grid=(r // tr,), in_specs=[spec] * 4, out_specs=[spec] * 3, out_shape=[shape] * 3, name=name,
        compiler_params=_cp("parallel"))(w, g, m, v)


def _me():
    return lax.axis_index("x"), lax.axis_index("y"), lax.axis_index("c")


def _flip(v, bit):
    return 1 - v if bit else v


HBM = pl.BlockSpec(memory_space=pl.ANY)
VMEM = pl.BlockSpec(memory_space=pltpu.VMEM)


def all_gather_rows(v, name):
    n = v.shape[1]

    def body(v_ref, out_ref, send_sems, recv_sems):
        x, y, c = _me()
        me = 4 * x + 2 * y + c
        out_ref[pl.ds(me, 1), :] = v_ref[...]
        copies = []
        for j in range(1, N_DEV):
            peer = (_flip(x, j & 4), _flip(y, j & 2), _flip(c, j & 1))
            copies.append(pltpu.make_async_remote_copy(
                src_ref=v_ref, dst_ref=out_ref.at[pl.ds(me, 1), :], send_sem=send_sems.at[j - 1],
                recv_sem=recv_sems.at[j - 1], device_id=peer, device_id_type=MESH))
        for cp in copies:
            cp.start()
        for cp in copies:
            cp.wait()

    return pl.pallas_call(
        body, in_specs=[VMEM], out_specs=VMEM, out_shape=jax.ShapeDtypeStruct((N_DEV, n), F32),
        scratch_shapes=[pltpu.SemaphoreType.DMA((N_DEV - 1,)), pltpu.SemaphoreType.DMA((N_DEV - 1,))], name=name)(v)


def _chip_peers(x, y):
    return [((1 - x, y), 2 * (1 - x) + y), ((x, 1 - y), 2 * x + (1 - y)), ((1 - x, 1 - y), 2 * (1 - x) + (1 - y))]


def cast_to_slot(w, name):
    r, c = w.shape
    tr = _row_tile(r, c, 1 << 18)

    def body(w_ref, out_ref):
        out_ref[...] = w_ref[...].astype(BF16)

    return pl.pallas_call(
        body, grid=(r // tr,), in_specs=[pl.BlockSpec((tr, c), lambda i: (i, 0))],
        out_specs=pl.BlockSpec((None, tr, c), lambda i: (2 * lax.axis_index("x") + lax.axis_index("y"), i, 0)),
        out_shape=jax.ShapeDtypeStruct((N_CHIPS, r, c), BF16), name=name, compiler_params=_cp("parallel"))(w)


def chip_all_gather(slots, name):
    n = len(slots)

    def body(*refs):
        outs = refs[n:2 * n]
        send_sems, recv_sems, pass_send_sems, pass_recv_sems = refs[2 * n:]
        x, y, c = _me()
        mine = 2 * x + y
        peers = _chip_peers(x, y)
        sent = []
        for a in range(n):
            half = outs[a].shape[1] // 2
            rows = outs[a].at[mine, pl.ds(c * half, half), :]
            for p, (chip, _) in enumerate(peers):
                cp = pltpu.make_async_remote_copy(
                    src_ref=rows, dst_ref=rows, send_sem=send_sems.at[a, p], recv_sem=recv_sems.at[a, p],
                    device_id=(*chip, c), device_id_type=MESH)
                cp.start()
                sent.append(cp)
        passed = []
        for a in range(n):
            half = outs[a].shape[1] // 2
            for p, (chip, slot) in enumerate(peers):
                rows = outs[a].at[slot, pl.ds(c * half, half), :]
                pltpu.make_async_remote_copy(
                    src_ref=rows, dst_ref=rows, send_sem=send_sems.at[a, p], recv_sem=recv_sems.at[a, p],
                    device_id=(*chip, c), device_id_type=MESH).wait_recv()
                cp = pltpu.make_async_remote_copy(
                    src_ref=rows, dst_ref=rows, send_sem=pass_send_sems.at[a, p], recv_sem=pass_recv_sems.at[a, p],
                    device_id=(x, y, 1 - c), device_id_type=MESH)
                cp.start()
                passed.append(cp)
        for a in range(n):
            half = outs[a].shape[1] // 2
            for p, (_, slot) in enumerate(peers):
                theirs = outs[a].at[slot, pl.ds((1 - c) * half, half), :]
                pltpu.make_async_remote_copy(
                    src_ref=theirs, dst_ref=theirs, send_sem=pass_send_sems.at[a, p], recv_sem=pass_recv_sems.at[a, p],
                    device_id=(x, y, 1 - c), device_id_type=MESH).wait_recv()
        for cp in sent + passed:
            cp.wait_send()

    sems = pltpu.SemaphoreType.DMA((n, 3))
    return pl.pallas_call(
        body, in_specs=[HBM] * n, out_specs=[HBM] * n,
        out_shape=[jax.ShapeDtypeStruct(s.shape, s.dtype) for s in slots],
        input_output_aliases={a: a for a in range(n)}, scratch_shapes=[sems, sems, sems, sems], name=name)(*slots)


def sibling_split(grads, name):
    n = len(grads)

    def body(*refs):
        ins, got = refs[:n], refs[n:2 * n]
        send_sems, recv_sems = refs[2 * n:]
        x, y, c = _me()
        started = []
        for a in range(n):
            half = ins[a].shape[1] // 2
            give = pltpu.make_async_remote_copy(
                src_ref=ins[a].at[:, pl.ds((1 - c) * half, half), :], dst_ref=got[a], send_sem=send_sems.at[a],
                recv_sem=recv_sems.at[a], device_id=(x, y, 1 - c), device_id_type=MESH)
            give.start()
            started.append(give)
        for cp in started:
            cp.wait()

    return pl.pallas_call(
        body, in_specs=[HBM] * n, out_specs=[HBM] * n,
        out_shape=[jax.ShapeDtypeStruct((g.shape[0], g.shape[1] // 2, g.shape[2]), g.dtype) for g in grads],
        scratch_shapes=[pltpu.SemaphoreType.DMA((n,)), pltpu.SemaphoreType.DMA((n,))], name=name)(*grads)


def pair_sum(g, got, name):
    k, half, c = got.shape
    tr = _row_tile(half, c, 1 << 18)
    nb = half // tr

    def body(g_ref, got_ref, out_ref):
        out_ref[...] = (g_ref[...] + got_ref[...]).astype(BF16)

    spec = pl.BlockSpec((None, tr, c), lambda j, i: (j, i, 0))
    return pl.pallas_call(
        body, grid=(k, nb),
        in_specs=[pl.BlockSpec((None, tr, c), lambda j, i: (j, lax.axis_index("c") * nb + i, 0)), spec],
        out_specs=spec, out_shape=jax.ShapeDtypeStruct(got.shape, BF16), name=name,
        compiler_params=_cp("parallel", "parallel"))(g, got)


def chip_scatter(parts, name):
    n = len(parts)

    def body(*refs):
        ins, outs = refs[:n], refs[n:2 * n]
        send_sems, recv_sems = refs[2 * n:]
        x, y, c = _me()
        started = []
        for a in range(n):
            for p, (chip, slot) in enumerate(_chip_peers(x, y)):
                cp = pltpu.make_async_remote_copy(
                    src_ref=ins[a].at[slot], dst_ref=outs[a].at[p], send_sem=send_sems.at[a, p],
                    recv_sem=recv_sems.at[a, p], device_id=(*chip, c), device_id_type=MESH)
                cp.start()
                started.append(cp)
        for cp in started:
            cp.wait()

    return pl.pallas_call(
        body, in_specs=[HBM] * n, out_specs=[HBM] * n,
        out_shape=[jax.ShapeDtypeStruct((3, *p.shape[1:]), p.dtype) for p in parts],
        scratch_shapes=[pltpu.SemaphoreType.DMA((n, 3)), pltpu.SemaphoreType.DMA((n, 3))], name=name)(*parts)


def chip_sum(part, landed, name):
    _, half, c = landed.shape
    tr = _row_tile(half, c, 1 << 17)
    nb = half // tr

    def body(part_ref, landed_ref, out_ref):
        up = lambda v: v.astype(F32)
        out_ref[...] = ((up(part_ref[...]) + up(landed_ref[0])) + up(landed_ref[1])) + up(landed_ref[2])

    return pl.pallas_call(
        body, grid=(nb,),
        in_specs=[pl.BlockSpec((None, tr, c), lambda i: (2 * lax.axis_index("x") + lax.axis_index("y"), i, 0)),
                  pl.BlockSpec((3, tr, c), lambda i: (0, i, 0))],
        out_specs=pl.BlockSpec((tr, c), lambda i: (lax.axis_index("c") * nb + i, 0)),
        out_shape=jax.ShapeDtypeStruct((2 * half, c), F32), name=name, compiler_params=_cp("parallel"))(part, landed)


def sibling_join(arrays, name):
    n = len(arrays)

    def body(*refs):
        ins, outs = refs[:n], refs[n:2 * n]
        send_sems, recv_sems = refs[2 * n:]
        x, y, c = _me()
        started = []
        for a in range(n):
            half = ins[a].shape[0] // 2
            give = pltpu.make_async_remote_copy(
                src_ref=ins[a].at[pl.ds(c * half, half), :], dst_ref=outs[a].at[pl.ds(c * half, half), :],
                send_sem=send_sems.at[a], recv_sem=recv_sems.at[a], device_id=(x, y, 1 - c), device_id_type=MESH)
            give.start()
            started.append(give)
        for cp in started:
            cp.wait()

    return pl.pallas_call(
        body, in_specs=[HBM] * n, out_specs=[HBM] * n,
        out_shape=[jax.ShapeDtypeStruct(h.shape, h.dtype) for h in arrays],
        input_output_aliases={a: a for a in range(n)},
        scratch_shapes=[pltpu.SemaphoreType.DMA((n,)), pltpu.SemaphoreType.DMA((n,))], name=name)(*arrays)


def reduce_to_owner(grads):
    got = sibling_split(grads, "grad_sibling_split")
    parts = [pair_sum(g, h, f"grad_pair_sum_{a}") for a, (g, h) in enumerate(zip(grads, got))]
    landed = chip_scatter(parts, "grad_chip_scatter")
    halves = [chip_sum(p, l, f"grad_chip_sum_{a}") for a, (p, l) in enumerate(zip(parts, landed))]
    return sibling_join(halves, "grad_sibling_join")


def _pad_row(v, n):
    return jnp.pad(v.reshape(1, -1), ((0, 0), (0, n - v.size)))


def local_step(x, target, mod, wts):
    d = D_MODEL
    row = lambda v: v.reshape(1, -1)
    mods = [[row(mod[l, i * d:(i + 1) * d]) for i in range(6)] for l in range(2)]
    saved = []
    for l in range(2):
        shift1, scale1, gate1, shift2, scale2, gate2 = mods[l]
        g_mix, g_mlp = row(wts["norm_mix"][l]), row(wts["norm_mlp"][l])
        h = norm_mod_fwd(x, g_mix, scale1, shift1, f"norm_mix_fwd_{l}")
        if l == 0:
            u = matmul_nn_chunked(h, wts["w_in"], F32, "in_proj_ab")[0]
            y_a = conv_mixer_fwd(u, wts["conv_w"])
            y_b, o_raw, states = hgrn_fwd(u, wts["lb_logits"], wts["hg_norm"])
            mix = jnp.concatenate([y_a, y_b], axis=1)
            y, x1 = proj_residual(mix, wts["w_out_ab"], x, gate1, "out_proj_ab")
            ctx = (u, o_raw, states)
        else:
            qkv = matmul_nn_chunked(h, wts["w_qkv"], F32, "in_proj_c")[0]
            qn, kn, vb = qk_norm_fwd(qkv, wts["qg"], wts["kg"])
            o, mix = sb_attn_fwd(qn, kn, vb)
            y, x1 = proj_residual(mix, wts["w_out_c"], x, gate1, "out_proj_c")
            ctx = (qkv, qn, kn, vb, o)
        h2 = norm_mod_fwd(x1, g_mlp, scale2, shift2, f"norm_mlp_fwd_{l}")
        act, r = mlp_up(h2, wts["w1"][l], f"mlp_up_{l}")
        y2, x2 = proj_residual(act, wts["w2"][l], x1, gate2, f"mlp_down_{l}")
        saved.append((x, h, mix, y, x1, h2, act, r, y2, ctx))
        x = x2

    dx, loss_row = loss_and_grad(x, target)
    small, big = {}, {}
    dmod = [None, None]
    d_norm_mix, d_norm_mlp = [None, None], [None, None]
    for l in (1, 0):
        shift1, scale1, gate1, shift2, scale2, gate2 = mods[l]
        g_mix, g_mlp = row(wts["norm_mix"][l]), row(wts["norm_mlp"][l])
        x0, h, mix, y, x1, h2, act, r, y2, ctx = saved[l]
        dy2, dgate2 = gate_bwd(dx, y2, gate2, f"mlp_gate_bwd_{l}")
        dz = mlp_down_bwd(dy2, wts["w2"][l], r, f"mlp_down_bwd_{l}")
        big[f"w2_{l}"] = matmul_tn_plain(act, dy2, f"mlp_w2_grad_{l}")
        dh2 = matmul_nt_chunked(dz, wts["w1"][l], f"mlp_up_bwd_{l}")
        big[f"w1_{l}"] = matmul_tn_chunked(h2, dz, f"mlp_w1_grad_{l}")
        dx1, d_norm_mlp[l], dscale2, dshift2 = norm_mod_bwd(dh2, x1, g_mlp, scale2, dx, f"norm_mlp_bwd_{l}")
        dy, dgate1 = gate_bwd(dx1, y, gate1, f"mix_gate_bwd_{l}")
        if l == 0:
            u, o_raw, states = ctx
            dmix = matmul_nt_plain(dy, wts["w_out_ab"], "out_proj_ab_bwd")
            big["w_out_ab"] = matmul_tn_plain(mix, dy, "w_out_ab_grad")
            dab, dac, dah, small["conv_w"] = conv_mixer_bwd(dmix, u, wts["conv_w"])
            dhq, dhf, dhi, dhg, small["hg_norm"], small["lb_logits"] = hgrn_bwd(
                dmix, u, o_raw, states, wts["lb_logits"], wts["hg_norm"])
            du = jnp.concatenate([dab, dac, dah, dhq, dhf, dhi, dhg], axis=1)
            dh = matmul_nt_chunked(du, wts["w_in"], "in_proj_ab_bwd")
            big["w_in"] = matmul_tn_chunked(h, du, "w_in_grad")
        else:
            qkv, qn, kn, vb, o = ctx
            do = matmul_nt_plain(dy, wts["w_out_c"], "out_proj_c_bwd")
            big["w_out_c"] = matmul_tn_plain(mix, dy, "w_out_c_grad")
            dqn, dkn, dv = sb_attn_bwd(qn, kn, vb, o, do)
            dq, dk, dvb, dqg, dkg = qk_norm_bwd(dqn, dkn, dv, qkv, wts["qg"], wts["kg"])
            small["q_norm"] = dqg[:, :SB_HEAD_DIM] + dqg[:, SB_HEAD_DIM:]
            small["k_norm"] = dkg[:, :SB_HEAD_DIM] + dkg[:, SB_HEAD_DIM:]
            dqkv = jnp.concatenate([dq, dk, dvb], axis=1)
            dh = matmul_nt_chunked(dqkv, wts["w_qkv"], "in_proj_c_bwd")
            big["w_qkv"] = matmul_tn_chunked(h, dqkv, "w_qkv_grad")
        dx, d_norm_mix[l], dscale1, dshift1 = norm_mod_bwd(dh, x0, g_mix, scale1, dx1, f"norm_mix_bwd_{l}")
        dmod[l] = jnp.concatenate([dshift1, dscale1, dgate1, dshift2, dscale2, dgate2], axis=1)
    small["mod"] = jnp.concatenate(dmod, axis=0)
    small["norm_mix"] = jnp.concatenate(d_norm_mix, axis=0)
    small["norm_mlp"] = jnp.concatenate(d_norm_mlp, axis=0)
    return loss_row, dx, small, big


SMALL_ORDER = ("mod", "norm_mix", "norm_mlp", "conv_w", "hg_norm", "lb_logits", "q_norm", "k_norm")


def kernel(x, c, ada_w, ada_b, norm_mix, norm_mlp, w_in_ab, conv_w, hg_norm, lb_logits, w_out_ab, w_qkv, q_norm, k_norm, w_out_c, mlp_w1, mlp_w2, loss_target, m_ada_w, m_ada_b, m_norm_mix, m_norm_mlp, m_w_in_ab, m_conv_w, m_hg_norm, m_lb_logits, m_w_out_ab, m_w_qkv, m_q_norm, m_k_norm, m_w_out_c, m_mlp_w1, m_mlp_w2, v_ada_w, v_ada_b, v_norm_mix, v_norm_mlp, v_w_in_ab, v_conv_w, v_hg_norm, v_lb_logits, v_w_out_ab, v_w_qkv, v_q_norm, v_k_norm, v_w_out_c, v_mlp_w1, v_mlp_w2):
    d = D_MODEL
    ax, ay, ac = _me()
    chip = 2 * ax + ay
    dev = 2 * chip + ac
    cols = ada_w.shape[2]

    first = all_gather_rows(_pad_row(jnp.concatenate([c.reshape(-1), conv_w.reshape(-1)]), 1536), "gather_cond")
    c_all = first[:, :d]
    conv_full = first[::2, d:d + 3 * HEAD_TILE].reshape(N_CHIPS, 3, HEAD_TILE).transpose(1, 0, 2).reshape(3, CONV_DIM)
    ada_b_shard = lax.dynamic_slice(ada_b, (0, chip * cols), (2, cols)).reshape(2, 1, cols)
    mod_cols = ada_fwd(c_all, ada_w, ada_b_shard)
    mod_all = all_gather_rows(mod_cols.reshape(1, -1), "gather_mod").reshape(N_DEV, 2, N_DEV, cols)
    mod = lax.dynamic_index_in_dim(mod_all[::2], dev, axis=2, keepdims=False).transpose(1, 0, 2).reshape(2, 6 * d)

    shards = [w_in_ab[0], w_out_ab[0], w_qkv[0], w_out_c[0], mlp_w1[0], mlp_w1[1], mlp_w2[0], mlp_w2[1]]
    g_in, g_out_ab, g_qkv, g_out_c, g_w1a, g_w1b, g_w2a, g_w2b = chip_all_gather(
        [cast_to_slot(s, f"cast_weight_{a}") for a, s in enumerate(shards)], "gather_weights")
    wts = dict(
        norm_mix=norm_mix, norm_mlp=norm_mlp, w_in=g_in, conv_w=conv_full, hg_norm=hg_norm, lb_logits=lb_logits,
        w_out_ab=g_out_ab.reshape(d, d), w_qkv=g_qkv, qg=jnp.tile(q_norm, (1, 2)), kg=jnp.tile(k_norm, (1, 2)),
        w_out_c=g_out_c.reshape(d, d), w1=[g_w1a, g_w1b], w2=[g_w2a.reshape(D_FF, d), g_w2b.reshape(D_FF, d)])

    loss_row, grad_x, small, big = local_step(x[0], loss_target[0], mod, wts)

    flat = jnp.concatenate([small[k].reshape(-1) for k in SMALL_ORDER] + [loss_row[0, :1]])
    n_small = -(-flat.size // 1024) * 1024
    gathered = all_gather_rows(_pad_row(flat, n_small), "gather_small")
    total = sum_rows(gathered.reshape(N_DEV, 8, n_small // 8), "small_sum").reshape(-1)
    sizes = [small[k].size for k in SMALL_ORDER]
    offs = [sum(sizes[:i]) for i in range(len(sizes) + 1)]
    tot = {k: total[offs[i]:offs[i + 1]].reshape(small[k].shape) for i, k in enumerate(SMALL_ORDER)}
    loss = total[offs[-1]]
    mod_rows = gathered[:, :2 * 6 * d].reshape(N_DEV, 2, 6 * d)
    dmod_cols = lax.dynamic_slice(mod_rows, (0, 0, chip * cols), (N_DEV, 2, cols)).transpose(1, 0, 2)
    g_ada_w = ada_w_grad(c_all, dmod_cols)

    as_chunks = lambda g: g.reshape(N_CHIPS, g.shape[0] // N_CHIPS, g.shape[1])
    names = ["w_in", "w_out_ab", "w_qkv", "w_out_c", "w1_0", "w1_1", "w2_0", "w2_1"]
    chunked = [big[k] if big[k].ndim == 3 else as_chunks(big[k]) for k in names]
    r_in, r_out_ab, r_qkv, r_out_c, r_w1a, r_w1b, r_w2a, r_w2b = reduce_to_owner(chunked)

    grads = dict(
        ada_w=g_ada_w, ada_b=tot["mod"], norm_mix=tot["norm_mix"], norm_mlp=tot["norm_mlp"], w_in_ab=r_in[None],
        conv_w=lax.dynamic_slice(tot["conv_w"], (0, chip * HEAD_TILE), (3, HEAD_TILE))[None], hg_norm=tot["hg_norm"],
        lb_logits=tot["lb_logits"], w_out_ab=r_out_ab[None], w_qkv=r_qkv[None], q_norm=tot["q_norm"],
        k_norm=tot["k_norm"], w_out_c=r_out_c[None], mlp_w1=jnp.stack([r_w1a, r_w1b]), mlp_w2=jnp.stack([r_w2a, r_w2b]))
    weights = dict(ada_w=ada_w, ada_b=ada_b, norm_mix=norm_mix, norm_mlp=norm_mlp, w_in_ab=w_in_ab, conv_w=conv_w,
                   hg_norm=hg_norm, lb_logits=lb_logits, w_out_ab=w_out_ab, w_qkv=w_qkv, q_norm=q_norm, k_norm=k_norm,
                   w_out_c=w_out_c, mlp_w1=mlp_w1, mlp_w2=mlp_w2)
    m_in = dict(ada_w=m_ada_w, ada_b=m_ada_b, norm_mix=m_norm_mix, norm_mlp=m_norm_mlp, w_in_ab=m_w_in_ab,
                conv_w=m_conv_w, hg_norm=m_hg_norm, lb_logits=m_lb_logits, w_out_ab=m_w_out_ab, w_qkv=m_w_qkv,
                q_norm=m_q_norm, k_norm=m_k_norm, w_out_c=m_w_out_c, mlp_w1=m_mlp_w1, mlp_w2=m_mlp_w2)
    v_in = dict(ada_w=v_ada_w, ada_b=v_ada_b, norm_mix=v_norm_mix, norm_mlp=v_norm_mlp, w_in_ab=v_w_in_ab,
                conv_w=v_conv_w, hg_norm=v_hg_norm, lb_logits=v_lb_logits, w_out_ab=v_w_out_ab, w_qkv=v_w_qkv,
                q_norm=v_q_norm, k_norm=v_k_norm, w_out_c=v_w_out_c, mlp_w1=v_mlp_w1, mlp_w2=v_mlp_w2)
    order = list(weights)
    large = ("ada_w", "w_in_ab", "w_out_ab", "w_qkv", "w_out_c", "mlp_w1", "mlp_w2")
    delta, new_m, new_v = {}, {}, {}
    for k in large:
        shape = weights[k].shape
        flat2 = lambda a: a.reshape(-1, shape[-1])
        dl, nm, nv = adamw(flat2(weights[k]), flat2(grads[k]), flat2(m_in[k]), flat2(v_in[k]), f"adamw_{k}")
        delta[k], new_m[k], new_v[k] = dl.reshape(shape), nm.reshape(shape), nv.reshape(shape)
    rest = [k for k in order if k not in large]
    n_rest = -(-sum(weights[k].size for k in rest) // 1024) * 1024
    pack = lambda tree: _pad_row(jnp.concatenate([tree[k].reshape(-1) for k in rest]), n_rest).reshape(8, n_rest // 8)
    dl, nm, nv = adamw(pack(weights), pack(grads), pack(m_in), pack(v_in), "adamw_small")
    off = 0
    for k in rest:
        size, shape = weights[k].size, weights[k].shape
        delta[k], new_m[k], new_v[k] = (a.reshape(-1)[off:off + size].reshape(shape) for a in (dl, nm, nv))
        off += size
    grads = {k: grads[k].reshape(weights[k].shape) for k in order}
    return (loss, grad_x[None], *[grads[k] for k in order], *[delta[k] for k in order],
            *[new_m[k] for k in order], *[new_v[k] for k in order])
```

```python
import functools

import jax
import jax.numpy as jnp
from jax import lax
from jax.experimental import pallas as pl
from jax.experimental.pallas import tpu as pltpu

F32 = jnp.float32
BF16 = jnp.bfloat16
HIGHEST = lax.Precision.HIGHEST
MESH = pl.DeviceIdType.MESH

D_MODEL = 1024
D_FF = 4096
CHUNK = 64
HEAD_TILE = 128
SB_HEAD_DIM = 64
CONV_DIM = 512
HG_DIM = 512
AB_IN = 3584
N_CHIPS = 4
N_DEV = 8
EPS = 1e-6
ATT_BLOCK = 128
ATT_TILE = 512
HG_SLAB = 256

ADAM_LR = 0.001
ADAM_B1 = 0.9
ADAM_B2 = 0.999
ADAM_EPS = 1e-08
ADAM_WD = 0.01
ADAM_STEP = 10

NN = (((1,), (0,)), ((), ()))
NT = (((1,), (1,)), ((), ()))
TN = (((0,), (0,)), ((), ()))


def _cp(*dims):
    return pltpu.CompilerParams(dimension_semantics=dims) if dims else pltpu.CompilerParams()


def _dot(a, b, dn, precision=None):
    return lax.dot_general(a, b, dn, preferred_element_type=F32, precision=precision)


def _sigmoid(z):
    return 1.0 / (1.0 + jnp.exp(-z))


def _matmul(a, b, *, dn, grid, a_spec, b_spec, acc_shape, epilogue, extras=(), extra_specs=(),
            out_shapes, out_specs, name):
    nk = grid[2]
    n_extra = len(extras)
    n_out = len(out_shapes)

    def body(*refs):
        a_ref, b_ref = refs[0], refs[1]
        extra_refs = refs[2:2 + n_extra]
        out_refs = refs[2 + n_extra:2 + n_extra + n_out]
        acc_ref = refs[-1]
        k = pl.program_id(2)
        part = _dot(a_ref[...], b_ref[...], dn)

        if nk == 1:
            epilogue(part, None, extra_refs, out_refs)
        else:
            @pl.when(k == 0)
            def _():
                acc_ref[...] = part

            @pl.when(k > 0)
            def _():
                acc_ref[...] += part

            @pl.when(k == nk - 1)
            def _():
                epilogue(acc_ref[...], None, extra_refs, out_refs)

    return pl.pallas_call(
        body, grid=grid, in_specs=[a_spec, b_spec, *extra_specs], out_specs=out_specs, out_shape=out_shapes,
        scratch_shapes=[pltpu.VMEM(acc_shape, F32)], name=name,
        compiler_params=_cp("parallel", "parallel", "arbitrary"))(a, b, *extras)


def _cols(ref, cols):
    return ref.at[:, cols] if cols is not None else ref


def _store(dtype):
    def epilogue(acc, cols, extra_refs, out_refs):
        _cols(out_refs[0], cols)[...] = acc.astype(dtype)
    return epilogue


def _tok_tile(s):
    return min(512, s)


def _matmul_resident(a, w, *, dn, blocks, accumulate, epilogue, extras=(), extra_specs=(), out_shapes, out_specs,
                     name):
    s, ka = a.shape
    tm = _tok_tile(s)
    n_extra = len(extras)

    def body(*refs):
        a_ref, w_ref = refs[0], refs[1]
        extra_refs = refs[2:2 + n_extra]
        out_refs = refs[2 + n_extra:]
        acc = None
        for w_index, a_cols, out_cols in blocks:
            part = _dot(_cols(a_ref, a_cols)[...], w_ref[w_index], dn)
            if accumulate:
                acc = part if acc is None else acc + part
            else:
                epilogue(part, out_cols, extra_refs, out_refs)
        if accumulate:
            epilogue(acc, None, extra_refs, out_refs)

    return pl.pallas_call(
        body, grid=(s // tm,),
        in_specs=[pl.BlockSpec((tm, ka), lambda i: (i, 0)), pl.BlockSpec(w.shape, lambda i: (0,) * w.ndim),
                  *extra_specs],
        out_specs=out_specs, out_shape=out_shapes, name=name, compiler_params=_cp("parallel"))(a, w, *extras)


def _col_blocks(n_blocks, width):
    return [slice(j * width, (j + 1) * width) for j in range(n_blocks)]


def matmul_nn_chunked(a, w, out_dtype, name, epilogue=None, out_shapes=None):
    s = a.shape[0]
    n4 = w.shape[2]
    tm = _tok_tile(s)
    if out_shapes is None:
        out_shapes = [jax.ShapeDtypeStruct((s, N_CHIPS * n4), out_dtype)]
        epilogue = _store(out_dtype)
    return _matmul_resident(
        a, w, dn=NN, blocks=[(j, None, cols) for j, cols in enumerate(_col_blocks(N_CHIPS, n4))], accumulate=False,
        epilogue=epilogue, out_shapes=out_shapes,
        out_specs=[pl.BlockSpec((tm, N_CHIPS * n4), lambda i: (i, 0))] * len(out_shapes), name=name)


def matmul_nt_chunked(dy, w, name):
    s = dy.shape[0]
    _, k, n4 = w.shape
    tm = _tok_tile(s)
    return _matmul_resident(
        dy, w, dn=NT, blocks=[(j, cols, None) for j, cols in enumerate(_col_blocks(N_CHIPS, n4))], accumulate=True,
        epilogue=_store(F32), out_shapes=[jax.ShapeDtypeStruct((s, k), F32)],
        out_specs=[pl.BlockSpec((tm, k), lambda i: (i, 0))], name=name)[0]


def matmul_tn_chunked(x, dy, name):
    s, k = x.shape
    n = dy.shape[1]
    n4 = n // N_CHIPS
    ts = _tok_tile(s)

    def body(x_ref, dy_ref, out_ref):
        xt = x_ref[...].T
        first = pl.program_id(0) == 0
        for j, cols in enumerate(_col_blocks(N_CHIPS, n4)):
            part = _dot(xt, dy_ref[:, cols], NN)

            @pl.when(first)
            def _():
                out_ref[j] = part

            @pl.when(jnp.logical_not(first))
            def _():
                out_ref[j] += part

    return pl.pallas_call(
        body, grid=(s // ts,),
        in_specs=[pl.BlockSpec((ts, k), lambda i: (i, 0)), pl.BlockSpec((ts, n), lambda i: (i, 0))],
        out_specs=pl.BlockSpec((N_CHIPS, k, n4), lambda i: (0, 0, 0)),
        out_shape=jax.ShapeDtypeStruct((N_CHIPS, k, n4), F32), name=name, compiler_params=_cp("arbitrary"))(x, dy)


def matmul_nn_plain(a, w, name, epilogue, extras, extra_specs, out_shapes, out_specs):
    return _matmul_resident(
        a, w, dn=NN, blocks=[((slice(None), slice(None)), None, None)], accumulate=True, epilogue=epilogue,
        extras=extras, extra_specs=extra_specs, out_shapes=out_shapes, out_specs=out_specs, name=name)


def matmul_nt_plain(dy, w, name, epilogue=None, extras=(), extra_specs=(), out_dtype=F32, tn=1024):
    s = dy.shape[0]
    k = w.shape[0]
    tm = _tok_tile(s)
    return _matmul_resident(
        dy, w, dn=NT, blocks=[((cols, slice(None)), None, cols) for cols in _col_blocks(k // tn, tn)],
        accumulate=False, epilogue=epilogue or _store(out_dtype), extras=extras, extra_specs=extra_specs,
        out_shapes=[jax.ShapeDtypeStruct((s, k), out_dtype)], out_specs=[pl.BlockSpec((tm, k), lambda i: (i, 0))],
        name=name)[0]


def matmul_tn_plain(x, dy, name, tk=1024):
    s, k = x.shape
    n = dy.shape[1]
    ts = _tok_tile(s)
    return _matmul(
        x, dy, dn=TN, grid=(k // tk, 1, s // ts),
        a_spec=pl.BlockSpec((ts, tk), lambda i, j, kk: (kk, i)),
        b_spec=pl.BlockSpec((ts, n), lambda i, j, kk: (kk, 0)),
        acc_shape=(tk, n), epilogue=_store(F32),
        out_shapes=[jax.ShapeDtypeStruct((k, n), F32)], out_specs=[pl.BlockSpec((tk, n), lambda i, j, kk: (i, 0))],
        name=name)[0]


def _row_spec(n):
    return pl.BlockSpec((1, n), lambda i: (0, 0))


def norm_mod_fwd(x, g, scale, shift, name):
    s, d = x.shape
    tm = _tok_tile(s)

    def body(x_ref, g_ref, sc_ref, sh_ref, h_ref):
        xv = x_ref[...]
        r = lax.rsqrt(jnp.mean(xv * xv, axis=-1, keepdims=True) + EPS)
        h_ref[...] = ((xv * r * g_ref[...]) * (1.0 + sc_ref[...]) + sh_ref[...]).astype(BF16)

    tile = pl.BlockSpec((tm, d), lambda i: (i, 0))
    return pl.pallas_call(
        body, grid=(s // tm,), in_specs=[tile, _row_spec(d), _row_spec(d), _row_spec(d)], out_specs=tile,
        out_shape=jax.ShapeDtypeStruct((s, d), BF16), name=name, compiler_params=_cp("parallel"))(x, g, scale, shift)


def norm_mod_bwd(dh, x, g, scale, dres, name):
    s, d = x.shape
    tm = _tok_tile(s)

    def body(dh_ref, x_ref, g_ref, sc_ref, dres_ref, dx_ref, dg_ref, dsc_ref, dsh_ref):
        @pl.when(pl.program_id(0) == 0)
        def _():
            dg_ref[...] = jnp.zeros_like(dg_ref)
            dsc_ref[...] = jnp.zeros_like(dsc_ref)
            dsh_ref[...] = jnp.zeros_like(dsh_ref)

        xv = x_ref[...]
        dhv = dh_ref[...]
        r = lax.rsqrt(jnp.mean(xv * xv, axis=-1, keepdims=True) + EPS)
        xn = xv * r
        gv = g_ref[...]
        s1 = 1.0 + sc_ref[...]
        dsh_ref[...] += jnp.sum(dhv, axis=0, keepdims=True)
        dsc_ref[...] += jnp.sum(dhv * xn * gv, axis=0, keepdims=True)
        dg_ref[...] += jnp.sum(dhv * xn * s1, axis=0, keepdims=True)
        dxn = dhv * gv * s1
        dx_ref[...] = dres_ref[...] + r * (dxn - xn * jnp.mean(dxn * xn, axis=-1, keepdims=True))

    tile = pl.BlockSpec((tm, d), lambda i: (i, 0))
    row = jax.ShapeDtypeStruct((1, d), F32)
    return pl.pallas_call(
        body, grid=(s // tm,), in_specs=[tile, tile, _row_spec(d), _row_spec(d), tile],
        out_specs=[tile, _row_spec(d), _row_spec(d), _row_spec(d)],
        out_shape=[jax.ShapeDtypeStruct((s, d), F32), row, row, row], name=name,
        compiler_params=_cp("arbitrary"))(dh, x, g, scale, dres)


def gate_bwd(dx, y, gate, name):
    s, d = dx.shape
    tm = _tok_tile(s)

    def body(dx_ref, y_ref, gate_ref, dy_ref, dgate_ref):
        @pl.when(pl.program_id(0) == 0)
        def _():
            dgate_ref[...] = jnp.zeros_like(dgate_ref)

        dxv = dx_ref[...]
        dy_ref[...] = (gate_ref[...] * dxv).astype(BF16)
        dgate_ref[...] += jnp.sum(dxv * y_ref[...], axis=0, keepdims=True)

    tile = pl.BlockSpec((tm, d), lambda i: (i, 0))
    return pl.pallas_call(
        body, grid=(s // tm,), in_specs=[tile, tile, _row_spec(d)], out_specs=[tile, _row_spec(d)],
        out_shape=[jax.ShapeDtypeStruct((s, d), BF16), jax.ShapeDtypeStruct((1, d), F32)], name=name,
        compiler_params=_cp("arbitrary"))(dx, y, gate)


def loss_and_grad(y, target):
    s, d = y.shape
    tm = _tok_tile(s)

    def body(y_ref, t_ref, dy_ref, loss_ref):
        @pl.when(pl.program_id(0) == 0)
        def _():
            loss_ref[...] = jnp.zeros_like(loss_ref)

        err = y_ref[...] - t_ref[...]
        dy_ref[...] = err * (1.0 / d)
        loss_ref[...] += jnp.sum(err * err) * (0.5 / d)

    tile = pl.BlockSpec((tm, d), lambda i: (i, 0))
    return pl.pallas_call(
        body, grid=(s // tm,), in_specs=[tile, tile], out_specs=[tile, _row_spec(128)],
        out_shape=[jax.ShapeDtypeStruct((s, d), F32), jax.ShapeDtypeStruct((1, 128), F32)], name="loss_and_grad",
        compiler_params=_cp("arbitrary"))(y, target)


def _proj_residual(acc, cols, extra_refs, out_refs):
    x_ref, gate_ref = extra_refs
    out_refs[0][...] = acc
    out_refs[1][...] = x_ref[...] + gate_ref[...] * acc


def proj_residual(a, w, x, gate, name):
    s, d = x.shape
    tm = _tok_tile(s)
    tile = pl.BlockSpec((tm, d), lambda i: (i, 0))
    shape = jax.ShapeDtypeStruct((s, d), F32)
    return matmul_nn_plain(
        a, w, name, _proj_residual, extras=(x, gate), extra_specs=(tile, _row_spec(d)),
        out_shapes=[shape, shape], out_specs=[tile, tile])


def _mlp_up(acc, cols, extra_refs, out_refs):
    r = jnp.maximum(acc, 0.0)
    _cols(out_refs[0], cols)[...] = (r * r).astype(BF16)
    _cols(out_refs[1], cols)[...] = r.astype(BF16)


def mlp_up(h, w1, name):
    shape = jax.ShapeDtypeStruct((h.shape[0], N_CHIPS * w1.shape[2]), BF16)
    return matmul_nn_chunked(h, w1, BF16, name, epilogue=_mlp_up, out_shapes=[shape, shape])


def _dact(acc, cols, extra_refs, out_refs):
    _cols(out_refs[0], cols)[...] = (acc * (2.0 * _cols(extra_refs[0], cols)[...].astype(F32))).astype(BF16)


def mlp_down_bwd(dy, w2, r, name):
    s = dy.shape[0]
    tm = _tok_tile(s)
    return matmul_nt_plain(dy, w2, name, epilogue=_dact, extras=(r,),
                           extra_specs=(pl.BlockSpec((tm, r.shape[1]), lambda i: (i, 0)),), out_dtype=BF16)


def _shift_down(p, n, row):
    return jnp.where(row >= n, pltpu.roll(p, n, 0), 0.0)


def _shift_up(p, n, row):
    rows = p.shape[0]
    return jnp.where(row < rows - n, pltpu.roll(p, rows - n, 0), 0.0)


def _u_col(block):
    return lambda i: (0, block + i)


def conv_mixer_fwd(u, conv_w):
    s = u.shape[0]
    nb = CONV_DIM // HEAD_TILE

    def body(ab_ref, ac_ref, ah_ref, w_ref, y_ref):
        row = lax.broadcasted_iota(jnp.int32, (s, HEAD_TILE), 0)
        p = ac_ref[...] * ah_ref[...]
        w = w_ref[...]
        conv = w[0:1] * _shift_down(p, 2, row) + w[1:2] * _shift_down(p, 1, row) + w[2:3] * p
        y_ref[...] = (ab_ref[...] * conv).astype(BF16)

    col = lambda b: pl.BlockSpec((s, HEAD_TILE), _u_col(b * nb))
    return pl.pallas_call(
        body, grid=(nb,), in_specs=[col(0), col(1), col(2), pl.BlockSpec((3, HEAD_TILE), lambda i: (0, i))],
        out_specs=pl.BlockSpec((s, HEAD_TILE), lambda i: (0, i)),
        out_shape=jax.ShapeDtypeStruct((s, CONV_DIM), BF16), name="conv_mixer_fwd",
        compiler_params=_cp("parallel"))(u, u, u, conv_w)


def conv_mixer_bwd(dmix, u, conv_w):
    s = u.shape[0]
    nb = CONV_DIM // HEAD_TILE

    def body(dy_ref, ab_ref, ac_ref, ah_ref, w_ref, dab_ref, dac_ref, dah_ref, dw_ref):
        row = lax.broadcasted_iota(jnp.int32, (s, HEAD_TILE), 0)
        ac = ac_ref[...]
        ah = ah_ref[...]
        p = ac * ah
        w = w_ref[...]
        p1 = _shift_down(p, 1, row)
        p2 = _shift_down(p, 2, row)
        conv = w[0:1] * p2 + w[1:2] * p1 + w[2:3] * p
        dy = dy_ref[...]
        dab_ref[...] = (dy * conv).astype(BF16)
        dconv = dy * ab_ref[...]
        dp = w[0:1] * _shift_up(dconv, 2, row) + w[1:2] * _shift_up(dconv, 1, row) + w[2:3] * dconv
        dac_ref[...] = (dp * ah).astype(BF16)
        dah_ref[...] = (dp * ac).astype(BF16)
        dw_ref[...] = jnp.concatenate(
            [jnp.sum(dconv * p2, axis=0, keepdims=True), jnp.sum(dconv * p1, axis=0, keepdims=True),
             jnp.sum(dconv * p, axis=0, keepdims=True)], axis=0)

    col = lambda b: pl.BlockSpec((s, HEAD_TILE), _u_col(b * nb))
    out = pl.BlockSpec((s, HEAD_TILE), lambda i: (0, i))
    wspec = pl.BlockSpec((3, HEAD_TILE), lambda i: (0, i))
    shape = jax.ShapeDtypeStruct((s, CONV_DIM), BF16)
    return pl.pallas_call(
        body, grid=(nb,), in_specs=[out, col(0), col(1), col(2), wspec], out_specs=[out, out, out, wspec],
        out_shape=[shape, shape, shape, jax.ShapeDtypeStruct((3, CONV_DIM), F32)], name="conv_mixer_bwd",
        compiler_params=_cp("parallel"))(dmix, u, u, u, conv_w)


def _chunk_cumsum(g, pos):
    for sh in (1, 2, 4, 8, 16, 32):
        g = g + jnp.where(pos >= sh, pltpu.roll(g, sh, 0), 0.0)
    return g


def _chunk_rev_cumsum(g, pos):
    rows = g.shape[0]
    for sh in (1, 2, 4, 8, 16, 32):
        g = g + jnp.where(pos < CHUNK - sh, pltpu.roll(g, rows - sh, 0), 0.0)
    return g


def _lower_bound(lb_ref):
    logits = lb_ref[...]
    e = jnp.exp(logits - jnp.max(logits, axis=0, keepdims=True))
    p = e / jnp.sum(e, axis=0, keepdims=True)
    return p[0:1], p


def _hg_gates(hf, lb):
    sg = _sigmoid(hf)
    f = lb + (1.0 - lb) * sg
    return sg, f, jnp.log(f), 1.0 - f


def _hg_specs(s, slab, order):
    n = s // slab
    col = lambda b: pl.BlockSpec((slab, HEAD_TILE), lambda h, i: (order(i, n), b + h))
    return n, col


def _to_chunks(ref3, val):
    for c in range(ref3.shape[0]):
        ref3[c] = val[c * CHUNK:(c + 1) * CHUNK]


def _from_chunks(ref3):
    return jnp.concatenate([ref3[c] for c in range(ref3.shape[0])], axis=0)


def _rolled(x, r):
    return pltpu.roll(x, r, 1) if r else x


def _decay_window(ba, bs, r, row):
    lam = jnp.exp(ba - _rolled(bs, r))
    return jnp.where(row >= r, lam, 0.0) if r else lam


def hgrn_fwd(u, lb_logits, hg_norm):
    s = u.shape[0]
    slab = min(HG_SLAB, s)
    cps = slab // CHUNK
    n, col = _hg_specs(s, slab, lambda i, n_: i)
    heads = HG_DIM // HEAD_TILE

    def body(q_ref, f_ref, i_ref, g_ref, lb_ref, nw_ref, y_ref, o_ref, st_ref, at_ref, q3, b3, k3, v3, o3):
        @pl.when(pl.program_id(1) == 0)
        def _():
            at_ref[...] = jnp.zeros_like(at_ref)

        pos = lax.broadcasted_iota(jnp.int32, (slab, HEAD_TILE), 0) & (CHUNK - 1)
        lb, _ = _lower_bound(lb_ref)
        q = q_ref[...]
        v = i_ref[...]
        _, _, g, kk = _hg_gates(f_ref[...], lb)
        b = _chunk_cumsum(g, pos)
        for ref, val in ((q3, q), (b3, b), (k3, kk), (v3, v)):
            _to_chunks(ref, val)

        for a in range(CHUNK // 8):
            wn = CHUNK - 8 * a
            qa, ba = q3[:, 8 * a:, :], b3[:, 8 * a:, :]
            bs, ks, vs = b3[:, :wn, :], k3[:, :wn, :], v3[:, :wn, :]
            row = lax.broadcasted_iota(jnp.int32, (cps, wn, HEAD_TILE), 1)
            acc = jnp.zeros((cps, wn, HEAD_TILE), F32)
            for r in range(8):
                lam = _decay_window(ba, bs, r, row)
                sc = jnp.sum(qa * _rolled(ks, r) * lam, axis=-1, keepdims=True)
                acc = acc + sc * _rolled(vs, r)
            if a == 0:
                o3[...] = acc
            else:
                o3[:, 8 * a:, :] += acc

        qhat = q * jnp.exp(b)
        for c in range(cps):
            rows = slice(c * CHUNK, (c + 1) * CHUNK)
            at = at_ref[...]
            st_ref[c] = at
            o_ref[rows, :] = o3[c] + _dot(qhat[rows], at, NT, HIGHEST)
            bc = b[(c + 1) * CHUNK - 1:(c + 1) * CHUNK]
            khat = kk[rows] * jnp.exp(bc - b[rows])
            at_ref[...] = at * jnp.exp(bc) + _dot(v[rows], khat, TN, HIGHEST)

        o = o_ref[...]
        r = lax.rsqrt(jnp.mean(o * o, axis=-1, keepdims=True) + EPS)
        hg = g_ref[...]
        y_ref[...] = (o * r * nw_ref[...] * (hg * _sigmoid(hg))).astype(BF16)

    out = pl.BlockSpec((slab, HEAD_TILE), lambda h, i: (i, h))
    par = lambda rows: pl.BlockSpec((rows, HEAD_TILE), lambda h, i: (0, h))
    return pl.pallas_call(
        body, grid=(heads, n), in_specs=[col(12), col(16), col(20), col(24), par(3), par(1)],
        out_specs=[out, out, pl.BlockSpec((None, cps, HEAD_TILE, HEAD_TILE), lambda h, i: (h, i, 0, 0))],
        out_shape=[jax.ShapeDtypeStruct((s, HG_DIM), BF16), jax.ShapeDtypeStruct((s, HG_DIM), F32),
                   jax.ShapeDtypeStruct((heads, s // CHUNK, HEAD_TILE, HEAD_TILE), F32)],
        scratch_shapes=[pltpu.VMEM((HEAD_TILE, HEAD_TILE), F32)] + [pltpu.VMEM((cps, CHUNK, HEAD_TILE), F32)] * 5,
        name="hgrn_fwd",
        compiler_params=_cp("parallel", "arbitrary"))(u, u, u, u, lb_logits, hg_norm)


def hgrn_bwd(dmix, u, o_raw, states, lb_logits, hg_norm):
    s = u.shape[0]
    slab = min(HG_SLAB, s)
    cps = slab // CHUNK
    n, col = _hg_specs(s, slab, lambda i, n_: n_ - 1 - i)
    heads = HG_DIM // HEAD_TILE

    def body(dy_ref, q_ref, f_ref, i_ref, g_ref, o_ref, st_ref, lb_ref, nw_ref,
             dq_ref, df_ref, di_ref, dg_ref, dnw_ref, dlb_ref, dat_ref, dlbacc_ref, dbc_ref,
             q3, b3, k3, v3, do3, dq3, dk3, dv3):
        step = pl.program_id(1)

        @pl.when(step == 0)
        def _():
            dat_ref[...] = jnp.zeros_like(dat_ref)
            dlbacc_ref[...] = jnp.zeros_like(dlbacc_ref)
            dnw_ref[...] = jnp.zeros_like(dnw_ref)

        pos = lax.broadcasted_iota(jnp.int32, (slab, HEAD_TILE), 0) & (CHUNK - 1)
        lb, probs = _lower_bound(lb_ref)
        q = q_ref[...]
        v = i_ref[...]
        sg_f, f, g, kk = _hg_gates(f_ref[...], lb)
        b = _chunk_cumsum(g, pos)

        o = o_ref[...]
        nw = nw_ref[...]
        r = lax.rsqrt(jnp.mean(o * o, axis=-1, keepdims=True) + EPS)
        hg = g_ref[...]
        sg = _sigmoid(hg)
        dy = dy_ref[...]
        d_on = dy * (hg * sg)
        dg_ref[...] = (dy * (o * r * nw) * (sg * (1.0 + hg * (1.0 - sg)))).astype(BF16)
        dnw_ref[...] += jnp.sum(d_on * o * r, axis=0, keepdims=True)
        t1 = d_on * nw
        do = r * t1 - o * (r * r * r) * jnp.mean(t1 * o, axis=-1, keepdims=True)

        eb = jnp.exp(b)
        qhat = q * eb
        dbc_ref[...] = jnp.zeros_like(dbc_ref)
        for c in reversed(range(cps)):
            rows = slice(c * CHUNK, (c + 1) * CHUNK)
            last = (c + 1) * CHUNK - 1
            at = st_ref[c]
            dat = dat_ref[...]
            bc = b[last:last + 1]
            ebc = jnp.exp(bc)
            dec = jnp.exp(bc - b[rows])
            khat = kk[rows] * dec
            at_next = at * ebc + _dot(v[rows], khat, TN, HIGHEST)
            dbc_ref[last:last + 1, :] = jnp.sum(dat * at_next, axis=0, keepdims=True)
            dq3[c] = eb[rows] * _dot(do[rows], at, NN, HIGHEST)
            dk3[c] = dec * _dot(v[rows], dat, NN, HIGHEST)
            dv3[c] = _dot(khat, dat, NT, HIGHEST)
            dat_ref[...] = dat * ebc + _dot(do[rows], qhat[rows], TN, HIGHEST)

        for ref, val in ((q3, q), (b3, b), (k3, kk), (v3, v), (do3, do)):
            _to_chunks(ref, val)
        for a in range(CHUNK // 8):
            wn = CHUNK - 8 * a
            qa, ba, doa = q3[:, 8 * a:, :], b3[:, 8 * a:, :], do3[:, 8 * a:, :]
            bs, ks, vs = b3[:, :wn, :], k3[:, :wn, :], v3[:, :wn, :]
            row = lax.broadcasted_iota(jnp.int32, (cps, wn, HEAD_TILE), 1)
            zero = jnp.zeros((cps, wn, HEAD_TILE), F32)
            dqa, dka, dva = zero, zero, zero
            for r in range(8):
                lam = _decay_window(ba, bs, r, row)
                kd, vd = _rolled(ks, r), _rolled(vs, r)
                sc = jnp.sum(qa * kd * lam, axis=-1, keepdims=True)
                pd = jnp.sum(doa * vd, axis=-1, keepdims=True)
                dqa = dqa + pd * kd * lam
                dka = dka + _rolled(pd * qa * lam, (wn - r) % wn)
                dva = dva + _rolled(sc * doa, (wn - r) % wn)
            dq3[:, 8 * a:, :] += dqa
            dk3[:, :wn, :] += dka
            dv3[:, :wn, :] += dva
        dq, dk, dv = _from_chunks(dq3), _from_chunks(dk3), _from_chunks(dv3)

        db = q * dq - kk * dk + dbc_ref[...]
        dgl = _chunk_rev_cumsum(db, pos)
        dfv = dgl / f - dk
        dq_ref[...] = dq.astype(BF16)
        di_ref[...] = dv.astype(BF16)
        df_ref[...] = (dfv * (1.0 - lb) * sg_f * (1.0 - sg_f)).astype(BF16)
        dlbacc_ref[...] += jnp.sum(dfv * (1.0 - sg_f), axis=0, keepdims=True)

        @pl.when(step == n - 1)
        def _():
            dlb = dlbacc_ref[...]
            sel = (lax.broadcasted_iota(jnp.int32, (3, HEAD_TILE), 0) == 0).astype(F32)
            dlb_ref[...] = dlb * probs[0:1] * (sel - probs)

    out = pl.BlockSpec((slab, HEAD_TILE), lambda h, i: (n - 1 - i, h))
    par = lambda rows: pl.BlockSpec((rows, HEAD_TILE), lambda h, i: (0, h))
    dyspec = pl.BlockSpec((slab, HEAD_TILE), lambda h, i: (n - 1 - i, CONV_DIM // HEAD_TILE + h))
    shape = jax.ShapeDtypeStruct((s, HG_DIM), BF16)
    slab_f32 = pltpu.VMEM((slab, HEAD_TILE), F32)
    return pl.pallas_call(
        body, grid=(heads, n),
        in_specs=[dyspec, col(12), col(16), col(20), col(24), out,
                  pl.BlockSpec((None, cps, HEAD_TILE, HEAD_TILE), lambda h, i: (h, n - 1 - i, 0, 0)), par(3), par(1)],
        out_specs=[out, out, out, out, par(1), par(3)],
        out_shape=[shape, shape, shape, shape, jax.ShapeDtypeStruct((1, HG_DIM), F32),
                   jax.ShapeDtypeStruct((3, HG_DIM), F32)],
        scratch_shapes=[pltpu.VMEM((HEAD_TILE, HEAD_TILE), F32), pltpu.VMEM((1, HEAD_TILE), F32), slab_f32]
        + [pltpu.VMEM((cps, CHUNK, HEAD_TILE), F32)] * 8,
        name="hgrn_bwd", compiler_params=_cp("parallel", "arbitrary"))(
            dmix, u, u, u, u, o_raw, states, lb_logits, hg_norm)


def _pair_ones():
    row = lax.broadcasted_iota(jnp.int32, (HEAD_TILE, HEAD_TILE), 0) // SB_HEAD_DIM
    col = lax.broadcasted_iota(jnp.int32, (HEAD_TILE, HEAD_TILE), 1) // SB_HEAD_DIM
    ones = (row == col).astype(BF16)
    return jnp.concatenate([ones, ones], axis=0)


def _pair_mean(x, ones):
    return _split_dot(x, ones) * (1.0 / SB_HEAD_DIM)


def _pair_rstd(x, ones):
    return lax.rsqrt(_pair_mean(x * x, ones) + EPS)


def _lane_tiles():
    return _col_blocks(D_MODEL // HEAD_TILE, HEAD_TILE)


def qk_norm_fwd(qkv, qg, kg):
    s = qkv.shape[0]
    tm = min(256, s)

    def body(q_ref, k_ref, v_ref, qg_ref, kg_ref, qn_ref, kn_ref, vb_ref):
        ones = _pair_ones()
        for cols in _lane_tiles():
            qv = q_ref[:, cols]
            kv = k_ref[:, cols]
            qn_ref[:, cols] = (qv * _pair_rstd(qv, ones) * qg_ref[...]).astype(BF16)
            kn_ref[:, cols] = (kv * _pair_rstd(kv, ones) * kg_ref[...]).astype(BF16)
        vb_ref[...] = v_ref[...].astype(BF16)

    col = lambda b: pl.BlockSpec((tm, D_MODEL), lambda i: (i, b))
    out = col(0)
    shape = jax.ShapeDtypeStruct((s, D_MODEL), BF16)
    return pl.pallas_call(
        body, grid=(s // tm,), in_specs=[col(0), col(1), col(2), _row_spec(HEAD_TILE), _row_spec(HEAD_TILE)],
        out_specs=[out, out, out], out_shape=[shape, shape, shape], name="qk_norm_fwd",
        compiler_params=_cp("parallel"))(qkv, qkv, qkv, qg, kg)


def qk_norm_bwd(dqn, dkn, dv, qkv, qg, kg):
    s = qkv.shape[0]
    tm = min(256, s)

    def body(dqn_ref, dkn_ref, dv_ref, q_ref, k_ref, qg_ref, kg_ref, dq_ref, dk_ref, dvb_ref, dqg_ref, dkg_ref):
        @pl.when(pl.program_id(0) == 0)
        def _():
            dqg_ref[...] = jnp.zeros_like(dqg_ref)
            dkg_ref[...] = jnp.zeros_like(dkg_ref)

        ones = _pair_ones()

        def one(x_ref, g_ref, dn_ref, dx_ref, dgain_ref):
            dgain = jnp.zeros((1, HEAD_TILE), F32)
            for cols in _lane_tiles():
                xv = x_ref[:, cols]
                r = _pair_rstd(xv, ones)
                xn = xv * r
                dn = dn_ref[:, cols]
                dgain = dgain + jnp.sum(dn * xn, axis=0, keepdims=True)
                t1 = dn * g_ref[...]
                dx_ref[:, cols] = (r * (t1 - xn * _pair_mean(t1 * xn, ones))).astype(BF16)
            dgain_ref[...] += dgain

        one(q_ref, qg_ref, dqn_ref, dq_ref, dqg_ref)
        one(k_ref, kg_ref, dkn_ref, dk_ref, dkg_ref)
        dvb_ref[...] = dv_ref[...].astype(BF16)

    col = lambda b: pl.BlockSpec((tm, D_MODEL), lambda i: (i, b))
    out = col(0)
    gain = _row_spec(HEAD_TILE)
    shape = jax.ShapeDtypeStruct((s, D_MODEL), BF16)
    grow = jax.ShapeDtypeStruct((1, HEAD_TILE), F32)
    return pl.pallas_call(
        body, grid=(s // tm,), in_specs=[out, out, out, col(0), col(1), gain, gain],
        out_specs=[out, out, out, gain, gain], out_shape=[shape, shape, shape, grow, grow], name="qk_norm_bwd",
        compiler_params=_cp("arbitrary"))(dqn, dkn, dv, qkv, qkv, qg, kg)


def _split_dot(x, u):
    hi = x.astype(BF16)
    lo = (x - hi.astype(F32)).astype(BF16)
    if u.shape[0] == 2 * x.shape[1]:
        return _dot(jnp.concatenate([hi, lo], axis=1), u, NN)
    return _dot(hi, u, NN) + _dot(lo, u, NN)


def _sb_tile(qs, kb, carry, causal, suffix, diag):
    z = _dot(qs, kb, NT)
    lb = jnp.minimum(z, 0.0) - jnp.log(1.0 + jnp.exp(-jnp.abs(z)))
    lom = lb - z
    if diag:
        lom = jnp.where(causal, lom, 0.0)
    ws = []
    for j in reversed(range(ATT_TILE // ATT_BLOCK)):
        cols = slice(j * ATT_BLOCK, (j + 1) * ATT_BLOCK)
        ws.append(jnp.exp(z[:, cols] + _split_dot(lom[:, cols], suffix) + carry))
        carry = carry + jnp.sum(lom[:, cols], axis=-1, keepdims=True)
    w = jnp.concatenate(ws[::-1], axis=1)
    if diag:
        w = jnp.where(causal, w, 0.0)
    return lb, w, carry


def _sb_consts():
    row = lax.broadcasted_iota(jnp.int32, (ATT_BLOCK, ATT_BLOCK), 0)
    col = lax.broadcasted_iota(jnp.int32, (ATT_BLOCK, ATT_BLOCK), 1)
    suffix = (row >= col).astype(BF16)
    suffix = jnp.concatenate([suffix, suffix], axis=0)
    trow = lax.broadcasted_iota(jnp.int32, (ATT_TILE, ATT_TILE), 0)
    tcol = lax.broadcasted_iota(jnp.int32, (ATT_TILE, ATT_TILE), 1)
    causal = tcol < trow
    lo = lax.broadcasted_iota(jnp.int32, (ATT_TILE, HEAD_TILE), 1) < SB_HEAD_DIM
    return suffix, causal, lo


def _rows(i):
    return pl.ds(pl.multiple_of(i * ATT_TILE, ATT_TILE), ATT_TILE)


def _head_query(qb, mask):
    return (jnp.where(mask, qb, 0.0) * (SB_HEAD_DIM ** -0.5)).astype(BF16)


def sb_attn_fwd(qn, kn, vb):
    s = qn.shape[0]
    nq = s // ATT_TILE
    nt = D_MODEL // HEAD_TILE

    def body(q_ref, k_ref, v_ref, o_ref, ob_ref):
        suffix, causal, lo = _sb_consts()

        def qtile(qi, _):
            qb = q_ref[_rows(qi), :].astype(F32)
            qs = [_head_query(qb, lo), _head_query(qb, ~lo)]

            def step(kj, state, diag):
                kb = k_ref[_rows(kj), :]
                vt = v_ref[_rows(kj), :]
                out = []
                for hh in range(2):
                    carry, acc = state[hh]
                    _, w, carry = _sb_tile(qs[hh], kb, carry, causal, suffix, diag)
                    out.append((carry, acc + _dot(w.astype(BF16), vt, NN)))
                return tuple(out)

            start = (jnp.zeros((ATT_TILE, 1), F32), jnp.zeros((ATT_TILE, HEAD_TILE), F32))
            state = step(qi, (start, start), True)
            state = lax.fori_loop(0, qi, lambda jj, st: step(qi - 1 - jj, st, False), state)
            o = jnp.where(lo, state[0][1], state[1][1])
            o_ref[_rows(qi), :] = o
            ob_ref[_rows(qi), :] = o.astype(BF16)
            return 0

        lax.fori_loop(0, nq, qtile, 0)

    spec = pl.BlockSpec((s, HEAD_TILE), lambda p: (0, p))
    return pl.pallas_call(
        body, grid=(nt,), in_specs=[spec, spec, spec], out_specs=[spec, spec],
        out_shape=[jax.ShapeDtypeStruct((s, D_MODEL), F32), jax.ShapeDtypeStruct((s, D_MODEL), BF16)],
        name="sb_attn_fwd", compiler_params=_cp("parallel"))(qn, kn, vb)


def sb_attn_bwd(qn, kn, vb, o, do):
    s = qn.shape[0]
    nq = s // ATT_TILE
    nt = D_MODEL // HEAD_TILE

    def body(q_ref, k_ref, v_ref, o_ref, do_ref, dq_ref, dk_ref, dv_ref):
        suffix, causal, lo = _sb_consts()
        dk_ref[...] = jnp.zeros_like(dk_ref)
        dv_ref[...] = jnp.zeros_like(dv_ref)

        def qtile(qi, _):
            qb = q_ref[_rows(qi), :].astype(F32)
            dob = do_ref[_rows(qi), :].astype(BF16).astype(F32)
            prod = dob * o_ref[_rows(qi), :]
            masks = [lo, ~lo]
            qs = [_head_query(qb, m) for m in masks]
            dos = [jnp.where(m, dob, 0.0).astype(BF16) for m in masks]
            totals = [jnp.sum(jnp.where(m, prod, 0.0), axis=-1, keepdims=True) for m in masks]

            def step(kj, state, diag):
                kb = k_ref[_rows(kj), :]
                vt = v_ref[_rows(kj), :]
                out = []
                dk = jnp.zeros((ATT_TILE, HEAD_TILE), F32)
                dv = jnp.zeros((ATT_TILE, HEAD_TILE), F32)
                for hh in range(2):
                    carry, carry_e, dq = state[hh]
                    lb, w, carry = _sb_tile(qs[hh], kb, carry, causal, suffix, diag)
                    wb = w.astype(BF16)
                    e = _dot(dos[hh], vt, NT) * wb.astype(F32)
                    befores = []
                    for j in reversed(range(ATT_TILE // ATT_BLOCK)):
                        cols = slice(j * ATT_BLOCK, (j + 1) * ATT_BLOCK)
                        befores.append(totals[hh] - carry_e - _split_dot(e[:, cols], suffix))
                        carry_e = carry_e + jnp.sum(e[:, cols], axis=-1, keepdims=True)
                    before = jnp.concatenate(befores[::-1], axis=1)
                    beta = jnp.exp(lb)
                    dz = e * (1.0 - beta) - before * beta
                    if diag:
                        dz = jnp.where(causal, dz, 0.0)
                    dzb = dz.astype(BF16)
                    dq = dq + _dot(dzb, kb, NN)
                    dk = dk + _dot(dzb, qs[hh], TN)
                    dv = dv + _dot(wb, dos[hh], TN)
                    out.append((carry, carry_e, dq))
                dk_ref[_rows(kj), :] += dk
                dv_ref[_rows(kj), :] += dv
                return tuple(out)

            zero = jnp.zeros((ATT_TILE, 1), F32)
            start = (zero, zero, jnp.zeros((ATT_TILE, HEAD_TILE), F32))
            state = step(qi, (start, start), True)
            state = lax.fori_loop(0, qi, lambda jj, st: step(qi - 1 - jj, st, False), state)
            dq_ref[_rows(qi), :] = jnp.where(lo, state[0][2], state[1][2]) * (SB_HEAD_DIM ** -0.5)
            return 0

        lax.fori_loop(0, nq, qtile, 0)

    spec = pl.BlockSpec((s, HEAD_TILE), lambda p: (0, p))
    shape = jax.ShapeDtypeStruct((s, D_MODEL), F32)
    return pl.pallas_call(
        body, grid=(nt,), in_specs=[spec] * 5, out_specs=[spec] * 3, out_shape=[shape] * 3,
        name="sb_attn_bwd", compiler_params=_cp("parallel"))(qn, kn, vb, o, do)


def ada_fwd(c_all, ada_w, ada_b_shard):
    layers, d, cols = ada_w.shape
    tn = 512

    def body(c_ref, w_ref, b_ref, out_ref):
        cv = c_ref[...]
        act = (cv * _sigmoid(cv)).astype(BF16)
        out_ref[...] = _dot(act, w_ref[...].astype(BF16), NN) + b_ref[...]

    return pl.pallas_call(
        body, grid=(layers, cols // tn),
        in_specs=[pl.BlockSpec((N_DEV, d), lambda l, j: (0, 0)), pl.BlockSpec((None, d, tn), lambda l, j: (l, 0, j)),
                  pl.BlockSpec((None, 1, tn), lambda l, j: (l, 0, j))],
        out_specs=pl.BlockSpec((None, N_DEV, tn), lambda l, j: (l, 0, j)),
        out_shape=jax.ShapeDtypeStruct((layers, N_DEV, cols), F32), name="ada_fwd",
        compiler_params=_cp("parallel", "parallel"))(c_all, ada_w, ada_b_shard)


def ada_w_grad(c_all, dmod):
    layers, _, cols = dmod.shape
    d = c_all.shape[1]
    tn = 512

    def body(c_ref, dm_ref, out_ref):
        cv = c_ref[...]
        out_ref[...] = _dot(cv * _sigmoid(cv), dm_ref[...], TN, HIGHEST)

    return pl.pallas_call(
        body, grid=(layers, cols // tn),
        in_specs=[pl.BlockSpec((N_DEV, d), lambda l, j: (0, 0)), pl.BlockSpec((None, N_DEV, tn), lambda l, j: (l, 0, j))],
        out_specs=pl.BlockSpec((None, d, tn), lambda l, j: (l, 0, j)),
        out_shape=jax.ShapeDtypeStruct((layers, d, cols), F32), name="ada_w_grad",
        compiler_params=_cp("parallel", "parallel"))(c_all, dmod)


def _row_tile(r, c, elems):
    best = 8
    for t in range(8, r + 1, 8):
        if r % t == 0 and t * c <= elems:
            best = t
    return best


def sum_rows(x, name):
    n, r, c = x.shape
    tr = _row_tile(r, c, 1 << 17)

    def body(x_ref, out_ref):
        acc = x_ref[0]
        for i in range(1, n):
            acc = acc + x_ref[i]
        out_ref[...] = acc

    return pl.pallas_call(
        body, grid=(r // tr,), in_specs=[pl.BlockSpec((n, tr, c), lambda i: (0, i, 0))],
        out_specs=pl.BlockSpec((tr, c), lambda i: (i, 0)), out_shape=jax.ShapeDtypeStruct((r, c), F32), name=name,
        compiler_params=_cp("parallel"))(x)


def adamw(w, g, m, v, name):
    r, c = w.shape
    tr = _row_tile(r, c, 1 << 17)
    c1 = 1.0 - ADAM_B1 ** ADAM_STEP
    c2 = 1.0 - ADAM_B2 ** ADAM_STEP

    def body(w_ref, g_ref, m_ref, v_ref, d_ref, nm_ref, nv_ref):
        gv = g_ref[...]
        nm = ADAM_B1 * m_ref[...] + (1.0 - ADAM_B1) * gv
        nv = ADAM_B2 * v_ref[...] + (1.0 - ADAM_B2) * (gv * gv)
        d_ref[...] = -ADAM_LR * ((nm / c1) / (jnp.sqrt(nv / c2) + ADAM_EPS) + ADAM_WD * w_ref[...])
        nm_ref[...] = nm
        nv_ref[...] = nv

    spec = pl.BlockSpec((tr, c), lambda i: (i, 0))
    shape = jax.ShapeDtypeStruct((r, c), F32)
    return pl.pallas_call(
        body, grid=(r // tr,), in_specs=[spec] * 4, out_specs=[spec] * 3, out_shape=[shape] * 3, name=name,
        compiler_params=_cp("parallel"))(w, g, m, v)


def _me():
    return lax.axis_index("x"), lax.axis_index("y"), lax.axis_index("c")


def _flip(v, bit):
    return 1 - v if bit else v


HBM = pl.BlockSpec(memory_space=pl.ANY)
VMEM = pl.BlockSpec(memory_space=pltpu.VMEM)


def all_gather_rows(v, name):
    n = v.shape[1]

    def body(v_ref, out_ref, send_sems, recv_sems):
        x, y, c = _me()
        me = 4 * x + 2 * y + c
        out_ref[pl.ds(me, 1), :] = v_ref[...]
        copies = []
        for j in range(1, N_DEV):
            peer = (_flip(x, j & 4), _flip(y, j & 2), _flip(c, j & 1))
            copies.append(pltpu.make_async_remote_copy(
                src_ref=v_ref, dst_ref=out_ref.at[pl.ds(me, 1), :], send_sem=send_sems.at[j - 1],
                recv_sem=recv_sems.at[j - 1], device_id=peer, device_id_type=MESH))
        for cp in copies:
            cp.start()
        for cp in copies:
            cp.wait()

    return pl.pallas_call(
        body, in_specs=[VMEM], out_specs=VMEM, out_shape=jax.ShapeDtypeStruct((N_DEV, n), F32),
        scratch_shapes=[pltpu.SemaphoreType.DMA((N_DEV - 1,)), pltpu.SemaphoreType.DMA((N_DEV - 1,))], name=name)(v)


def _chip_peers(x, y):
    return [((1 - x, y), 2 * (1 - x) + y), ((x, 1 - y), 2 * x + (1 - y)), ((1 - x, 1 - y), 2 * (1 - x) + (1 - y))]


def cast_to_slot(w, name):
    r, c = w.shape
    tr = _row_tile(r, c, 1 << 18)

    def body(w_ref, out_ref):
        out_ref[...] = w_ref[...].astype(BF16)

    return pl.pallas_call(
        body, grid=(r // tr,), in_specs=[pl.BlockSpec((tr, c), lambda i: (i, 0))],
        out_specs=pl.BlockSpec((None, tr, c), lambda i: (2 * lax.axis_index("x") + lax.axis_index("y"), i, 0)),
        out_shape=jax.ShapeDtypeStruct((N_CHIPS, r, c), BF16), name=name, compiler_params=_cp("parallel"))(w)


def chip_all_gather(slots, name):
    n = len(slots)

    def body(*refs):
        outs = refs[n:2 * n]
        send_sems, recv_sems, pass_send_sems, pass_recv_sems = refs[2 * n:]
        x, y, c = _me()
        mine = 2 * x + y
        peers = _chip_peers(x, y)
        sent = []
        for a in range(n):
            half = outs[a].shape[1] // 2
            rows = outs[a].at[mine, pl.ds(c * half, half), :]
            for p, (chip, _) in enumerate(peers):
                cp = pltpu.make_async_remote_copy(
                    src_ref=rows, dst_ref=rows, send_sem=send_sems.at[a, p], recv_sem=recv_sems.at[a, p],
                    device_id=(*chip, c), device_id_type=MESH)
                cp.start()
                sent.append(cp)
        passed = []
        for a in range(n):
            half = outs[a].shape[1] // 2
            for p, (chip, slot) in enumerate(peers):
                rows = outs[a].at[slot, pl.ds(c * half, half), :]
                pltpu.make_async_remote_copy(
                    src_ref=rows, dst_ref=rows, send_sem=send_sems.at[a, p], recv_sem=recv_sems.at[a, p],
                    device_id=(*chip, c), device_id_type=MESH).wait_recv()
                cp = pltpu.make_async_remote_copy(
                    src_ref=rows, dst_ref=rows, send_sem=pass_send_sems.at[a, p], recv_sem=pass_recv_sems.at[a, p],
                    device_id=(x, y, 1 - c), device_id_type=MESH)
                cp.start()
                passed.append(cp)
        for a in range(n):
            half = outs[a].shape[1] // 2
            for p, (_, slot) in enumerate(peers):
                theirs = outs[a].at[slot, pl.ds((1 - c) * half, half), :]
                pltpu.make_async_remote_copy(
                    src_ref=theirs, dst_ref=theirs, send_sem=pass_send_sems.at[a, p], recv_sem=pass_recv_sems.at[a, p],
                    device_id=(x, y, 1 - c), device_id_type=MESH).wait_recv()
        for cp in sent + passed:
            cp.wait_send()

    sems = pltpu.SemaphoreType.DMA((n, 3))
    return pl.pallas_call(
        body, in_specs=[HBM] * n, out_specs=[HBM] * n,
        out_shape=[jax.ShapeDtypeStruct(s.shape, s.dtype) for s in slots],
        input_output_aliases={a: a for a in range(n)}, scratch_shapes=[sems, sems, sems, sems], name=name)(*slots)


def sibling_split(grads, name):
    n = len(grads)

    def body(*refs):
        ins, got = refs[:n], refs[n:2 * n]
        send_sems, recv_sems = refs[2 * n:]
        x, y, c = _me()
        started = []
        for a in range(n):
            half = ins[a].shape[1] // 2
            give = pltpu.make_async_remote_copy(
                src_ref=ins[a].at[:, pl.ds((1 - c) * half, half), :], dst_ref=got[a], send_sem=send_sems.at[a],
                recv_sem=recv_sems.at[a], device_id=(x, y, 1 - c), device_id_type=MESH)
            give.start()
            started.append(give)
        for cp in started:
            cp.wait()

    return pl.pallas_call(
        body, in_specs=[HBM] * n, out_specs=[HBM] * n,
        out_shape=[jax.ShapeDtypeStruct((g.shape[0], g.shape[1] // 2, g.shape[2]), g.dtype) for g in grads],
        scratch_shapes=[pltpu.SemaphoreType.DMA((n,)), pltpu.SemaphoreType.DMA((n,))], name=name)(*grads)


def pair_sum(g, got, name):
    k, half, c = got.shape
    tr = _row_tile(half, c, 1 << 18)
    nb = half // tr

    def body(g_ref, got_ref, out_ref):
        out_ref[...] = (g_ref[...] + got_ref[...]).astype(BF16)

    spec = pl.BlockSpec((None, tr, c), lambda j, i: (j, i, 0))
    return pl.pallas_call(
        body, grid=(k, nb),
        in_specs=[pl.BlockSpec((None, tr, c), lambda j, i: (j, lax.axis_index("c") * nb + i, 0)), spec],
        out_specs=spec, out_shape=jax.ShapeDtypeStruct(got.shape, BF16), name=name,
        compiler_params=_cp("parallel", "parallel"))(g, got)


def chip_scatter(parts, name):
    n = len(parts)

    def body(*refs):
        ins, outs = refs[:n], refs[n:2 * n]
        send_sems, recv_sems = refs[2 * n:]
        x, y, c = _me()
        started = []
        for a in range(n):
            for p, (chip, slot) in enumerate(_chip_peers(x, y)):
                cp = pltpu.make_async_remote_copy(
                    src_ref=ins[a].at[slot], dst_ref=outs[a].at[p], send_sem=send_sems.at[a, p],
                    recv_sem=recv_sems.at[a, p], device_id=(*chip, c), device_id_type=MESH)
                cp.start()
                started.append(cp)
        for cp in started:
            cp.wait()

    return pl.pallas_call(
        body, in_specs=[HBM] * n, out_specs=[HBM] * n,
        out_shape=[jax.ShapeDtypeStruct((3, *p.shape[1:]), p.dtype) for p in parts],
        scratch_shapes=[pltpu.SemaphoreType.DMA((n, 3)), pltpu.SemaphoreType.DMA((n, 3))], name=name)(*parts)


def chip_sum(part, landed, name):
    _, half, c = landed.shape
    tr = _row_tile(half, c, 1 << 17)
    nb = half // tr

    def body(part_ref, landed_ref, out_ref):
        up = lambda v: v.astype(F32)
        out_ref[...] = ((up(part_ref[...]) + up(landed_ref[0])) + up(landed_ref[1])) + up(landed_ref[2])

    return pl.pallas_call(
        body, grid=(nb,),
        in_specs=[pl.BlockSpec((None, tr, c), lambda i: (2 * lax.axis_index("x") + lax.axis_index("y"), i, 0)),
                  pl.BlockSpec((3, tr, c), lambda i: (0, i, 0))],
        out_specs=pl.BlockSpec((tr, c), lambda i: (lax.axis_index("c") * nb + i, 0)),
        out_shape=jax.ShapeDtypeStruct((2 * half, c), F32), name=name, compiler_params=_cp("parallel"))(part, landed)


def sibling_join(arrays, name):
    n = len(arrays)

    def body(*refs):
        ins, outs = refs[:n], refs[n:2 * n]
        send_sems, recv_sems = refs[2 * n:]
        x, y, c = _me()
        started = []
        for a in range(n):
            half = ins[a].shape[0] // 2
            give = pltpu.make_async_remote_copy(
                src_ref=ins[a].at[pl.ds(c * half, half), :], dst_ref=outs[a].at[pl.ds(c * half, half), :],
                send_sem=send_sems.at[a], recv_sem=recv_sems.at[a], device_id=(x, y, 1 - c), device_id_type=MESH)
            give.start()
            started.append(give)
        for cp in started:
            cp.wait()

    return pl.pallas_call(
        body, in_specs=[HBM] * n, out_specs=[HBM] * n,
        out_shape=[jax.ShapeDtypeStruct(h.shape, h.dtype) for h in arrays],
        input_output_aliases={a: a for a in range(n)},
        scratch_shapes=[pltpu.SemaphoreType.DMA((n,)), pltpu.SemaphoreType.DMA((n,))], name=name)(*arrays)


def reduce_to_owner(grads):
    got = sibling_split(grads, "grad_sibling_split")
    parts = [pair_sum(g, h, f"grad_pair_sum_{a}") for a, (g, h) in enumerate(zip(grads, got))]
    landed = chip_scatter(parts, "grad_chip_scatter")
    halves = [chip_sum(p, l, f"grad_chip_sum_{a}") for a, (p, l) in enumerate(zip(parts, landed))]
    return sibling_join(halves, "grad_sibling_join")


def _pad_row(v, n):
    return jnp.pad(v.reshape(1, -1), ((0, 0), (0, n - v.size)))


def local_step(x, target, mod, wts):
    d = D_MODEL
    row = lambda v: v.reshape(1, -1)
    mods = [[row(mod[l, i * d:(i + 1) * d]) for i in range(6)] for l in range(2)]
    saved = []
    for l in range(2):
        shift1, scale1, gate1, shift2, scale2, gate2 = mods[l]
        g_mix, g_mlp = row(wts["norm_mix"][l]), row(wts["norm_mlp"][l])
        h = norm_mod_fwd(x, g_mix, scale1, shift1, f"norm_mix_fwd_{l}")
        if l == 0:
            u = matmul_nn_chunked(h, wts["w_in"], F32, "in_proj_ab")[0]
            y_a = conv_mixer_fwd(u, wts["conv_w"])
            y_b, o_raw, states = hgrn_fwd(u, wts["lb_logits"], wts["hg_norm"])
            mix = jnp.concatenate([y_a, y_b], axis=1)
            y, x1 = proj_residual(mix, wts["w_out_ab"], x, gate1, "out_proj_ab")
            ctx = (u, o_raw, states)
        else:
            qkv = matmul_nn_chunked(h, wts["w_qkv"], F32, "in_proj_c")[0]
            qn, kn, vb = qk_norm_fwd(qkv, wts["qg"], wts["kg"])
            o, mix = sb_attn_fwd(qn, kn, vb)
            y, x1 = proj_residual(mix, wts["w_out_c"], x, gate1, "out_proj_c")
            ctx = (qkv, qn, kn, vb, o)
        h2 = norm_mod_fwd(x1, g_mlp, scale2, shift2, f"norm_mlp_fwd_{l}")
        act, r = mlp_up(h2, wts["w1"][l], f"mlp_up_{l}")
        y2, x2 = proj_residual(act, wts["w2"][l], x1, gate2, f"mlp_down_{l}")
        saved.append((x, h, mix, y, x1, h2, act, r, y2, ctx))
        x = x2

    dx, loss_row = loss_and_grad(x, target)
    small, big = {}, {}
    dmod = [None, None]
    d_norm_mix, d_norm_mlp = [None, None], [None, None]
    for l in (1, 0):
        shift1, scale1, gate1, shift2, scale2, gate2 = mods[l]
        g_mix, g_mlp = row(wts["norm_mix"][l]), row(wts["norm_mlp"][l])
        x0, h, mix, y, x1, h2, act, r, y2, ctx = saved[l]
        dy2, dgate2 = gate_bwd(dx, y2, gate2, f"mlp_gate_bwd_{l}")
        dz = mlp_down_bwd(dy2, wts["w2"][l], r, f"mlp_down_bwd_{l}")
        big[f"w2_{l}"] = matmul_tn_plain(act, dy2, f"mlp_w2_grad_{l}")
        dh2 = matmul_nt_chunked(dz, wts["w1"][l], f"mlp_up_bwd_{l}")
        big[f"w1_{l}"] = matmul_tn_chunked(h2, dz, f"mlp_w1_grad_{l}")
        dx1, d_norm_mlp[l], dscale2, dshift2 = norm_mod_bwd(dh2, x1, g_mlp, scale2, dx, f"norm_mlp_bwd_{l}")
        dy, dgate1 = gate_bwd(dx1, y, gate1, f"mix_gate_bwd_{l}")
        if l == 0:
            u, o_raw, states = ctx
            dmix = matmul_nt_plain(dy, wts["w_out_ab"], "out_proj_ab_bwd")
            big["w_out_ab"] = matmul_tn_plain(mix, dy, "w_out_ab_grad")
            dab, dac, dah, small["conv_w"] = conv_mixer_bwd(dmix, u, wts["conv_w"])
            dhq, dhf, dhi, dhg, small["hg_norm"], small["lb_logits"] = hgrn_bwd(
                dmix, u, o_raw, states, wts["lb_logits"], wts["hg_norm"])
            du = jnp.concatenate([dab, dac, dah, dhq, dhf, dhi, dhg], axis=1)
            dh = matmul_nt_chunked(du, wts["w_in"], "in_proj_ab_bwd")
            big["w_in"] = matmul_tn_chunked(h, du, "w_in_grad")
        else:
            qkv, qn, kn, vb, o = ctx
            do = matmul_nt_plain(dy, wts["w_out_c"], "out_proj_c_bwd")
            big["w_out_c"] = matmul_tn_plain(mix, dy, "w_out_c_grad")
            dqn, dkn, dv = sb_attn_bwd(qn, kn, vb, o, do)
            dq, dk, dvb, dqg, dkg = qk_norm_bwd(dqn, dkn, dv, qkv, wts["qg"], wts["kg"])
            small["q_norm"] = dqg[:, :SB_HEAD_DIM] + dqg[:, SB_HEAD_DIM:]
            small["k_norm"] = dkg[:, :SB_HEAD_DIM] + dkg[:, SB_HEAD_DIM:]
            dqkv = jnp.concatenate([dq, dk, dvb], axis=1)
            dh = matmul_nt_chunked(dqkv, wts["w_qkv"], "in_proj_c_bwd")
            big["w_qkv"] = matmul_tn_chunked(h, dqkv, "w_qkv_grad")
        dx, d_norm_mix[l], dscale1, dshift1 = norm_mod_bwd(dh, x0, g_mix, scale1, dx1, f"norm_mix_bwd_{l}")
        dmod[l] = jnp.concatenate([dshift1, dscale1, dgate1, dshift2, dscale2, dgate2], axis=1)
    small["mod"] = jnp.concatenate(dmod, axis=0)
    small["norm_mix"] = jnp.concatenate(d_norm_mix, axis=0)
    small["norm_mlp"] = jnp.concatenate(d_norm_mlp, axis=0)
    return loss_row, dx, small, big


SMALL_ORDER = ("mod", "norm_mix", "norm_mlp", "conv_w", "hg_norm", "lb_logits", "q_norm", "k_norm")


def kernel(x, c, ada_w, ada_b, norm_mix, norm_mlp, w_in_ab, conv_w, hg_norm, lb_logits, w_out_ab, w_qkv, q_norm, k_norm, w_out_c, mlp_w1, mlp_w2, loss_target, m_ada_w, m_ada_b, m_norm_mix, m_norm_mlp, m_w_in_ab, m_conv_w, m_hg_norm, m_lb_logits, m_w_out_ab, m_w_qkv, m_q_norm, m_k_norm, m_w_out_c, m_mlp_w1, m_mlp_w2, v_ada_w, v_ada_b, v_norm_mix, v_norm_mlp, v_w_in_ab, v_conv_w, v_hg_norm, v_lb_logits, v_w_out_ab, v_w_qkv, v_q_norm, v_k_norm, v_w_out_c, v_mlp_w1, v_mlp_w2):
    d = D_MODEL
    ax, ay, ac = _me()
    chip = 2 * ax + ay
    dev = 2 * chip + ac
    cols = ada_w.shape[2]

    first = all_gather_rows(_pad_row(jnp.concatenate([c.reshape(-1), conv_w.reshape(-1)]), 1536), "gather_cond")
    c_all = first[:, :d]
    conv_full = first[::2, d:d + 3 * HEAD_TILE].reshape(N_CHIPS, 3, HEAD_TILE).transpose(1, 0, 2).reshape(3, CONV_DIM)
    ada_b_shard = lax.dynamic_slice(ada_b, (0, chip * cols), (2, cols)).reshape(2, 1, cols)
    mod_cols = ada_fwd(c_all, ada_w, ada_b_shard)
    mod_all = all_gather_rows(mod_cols.reshape(1, -1), "gather_mod").reshape(N_DEV, 2, N_DEV, cols)
    mod = lax.dynamic_index_in_dim(mod_all[::2], dev, axis=2, keepdims=False).transpose(1, 0, 2).reshape(2, 6 * d)

    shards = [w_in_ab[0], w_out_ab[0], w_qkv[0], w_out_c[0], mlp_w1[0], mlp_w1[1], mlp_w2[0], mlp_w2[1]]
    g_in, g_out_ab, g_qkv, g_out_c, g_w1a, g_w1b, g_w2a, g_w2b = chip_all_gather(
        [cast_to_slot(s, f"cast_weight_{a}") for a, s in enumerate(shards)], "gather_weights")
    wts = dict(
        norm_mix=norm_mix, norm_mlp=norm_mlp, w_in=g_in, conv_w=conv_full, hg_norm=hg_norm, lb_logits=lb_logits,
        w_out_ab=g_out_ab.reshape(d, d), w_qkv=g_qkv, qg=jnp.tile(q_norm, (1, 2)), kg=jnp.tile(k_norm, (1, 2)),
        w_out_c=g_out_c.reshape(d, d), w1=[g_w1a, g_w1b], w2=[g_w2a.reshape(D_FF, d), g_w2b.reshape(D_FF, d)])

    loss_row, grad_x, small, big = local_step(x[0], loss_target[0], mod, wts)

    flat = jnp.concatenate([small[k].reshape(-1) for k in SMALL_ORDER] + [loss_row[0, :1]])
    n_small = -(-flat.size // 1024) * 1024
    gathered = all_gather_rows(_pad_row(flat, n_small), "gather_small")
    total = sum_rows(gathered.reshape(N_DEV, 8, n_small // 8), "small_sum").reshape(-1)
    sizes = [small[k].size for k in SMALL_ORDER]
    offs = [sum(sizes[:i]) for i in range(len(sizes) + 1)]
    tot = {k: total[offs[i]:offs[i + 1]].reshape(small[k].shape) for i, k in enumerate(SMALL_ORDER)}
    loss = total[offs[-1]]
    mod_rows = gathered[:, :2 * 6 * d].reshape(N_DEV, 2, 6 * d)
    dmod_cols = lax.dynamic_slice(mod_rows, (0, 0, chip * cols), (N_DEV, 2, cols)).transpose(1, 0, 2)
    g_ada_w = ada_w_grad(c_all, dmod_cols)

    as_chunks = lambda g: g.reshape(N_CHIPS, g.shape[0] // N_CHIPS, g.shape[1])
    names = ["w_in", "w_out_ab", "w_qkv", "w_out_c", "w1_0", "w1_1", "w2_0", "w2_1"]
    chunked = [big[k] if big[k].ndim == 3 else as_chunks(big[k]) for k in names]
    r_in, r_out_ab, r_qkv, r_out_c, r_w1a, r_w1b, r_w2a, r_w2b = reduce_to_owner(chunked)

    grads = dict(
        ada_w=g_ada_w, ada_b=tot["mod"], norm_mix=tot["norm_mix"], norm_mlp=tot["norm_mlp"], w_in_ab=r_in[None],
        conv_w=lax.dynamic_slice(tot["conv_w"], (0, chip * HEAD_TILE), (3, HEAD_TILE))[None], hg_norm=tot["hg_norm"],
        lb_logits=tot["lb_logits"], w_out_ab=r_out_ab[None], w_qkv=r_qkv[None], q_norm=tot["q_norm"],
        k_norm=tot["k_norm"], w_out_c=r_out_c[None], mlp_w1=jnp.stack([r_w1a, r_w1b]), mlp_w2=jnp.stack([r_w2a, r_w2b]))
    weights = dict(ada_w=ada_w, ada_b=ada_b, norm_mix=norm_mix, norm_mlp=norm_mlp, w_in_ab=w_in_ab, conv_w=conv_w,
                   hg_norm=hg_norm, lb_logits=lb_logits, w_out_ab=w_out_ab, w_qkv=w_qkv, q_norm=q_norm, k_norm=k_norm,
                   w_out_c=w_out_c, mlp_w1=mlp_w1, mlp_w2=mlp_w2)
    m_in = dict(ada_w=m_ada_w, ada_b=m_ada_b, norm_mix=m_norm_mix, norm_mlp=m_norm_mlp, w_in_ab=m_w_in_ab,
                conv_w=m_conv_w, hg_norm=m_hg_norm, lb_logits=m_lb_logits, w_out_ab=m_w_out_ab, w_qkv=m_w_qkv,
                q_norm=m_q_norm, k_norm=m_k_norm, w_out_c=m_w_out_c, mlp_w1=m_mlp_w1, mlp_w2=m_mlp_w2)
    v_in = dict(ada_w=v_ada_w, ada_b=v_ada_b, norm_mix=v_norm_mix, norm_mlp=v_norm_mlp, w_in_ab=v_w_in_ab,
                conv_w=v_conv_w, hg_norm=v_hg_norm, lb_logits=v_lb_logits, w_out_ab=v_w_out_ab, w_qkv=v_w_qkv,
                q_norm=v_q_norm, k_norm=v_k_norm, w_out_c=v_w_out_c, mlp_w1=v_mlp_w1, mlp_w2=v_mlp_w2)
    order = list(weights)
    large = ("ada_w", "w_in_ab", "w_out_ab", "w_qkv", "w_out_c", "mlp_w1", "mlp_w2")
    delta, new_m, new_v = {}, {}, {}
    for k in large:
        shape = weights[k].shape
        flat2 = lambda a: a.reshape(-1, shape[-1])
        dl, nm, nv = adamw(flat2(weights[k]), flat2(grads[k]), flat2(m_in[k]), flat2(v_in[k]), f"adamw_{k}")
        delta[k], new_m[k], new_v[k] = dl.reshape(shape), nm.reshape(shape), nv.reshape(shape)
    rest = [k for k in order if k not in large]
    n_rest = -(-sum(weights[k].size for k in rest) // 1024) * 1024
    pack = lambda tree: _pad_row(jnp.concatenate([tree[k].reshape(-1) for k in rest]), n_rest).reshape(8, n_rest // 8)
    dl, nm, nv = adamw(pack(weights), pack(grads), pack(m_in), pack(v_in), "adamw_small")
    off = 0
    for k in rest:
        size, shape = weights[k].size, weights[k].shape
        delta[k], new_m[k], new_v[k] = (a.reshape(-1)[off:off + size].reshape(shape) for a in (dl, nm, nv))
        off += size
    grads = {k: grads[k].reshape(weights[k].shape) for k in order}
    return (loss, grad_x[None], *[grads[k] for k in order], *[delta[k] for k in order],
            *[new_m[k] for k in order], *[new_v[k] for k in order])
```

```python
import functools

import jax
import jax.numpy as jnp
from jax import lax
from jax.experimental import pallas as pl
from jax.experimental.pallas import tpu as pltpu

F32 = jnp.float32
BF16 = jnp.bfloat16
HIGHEST = lax.Precision.HIGHEST
MESH = pl.DeviceIdType.MESH

D_MODEL = 1024
D_FF = 4096
CHUNK = 64
HEAD_TILE = 128
SB_HEAD_DIM = 64
CONV_DIM = 512
HG_DIM = 512
AB_IN = 3584
N_CHIPS = 4
N_DEV = 8
EPS = 1e-6
ATT_BLOCK = 128
ATT_TILE = 512
HG_SLAB = 256

ADAM_LR = 0.001
ADAM_B1 = 0.9
ADAM_B2 = 0.999
ADAM_EPS = 1e-08
ADAM_WD = 0.01
ADAM_STEP = 10

NN = (((1,), (0,)), ((), ()))
NT = (((1,), (1,)), ((), ()))
TN = (((0,), (0,)), ((), ()))


def _cp(*dims):
    return pltpu.CompilerParams(dimension_semantics=dims) if dims else pltpu.CompilerParams()


def _dot(a, b, dn, precision=None):
    return lax.dot_general(a, b, dn, preferred_element_type=F32, precision=precision)


def _sigmoid(z):
    return 1.0 / (1.0 + jnp.exp(-z))


def _matmul(a, b, *, dn, grid, a_spec, b_spec, acc_shape, epilogue, extras=(), extra_specs=(),
            out_shapes, out_specs, name):
    nk = grid[2]
    n_extra = len(extras)
    n_out = len(out_shapes)

    def body(*refs):
        a_ref, b_ref = refs[0], refs[1]
        extra_refs = refs[2:2 + n_extra]
        out_refs = refs[2 + n_extra:2 + n_extra + n_out]
        acc_ref = refs[-1]
        k = pl.program_id(2)
        part = _dot(a_ref[...], b_ref[...], dn)

        if nk == 1:
            epilogue(part, None, extra_refs, out_refs)
        else:
            @pl.when(k == 0)
            def _():
                acc_ref[...] = part

            @pl.when(k > 0)
            def _():
                acc_ref[...] += part

            @pl.when(k == nk - 1)
            def _():
                epilogue(acc_ref[...], None, extra_refs, out_refs)

    return pl.pallas_call(
        body, grid=grid, in_specs=[a_spec, b_spec, *extra_specs], out_specs=out_specs, out_shape=out_shapes,
        scratch_shapes=[pltpu.VMEM(acc_shape, F32)], name=name,
        compiler_params=_cp("parallel", "parallel", "arbitrary"))(a, b, *extras)


def _cols(ref, cols):
    return ref.at[:, cols] if cols is not None else ref


def _store(dtype):
    def epilogue(acc, cols, extra_refs, out_refs):
        _cols(out_refs[0], cols)[...] = acc.astype(dtype)
    return epilogue


def _tok_tile(s):
    return min(512, s)


def _matmul_resident(a, w, *, dn, blocks, accumulate, epilogue, extras=(), extra_specs=(), out_shapes, out_specs,
                     name):
    s, ka = a.shape
    tm = _tok_tile(s)
    n_extra = len(extras)

    def body(*refs):
        a_ref, w_ref = refs[0], refs[1]
        extra_refs = refs[2:2 + n_extra]
        out_refs = refs[2 + n_extra:]
        acc = None
        for w_index, a_cols, out_cols in blocks:
            part = _dot(_cols(a_ref, a_cols)[...], w_ref[w_index], dn)
            if accumulate:
                acc = part if acc is None else acc + part
            else:
                epilogue(part, out_cols, extra_refs, out_refs)
        if accumulate:
            epilogue(acc, None, extra_refs, out_refs)

    return pl.pallas_call(
        body, grid=(s // tm,),
        in_specs=[pl.BlockSpec((tm, ka), lambda i: (i, 0)), pl.BlockSpec(w.shape, lambda i: (0,) * w.ndim),
                  *extra_specs],
        out_specs=out_specs, out_shape=out_shapes, name=name, compiler_params=_cp("parallel"))(a, w, *extras)


def _col_blocks(n_blocks, width):
    return [slice(j * width, (j + 1) * width) for j in range(n_blocks)]


def matmul_nn_chunked(a, w, out_dtype, name, epilogue=None, out_shapes=None):
    s = a.shape[0]
    n4 = w.shape[2]
    tm = _tok_tile(s)
    if out_shapes is None:
        out_shapes = [jax.ShapeDtypeStruct((s, N_CHIPS * n4), out_dtype)]
        epilogue = _store(out_dtype)
    return _matmul_resident(
        a, w, dn=NN, blocks=[(j, None, cols) for j, cols in enumerate(_col_blocks(N_CHIPS, n4))], accumulate=False,
        epilogue=epilogue, out_shapes=out_shapes,
        out_specs=[pl.BlockSpec((tm, N_CHIPS * n4), lambda i: (i, 0))] * len(out_shapes), name=name)


def matmul_nt_chunked(dy, w, name):
    s = dy.shape[0]
    _, k, n4 = w.shape
    tm = _tok_tile(s)
    return _matmul_resident(
        dy, w, dn=NT, blocks=[(j, cols, None) for j, cols in enumerate(_col_blocks(N_CHIPS, n4))], accumulate=True,
        epilogue=_store(F32), out_shapes=[jax.ShapeDtypeStruct((s, k), F32)],
        out_specs=[pl.BlockSpec((tm, k), lambda i: (i, 0))], name=name)[0]


def matmul_tn_chunked(x, dy, name):
    s, k = x.shape
    n = dy.shape[1]
    n4 = n // N_CHIPS
    ts = _tok_tile(s)

    def body(x_ref, dy_ref, out_ref):
        xt = x_ref[...].T
        first = pl.program_id(0) == 0
        for j, cols in enumerate(_col_blocks(N_CHIPS, n4)):
            part = _dot(xt, dy_ref[:, cols], NN)

            @pl.when(first)
            def _():
                out_ref[j] = part

            @pl.when(jnp.logical_not(first))
            def _():
                out_ref[j] += part

    return pl.pallas_call(
        body, grid=(s // ts,),
        in_specs=[pl.BlockSpec((ts, k), lambda i: (i, 0)), pl.BlockSpec((ts, n), lambda i: (i, 0))],
        out_specs=pl.BlockSpec((N_CHIPS, k, n4), lambda i: (0, 0, 0)),
        out_shape=jax.ShapeDtypeStruct((N_CHIPS, k, n4), F32), name=name, compiler_params=_cp("arbitrary"))(x, dy)


def matmul_nn_plain(a, w, name, epilogue, extras, extra_specs, out_shapes, out_specs):
    return _matmul_resident(
        a, w, dn=NN, blocks=[((slice(None), slice(None)), None, None)], accumulate=True, epilogue=epilogue,
        extras=extras, extra_specs=extra_specs, out_shapes=out_shapes, out_specs=out_specs, name=name)


def matmul_nt_plain(dy, w, name, epilogue=None, extras=(), extra_specs=(), out_dtype=F32, tn=1024):
    s = dy.shape[0]
    k = w.shape[0]
    tm = _tok_tile(s)
    return _matmul_resident(
        dy, w, dn=NT, blocks=[((cols, slice(None)), None, cols) for cols in _col_blocks(k // tn, tn)],
        accumulate=False, epilogue=epilogue or _store(out_dtype), extras=extras, extra_specs=extra_specs,
        out_shapes=[jax.ShapeDtypeStruct((s, k), out_dtype)], out_specs=[pl.BlockSpec((tm, k), lambda i: (i, 0))],
        name=name)[0]


def matmul_tn_plain(x, dy, name, tk=1024):
    s, k = x.shape
    n = dy.shape[1]
    ts = _tok_tile(s)
    return _matmul(
        x, dy, dn=TN, grid=(k // tk, 1, s // ts),
        a_spec=pl.BlockSpec((ts, tk), lambda i, j, kk: (kk, i)),
        b_spec=pl.BlockSpec((ts, n), lambda i, j, kk: (kk, 0)),
        acc_shape=(tk, n), epilogue=_store(F32),
        out_shapes=[jax.ShapeDtypeStruct((k, n), F32)], out_specs=[pl.BlockSpec((tk, n), lambda i, j, kk: (i, 0))],
        name=name)[0]


def _row_spec(n):
    return pl.BlockSpec((1, n), lambda i: (0, 0))


def norm_mod_fwd(x, g, scale, shift, name):
    s, d = x.shape
    tm = _tok_tile(s)

    def body(x_ref, g_ref, sc_ref, sh_ref, h_ref):
        xv = x_ref[...]
        r = lax.rsqrt(jnp.mean(xv * xv, axis=-1, keepdims=True) + EPS)
        h_ref[...] = ((xv * r * g_ref[...]) * (1.0 + sc_ref[...]) + sh_ref[...]).astype(BF16)

    tile = pl.BlockSpec((tm, d), lambda i: (i, 0))
    return pl.pallas_call(
        body, grid=(s // tm,), in_specs=[tile, _row_spec(d), _row_spec(d), _row_spec(d)], out_specs=tile,
        out_shape=jax.ShapeDtypeStruct((s, d), BF16), name=name, compiler_params=_cp("parallel"))(x, g, scale, shift)


def norm_mod_bwd(dh, x, g, scale, dres, name):
    s, d = x.shape
    tm = _tok_tile(s)

    def body(dh_ref, x_ref, g_ref, sc_ref, dres_ref, dx_ref, dg_ref, dsc_ref, dsh_ref):
        @pl.when(pl.program_id(0) == 0)
        def _():
            dg_ref[...] = jnp.zeros_like(dg_ref)
            dsc_ref[...] = jnp.zeros_like(dsc_ref)
            dsh_ref[...] = jnp.zeros_like(dsh_ref)

        xv = x_ref[...]
        dhv = dh_ref[...]
        r = lax.rsqrt(jnp.mean(xv * xv, axis=-1, keepdims=True) + EPS)
        xn = xv * r
        gv = g_ref[...]
        s1 = 1.0 + sc_ref[...]
        dsh_ref[...] += jnp.sum(dhv, axis=0, keepdims=True)
        dsc_ref[...] += jnp.sum(dhv * xn * gv, axis=0, keepdims=True)
        dg_ref[...] += jnp.sum(dhv * xn * s1, axis=0, keepdims=True)
        dxn = dhv * gv * s1
        dx_ref[...] = dres_ref[...] + r * (dxn - xn * jnp.mean(dxn * xn, axis=-1, keepdims=True))

    tile = pl.BlockSpec((tm, d), lambda i: (i, 0))
    row = jax.ShapeDtypeStruct((1, d), F32)
    return pl.pallas_call(
        body, grid=(s // tm,), in_specs=[tile, tile, _row_spec(d), _row_spec(d), tile],
        out_specs=[tile, _row_spec(d), _row_spec(d), _row_spec(d)],
        out_shape=[jax.ShapeDtypeStruct((s, d), F32), row, row, row], name=name,
        compiler_params=_cp("arbitrary"))(dh, x, g, scale, dres)


def gate_bwd(dx, y, gate, name):
    s, d = dx.shape
    tm = _tok_tile(s)

    def body(dx_ref, y_ref, gate_ref, dy_ref, dgate_ref):
        @pl.when(pl.program_id(0) == 0)
        def _():
            dgate_ref[...] = jnp.zeros_like(dgate_ref)

        dxv = dx_ref[...]
        dy_ref[...] = (gate_ref[...] * dxv).astype(BF16)
        dgate_ref[...] += jnp.sum(dxv * y_ref[...], axis=0, keepdims=True)

    tile = pl.BlockSpec((tm, d), lambda i: (i, 0))
    return pl.pallas_call(
        body, grid=(s // tm,), in_specs=[tile, tile, _row_spec(d)], out_specs=[tile, _row_spec(d)],
        out_shape=[jax.ShapeDtypeStruct((s, d), BF16), jax.ShapeDtypeStruct((1, d), F32)], name=name,
        compiler_params=_cp("arbitrary"))(dx, y, gate)


def loss_and_grad(y, target):
    s, d = y.shape
    tm = _tok_tile(s)

    def body(y_ref, t_ref, dy_ref, loss_ref):
        @pl.when(pl.program_id(0) == 0)
        def _():
            loss_ref[...] = jnp.zeros_like(loss_ref)

        err = y_ref[...] - t_ref[...]
        dy_ref[...] = err * (1.0 / d)
        loss_ref[...] += jnp.sum(err * err) * (0.5 / d)

    tile = pl.BlockSpec((tm, d), lambda i: (i, 0))
    return pl.pallas_call(
        body, grid=(s // tm,), in_specs=[tile, tile], out_specs=[tile, _row_spec(128)],
        out_shape=[jax.ShapeDtypeStruct((s, d), F32), jax.ShapeDtypeStruct((1, 128), F32)], name="loss_and_grad",
        compiler_params=_cp("arbitrary"))(y, target)


def _proj_residual(acc, cols, extra_refs, out_refs):
    x_ref, gate_ref = extra_refs
    out_refs[0][...] = acc
    out_refs[1][...] = x_ref[...] + gate_ref[...] * acc


def proj_residual(a, w, x, gate, name):
    s, d = x.shape
    tm = _tok_tile(s)
    tile = pl.BlockSpec((tm, d), lambda i: (i, 0))
    shape = jax.ShapeDtypeStruct((s, d), F32)
    return matmul_nn_plain(
        a, w, name, _proj_residual, extras=(x, gate), extra_specs=(tile, _row_spec(d)),
        out_shapes=[shape, shape], out_specs=[tile, tile])


def _mlp_up(acc, cols, extra_refs, out_refs):
    r = jnp.maximum(acc, 0.0)
    _cols(out_refs[0], cols)[...] = (r * r).astype(BF16)
    _cols(out_refs[1], cols)[...] = r.astype(BF16)


def mlp_up(h, w1, name):
    shape = jax.ShapeDtypeStruct((h.shape[0], N_CHIPS * w1.shape[2]), BF16)
    return matmul_nn_chunked(h, w1, BF16, name, epilogue=_mlp_up, out_shapes=[shape, shape])


def _dact(acc, cols, extra_refs, out_refs):
    _cols(out_refs[0], cols)[...] = (acc * (2.0 * _cols(extra_refs[0], cols)[...].astype(F32))).astype(BF16)


def mlp_down_bwd(dy, w2, r, name):
    s = dy.shape[0]
    tm = _tok_tile(s)
    return matmul_nt_plain(dy, w2, name, epilogue=_dact, extras=(r,),
                           extra_specs=(pl.BlockSpec((tm, r.shape[1]), lambda i: (i, 0)),), out_dtype=BF16)


def _shift_down(p, n, row):
    return jnp.where(row >= n, pltpu.roll(p, n, 0), 0.0)


def _shift_up(p, n, row):
    rows = p.shape[0]
    return jnp.where(row < rows - n, pltpu.roll(p, rows - n, 0), 0.0)


def _u_col(block):
    return lambda i: (0, block + i)


def conv_mixer_fwd(u, conv_w):
    s = u.shape[0]
    nb = CONV_DIM // HEAD_TILE

    def body(ab_ref, ac_ref, ah_ref, w_ref, y_ref):
        row = lax.broadcasted_iota(jnp.int32, (s, HEAD_TILE), 0)
        p = ac_ref[...] * ah_ref[...]
        w = w_ref[...]
        conv = w[0:1] * _shift_down(p, 2, row) + w[1:2] * _shift_down(p, 1, row) + w[2:3] * p
        y_ref[...] = (ab_ref[...] * conv).astype(BF16)

    col = lambda b: pl.BlockSpec((s, HEAD_TILE), _u_col(b * nb))
    return pl.pallas_call(
        body, grid=(nb,), in_specs=[col(0), col(1), col(2), pl.BlockSpec((3, HEAD_TILE), lambda i: (0, i))],
        out_specs=pl.BlockSpec((s, HEAD_TILE), lambda i: (0, i)),
        out_shape=jax.ShapeDtypeStruct((s, CONV_DIM), BF16), name="conv_mixer_fwd",
        compiler_params=_cp("parallel"))(u, u, u, conv_w)


def conv_mixer_bwd(dmix, u, conv_w):
    s = u.shape[0]
    nb = CONV_DIM // HEAD_TILE

    def body(dy_ref, ab_ref, ac_ref, ah_ref, w_ref, dab_ref, dac_ref, dah_ref, dw_ref):
        row = lax.broadcasted_iota(jnp.int32, (s, HEAD_TILE), 0)
        ac = ac_ref[...]
        ah = ah_ref[...]
        p = ac * ah
        w = w_ref[...]
        p1 = _shift_down(p, 1, row)
        p2 = _shift_down(p, 2, row)
        conv = w[0:1] * p2 + w[1:2] * p1 + w[2:3] * p
        dy = dy_ref[...]
        dab_ref[...] = (dy * conv).astype(BF16)
        dconv = dy * ab_ref[...]
        dp = w[0:1] * _shift_up(dconv, 2, row) + w[1:2] * _shift_up(dconv, 1, row) + w[2:3] * dconv
        dac_ref[...] = (dp * ah).astype(BF16)
        dah_ref[...] = (dp * ac).astype(BF16)
        dw_ref[...] = jnp.concatenate(
            [jnp.sum(dconv * p2, axis=0, keepdims=True), jnp.sum(dconv * p1, axis=0, keepdims=True),
             jnp.sum(dconv * p, axis=0, keepdims=True)], axis=0)

    col = lambda b: pl.BlockSpec((s, HEAD_TILE), _u_col(b * nb))
    out = pl.BlockSpec((s, HEAD_TILE), lambda i: (0, i))
    wspec = pl.BlockSpec((3, HEAD_TILE), lambda i: (0, i))
    shape = jax.ShapeDtypeStruct((s, CONV_DIM), BF16)
    return pl.pallas_call(
        body, grid=(nb,), in_specs=[out, col(0), col(1), col(2), wspec], out_specs=[out, out, out, wspec],
        out_shape=[shape, shape, shape, jax.ShapeDtypeStruct((3, CONV_DIM), F32)], name="conv_mixer_bwd",
        compiler_params=_cp("parallel"))(dmix, u, u, u, conv_w)


def _chunk_cumsum(g, pos):
    for sh in (1, 2, 4, 8, 16, 32):
        g = g + jnp.where(pos >= sh, pltpu.roll(g, sh, 0), 0.0)
    return g


def _chunk_rev_cumsum(g, pos):
    rows = g.shape[0]
    for sh in (1, 2, 4, 8, 16, 32):
        g = g + jnp.where(pos < CHUNK - sh, pltpu.roll(g, rows - sh, 0), 0.0)
    return g


def _lower_bound(lb_ref):
    logits = lb_ref[...]
    e = jnp.exp(logits - jnp.max(logits, axis=0, keepdims=True))
    p = e / jnp.sum(e, axis=0, keepdims=True)
    return p[0:1], p


def _hg_gates(hf, lb):
    sg = _sigmoid(hf)
    f = lb + (1.0 - lb) * sg
    return sg, f, jnp.log(f), 1.0 - f


def _hg_specs(s, slab, order):
    n = s // slab
    col = lambda b: pl.BlockSpec((slab, HEAD_TILE), lambda h, i: (order(i, n), b + h))
    return n, col


def _to_chunks(ref3, val):
    for c in range(ref3.shape[0]):
        ref3[c] = val[c * CHUNK:(c + 1) * CHUNK]


def _from_chunks(ref3):
    return jnp.concatenate([ref3[c] for c in range(ref3.shape[0])], axis=0)


def _rolled(x, r):
    return pltpu.roll(x, r, 1) if r else x


def _decay_window(ba, bs, r, row):
    lam = jnp.exp(ba - _rolled(bs, r))
    return jnp.where(row >= r, lam, 0.0) if r else lam


def hgrn_fwd(u, lb_logits, hg_norm, riding=()):
    s = u.shape[0]
    slab = min(HG_SLAB, s)
    cps = slab // CHUNK
    n, col = _hg_specs(s, slab, lambda i, n_: i)
    heads = HG_DIM // HEAD_TILE
    nr = len(riding)

    def body(*refs):
        q_ref, f_ref, i_ref, g_ref, lb_ref, nw_ref = refs[:6]
        y_ref, o_ref, st_ref = refs[6 + nr:9 + nr]
        rode = refs[9 + nr:9 + 2 * nr]
        at_ref, q3, b3, k3, v3, o3 = refs[9 + 2 * nr:15 + 2 * nr]
        sems = refs[15 + 2 * nr:]
        if nr:
            @pl.when((pl.program_id(0) == 0) & (pl.program_id(1) == 0))
            def _():
                gather_start(rode, *sems[:2])

        @pl.when(pl.program_id(1) == 0)
        def _():
            at_ref[...] = jnp.zeros_like(at_ref)

        pos = lax.broadcasted_iota(jnp.int32, (slab, HEAD_TILE), 0) & (CHUNK - 1)
        lb, _ = _lower_bound(lb_ref)
        q = q_ref[...]
        v = i_ref[...]
        _, _, g, kk = _hg_gates(f_ref[...], lb)
        b = _chunk_cumsum(g, pos)
        for ref, val in ((q3, q), (b3, b), (k3, kk), (v3, v)):
            _to_chunks(ref, val)

        for a in range(CHUNK // 8):
            wn = CHUNK - 8 * a
            qa, ba = q3[:, 8 * a:, :], b3[:, 8 * a:, :]
            bs, ks, vs = b3[:, :wn, :], k3[:, :wn, :], v3[:, :wn, :]
            row = lax.broadcasted_iota(jnp.int32, (cps, wn, HEAD_TILE), 1)
            acc = jnp.zeros((cps, wn, HEAD_TILE), F32)
            for r in range(8):
                lam = _decay_window(ba, bs, r, row)
                sc = jnp.sum(qa * _rolled(ks, r) * lam, axis=-1, keepdims=True)
                acc = acc + sc * _rolled(vs, r)
            if a == 0:
                o3[...] = acc
            else:
                o3[:, 8 * a:, :] += acc

        qhat = q * jnp.exp(b)
        for c in range(cps):
            rows = slice(c * CHUNK, (c + 1) * CHUNK)
            at = at_ref[...]
            st_ref[c] = at
            o_ref[rows, :] = o3[c] + _dot(qhat[rows], at, NT, HIGHEST)
            bc = b[(c + 1) * CHUNK - 1:(c + 1) * CHUNK]
            khat = kk[rows] * jnp.exp(bc - b[rows])
            at_ref[...] = at * jnp.exp(bc) + _dot(v[rows], khat, TN, HIGHEST)

        o = o_ref[...]
        r = lax.rsqrt(jnp.mean(o * o, axis=-1, keepdims=True) + EPS)
        hg = g_ref[...]
        y_ref[...] = (o * r * nw_ref[...] * (hg * _sigmoid(hg))).astype(BF16)

        if nr:
            @pl.when((pl.program_id(0) == heads - 1) & (pl.program_id(1) == n - 1))
            def _():
                gather_finish(rode, *sems)

    out = pl.BlockSpec((slab, HEAD_TILE), lambda h, i: (i, h))
    par = lambda rows: pl.BlockSpec((rows, HEAD_TILE), lambda h, i: (0, h))
    return pl.pallas_call(
        body, grid=(heads, n), in_specs=[col(12), col(16), col(20), col(24), par(3), par(1)] + [HBM] * nr,
        out_specs=[out, out, pl.BlockSpec((None, cps, HEAD_TILE, HEAD_TILE), lambda h, i: (h, i, 0, 0))] + [HBM] * nr,
        out_shape=[jax.ShapeDtypeStruct((s, HG_DIM), BF16), jax.ShapeDtypeStruct((s, HG_DIM), F32),
                   jax.ShapeDtypeStruct((heads, s // CHUNK, HEAD_TILE, HEAD_TILE), F32)]
        + [jax.ShapeDtypeStruct(w.shape, w.dtype) for w in riding],
        input_output_aliases={6 + a: 3 + a for a in range(nr)},
        scratch_shapes=[pltpu.VMEM((HEAD_TILE, HEAD_TILE), F32)] + [pltpu.VMEM((cps, CHUNK, HEAD_TILE), F32)] * 5
        + (gather_semaphores(nr) if nr else []),
        name="hgrn_fwd",
        compiler_params=_cp("arbitrary", "arbitrary"))(u, u, u, u, lb_logits, hg_norm, *riding)


def hgrn_bwd(dmix, u, o_raw, states, lb_logits, hg_norm):
    s = u.shape[0]
    slab = min(HG_SLAB, s)
    cps = slab // CHUNK
    n, col = _hg_specs(s, slab, lambda i, n_: n_ - 1 - i)
    heads = HG_DIM // HEAD_TILE

    def body(dy_ref, q_ref, f_ref, i_ref, g_ref, o_ref, st_ref, lb_ref, nw_ref,
             dq_ref, df_ref, di_ref, dg_ref, dnw_ref, dlb_ref, dat_ref, dlbacc_ref, dbc_ref,
             q3, b3, k3, v3, do3, dq3, dk3, dv3):
        step = pl.program_id(1)

        @pl.when(step == 0)
        def _():
            dat_ref[...] = jnp.zeros_like(dat_ref)
            dlbacc_ref[...] = jnp.zeros_like(dlbacc_ref)
            dnw_ref[...] = jnp.zeros_like(dnw_ref)

        pos = lax.broadcasted_iota(jnp.int32, (slab, HEAD_TILE), 0) & (CHUNK - 1)
        lb, probs = _lower_bound(lb_ref)
        q = q_ref[...]
        v = i_ref[...]
        sg_f, f, g, kk = _hg_gates(f_ref[...], lb)
        b = _chunk_cumsum(g, pos)

        o = o_ref[...]
        nw = nw_ref[...]
        r = lax.rsqrt(jnp.mean(o * o, axis=-1, keepdims=True) + EPS)
        hg = g_ref[...]
        sg = _sigmoid(hg)
        dy = dy_ref[...]
        d_on = dy * (hg * sg)
        dg_ref[...] = (dy * (o * r * nw) * (sg * (1.0 + hg * (1.0 - sg)))).astype(BF16)
        dnw_ref[...] += jnp.sum(d_on * o * r, axis=0, keepdims=True)
        t1 = d_on * nw
        do = r * t1 - o * (r * r * r) * jnp.mean(t1 * o, axis=-1, keepdims=True)

        eb = jnp.exp(b)
        qhat = q * eb
        dbc_ref[...] = jnp.zeros_like(dbc_ref)
        for c in reversed(range(cps)):
            rows = slice(c * CHUNK, (c + 1) * CHUNK)
            last = (c + 1) * CHUNK - 1
            at = st_ref[c]
            dat = dat_ref[...]
            bc = b[last:last + 1]
            ebc = jnp.exp(bc)
            dec = jnp.exp(bc - b[rows])
            khat = kk[rows] * dec
            at_next = at * ebc + _dot(v[rows], khat, TN, HIGHEST)
            dbc_ref[last:last + 1, :] = jnp.sum(dat * at_next, axis=0, keepdims=True)
            dq3[c] = eb[rows] * _dot(do[rows], at, NN, HIGHEST)
            dk3[c] = dec * _dot(v[rows], dat, NN, HIGHEST)
            dv3[c] = _dot(khat, dat, NT, HIGHEST)
            dat_ref[...] = dat * ebc + _dot(do[rows], qhat[rows], TN, HIGHEST)

        for ref, val in ((q3, q), (b3, b), (k3, kk), (v3, v), (do3, do)):
            _to_chunks(ref, val)
        for a in range(CHUNK // 8):
            wn = CHUNK - 8 * a
            qa, ba, doa = q3[:, 8 * a:, :], b3[:, 8 * a:, :], do3[:, 8 * a:, :]
            bs, ks, vs = b3[:, :wn, :], k3[:, :wn, :], v3[:, :wn, :]
            row = lax.broadcasted_iota(jnp.int32, (cps, wn, HEAD_TILE), 1)
            zero = jnp.zeros((cps, wn, HEAD_TILE), F32)
            dqa, dka, dva = zero, zero, zero
            for r in range(8):
                lam = _decay_window(ba, bs, r, row)
                kd, vd = _rolled(ks, r), _rolled(vs, r)
                sc = jnp.sum(qa * kd * lam, axis=-1, keepdims=True)
                pd = jnp.sum(doa * vd, axis=-1, keepdims=True)
                dqa = dqa + pd * kd * lam
                dka = dka + _rolled(pd * qa * lam, (wn - r) % wn)
                dva = dva + _rolled(sc * doa, (wn - r) % wn)
            dq3[:, 8 * a:, :] += dqa
            dk3[:, :wn, :] += dka
            dv3[:, :wn, :] += dva
        dq, dk, dv = _from_chunks(dq3), _from_chunks(dk3), _from_chunks(dv3)

        db = q * dq - kk * dk + dbc_ref[...]
        dgl = _chunk_rev_cumsum(db, pos)
        dfv = dgl / f - dk
        dq_ref[...] = dq.astype(BF16)
        di_ref[...] = dv.astype(BF16)
        df_ref[...] = (dfv * (1.0 - lb) * sg_f * (1.0 - sg_f)).astype(BF16)
        dlbacc_ref[...] += jnp.sum(dfv * (1.0 - sg_f), axis=0, keepdims=True)

        @pl.when(step == n - 1)
        def _():
            dlb = dlbacc_ref[...]
            sel = (lax.broadcasted_iota(jnp.int32, (3, HEAD_TILE), 0) == 0).astype(F32)
            dlb_ref[...] = dlb * probs[0:1] * (sel - probs)

    out = pl.BlockSpec((slab, HEAD_TILE), lambda h, i: (n - 1 - i, h))
    par = lambda rows: pl.BlockSpec((rows, HEAD_TILE), lambda h, i: (0, h))
    dyspec = pl.BlockSpec((slab, HEAD_TILE), lambda h, i: (n - 1 - i, CONV_DIM // HEAD_TILE + h))
    shape = jax.ShapeDtypeStruct((s, HG_DIM), BF16)
    slab_f32 = pltpu.VMEM((slab, HEAD_TILE), F32)
    return pl.pallas_call(
        body, grid=(heads, n),
        in_specs=[dyspec, col(12), col(16), col(20), col(24), out,
                  pl.BlockSpec((None, cps, HEAD_TILE, HEAD_TILE), lambda h, i: (h, n - 1 - i, 0, 0)), par(3), par(1)],
        out_specs=[out, out, out, out, par(1), par(3)],
        out_shape=[shape, shape, shape, shape, jax.ShapeDtypeStruct((1, HG_DIM), F32),
                   jax.ShapeDtypeStruct((3, HG_DIM), F32)],
        scratch_shapes=[pltpu.VMEM((HEAD_TILE, HEAD_TILE), F32), pltpu.VMEM((1, HEAD_TILE), F32), slab_f32]
        + [pltpu.VMEM((cps, CHUNK, HEAD_TILE), F32)] * 8,
        name="hgrn_bwd", compiler_params=_cp("parallel", "arbitrary"))(
            dmix, u, u, u, u, o_raw, states, lb_logits, hg_norm)


def _pair_ones():
    row = lax.broadcasted_iota(jnp.int32, (HEAD_TILE, HEAD_TILE), 0) // SB_HEAD_DIM
    col = lax.broadcasted_iota(jnp.int32, (HEAD_TILE, HEAD_TILE), 1) // SB_HEAD_DIM
    ones = (row == col).astype(BF16)
    return jnp.concatenate([ones, ones], axis=0)


def _pair_mean(x, ones):
    return _split_dot(x, ones) * (1.0 / SB_HEAD_DIM)


def _pair_rstd(x, ones):
    return lax.rsqrt(_pair_mean(x * x, ones) + EPS)


def _lane_tiles():
    return _col_blocks(D_MODEL // HEAD_TILE, HEAD_TILE)


def qk_norm_fwd(qkv, qg, kg):
    s = qkv.shape[0]
    tm = min(256, s)

    def body(q_ref, k_ref, v_ref, qg_ref, kg_ref, qn_ref, kn_ref, vb_ref):
        ones = _pair_ones()
        for cols in _lane_tiles():
            qv = q_ref[:, cols]
            kv = k_ref[:, cols]
            qn_ref[:, cols] = (qv * _pair_rstd(qv, ones) * qg_ref[...]).astype(BF16)
            kn_ref[:, cols] = (kv * _pair_rstd(kv, ones) * kg_ref[...]).astype(BF16)
        vb_ref[...] = v_ref[...].astype(BF16)

    col = lambda b: pl.BlockSpec((tm, D_MODEL), lambda i: (i, b))
    out = col(0)
    shape = jax.ShapeDtypeStruct((s, D_MODEL), BF16)
    return pl.pallas_call(
        body, grid=(s // tm,), in_specs=[col(0), col(1), col(2), _row_spec(HEAD_TILE), _row_spec(HEAD_TILE)],
        out_specs=[out, out, out], out_shape=[shape, shape, shape], name="qk_norm_fwd",
        compiler_params=_cp("parallel"))(qkv, qkv, qkv, qg, kg)


def qk_norm_bwd(dqn, dkn, dv, qkv, qg, kg):
    s = qkv.shape[0]
    tm = min(256, s)

    def body(dqn_ref, dkn_ref, dv_ref, q_ref, k_ref, qg_ref, kg_ref, dq_ref, dk_ref, dvb_ref, dqg_ref, dkg_ref):
        @pl.when(pl.program_id(0) == 0)
        def _():
            dqg_ref[...] = jnp.zeros_like(dqg_ref)
            dkg_ref[...] = jnp.zeros_like(dkg_ref)

        ones = _pair_ones()

        def one(x_ref, g_ref, dn_ref, dx_ref, dgain_ref):
            dgain = jnp.zeros((1, HEAD_TILE), F32)
            for cols in _lane_tiles():
                xv = x_ref[:, cols]
                r = _pair_rstd(xv, ones)
                xn = xv * r
                dn = dn_ref[:, cols]
                dgain = dgain + jnp.sum(dn * xn, axis=0, keepdims=True)
                t1 = dn * g_ref[...]
                dx_ref[:, cols] = (r * (t1 - xn * _pair_mean(t1 * xn, ones))).astype(BF16)
            dgain_ref[...] += dgain

        one(q_ref, qg_ref, dqn_ref, dq_ref, dqg_ref)
        one(k_ref, kg_ref, dkn_ref, dk_ref, dkg_ref)
        dvb_ref[...] = dv_ref[...].astype(BF16)

    col = lambda b: pl.BlockSpec((tm, D_MODEL), lambda i: (i, b))
    out = col(0)
    gain = _row_spec(HEAD_TILE)
    shape = jax.ShapeDtypeStruct((s, D_MODEL), BF16)
    grow = jax.ShapeDtypeStruct((1, HEAD_TILE), F32)
    return pl.pallas_call(
        body, grid=(s // tm,), in_specs=[out, out, out, col(0), col(1), gain, gain],
        out_specs=[out, out, out, gain, gain], out_shape=[shape, shape, shape, grow, grow], name="qk_norm_bwd",
        compiler_params=_cp("arbitrary"))(dqn, dkn, dv, qkv, qkv, qg, kg)


def _split_dot(x, u):
    hi = x.astype(BF16)
    lo = (x - hi.astype(F32)).astype(BF16)
    if u.shape[0] == 2 * x.shape[1]:
        return _dot(jnp.concatenate([hi, lo], axis=1), u, NN)
    return _dot(hi, u, NN) + _dot(lo, u, NN)


def _sb_tile(qs, kb, carry, causal, suffix, diag):
    z = _dot(qs, kb, NT)
    lb = jnp.minimum(z, 0.0) - jnp.log(1.0 + jnp.exp(-jnp.abs(z)))
    lom = lb - z
    if diag:
        lom = jnp.where(causal, lom, 0.0)
    ws = []
    for j in reversed(range(ATT_TILE // ATT_BLOCK)):
        cols = slice(j * ATT_BLOCK, (j + 1) * ATT_BLOCK)
        ws.append(jnp.exp(z[:, cols] + _split_dot(lom[:, cols], suffix) + carry))
        carry = carry + jnp.sum(lom[:, cols], axis=-1, keepdims=True)
    w = jnp.concatenate(ws[::-1], axis=1)
    if diag:
        w = jnp.where(causal, w, 0.0)
    return lb, w, carry


def _sb_consts():
    row = lax.broadcasted_iota(jnp.int32, (ATT_BLOCK, ATT_BLOCK), 0)
    col = lax.broadcasted_iota(jnp.int32, (ATT_BLOCK, ATT_BLOCK), 1)
    suffix = (row >= col).astype(BF16)
    suffix = jnp.concatenate([suffix, suffix], axis=0)
    trow = lax.broadcasted_iota(jnp.int32, (ATT_TILE, ATT_TILE), 0)
    tcol = lax.broadcasted_iota(jnp.int32, (ATT_TILE, ATT_TILE), 1)
    causal = tcol < trow
    lo = lax.broadcasted_iota(jnp.int32, (ATT_TILE, HEAD_TILE), 1) < SB_HEAD_DIM
    return suffix, causal, lo


def _rows(i):
    return pl.ds(pl.multiple_of(i * ATT_TILE, ATT_TILE), ATT_TILE)


def _head_query(qb, mask):
    return (jnp.where(mask, qb, 0.0) * (SB_HEAD_DIM ** -0.5)).astype(BF16)


def sb_attn_fwd(qn, kn, vb):
    s = qn.shape[0]
    nq = s // ATT_TILE
    nt = D_MODEL // HEAD_TILE

    def body(q_ref, k_ref, v_ref, o_ref, ob_ref):
        suffix, causal, lo = _sb_consts()

        def qtile(qi, _):
            qb = q_ref[_rows(qi), :].astype(F32)
            qs = [_head_query(qb, lo), _head_query(qb, ~lo)]

            def step(kj, state, diag):
                kb = k_ref[_rows(kj), :]
                vt = v_ref[_rows(kj), :]
                out = []
                for hh in range(2):
                    carry, acc = state[hh]
                    _, w, carry = _sb_tile(qs[hh], kb, carry, causal, suffix, diag)
                    out.append((carry, acc + _dot(w.astype(BF16), vt, NN)))
                return tuple(out)

            start = (jnp.zeros((ATT_TILE, 1), F32), jnp.zeros((ATT_TILE, HEAD_TILE), F32))
            state = step(qi, (start, start), True)
            state = lax.fori_loop(0, qi, lambda jj, st: step(qi - 1 - jj, st, False), state)
            o = jnp.where(lo, state[0][1], state[1][1])
            o_ref[_rows(qi), :] = o
            ob_ref[_rows(qi), :] = o.astype(BF16)
            return 0

        lax.fori_loop(0, nq, qtile, 0)

    spec = pl.BlockSpec((s, HEAD_TILE), lambda p: (0, p))
    return pl.pallas_call(
        body, grid=(nt,), in_specs=[spec, spec, spec], out_specs=[spec, spec],
        out_shape=[jax.ShapeDtypeStruct((s, D_MODEL), F32), jax.ShapeDtypeStruct((s, D_MODEL), BF16)],
        name="sb_attn_fwd", compiler_params=_cp("parallel"))(qn, kn, vb)


def sb_attn_bwd(qn, kn, vb, o, do):
    s = qn.shape[0]
    nq = s // ATT_TILE
    nt = D_MODEL // HEAD_TILE

    def body(q_ref, k_ref, v_ref, o_ref, do_ref, dq_ref, dk_ref, dv_ref):
        suffix, causal, lo = _sb_consts()
        dk_ref[...] = jnp.zeros_like(dk_ref)
        dv_ref[...] = jnp.zeros_like(dv_ref)

        def qtile(qi, _):
            qb = q_ref[_rows(qi), :].astype(F32)
            dob = do_ref[_rows(qi), :].astype(BF16).astype(F32)
            prod = dob * o_ref[_rows(qi), :]
            masks = [lo, ~lo]
            qs = [_head_query(qb, m) for m in masks]
            dos = [jnp.where(m, dob, 0.0).astype(BF16) for m in masks]
            totals = [jnp.sum(jnp.where(m, prod, 0.0), axis=-1, keepdims=True) for m in masks]

            def step(kj, state, diag):
                kb = k_ref[_rows(kj), :]
                vt = v_ref[_rows(kj), :]
                out = []
                dk = jnp.zeros((ATT_TILE, HEAD_TILE), F32)
                dv = jnp.zeros((ATT_TILE, HEAD_TILE), F32)
                for hh in range(2):
                    carry, carry_e, dq = state[hh]
                    lb, w, carry = _sb_tile(qs[hh], kb, carry, causal, suffix, diag)
                    wb = w.astype(BF16)
                    e = _dot(dos[hh], vt, NT) * wb.astype(F32)
                    befores = []
                    for j in reversed(range(ATT_TILE // ATT_BLOCK)):
                        cols = slice(j * ATT_BLOCK, (j + 1) * ATT_BLOCK)
                        befores.append(totals[hh] - carry_e - _split_dot(e[:, cols], suffix))
                        carry_e = carry_e + jnp.sum(e[:, cols], axis=-1, keepdims=True)
                    before = jnp.concatenate(befores[::-1], axis=1)
                    beta = jnp.exp(lb)
                    dz = e * (1.0 - beta) - before * beta
                    if diag:
                        dz = jnp.where(causal, dz, 0.0)
                    dzb = dz.astype(BF16)
                    dq = dq + _dot(dzb, kb, NN)
                    dk = dk + _dot(dzb, qs[hh], TN)
                    dv = dv + _dot(wb, dos[hh], TN)
                    out.append((carry, carry_e, dq))
                dk_ref[_rows(kj), :] += dk
                dv_ref[_rows(kj), :] += dv
                return tuple(out)

            zero = jnp.zeros((ATT_TILE, 1), F32)
            start = (zero, zero, jnp.zeros((ATT_TILE, HEAD_TILE), F32))
            state = step(qi, (start, start), True)
            state = lax.fori_loop(0, qi, lambda jj, st: step(qi - 1 - jj, st, False), state)
            dq_ref[_rows(qi), :] = jnp.where(lo, state[0][2], state[1][2]) * (SB_HEAD_DIM ** -0.5)
            return 0

        lax.fori_loop(0, nq, qtile, 0)

    spec = pl.BlockSpec((s, HEAD_TILE), lambda p: (0, p))
    shape = jax.ShapeDtypeStruct((s, D_MODEL), F32)
    return pl.pallas_call(
        body, grid=(nt,), in_specs=[spec] * 5, out_specs=[spec] * 3, out_shape=[shape] * 3,
        name="sb_attn_bwd", compiler_params=_cp("parallel"))(qn, kn, vb, o, do)


def ada_fwd(c_all, ada_w, ada_b_shard):
    layers, d, cols = ada_w.shape
    tn = 512

    def body(c_ref, w_ref, b_ref, out_ref):
        cv = c_ref[...]
        act = (cv * _sigmoid(cv)).astype(BF16)
        out_ref[...] = _dot(act, w_ref[...].astype(BF16), NN) + b_ref[...]

    return pl.pallas_call(
        body, grid=(layers, cols // tn),
        in_specs=[pl.BlockSpec((N_DEV, d), lambda l, j: (0, 0)), pl.BlockSpec((None, d, tn), lambda l, j: (l, 0, j)),
                  pl.BlockSpec((None, 1, tn), lambda l, j: (l, 0, j))],
        out_specs=pl.BlockSpec((None, N_DEV, tn), lambda l, j: (l, 0, j)),
        out_shape=jax.ShapeDtypeStruct((layers, N_DEV, cols), F32), name="ada_fwd",
        compiler_params=_cp("parallel", "parallel"))(c_all, ada_w, ada_b_shard)


def ada_w_grad(c_all, dmod):
    layers, _, cols = dmod.shape
    d = c_all.shape[1]
    tn = 512

    def body(c_ref, dm_ref, out_ref):
        cv = c_ref[...]
        out_ref[...] = _dot(cv * _sigmoid(cv), dm_ref[...], TN, HIGHEST)

    return pl.pallas_call(
        body, grid=(layers, cols // tn),
        in_specs=[pl.BlockSpec((N_DEV, d), lambda l, j: (0, 0)), pl.BlockSpec((None, N_DEV, tn), lambda l, j: (l, 0, j))],
        out_specs=pl.BlockSpec((None, d, tn), lambda l, j: (l, 0, j)),
        out_shape=jax.ShapeDtypeStruct((layers, d, cols), F32), name="ada_w_grad",
        compiler_params=_cp("parallel", "parallel"))(c_all, dmod)


def _row_tile(r, c, elems):
    best = 8
    for t in range(8, r + 1, 8):
        if r % t == 0 and t * c <= elems:
            best = t
    return best


def sum_rows(x, name):
    n, r, c = x.shape
    tr = _row_tile(r, c, 1 << 17)

    def body(x_ref, out_ref):
        acc = x_ref[0]
        for i in range(1, n):
            acc = acc + x_ref[i]
        out_ref[...] = acc

    return pl.pallas_call(
        body, grid=(r // tr,), in_specs=[pl.BlockSpec((n, tr, c), lambda i: (0, i, 0))],
        out_specs=pl.BlockSpec((tr, c), lambda i: (i, 0)), out_shape=jax.ShapeDtypeStruct((r, c), F32), name=name,
        compiler_params=_cp("parallel"))(x)


def adamw(w, g, m, v, name):
    r, c = w.shape
    tr = _row_tile(r, c, 1 << 17)
    c1 = 1.0 - ADAM_B1 ** ADAM_STEP
    c2 = 1.0 - ADAM_B2 ** ADAM_STEP

    def body(w_ref, g_ref, m_ref, v_ref, d_ref, nm_ref, nv_ref):
        gv = g_ref[...]
        nm = ADAM_B1 * m_ref[...] + (1.0 - ADAM_B1) * gv
        nv = ADAM_B2 * v_ref[...] + (1.0 - ADAM_B2) * (gv * gv)
        d_ref[...] = -ADAM_LR * ((nm / c1) / (jnp.sqrt(nv / c2) + ADAM_EPS) + ADAM_WD * w_ref[...])
        nm_ref[...] = nm
        nv_ref[...] = nv

    spec = pl.BlockSpec((tr, c), lambda i: (i, 0))
    shape = jax.ShapeDtypeStruct((r, c), F32)
    return pl.pallas_call(
        body, grid=(r // tr,), in_specs=[spec] * 4, out_specs=[spec] * 3, out_shape=[shape] * 3, name=name,
        compiler_params=_cp("parallel"))(w, g, m, v)


def _me():
    return lax.axis_index("x"), lax.axis_index("y"), lax.axis_index("c")


def _flip(v, bit):
    return 1 - v if bit else v


HBM = pl.BlockSpec(memory_space=pl.ANY)
VMEM = pl.BlockSpec(memory_space=pltpu.VMEM)


def all_gather_rows(v, name):
    n = v.shape[1]

    def body(v_ref, out_ref, send_sems, recv_sems):
        x, y, c = _me()
        me = 4 * x + 2 * y + c
        out_ref[pl.ds(me, 1), :] = v_ref[...]
        copies = []
        for j in range(1, N_DEV):
            peer = (_flip(x, j & 4), _flip(y, j & 2), _flip(c, j & 1))
            copies.append(pltpu.make_async_remote_copy(
                src_ref=v_ref, dst_ref=out_ref.at[pl.ds(me, 1), :], send_sem=send_sems.at[j - 1],
                recv_sem=recv_sems.at[j - 1], device_id=peer, device_id_type=MESH))
        for cp in copies:
            cp.start()
        for cp in copies:
            cp.wait()

    return pl.pallas_call(
        body, in_specs=[VMEM], out_specs=VMEM, out_shape=jax.ShapeDtypeStruct((N_DEV, n), F32),
        scratch_shapes=[pltpu.SemaphoreType.DMA((N_DEV - 1,)), pltpu.SemaphoreType.DMA((N_DEV - 1,))], name=name)(v)


def _chip_peers(x, y):
    return [((1 - x, y), 2 * (1 - x) + y), ((x, 1 - y), 2 * x + (1 - y)), ((1 - x, 1 - y), 2 * (1 - x) + (1 - y))]


def cast_to_slot(w, name):
    r, c = w.shape
    tr = _row_tile(r, c, 1 << 18)

    def body(w_ref, out_ref):
        out_ref[...] = w_ref[...].astype(BF16)

    return pl.pallas_call(
        body, grid=(r // tr,), in_specs=[pl.BlockSpec((tr, c), lambda i: (i, 0))],
        out_specs=pl.BlockSpec((None, tr, c), lambda i: (2 * lax.axis_index("x") + lax.axis_index("y"), i, 0)),
        out_shape=jax.ShapeDtypeStruct((N_CHIPS, r, c), BF16), name=name, compiler_params=_cp("parallel"))(w)


def chip_all_gather(slots, name):
    n = len(slots)

    def body(*refs):
        gather_start(refs[n:2 * n], *refs[2 * n:2 * n + 2])
        gather_finish(refs[n:2 * n], *refs[2 * n:])

    return pl.pallas_call(
        body, in_specs=[HBM] * n, out_specs=[HBM] * n,
        out_shape=[jax.ShapeDtypeStruct(s.shape, s.dtype) for s in slots],
        input_output_aliases={a: a for a in range(n)}, scratch_shapes=gather_semaphores(n), name=name)(*slots)


def gather_semaphores(n):
    return [pltpu.SemaphoreType.DMA((n, 3))] * 4


def _chip_copies(outs, send_sems, recv_sems):
    x, y, c = _me()
    copies = []
    for a, out in enumerate(outs):
        half = out.shape[1] // 2
        rows = out.at[2 * x + y, pl.ds(c * half, half), :]
        for p, (chip, _) in enumerate(_chip_peers(x, y)):
            copies.append(pltpu.make_async_remote_copy(
                src_ref=rows, dst_ref=rows, send_sem=send_sems.at[a, p], recv_sem=recv_sems.at[a, p],
                device_id=(*chip, c), device_id_type=MESH))
    return copies


def gather_start(outs, send_sems, recv_sems):
    for cp in _chip_copies(outs, send_sems, recv_sems):
        cp.start()


def gather_finish(outs, send_sems, recv_sems, pass_send_sems, pass_recv_sems):
    x, y, c = _me()
    peers = _chip_peers(x, y)
    passed = []
    for a, out in enumerate(outs):
        half = out.shape[1] // 2
        for p, (chip, slot) in enumerate(peers):
            rows = out.at[slot, pl.ds(c * half, half), :]
            pltpu.make_async_remote_copy(
                src_ref=rows, dst_ref=rows, send_sem=send_sems.at[a, p], recv_sem=recv_sems.at[a, p],
                device_id=(*chip, c), device_id_type=MESH).wait_recv()
            cp = pltpu.make_async_remote_copy(
                src_ref=rows, dst_ref=rows, send_sem=pass_send_sems.at[a, p], recv_sem=pass_recv_sems.at[a, p],
                device_id=(x, y, 1 - c), device_id_type=MESH)
            cp.start()
            passed.append(cp)
    for a, out in enumerate(outs):
        half = out.shape[1] // 2
        for p, (_, slot) in enumerate(peers):
            theirs = out.at[slot, pl.ds((1 - c) * half, half), :]
            pltpu.make_async_remote_copy(
                src_ref=theirs, dst_ref=theirs, send_sem=pass_send_sems.at[a, p], recv_sem=pass_recv_sems.at[a, p],
                device_id=(x, y, 1 - c), device_id_type=MESH).wait_recv()
    for cp in _chip_copies(outs, send_sems, recv_sems) + passed:
        cp.wait_send()


def sibling_split(grads, name):
    n = len(grads)

    def body(*refs):
        ins, got = refs[:n], refs[n:2 * n]
        send_sems, recv_sems = refs[2 * n:]
        x, y, c = _me()
        started = []
        for a in range(n):
            half = ins[a].shape[1] // 2
            give = pltpu.make_async_remote_copy(
                src_ref=ins[a].at[:, pl.ds((1 - c) * half, half), :], dst_ref=got[a], send_sem=send_sems.at[a],
                recv_sem=recv_sems.at[a], device_id=(x, y, 1 - c), device_id_type=MESH)
            give.start()
            started.append(give)
        for cp in started:
            cp.wait()

    return pl.pallas_call(
        body, in_specs=[HBM] * n, out_specs=[HBM] * n,
        out_shape=[jax.ShapeDtypeStruct((g.shape[0], g.shape[1] // 2, g.shape[2]), g.dtype) for g in grads],
        scratch_shapes=[pltpu.SemaphoreType.DMA((n,)), pltpu.SemaphoreType.DMA((n,))], name=name)(*grads)


def pair_sum(g, got, name):
    k, half, c = got.shape
    tr = _row_tile(half, c, 1 << 18)
    nb = half // tr

    def body(g_ref, got_ref, out_ref):
        out_ref[...] = (g_ref[...] + got_ref[...]).astype(BF16)

    spec = pl.BlockSpec((None, tr, c), lambda j, i: (j, i, 0))
    return pl.pallas_call(
        body, grid=(k, nb),
        in_specs=[pl.BlockSpec((None, tr, c), lambda j, i: (j, lax.axis_index("c") * nb + i, 0)), spec],
        out_specs=spec, out_shape=jax.ShapeDtypeStruct(got.shape, BF16), name=name,
        compiler_params=_cp("parallel", "parallel"))(g, got)


def chip_scatter(parts, name):
    n = len(parts)

    def body(*refs):
        ins, outs = refs[:n], refs[n:2 * n]
        send_sems, recv_sems = refs[2 * n:]
        x, y, c = _me()
        started = []
        for a in range(n):
            for p, (chip, slot) in enumerate(_chip_peers(x, y)):
                cp = pltpu.make_async_remote_copy(
                    src_ref=ins[a].at[slot], dst_ref=outs[a].at[p], send_sem=send_sems.at[a, p],
                    recv_sem=recv_sems.at[a, p], device_id=(*chip, c), device_id_type=MESH)
                cp.start()
                started.append(cp)
        for cp in started:
            cp.wait()

    return pl.pallas_call(
        body, in_specs=[HBM] * n, out_specs=[HBM] * n,
        out_shape=[jax.ShapeDtypeStruct((3, *p.shape[1:]), p.dtype) for p in parts],
        scratch_shapes=[pltpu.SemaphoreType.DMA((n, 3)), pltpu.SemaphoreType.DMA((n, 3))], name=name)(*parts)


def chip_sum(part, landed, name):
    _, half, c = landed.shape
    tr = _row_tile(half, c, 1 << 17)
    nb = half // tr

    def body(part_ref, landed_ref, out_ref):
        up = lambda v: v.astype(F32)
        out_ref[...] = ((up(part_ref[...]) + up(landed_ref[0])) + up(landed_ref[1])) + up(landed_ref[2])

    return pl.pallas_call(
        body, grid=(nb,),
        in_specs=[pl.BlockSpec((None, tr, c), lambda i: (2 * lax.axis_index("x") + lax.axis_index("y"), i, 0)),
                  pl.BlockSpec((3, tr, c), lambda i: (0, i, 0))],
        out_specs=pl.BlockSpec((tr, c), lambda i: (lax.axis_index("c") * nb + i, 0)),
        out_shape=jax.ShapeDtypeStruct((2 * half, c), F32), name=name, compiler_params=_cp("parallel"))(part, landed)


def sibling_join(arrays, name):
    n = len(arrays)

    def body(*refs):
        ins, outs = refs[:n], refs[n:2 * n]
        send_sems, recv_sems = refs[2 * n:]
        x, y, c = _me()
        started = []
        for a in range(n):
            half = ins[a].shape[0] // 2
            give = pltpu.make_async_remote_copy(
                src_ref=ins[a].at[pl.ds(c * half, half), :], dst_ref=outs[a].at[pl.ds(c * half, half), :],
                send_sem=send_sems.at[a], recv_sem=recv_sems.at[a], device_id=(x, y, 1 - c), device_id_type=MESH)
            give.start()
            started.append(give)
        for cp in started:
            cp.wait()

    return pl.pallas_call(
        body, in_specs=[HBM] * n, out_specs=[HBM] * n,
        out_shape=[jax.ShapeDtypeStruct(h.shape, h.dtype) for h in arrays],
        input_output_aliases={a: a for a in range(n)},
        scratch_shapes=[pltpu.SemaphoreType.DMA((n,)), pltpu.SemaphoreType.DMA((n,))], name=name)(*arrays)


def reduce_to_owner(grads):
    got = sibling_split(grads, "grad_sibling_split")
    parts = [pair_sum(g, h, f"grad_pair_sum_{a}") for a, (g, h) in enumerate(zip(grads, got))]
    landed = chip_scatter(parts, "grad_chip_scatter")
    halves = [chip_sum(p, l, f"grad_chip_sum_{a}") for a, (p, l) in enumerate(zip(parts, landed))]
    return sibling_join(halves, "grad_sibling_join")


def _pad_row(v, n):
    return jnp.pad(v.reshape(1, -1), ((0, 0), (0, n - v.size)))


def local_step(x, target, mod, wts):
    d = D_MODEL
    row = lambda v: v.reshape(1, -1)
    mods = [[row(mod[l, i * d:(i + 1) * d]) for i in range(6)] for l in range(2)]
    saved = []
    for l in range(2):
        shift1, scale1, gate1, shift2, scale2, gate2 = mods[l]
        g_mix, g_mlp = row(wts["norm_mix"][l]), row(wts["norm_mlp"][l])
        h = norm_mod_fwd(x, g_mix, scale1, shift1, f"norm_mix_fwd_{l}")
        if l == 0:
            u = matmul_nn_chunked(h, wts["w_in"], F32, "in_proj_ab")[0]
            y_a = conv_mixer_fwd(u, wts["conv_w"])
            y_b, o_raw, states, *rode = hgrn_fwd(u, wts["lb_logits"], wts["hg_norm"], wts.get("riding", ()))
            if rode:
                r_qkv, r_out_c, r_w1, r_w2 = rode
                wts = dict(wts, w_qkv=r_qkv, w_out_c=r_out_c.reshape(d, d), w1=[wts["w1"][0], r_w1],
                           w2=[wts["w2"][0], r_w2.reshape(D_FF, d)])
            mix = jnp.concatenate([y_a, y_b], axis=1)
            y, x1 = proj_residual(mix, wts["w_out_ab"], x, gate1, "out_proj_ab")
            ctx = (u, o_raw, states)
        else:
            qkv = matmul_nn_chunked(h, wts["w_qkv"], F32, "in_proj_c")[0]
            qn, kn, vb = qk_norm_fwd(qkv, wts["qg"], wts["kg"])
            o, mix = sb_attn_fwd(qn, kn, vb)
            y, x1 = proj_residual(mix, wts["w_out_c"], x, gate1, "out_proj_c")
            ctx = (qkv, qn, kn, vb, o)
        h2 = norm_mod_fwd(x1, g_mlp, scale2, shift2, f"norm_mlp_fwd_{l}")
        act, r = mlp_up(h2, wts["w1"][l], f"mlp_up_{l}")
        y2, x2 = proj_residual(act, wts["w2"][l], x1, gate2, f"mlp_down_{l}")
        saved.append((x, h, mix, y, x1, h2, act, r, y2, ctx))
        x = x2

    dx, loss_row = loss_and_grad(x, target)
    small, big = {}, {}
    dmod = [None, None]
    d_norm_mix, d_norm_mlp = [None, None], [None, None]
    for l in (1, 0):
        shift1, scale1, gate1, shift2, scale2, gate2 = mods[l]
        g_mix, g_mlp = row(wts["norm_mix"][l]), row(wts["norm_mlp"][l])
        x0, h, mix, y, x1, h2, act, r, y2, ctx = saved[l]
        dy2, dgate2 = gate_bwd(dx, y2, gate2, f"mlp_gate_bwd_{l}")
        dz = mlp_down_bwd(dy2, wts["w2"][l], r, f"mlp_down_bwd_{l}")
        big[f"w2_{l}"] = matmul_tn_plain(act, dy2, f"mlp_w2_grad_{l}")
        dh2 = matmul_nt_chunked(dz, wts["w1"][l], f"mlp_up_bwd_{l}")
        big[f"w1_{l}"] = matmul_tn_chunked(h2, dz, f"mlp_w1_grad_{l}")
        dx1, d_norm_mlp[l], dscale2, dshift2 = norm_mod_bwd(dh2, x1, g_mlp, scale2, dx, f"norm_mlp_bwd_{l}")
        dy, dgate1 = gate_bwd(dx1, y, gate1, f"mix_gate_bwd_{l}")
        if l == 0:
            u, o_raw, states = ctx
            dmix = matmul_nt_plain(dy, wts["w_out_ab"], "out_proj_ab_bwd")
            big["w_out_ab"] = matmul_tn_plain(mix, dy, "w_out_ab_grad")
            dab, dac, dah, small["conv_w"] = conv_mixer_bwd(dmix, u, wts["conv_w"])
            dhq, dhf, dhi, dhg, small["hg_norm"], small["lb_logits"] = hgrn_bwd(
                dmix, u, o_raw, states, wts["lb_logits"], wts["hg_norm"])
            du = jnp.concatenate([dab, dac, dah, dhq, dhf, dhi, dhg], axis=1)
            dh = matmul_nt_chunked(du, wts["w_in"], "in_proj_ab_bwd")
            big["w_in"] = matmul_tn_chunked(h, du, "w_in_grad")
        else:
            qkv, qn, kn, vb, o = ctx
            do = matmul_nt_plain(dy, wts["w_out_c"], "out_proj_c_bwd")
            big["w_out_c"] = matmul_tn_plain(mix, dy, "w_out_c_grad")
            dqn, dkn, dv = sb_attn_bwd(qn, kn, vb, o, do)
            dq, dk, dvb, dqg, dkg = qk_norm_bwd(dqn, dkn, dv, qkv, wts["qg"], wts["kg"])
            small["q_norm"] = dqg[:, :SB_HEAD_DIM] + dqg[:, SB_HEAD_DIM:]
            small["k_norm"] = dkg[:, :SB_HEAD_DIM] + dkg[:, SB_HEAD_DIM:]
            dqkv = jnp.concatenate([dq, dk, dvb], axis=1)
            dh = matmul_nt_chunked(dqkv, wts["w_qkv"], "in_proj_c_bwd")
            big["w_qkv"] = matmul_tn_chunked(h, dqkv, "w_qkv_grad")
        dx, d_norm_mix[l], dscale1, dshift1 = norm_mod_bwd(dh, x0, g_mix, scale1, dx1, f"norm_mix_bwd_{l}")
        dmod[l] = jnp.concatenate([dshift1, dscale1, dgate1, dshift2, dscale2, dgate2], axis=1)
    small["mod"] = jnp.concatenate(dmod, axis=0)
    small["norm_mix"] = jnp.concatenate(d_norm_mix, axis=0)
    small["norm_mlp"] = jnp.concatenate(d_norm_mlp, axis=0)
    return loss_row, dx, small, big


SMALL_ORDER = ("mod", "norm_mix", "norm_mlp", "conv_w", "hg_norm", "lb_logits", "q_norm", "k_norm")


def kernel(x, c, ada_w, ada_b, norm_mix, norm_mlp, w_in_ab, conv_w, hg_norm, lb_logits, w_out_ab, w_qkv, q_norm, k_norm, w_out_c, mlp_w1, mlp_w2, loss_target, m_ada_w, m_ada_b, m_norm_mix, m_norm_mlp, m_w_in_ab, m_conv_w, m_hg_norm, m_lb_logits, m_w_out_ab, m_w_qkv, m_q_norm, m_k_norm, m_w_out_c, m_mlp_w1, m_mlp_w2, v_ada_w, v_ada_b, v_norm_mix, v_norm_mlp, v_w_in_ab, v_conv_w, v_hg_norm, v_lb_logits, v_w_out_ab, v_w_qkv, v_q_norm, v_k_norm, v_w_out_c, v_mlp_w1, v_mlp_w2):
    d = D_MODEL
    ax, ay, ac = _me()
    chip = 2 * ax + ay
    dev = 2 * chip + ac
    cols = ada_w.shape[2]

    first = all_gather_rows(_pad_row(jnp.concatenate([c.reshape(-1), conv_w.reshape(-1)]), 1536), "gather_cond")
    c_all = first[:, :d]
    conv_full = first[::2, d:d + 3 * HEAD_TILE].reshape(N_CHIPS, 3, HEAD_TILE).transpose(1, 0, 2).reshape(3, CONV_DIM)
    ada_b_shard = lax.dynamic_slice(ada_b, (0, chip * cols), (2, cols)).reshape(2, 1, cols)
    mod_cols = ada_fwd(c_all, ada_w, ada_b_shard)
    mod_all = all_gather_rows(mod_cols.reshape(1, -1), "gather_mod").reshape(N_DEV, 2, N_DEV, cols)
    mod = lax.dynamic_index_in_dim(mod_all[::2], dev, axis=2, keepdims=False).transpose(1, 0, 2).reshape(2, 6 * d)

    shards = [w_in_ab[0], w_out_ab[0], mlp_w1[0], mlp_w2[0], w_qkv[0], w_out_c[0], mlp_w1[1], mlp_w2[1]]
    slots = [cast_to_slot(s, f"cast_weight_{a}") for a, s in enumerate(shards)]
    g_in, g_out_ab, g_w1a, g_w2a = chip_all_gather(slots[:4], "gather_weights")
    wts = dict(
        norm_mix=norm_mix, norm_mlp=norm_mlp, w_in=g_in, conv_w=conv_full, hg_norm=hg_norm, lb_logits=lb_logits,
        w_out_ab=g_out_ab.reshape(d, d), qg=jnp.tile(q_norm, (1, 2)), kg=jnp.tile(k_norm, (1, 2)),
        w1=[g_w1a], w2=[g_w2a.reshape(D_FF, d)], riding=slots[4:])

    loss_row, grad_x, small, big = local_step(x[0], loss_target[0], mod, wts)

    flat = jnp.concatenate([small[k].reshape(-1) for k in SMALL_ORDER] + [loss_row[0, :1]])
    n_small = -(-flat.size // 1024) * 1024
    gathered = all_gather_rows(_pad_row(flat, n_small), "gather_small")
    total = sum_rows(gathered.reshape(N_DEV, 8, n_small // 8), "small_sum").reshape(-1)
    sizes = [small[k].size for k in SMALL_ORDER]
    offs = [sum(sizes[:i]) for i in range(len(sizes) + 1)]
    tot = {k: total[offs[i]:offs[i + 1]].reshape(small[k].shape) for i, k in enumerate(SMALL_ORDER)}
    loss = total[offs[-1]]
    mod_rows = gathered[:, :2 * 6 * d].reshape(N_DEV, 2, 6 * d)
    dmod_cols = lax.dynamic_slice(mod_rows, (0, 0, chip * cols), (N_DEV, 2, cols)).transpose(1, 0, 2)
    g_ada_w = ada_w_grad(c_all, dmod_cols)

    as_chunks = lambda g: g.reshape(N_CHIPS, g.shape[0] // N_CHIPS, g.shape[1])
    names = ["w_in", "w_out_ab", "w_qkv", "w_out_c", "w1_0", "w1_1", "w2_0", "w2_1"]
    chunked = [big[k] if big[k].ndim == 3 else as_chunks(big[k]) for k in names]
    r_in, r_out_ab, r_qkv, r_out_c, r_w1a, r_w1b, r_w2a, r_w2b = reduce_to_owner(chunked)

    grads = dict(
        ada_w=g_ada_w, ada_b=tot["mod"], norm_mix=tot["norm_mix"], norm_mlp=tot["norm_mlp"], w_in_ab=r_in[None],
        conv_w=lax.dynamic_slice(tot["conv_w"], (0, chip * HEAD_TILE), (3, HEAD_TILE))[None], hg_norm=tot["hg_norm"],
        lb_logits=tot["lb_logits"], w_out_ab=r_out_ab[None], w_qkv=r_qkv[None], q_norm=tot["q_norm"],
        k_norm=tot["k_norm"], w_out_c=r_out_c[None], mlp_w1=jnp.stack([r_w1a, r_w1b]), mlp_w2=jnp.stack([r_w2a, r_w2b]))
    weights = dict(ada_w=ada_w, ada_b=ada_b, norm_mix=norm_mix, norm_mlp=norm_mlp, w_in_ab=w_in_ab, conv_w=conv_w,
                   hg_norm=hg_norm, lb_logits=lb_logits, w_out_ab=w_out_ab, w_qkv=w_qkv, q_norm=q_norm, k_norm=k_norm,
                   w_out_c=w_out_c, mlp_w1=mlp_w1, mlp_w2=mlp_w2)
    m_in = dict(ada_w=m_ada_w, ada_b=m_ada_b, norm_mix=m_norm_mix, norm_mlp=m_norm_mlp, w_in_ab=m_w_in_ab,
                conv_w=m_conv_w, hg_norm=m_hg_norm, lb_logits=m_lb_logits, w_out_ab=m_w_out_ab, w_qkv=m_w_qkv,
                q_norm=m_q_norm, k_norm=m_k_norm, w_out_c=m_w_out_c, mlp_w1=m_mlp_w1, mlp_w2=m_mlp_w2)
    v_in = dict(ada_w=v_ada_w, ada_b=v_ada_b, norm_mix=v_norm_mix, norm_mlp=v_norm_mlp, w_in_ab=v_w_in_ab,
                conv_w=v_conv_w, hg_norm=v_hg_norm, lb_logits=v_lb_logits, w_out_ab=v_w_out_ab, w_qkv=v_w_qkv,
                q_norm=v_q_norm, k_norm=v_k_norm, w_out_c=v_w_out_c, mlp_w1=v_mlp_w1, mlp_w2=v_mlp_w2)
    order = list(weights)
    large = ("ada_w", "w_in_ab", "w_out_ab", "w_qkv", "w_out_c", "mlp_w1", "mlp_w2")
    delta, new_m, new_v = {}, {}, {}
    for k in large:
        shape = weights[k].shape
        flat2 = lambda a: a.reshape(-1, shape[-1])
        dl, nm, nv = adamw(flat2(weights[k]), flat2(grads[k]), flat2(m_in[k]), flat2(v_in[k]), f"adamw_{k}")
        delta[k], new_m[k], new_v[k] = dl.reshape(shape), nm.reshape(shape), nv.reshape(shape)
    rest = [k for k in order if k not in large]
    n_rest = -(-sum(weights[k].size for k in rest) // 1024) * 1024
    pack = lambda tree: _pad_row(jnp.concatenate([tree[k].reshape(-1) for k in rest]), n_rest).reshape(8, n_rest // 8)
    dl, nm, nv = adamw(pack(weights), pack(grads), pack(m_in), pack(v_in), "adamw_small")
    off = 0
    for k in rest:
        size, shape = weights[k].size, weights[k].shape
        delta[k], new_m[k], new_v[k] = (a.reshape(-1)[off:off + size].reshape(shape) for a in (dl, nm, nv))
        off += size
    grads = {k: grads[k].reshape(weights[k].shape) for k in order}
    return (loss, grad_x[None], *[grads[k] for k in order], *[delta[k] for k in order],
            *[new_m[k] for k in order], *[new_v[k] for k in order])
```

```python
import functools

import jax
import jax.numpy as jnp
from jax import lax
from jax.experimental import pallas as pl
from jax.experimental.pallas import tpu as pltpu

F32 = jnp.float32
BF16 = jnp.bfloat16
HIGHEST = lax.Precision.HIGHEST
MESH = pl.DeviceIdType.MESH

D_MODEL = 1024
D_FF = 4096
CHUNK = 64
HEAD_TILE = 128
SB_HEAD_DIM = 64
CONV_DIM = 512
HG_DIM = 512
AB_IN = 3584
N_CHIPS = 4
N_DEV = 8
EPS = 1e-6
ATT_BLOCK = 128
ATT_TILE = 512
HG_SLAB = 256

ADAM_LR = 0.001
ADAM_B1 = 0.9
ADAM_B2 = 0.999
ADAM_EPS = 1e-08
ADAM_WD = 0.01
ADAM_STEP = 10

NN = (((1,), (0,)), ((), ()))
NT = (((1,), (1,)), ((), ()))
TN = (((0,), (0,)), ((), ()))


def _cp(*dims):
    return pltpu.CompilerParams(dimension_semantics=dims) if dims else pltpu.CompilerParams()


def _dot(a, b, dn, precision=None):
    return lax.dot_general(a, b, dn, preferred_element_type=F32, precision=precision)


def _sigmoid(z):
    return 1.0 / (1.0 + jnp.exp(-z))


def _matmul(a, b, *, dn, grid, a_spec, b_spec, acc_shape, epilogue, extras=(), extra_specs=(),
            out_shapes, out_specs, name):
    nk = grid[2]
    n_extra = len(extras)
    n_out = len(out_shapes)

    def body(*refs):
        a_ref, b_ref = refs[0], refs[1]
        extra_refs = refs[2:2 + n_extra]
        out_refs = refs[2 + n_extra:2 + n_extra + n_out]
        acc_ref = refs[-1]
        k = pl.program_id(2)
        part = _dot(a_ref[...], b_ref[...], dn)

        if nk == 1:
            epilogue(part, None, extra_refs, out_refs)
        else:
            @pl.when(k == 0)
            def _():
                acc_ref[...] = part

            @pl.when(k > 0)
            def _():
                acc_ref[...] += part

            @pl.when(k == nk - 1)
            def _():
                epilogue(acc_ref[...], None, extra_refs, out_refs)

    return pl.pallas_call(
        body, grid=grid, in_specs=[a_spec, b_spec, *extra_specs], out_specs=out_specs, out_shape=out_shapes,
        scratch_shapes=[pltpu.VMEM(acc_shape, F32)], name=name,
        compiler_params=_cp("parallel", "parallel", "arbitrary"))(a, b, *extras)


def _cols(ref, cols):
    return ref.at[:, cols] if cols is not None else ref


def _store(dtype):
    def epilogue(acc, cols, extra_refs, out_refs):
        _cols(out_refs[0], cols)[...] = acc.astype(dtype)
    return epilogue


def _tok_tile(s):
    return min(512, s)


def _matmul_resident(a, w, *, dn, blocks, accumulate, epilogue, extras=(), extra_specs=(), out_shapes, out_specs,
                     name):
    s, ka = a.shape
    tm = _tok_tile(s)
    n_extra = len(extras)

    def body(*refs):
        a_ref, w_ref = refs[0], refs[1]
        extra_refs = refs[2:2 + n_extra]
        out_refs = refs[2 + n_extra:]
        acc = None
        for w_index, a_cols, out_cols in blocks:
            part = _dot(_cols(a_ref, a_cols)[...], w_ref[w_index], dn)
            if accumulate:
                acc = part if acc is None else acc + part
            else:
                epilogue(part, out_cols, extra_refs, out_refs)
        if accumulate:
            epilogue(acc, None, extra_refs, out_refs)

    return pl.pallas_call(
        body, grid=(s // tm,),
        in_specs=[pl.BlockSpec((tm, ka), lambda i: (i, 0)), pl.BlockSpec(w.shape, lambda i: (0,) * w.ndim),
                  *extra_specs],
        out_specs=out_specs, out_shape=out_shapes, name=name, compiler_params=_cp("parallel"))(a, w, *extras)


def _col_blocks(n_blocks, width):
    return [slice(j * width, (j + 1) * width) for j in range(n_blocks)]


def matmul_nn_chunked(a, w, out_dtype, name, epilogue=None, out_shapes=None):
    s = a.shape[0]
    n4 = w.shape[2]
    tm = _tok_tile(s)
    if out_shapes is None:
        out_shapes = [jax.ShapeDtypeStruct((s, N_CHIPS * n4), out_dtype)]
        epilogue = _store(out_dtype)
    return _matmul_resident(
        a, w, dn=NN, blocks=[(j, None, cols) for j, cols in enumerate(_col_blocks(N_CHIPS, n4))], accumulate=False,
        epilogue=epilogue, out_shapes=out_shapes,
        out_specs=[pl.BlockSpec((tm, N_CHIPS * n4), lambda i: (i, 0))] * len(out_shapes), name=name)


def matmul_nt_chunked(dy, w, name):
    s = dy.shape[0]
    _, k, n4 = w.shape
    tm = _tok_tile(s)
    return _matmul_resident(
        dy, w, dn=NT, blocks=[(j, cols, None) for j, cols in enumerate(_col_blocks(N_CHIPS, n4))], accumulate=True,
        epilogue=_store(F32), out_shapes=[jax.ShapeDtypeStruct((s, k), F32)],
        out_specs=[pl.BlockSpec((tm, k), lambda i: (i, 0))], name=name)[0]


def matmul_tn_chunked(x, dy, name):
    s, k = x.shape
    n = dy.shape[1]
    n4 = n // N_CHIPS
    ts = _tok_tile(s)

    def body(x_ref, dy_ref, out_ref):
        xt = x_ref[...].T
        first = pl.program_id(0) == 0
        for j, cols in enumerate(_col_blocks(N_CHIPS, n4)):
            part = _dot(xt, dy_ref[:, cols], NN)

            @pl.when(first)
            def _():
                out_ref[j] = part

            @pl.when(jnp.logical_not(first))
            def _():
                out_ref[j] += part

    return pl.pallas_call(
        body, grid=(s // ts,),
        in_specs=[pl.BlockSpec((ts, k), lambda i: (i, 0)), pl.BlockSpec((ts, n), lambda i: (i, 0))],
        out_specs=pl.BlockSpec((N_CHIPS, k, n4), lambda i: (0, 0, 0)),
        out_shape=jax.ShapeDtypeStruct((N_CHIPS, k, n4), F32), name=name, compiler_params=_cp("arbitrary"))(x, dy)


def matmul_nn_plain(a, w, name, epilogue, extras, extra_specs, out_shapes, out_specs):
    return _matmul_resident(
        a, w, dn=NN, blocks=[((slice(None), slice(None)), None, None)], accumulate=True, epilogue=epilogue,
        extras=extras, extra_specs=extra_specs, out_shapes=out_shapes, out_specs=out_specs, name=name)


def matmul_nt_plain(dy, w, name, epilogue=None, extras=(), extra_specs=(), out_dtype=F32, tn=1024):
    s = dy.shape[0]
    k = w.shape[0]
    tm = _tok_tile(s)
    return _matmul_resident(
        dy, w, dn=NT, blocks=[((cols, slice(None)), None, cols) for cols in _col_blocks(k // tn, tn)],
        accumulate=False, epilogue=epilogue or _store(out_dtype), extras=extras, extra_specs=extra_specs,
        out_shapes=[jax.ShapeDtypeStruct((s, k), out_dtype)], out_specs=[pl.BlockSpec((tm, k), lambda i: (i, 0))],
        name=name)[0]


def matmul_tn_plain(x, dy, name, tk=1024):
    s, k = x.shape
    n = dy.shape[1]
    ts = _tok_tile(s)
    return _matmul(
        x, dy, dn=TN, grid=(k // tk, 1, s // ts),
        a_spec=pl.BlockSpec((ts, tk), lambda i, j, kk: (kk, i)),
        b_spec=pl.BlockSpec((ts, n), lambda i, j, kk: (kk, 0)),
        acc_shape=(tk, n), epilogue=_store(F32),
        out_shapes=[jax.ShapeDtypeStruct((k, n), F32)], out_specs=[pl.BlockSpec((tk, n), lambda i, j, kk: (i, 0))],
        name=name)[0]


def _row_spec(n):
    return pl.BlockSpec((1, n), lambda i: (0, 0))


def norm_mod_fwd(x, g, scale, shift, name):
    s, d = x.shape
    tm = _tok_tile(s)

    def body(x_ref, g_ref, sc_ref, sh_ref, h_ref):
        xv = x_ref[...]
        r = lax.rsqrt(jnp.mean(xv * xv, axis=-1, keepdims=True) + EPS)
        h_ref[...] = ((xv * r * g_ref[...]) * (1.0 + sc_ref[...]) + sh_ref[...]).astype(BF16)

    tile = pl.BlockSpec((tm, d), lambda i: (i, 0))
    return pl.pallas_call(
        body, grid=(s // tm,), in_specs=[tile, _row_spec(d), _row_spec(d), _row_spec(d)], out_specs=tile,
        out_shape=jax.ShapeDtypeStruct((s, d), BF16), name=name, compiler_params=_cp("parallel"))(x, g, scale, shift)


def norm_mod_bwd(dh, x, g, scale, dres, name):
    s, d = x.shape
    tm = _tok_tile(s)

    def body(dh_ref, x_ref, g_ref, sc_ref, dres_ref, dx_ref, dg_ref, dsc_ref, dsh_ref):
        @pl.when(pl.program_id(0) == 0)
        def _():
            dg_ref[...] = jnp.zeros_like(dg_ref)
            dsc_ref[...] = jnp.zeros_like(dsc_ref)
            dsh_ref[...] = jnp.zeros_like(dsh_ref)

        xv = x_ref[...]
        dhv = dh_ref[...]
        r = lax.rsqrt(jnp.mean(xv * xv, axis=-1, keepdims=True) + EPS)
        xn = xv * r
        gv = g_ref[...]
        s1 = 1.0 + sc_ref[...]
        dsh_ref[...] += jnp.sum(dhv, axis=0, keepdims=True)
        dsc_ref[...] += jnp.sum(dhv * xn * gv, axis=0, keepdims=True)
        dg_ref[...] += jnp.sum(dhv * xn * s1, axis=0, keepdims=True)
        dxn = dhv * gv * s1
        dx_ref[...] = dres_ref[...] + r * (dxn - xn * jnp.mean(dxn * xn, axis=-1, keepdims=True))

    tile = pl.BlockSpec((tm, d), lambda i: (i, 0))
    row = jax.ShapeDtypeStruct((1, d), F32)
    return pl.pallas_call(
        body, grid=(s // tm,), in_specs=[tile, tile, _row_spec(d), _row_spec(d), tile],
        out_specs=[tile, _row_spec(d), _row_spec(d), _row_spec(d)],
        out_shape=[jax.ShapeDtypeStruct((s, d), F32), row, row, row], name=name,
        compiler_params=_cp("arbitrary"))(dh, x, g, scale, dres)


def gate_bwd(dx, y, gate, name):
    s, d = dx.shape
    tm = _tok_tile(s)

    def body(dx_ref, y_ref, gate_ref, dy_ref, dgate_ref):
        @pl.when(pl.program_id(0) == 0)
        def _():
            dgate_ref[...] = jnp.zeros_like(dgate_ref)

        dxv = dx_ref[...]
        dy_ref[...] = (gate_ref[...] * dxv).astype(BF16)
        dgate_ref[...] += jnp.sum(dxv * y_ref[...], axis=0, keepdims=True)

    tile = pl.BlockSpec((tm, d), lambda i: (i, 0))
    return pl.pallas_call(
        body, grid=(s // tm,), in_specs=[tile, tile, _row_spec(d)], out_specs=[tile, _row_spec(d)],
        out_shape=[jax.ShapeDtypeStruct((s, d), BF16), jax.ShapeDtypeStruct((1, d), F32)], name=name,
        compiler_params=_cp("arbitrary"))(dx, y, gate)


def loss_and_grad(y, target):
    s, d = y.shape
    tm = _tok_tile(s)

    def body(y_ref, t_ref, dy_ref, loss_ref):
        @pl.when(pl.program_id(0) == 0)
        def _():
            loss_ref[...] = jnp.zeros_like(loss_ref)

        err = y_ref[...] - t_ref[...]
        dy_ref[...] = err * (1.0 / d)
        loss_ref[...] += jnp.sum(err * err) * (0.5 / d)

    tile = pl.BlockSpec((tm, d), lambda i: (i, 0))
    return pl.pallas_call(
        body, grid=(s // tm,), in_specs=[tile, tile], out_specs=[tile, _row_spec(128)],
        out_shape=[jax.ShapeDtypeStruct((s, d), F32), jax.ShapeDtypeStruct((1, 128), F32)], name="loss_and_grad",
        compiler_params=_cp("arbitrary"))(y, target)


def _proj_residual(acc, cols, extra_refs, out_refs):
    x_ref, gate_ref = extra_refs
    out_refs[0][...] = acc
    out_refs[1][...] = x_ref[...] + gate_ref[...] * acc


def proj_residual(a, w, x, gate, name):
    s, d = x.shape
    tm = _tok_tile(s)
    tile = pl.BlockSpec((tm, d), lambda i: (i, 0))
    shape = jax.ShapeDtypeStruct((s, d), F32)
    return matmul_nn_plain(
        a, w, name, _proj_residual, extras=(x, gate), extra_specs=(tile, _row_spec(d)),
        out_shapes=[shape, shape], out_specs=[tile, tile])


def _mlp_up(acc, cols, extra_refs, out_refs):
    r = jnp.maximum(acc, 0.0)
    _cols(out_refs[0], cols)[...] = (r * r).astype(BF16)
    _cols(out_refs[1], cols)[...] = r.astype(BF16)


def mlp_up(h, w1, name):
    shape = jax.ShapeDtypeStruct((h.shape[0], N_CHIPS * w1.shape[2]), BF16)
    return matmul_nn_chunked(h, w1, BF16, name, epilogue=_mlp_up, out_shapes=[shape, shape])


def _dact(acc, cols, extra_refs, out_refs):
    _cols(out_refs[0], cols)[...] = (acc * (2.0 * _cols(extra_refs[0], cols)[...].astype(F32))).astype(BF16)


def mlp_down_bwd(dy, w2, r, name):
    s = dy.shape[0]
    tm = _tok_tile(s)
    return matmul_nt_plain(dy, w2, name, epilogue=_dact, extras=(r,),
                           extra_specs=(pl.BlockSpec((tm, r.shape[1]), lambda i: (i, 0)),), out_dtype=BF16)


def _shift_down(p, n, row):
    return jnp.where(row >= n, pltpu.roll(p, n, 0), 0.0)


def _shift_up(p, n, row):
    rows = p.shape[0]
    return jnp.where(row < rows - n, pltpu.roll(p, rows - n, 0), 0.0)


def _u_col(block):
    return lambda i: (0, block + i)


def conv_mixer_fwd(u, conv_w):
    s = u.shape[0]
    nb = CONV_DIM // HEAD_TILE

    def body(ab_ref, ac_ref, ah_ref, w_ref, y_ref):
        row = lax.broadcasted_iota(jnp.int32, (s, HEAD_TILE), 0)
        p = ac_ref[...] * ah_ref[...]
        w = w_ref[...]
        conv = w[0:1] * _shift_down(p, 2, row) + w[1:2] * _shift_down(p, 1, row) + w[2:3] * p
        y_ref[...] = (ab_ref[...] * conv).astype(BF16)

    col = lambda b: pl.BlockSpec((s, HEAD_TILE), _u_col(b * nb))
    return pl.pallas_call(
        body, grid=(nb,), in_specs=[col(0), col(1), col(2), pl.BlockSpec((3, HEAD_TILE), lambda i: (0, i))],
        out_specs=pl.BlockSpec((s, HEAD_TILE), lambda i: (0, i)),
        out_shape=jax.ShapeDtypeStruct((s, CONV_DIM), BF16), name="conv_mixer_fwd",
        compiler_params=_cp("parallel"))(u, u, u, conv_w)


def conv_mixer_bwd(dmix, u, conv_w):
    s = u.shape[0]
    nb = CONV_DIM // HEAD_TILE

    def body(dy_ref, ab_ref, ac_ref, ah_ref, w_ref, dab_ref, dac_ref, dah_ref, dw_ref):
        row = lax.broadcasted_iota(jnp.int32, (s, HEAD_TILE), 0)
        ac = ac_ref[...]
        ah = ah_ref[...]
        p = ac * ah
        w = w_ref[...]
        p1 = _shift_down(p, 1, row)
        p2 = _shift_down(p, 2, row)
        conv = w[0:1] * p2 + w[1:2] * p1 + w[2:3] * p
        dy = dy_ref[...]
        dab_ref[...] = (dy * conv).astype(BF16)
        dconv = dy * ab_ref[...]
        dp = w[0:1] * _shift_up(dconv, 2, row) + w[1:2] * _shift_up(dconv, 1, row) + w[2:3] * dconv
        dac_ref[...] = (dp * ah).astype(BF16)
        dah_ref[...] = (dp * ac).astype(BF16)
        dw_ref[...] = jnp.concatenate(
            [jnp.sum(dconv * p2, axis=0, keepdims=True), jnp.sum(dconv * p1, axis=0, keepdims=True),
             jnp.sum(dconv * p, axis=0, keepdims=True)], axis=0)

    col = lambda b: pl.BlockSpec((s, HEAD_TILE), _u_col(b * nb))
    out = pl.BlockSpec((s, HEAD_TILE), lambda i: (0, i))
    wspec = pl.BlockSpec((3, HEAD_TILE), lambda i: (0, i))
    shape = jax.ShapeDtypeStruct((s, CONV_DIM), BF16)
    return pl.pallas_call(
        body, grid=(nb,), in_specs=[out, col(0), col(1), col(2), wspec], out_specs=[out, out, out, wspec],
        out_shape=[shape, shape, shape, jax.ShapeDtypeStruct((3, CONV_DIM), F32)], name="conv_mixer_bwd",
        compiler_params=_cp("parallel"))(dmix, u, u, u, conv_w)


def _chunk_cumsum(g, pos):
    for sh in (1, 2, 4, 8, 16, 32):
        g = g + jnp.where(pos >= sh, pltpu.roll(g, sh, 0), 0.0)
    return g


def _chunk_rev_cumsum(g, pos):
    rows = g.shape[0]
    for sh in (1, 2, 4, 8, 16, 32):
        g = g + jnp.where(pos < CHUNK - sh, pltpu.roll(g, rows - sh, 0), 0.0)
    return g


def _lower_bound(lb_ref):
    logits = lb_ref[...]
    e = jnp.exp(logits - jnp.max(logits, axis=0, keepdims=True))
    p = e / jnp.sum(e, axis=0, keepdims=True)
    return p[0:1], p


def _hg_gates(hf, lb):
    sg = _sigmoid(hf)
    f = lb + (1.0 - lb) * sg
    return sg, f, jnp.log(f), 1.0 - f


def _hg_specs(s, slab, order):
    n = s // slab
    col = lambda b: pl.BlockSpec((slab, HEAD_TILE), lambda h, i: (order(i, n), b + h))
    return n, col


def _to_chunks(ref3, val):
    for c in range(ref3.shape[0]):
        ref3[c] = val[c * CHUNK:(c + 1) * CHUNK]


def _from_chunks(ref3):
    return jnp.concatenate([ref3[c] for c in range(ref3.shape[0])], axis=0)


def _rolled(x, r):
    return pltpu.roll(x, r, 1) if r else x


def _decay_window(ba, bs, r, row):
    lam = jnp.exp(ba - _rolled(bs, r))
    return jnp.where(row >= r, lam, 0.0) if r else lam


def hgrn_fwd(u, lb_logits, hg_norm, riding=()):
    s = u.shape[0]
    slab = min(HG_SLAB, s)
    cps = slab // CHUNK
    n, col = _hg_specs(s, slab, lambda i, n_: i)
    heads = HG_DIM // HEAD_TILE
    nr = len(riding)

    def body(*refs):
        q_ref, f_ref, i_ref, g_ref, lb_ref, nw_ref = refs[:6]
        y_ref, o_ref, st_ref = refs[6 + nr:9 + nr]
        rode = refs[9 + nr:9 + 2 * nr]
        at_ref, q3, b3, k3, v3, o3 = refs[9 + 2 * nr:15 + 2 * nr]
        sems = refs[15 + 2 * nr:]
        if nr:
            @pl.when((pl.program_id(0) == 0) & (pl.program_id(1) == 0))
            def _():
                gather_start(rode, *sems[:2])

        @pl.when(pl.program_id(1) == 0)
        def _():
            at_ref[...] = jnp.zeros_like(at_ref)

        pos = lax.broadcasted_iota(jnp.int32, (slab, HEAD_TILE), 0) & (CHUNK - 1)
        lb, _ = _lower_bound(lb_ref)
        q = q_ref[...]
        v = i_ref[...]
        _, _, g, kk = _hg_gates(f_ref[...], lb)
        b = _chunk_cumsum(g, pos)
        for ref, val in ((q3, q), (b3, b), (k3, kk), (v3, v)):
            _to_chunks(ref, val)

        for a in range(CHUNK // 8):
            wn = CHUNK - 8 * a
            qa, ba = q3[:, 8 * a:, :], b3[:, 8 * a:, :]
            bs, ks, vs = b3[:, :wn, :], k3[:, :wn, :], v3[:, :wn, :]
            row = lax.broadcasted_iota(jnp.int32, (cps, wn, HEAD_TILE), 1)
            acc = jnp.zeros((cps, wn, HEAD_TILE), F32)
            for r in range(8):
                lam = _decay_window(ba, bs, r, row)
                sc = jnp.sum(qa * _rolled(ks, r) * lam, axis=-1, keepdims=True)
                acc = acc + sc * _rolled(vs, r)
            if a == 0:
                o3[...] = acc
            else:
                o3[:, 8 * a:, :] += acc

        qhat = q * jnp.exp(b)
        for c in range(cps):
            rows = slice(c * CHUNK, (c + 1) * CHUNK)
            at = at_ref[...]
            st_ref[c] = at
            o_ref[rows, :] = o3[c] + _dot(qhat[rows], at, NT, HIGHEST)
            bc = b[(c + 1) * CHUNK - 1:(c + 1) * CHUNK]
            khat = kk[rows] * jnp.exp(bc - b[rows])
            at_ref[...] = at * jnp.exp(bc) + _dot(v[rows], khat, TN, HIGHEST)

        o = o_ref[...]
        r = lax.rsqrt(jnp.mean(o * o, axis=-1, keepdims=True) + EPS)
        hg = g_ref[...]
        y_ref[...] = (o * r * nw_ref[...] * (hg * _sigmoid(hg))).astype(BF16)

        if nr:
            @pl.when((pl.program_id(0) == heads - 1) & (pl.program_id(1) == n - 1))
            def _():
                gather_finish(rode, *sems)

    out = pl.BlockSpec((slab, HEAD_TILE), lambda h, i: (i, h))
    par = lambda rows: pl.BlockSpec((rows, HEAD_TILE), lambda h, i: (0, h))
    return pl.pallas_call(
        body, grid=(heads, n), in_specs=[col(12), col(16), col(20), col(24), par(3), par(1)] + [HBM] * nr,
        out_specs=[out, out, pl.BlockSpec((None, cps, HEAD_TILE, HEAD_TILE), lambda h, i: (h, i, 0, 0))] + [HBM] * nr,
        out_shape=[jax.ShapeDtypeStruct((s, HG_DIM), BF16), jax.ShapeDtypeStruct((s, HG_DIM), F32),
                   jax.ShapeDtypeStruct((heads, s // CHUNK, HEAD_TILE, HEAD_TILE), F32)]
        + [jax.ShapeDtypeStruct(w.shape, w.dtype) for w in riding],
        input_output_aliases={6 + a: 3 + a for a in range(nr)},
        scratch_shapes=[pltpu.VMEM((HEAD_TILE, HEAD_TILE), F32)] + [pltpu.VMEM((cps, CHUNK, HEAD_TILE), F32)] * 5
        + (gather_semaphores(nr) if nr else []),
        name="hgrn_fwd",
        compiler_params=_cp("arbitrary", "arbitrary"))(u, u, u, u, lb_logits, hg_norm, *riding)


def hgrn_bwd(dmix, u, o_raw, states, lb_logits, hg_norm, riding=()):
    s = u.shape[0]
    slab = min(HG_SLAB, s)
    cps = slab // CHUNK
    n, col = _hg_specs(s, slab, lambda i, n_: n_ - 1 - i)
    heads = HG_DIM // HEAD_TILE
    nr = len(riding)

    def body(*refs):
        dy_ref, q_ref, f_ref, i_ref, g_ref, o_ref, st_ref, lb_ref, nw_ref = refs[:9]
        parts = refs[9:9 + nr]
        dq_ref, df_ref, di_ref, dg_ref, dnw_ref, dlb_ref = refs[9 + nr:15 + nr]
        landed = refs[15 + nr:15 + 2 * nr]
        dat_ref, dlbacc_ref, dbc_ref, q3, b3, k3, v3, do3, dq3, dk3, dv3 = refs[15 + 2 * nr:26 + 2 * nr]
        sems = refs[26 + 2 * nr:]
        step = pl.program_id(1)
        if nr:
            @pl.when((pl.program_id(0) == 0) & (step == 0))
            def _():
                for cp in scatter_copies(parts, landed, *sems):
                    cp.start()

        @pl.when(step == 0)
        def _():
            dat_ref[...] = jnp.zeros_like(dat_ref)
            dlbacc_ref[...] = jnp.zeros_like(dlbacc_ref)
            dnw_ref[...] = jnp.zeros_like(dnw_ref)

        pos = lax.broadcasted_iota(jnp.int32, (slab, HEAD_TILE), 0) & (CHUNK - 1)
        lb, probs = _lower_bound(lb_ref)
        q = q_ref[...]
        v = i_ref[...]
        sg_f, f, g, kk = _hg_gates(f_ref[...], lb)
        b = _chunk_cumsum(g, pos)

        o = o_ref[...]
        nw = nw_ref[...]
        r = lax.rsqrt(jnp.mean(o * o, axis=-1, keepdims=True) + EPS)
        hg = g_ref[...]
        sg = _sigmoid(hg)
        dy = dy_ref[...]
        d_on = dy * (hg * sg)
        dg_ref[...] = (dy * (o * r * nw) * (sg * (1.0 + hg * (1.0 - sg)))).astype(BF16)
        dnw_ref[...] += jnp.sum(d_on * o * r, axis=0, keepdims=True)
        t1 = d_on * nw
        do = r * t1 - o * (r * r * r) * jnp.mean(t1 * o, axis=-1, keepdims=True)

        eb = jnp.exp(b)
        qhat = q * eb
        dbc_ref[...] = jnp.zeros_like(dbc_ref)
        for c in reversed(range(cps)):
            rows = slice(c * CHUNK, (c + 1) * CHUNK)
            last = (c + 1) * CHUNK - 1
            at = st_ref[c]
            dat = dat_ref[...]
            bc = b[last:last + 1]
            ebc = jnp.exp(bc)
            dec = jnp.exp(bc - b[rows])
            khat = kk[rows] * dec
            at_next = at * ebc + _dot(v[rows], khat, TN, HIGHEST)
            dbc_ref[last:last + 1, :] = jnp.sum(dat * at_next, axis=0, keepdims=True)
            dq3[c] = eb[rows] * _dot(do[rows], at, NN, HIGHEST)
            dk3[c] = dec * _dot(v[rows], dat, NN, HIGHEST)
            dv3[c] = _dot(khat, dat, NT, HIGHEST)
            dat_ref[...] = dat * ebc + _dot(do[rows], qhat[rows], TN, HIGHEST)

        for ref, val in ((q3, q), (b3, b), (k3, kk), (v3, v), (do3, do)):
            _to_chunks(ref, val)
        for a in range(CHUNK // 8):
            wn = CHUNK - 8 * a
            qa, ba, doa = q3[:, 8 * a:, :], b3[:, 8 * a:, :], do3[:, 8 * a:, :]
            bs, ks, vs = b3[:, :wn, :], k3[:, :wn, :], v3[:, :wn, :]
            row = lax.broadcasted_iota(jnp.int32, (cps, wn, HEAD_TILE), 1)
            zero = jnp.zeros((cps, wn, HEAD_TILE), F32)
            dqa, dka, dva = zero, zero, zero
            for r in range(8):
                lam = _decay_window(ba, bs, r, row)
                kd, vd = _rolled(ks, r), _rolled(vs, r)
                sc = jnp.sum(qa * kd * lam, axis=-1, keepdims=True)
                pd = jnp.sum(doa * vd, axis=-1, keepdims=True)
                dqa = dqa + pd * kd * lam
                dka = dka + _rolled(pd * qa * lam, (wn - r) % wn)
                dva = dva + _rolled(sc * doa, (wn - r) % wn)
            dq3[:, 8 * a:, :] += dqa
            dk3[:, :wn, :] += dka
            dv3[:, :wn, :] += dva
        dq, dk, dv = _from_chunks(dq3), _from_chunks(dk3), _from_chunks(dv3)

        db = q * dq - kk * dk + dbc_ref[...]
        dgl = _chunk_rev_cumsum(db, pos)
        dfv = dgl / f - dk
        dq_ref[...] = dq.astype(BF16)
        di_ref[...] = dv.astype(BF16)
        df_ref[...] = (dfv * (1.0 - lb) * sg_f * (1.0 - sg_f)).astype(BF16)
        dlbacc_ref[...] += jnp.sum(dfv * (1.0 - sg_f), axis=0, keepdims=True)

        @pl.when(step == n - 1)
        def _():
            dlb = dlbacc_ref[...]
            sel = (lax.broadcasted_iota(jnp.int32, (3, HEAD_TILE), 0) == 0).astype(F32)
            dlb_ref[...] = dlb * probs[0:1] * (sel - probs)

        if nr:
            @pl.when((pl.program_id(0) == heads - 1) & (step == n - 1))
            def _():
                for cp in scatter_copies(parts, landed, *sems):
                    cp.wait()

    out = pl.BlockSpec((slab, HEAD_TILE), lambda h, i: (n - 1 - i, h))
    par = lambda rows: pl.BlockSpec((rows, HEAD_TILE), lambda h, i: (0, h))
    dyspec = pl.BlockSpec((slab, HEAD_TILE), lambda h, i: (n - 1 - i, CONV_DIM // HEAD_TILE + h))
    shape = jax.ShapeDtypeStruct((s, HG_DIM), BF16)
    slab_f32 = pltpu.VMEM((slab, HEAD_TILE), F32)
    return pl.pallas_call(
        body, grid=(heads, n),
        in_specs=[dyspec, col(12), col(16), col(20), col(24), out,
                  pl.BlockSpec((None, cps, HEAD_TILE, HEAD_TILE), lambda h, i: (h, n - 1 - i, 0, 0)), par(3), par(1)]
        + [HBM] * nr,
        out_specs=[out, out, out, out, par(1), par(3)] + [HBM] * nr,
        out_shape=[shape, shape, shape, shape, jax.ShapeDtypeStruct((1, HG_DIM), F32),
                   jax.ShapeDtypeStruct((3, HG_DIM), F32)] + scatter_shapes(riding),
        scratch_shapes=[pltpu.VMEM((HEAD_TILE, HEAD_TILE), F32), pltpu.VMEM((1, HEAD_TILE), F32), slab_f32]
        + [pltpu.VMEM((cps, CHUNK, HEAD_TILE), F32)] * 8 + (scatter_semaphores(nr) if nr else []),
        name="hgrn_bwd", compiler_params=_cp("arbitrary", "arbitrary"))(
            dmix, u, u, u, u, o_raw, states, lb_logits, hg_norm, *riding)


def _pair_ones():
    row = lax.broadcasted_iota(jnp.int32, (HEAD_TILE, HEAD_TILE), 0) // SB_HEAD_DIM
    col = lax.broadcasted_iota(jnp.int32, (HEAD_TILE, HEAD_TILE), 1) // SB_HEAD_DIM
    ones = (row == col).astype(BF16)
    return jnp.concatenate([ones, ones], axis=0)


def _pair_mean(x, ones):
    return _split_dot(x, ones) * (1.0 / SB_HEAD_DIM)


def _pair_rstd(x, ones):
    return lax.rsqrt(_pair_mean(x * x, ones) + EPS)


def _lane_tiles():
    return _col_blocks(D_MODEL // HEAD_TILE, HEAD_TILE)


def qk_norm_fwd(qkv, qg, kg):
    s = qkv.shape[0]
    tm = min(256, s)

    def body(q_ref, k_ref, v_ref, qg_ref, kg_ref, qn_ref, kn_ref, vb_ref):
        ones = _pair_ones()
        for cols in _lane_tiles():
            qv = q_ref[:, cols]
            kv = k_ref[:, cols]
            qn_ref[:, cols] = (qv * _pair_rstd(qv, ones) * qg_ref[...]).astype(BF16)
            kn_ref[:, cols] = (kv * _pair_rstd(kv, ones) * kg_ref[...]).astype(BF16)
        vb_ref[...] = v_ref[...].astype(BF16)

    col = lambda b: pl.BlockSpec((tm, D_MODEL), lambda i: (i, b))
    out = col(0)
    shape = jax.ShapeDtypeStruct((s, D_MODEL), BF16)
    return pl.pallas_call(
        body, grid=(s // tm,), in_specs=[col(0), col(1), col(2), _row_spec(HEAD_TILE), _row_spec(HEAD_TILE)],
        out_specs=[out, out, out], out_shape=[shape, shape, shape], name="qk_norm_fwd",
        compiler_params=_cp("parallel"))(qkv, qkv, qkv, qg, kg)


def qk_norm_bwd(dqn, dkn, dv, qkv, qg, kg):
    s = qkv.shape[0]
    tm = min(256, s)

    def body(dqn_ref, dkn_ref, dv_ref, q_ref, k_ref, qg_ref, kg_ref, dq_ref, dk_ref, dvb_ref, dqg_ref, dkg_ref):
        @pl.when(pl.program_id(0) == 0)
        def _():
            dqg_ref[...] = jnp.zeros_like(dqg_ref)
            dkg_ref[...] = jnp.zeros_like(dkg_ref)

        ones = _pair_ones()

        def one(x_ref, g_ref, dn_ref, dx_ref, dgain_ref):
            dgain = jnp.zeros((1, HEAD_TILE), F32)
            for cols in _lane_tiles():
                xv = x_ref[:, cols]
                r = _pair_rstd(xv, ones)
                xn = xv * r
                dn = dn_ref[:, cols]
                dgain = dgain + jnp.sum(dn * xn, axis=0, keepdims=True)
                t1 = dn * g_ref[...]
                dx_ref[:, cols] = (r * (t1 - xn * _pair_mean(t1 * xn, ones))).astype(BF16)
            dgain_ref[...] += dgain

        one(q_ref, qg_ref, dqn_ref, dq_ref, dqg_ref)
        one(k_ref, kg_ref, dkn_ref, dk_ref, dkg_ref)
        dvb_ref[...] = dv_ref[...].astype(BF16)

    col = lambda b: pl.BlockSpec((tm, D_MODEL), lambda i: (i, b))
    out = col(0)
    gain = _row_spec(HEAD_TILE)
    shape = jax.ShapeDtypeStruct((s, D_MODEL), BF16)
    grow = jax.ShapeDtypeStruct((1, HEAD_TILE), F32)
    return pl.pallas_call(
        body, grid=(s // tm,), in_specs=[out, out, out, col(0), col(1), gain, gain],
        out_specs=[out, out, out, gain, gain], out_shape=[shape, shape, shape, grow, grow], name="qk_norm_bwd",
        compiler_params=_cp("arbitrary"))(dqn, dkn, dv, qkv, qkv, qg, kg)


def _split_dot(x, u):
    hi = x.astype(BF16)
    lo = (x - hi.astype(F32)).astype(BF16)
    if u.shape[0] == 2 * x.shape[1]:
        return _dot(jnp.concatenate([hi, lo], axis=1), u, NN)
    return _dot(hi, u, NN) + _dot(lo, u, NN)


def _sb_tile(qs, kb, carry, causal, suffix, diag):
    z = _dot(qs, kb, NT)
    lb = jnp.minimum(z, 0.0) - jnp.log(1.0 + jnp.exp(-jnp.abs(z)))
    lom = lb - z
    if diag:
        lom = jnp.where(causal, lom, 0.0)
    ws = []
    for j in reversed(range(ATT_TILE // ATT_BLOCK)):
        cols = slice(j * ATT_BLOCK, (j + 1) * ATT_BLOCK)
        ws.append(jnp.exp(z[:, cols] + _split_dot(lom[:, cols], suffix) + carry))
        carry = carry + jnp.sum(lom[:, cols], axis=-1, keepdims=True)
    w = jnp.concatenate(ws[::-1], axis=1)
    if diag:
        w = jnp.where(causal, w, 0.0)
    return lb, w, carry


def _sb_consts():
    row = lax.broadcasted_iota(jnp.int32, (ATT_BLOCK, ATT_BLOCK), 0)
    col = lax.broadcasted_iota(jnp.int32, (ATT_BLOCK, ATT_BLOCK), 1)
    suffix = (row >= col).astype(BF16)
    suffix = jnp.concatenate([suffix, suffix], axis=0)
    trow = lax.broadcasted_iota(jnp.int32, (ATT_TILE, ATT_TILE), 0)
    tcol = lax.broadcasted_iota(jnp.int32, (ATT_TILE, ATT_TILE), 1)
    causal = tcol < trow
    lo = lax.broadcasted_iota(jnp.int32, (ATT_TILE, HEAD_TILE), 1) < SB_HEAD_DIM
    return suffix, causal, lo


def _rows(i):
    return pl.ds(pl.multiple_of(i * ATT_TILE, ATT_TILE), ATT_TILE)


def _head_query(qb, mask):
    return (jnp.where(mask, qb, 0.0) * (SB_HEAD_DIM ** -0.5)).astype(BF16)


def sb_attn_fwd(qn, kn, vb):
    s = qn.shape[0]
    nq = s // ATT_TILE
    nt = D_MODEL // HEAD_TILE

    def body(q_ref, k_ref, v_ref, o_ref, ob_ref):
        suffix, causal, lo = _sb_consts()

        def qtile(qi, _):
            qb = q_ref[_rows(qi), :].astype(F32)
            qs = [_head_query(qb, lo), _head_query(qb, ~lo)]

            def step(kj, state, diag):
                kb = k_ref[_rows(kj), :]
                vt = v_ref[_rows(kj), :]
                out = []
                for hh in range(2):
                    carry, acc = state[hh]
                    _, w, carry = _sb_tile(qs[hh], kb, carry, causal, suffix, diag)
                    out.append((carry, acc + _dot(w.astype(BF16), vt, NN)))
                return tuple(out)

            start = (jnp.zeros((ATT_TILE, 1), F32), jnp.zeros((ATT_TILE, HEAD_TILE), F32))
            state = step(qi, (start, start), True)
            state = lax.fori_loop(0, qi, lambda jj, st: step(qi - 1 - jj, st, False), state)
            o = jnp.where(lo, state[0][1], state[1][1])
            o_ref[_rows(qi), :] = o
            ob_ref[_rows(qi), :] = o.astype(BF16)
            return 0

        lax.fori_loop(0, nq, qtile, 0)

    spec = pl.BlockSpec((s, HEAD_TILE), lambda p: (0, p))
    return pl.pallas_call(
        body, grid=(nt,), in_specs=[spec, spec, spec], out_specs=[spec, spec],
        out_shape=[jax.ShapeDtypeStruct((s, D_MODEL), F32), jax.ShapeDtypeStruct((s, D_MODEL), BF16)],
        name="sb_attn_fwd", compiler_params=_cp("parallel"))(qn, kn, vb)


def sb_attn_bwd(qn, kn, vb, o, do):
    s = qn.shape[0]
    nq = s // ATT_TILE
    nt = D_MODEL // HEAD_TILE

    def body(q_ref, k_ref, v_ref, o_ref, do_ref, dq_ref, dk_ref, dv_ref):
        suffix, causal, lo = _sb_consts()
        dk_ref[...] = jnp.zeros_like(dk_ref)
        dv_ref[...] = jnp.zeros_like(dv_ref)

        def qtile(qi, _):
            qb = q_ref[_rows(qi), :].astype(F32)
            dob = do_ref[_rows(qi), :].astype(BF16).astype(F32)
            prod = dob * o_ref[_rows(qi), :]
            masks = [lo, ~lo]
            qs = [_head_query(qb, m) for m in masks]
            dos = [jnp.where(m, dob, 0.0).astype(BF16) for m in masks]
            totals = [jnp.sum(jnp.where(m, prod, 0.0), axis=-1, keepdims=True) for m in masks]

            def step(kj, state, diag):
                kb = k_ref[_rows(kj), :]
                vt = v_ref[_rows(kj), :]
                out = []
                dk = jnp.zeros((ATT_TILE, HEAD_TILE), F32)
                dv = jnp.zeros((ATT_TILE, HEAD_TILE), F32)
                for hh in range(2):
                    carry, carry_e, dq = state[hh]
                    lb, w, carry = _sb_tile(qs[hh], kb, carry, causal, suffix, diag)
                    wb = w.astype(BF16)
                    e = _dot(dos[hh], vt, NT) * wb.astype(F32)
                    befores = []
                    for j in reversed(range(ATT_TILE // ATT_BLOCK)):
                        cols = slice(j * ATT_BLOCK, (j + 1) * ATT_BLOCK)
                        befores.append(totals[hh] - carry_e - _split_dot(e[:, cols], suffix))
                        carry_e = carry_e + jnp.sum(e[:, cols], axis=-1, keepdims=True)
                    before = jnp.concatenate(befores[::-1], axis=1)
                    beta = jnp.exp(lb)
                    dz = e * (1.0 - beta) - before * beta
                    if diag:
                        dz = jnp.where(causal, dz, 0.0)
                    dzb = dz.astype(BF16)
                    dq = dq + _dot(dzb, kb, NN)
                    dk = dk + _dot(dzb, qs[hh], TN)
                    dv = dv + _dot(wb, dos[hh], TN)
                    out.append((carry, carry_e, dq))
                dk_ref[_rows(kj), :] += dk
                dv_ref[_rows(kj), :] += dv
                return tuple(out)

            zero = jnp.zeros((ATT_TILE, 1), F32)
            start = (zero, zero, jnp.zeros((ATT_TILE, HEAD_TILE), F32))
            state = step(qi, (start, start), True)
            state = lax.fori_loop(0, qi, lambda jj, st: step(qi - 1 - jj, st, False), state)
            dq_ref[_rows(qi), :] = jnp.where(lo, state[0][2], state[1][2]) * (SB_HEAD_DIM ** -0.5)
            return 0

        lax.fori_loop(0, nq, qtile, 0)

    spec = pl.BlockSpec((s, HEAD_TILE), lambda p: (0, p))
    shape = jax.ShapeDtypeStruct((s, D_MODEL), F32)
    return pl.pallas_call(
        body, grid=(nt,), in_specs=[spec] * 5, out_specs=[spec] * 3, out_shape=[shape] * 3,
        name="sb_attn_bwd", compiler_params=_cp("parallel"))(qn, kn, vb, o, do)


def ada_fwd(c_all, ada_w, ada_b_shard):
    layers, d, cols = ada_w.shape
    tn = 512

    def body(c_ref, w_ref, b_ref, out_ref):
        cv = c_ref[...]
        act = (cv * _sigmoid(cv)).astype(BF16)
        out_ref[...] = _dot(act, w_ref[...].astype(BF16), NN) + b_ref[...]

    return pl.pallas_call(
        body, grid=(layers, cols // tn),
        in_specs=[pl.BlockSpec((N_DEV, d), lambda l, j: (0, 0)), pl.BlockSpec((None, d, tn), lambda l, j: (l, 0, j)),
                  pl.BlockSpec((None, 1, tn), lambda l, j: (l, 0, j))],
        out_specs=pl.BlockSpec((None, N_DEV, tn), lambda l, j: (l, 0, j)),
        out_shape=jax.ShapeDtypeStruct((layers, N_DEV, cols), F32), name="ada_fwd",
        compiler_params=_cp("parallel", "parallel"))(c_all, ada_w, ada_b_shard)


def ada_w_grad(c_all, dmod):
    layers, _, cols = dmod.shape
    d = c_all.shape[1]
    tn = 512

    def body(c_ref, dm_ref, out_ref):
        cv = c_ref[...]
        out_ref[...] = _dot(cv * _sigmoid(cv), dm_ref[...], TN, HIGHEST)

    return pl.pallas_call(
        body, grid=(layers, cols // tn),
        in_specs=[pl.BlockSpec((N_DEV, d), lambda l, j: (0, 0)), pl.BlockSpec((None, N_DEV, tn), lambda l, j: (l, 0, j))],
        out_specs=pl.BlockSpec((None, d, tn), lambda l, j: (l, 0, j)),
        out_shape=jax.ShapeDtypeStruct((layers, d, cols), F32), name="ada_w_grad",
        compiler_params=_cp("parallel", "parallel"))(c_all, dmod)


def _row_tile(r, c, elems):
    best = 8
    for t in range(8, r + 1, 8):
        if r % t == 0 and t * c <= elems:
            best = t
    return best


def sum_rows(x, name):
    n, r, c = x.shape
    tr = _row_tile(r, c, 1 << 17)

    def body(x_ref, out_ref):
        acc = x_ref[0]
        for i in range(1, n):
            acc = acc + x_ref[i]
        out_ref[...] = acc

    return pl.pallas_call(
        body, grid=(r // tr,), in_specs=[pl.BlockSpec((n, tr, c), lambda i: (0, i, 0))],
        out_specs=pl.BlockSpec((tr, c), lambda i: (i, 0)), out_shape=jax.ShapeDtypeStruct((r, c), F32), name=name,
        compiler_params=_cp("parallel"))(x)


def adamw(w, g, m, v, name):
    r, c = w.shape
    tr = _row_tile(r, c, 1 << 17)
    c1 = 1.0 - ADAM_B1 ** ADAM_STEP
    c2 = 1.0 - ADAM_B2 ** ADAM_STEP

    def body(w_ref, g_ref, m_ref, v_ref, d_ref, nm_ref, nv_ref):
        gv = g_ref[...]
        nm = ADAM_B1 * m_ref[...] + (1.0 - ADAM_B1) * gv
        nv = ADAM_B2 * v_ref[...] + (1.0 - ADAM_B2) * (gv * gv)
        d_ref[...] = -ADAM_LR * ((nm / c1) / (jnp.sqrt(nv / c2) + ADAM_EPS) + ADAM_WD * w_ref[...])
        nm_ref[...] = nm
        nv_ref[...] = nv

    spec = pl.BlockSpec((tr, c), lambda i: (i, 0))
    shape = jax.ShapeDtypeStruct((r, c), F32)
    return pl.pallas_call(
        body, grid=(r // tr,), in_specs=[spec] * 4, out_specs=[spec] * 3, out_shape=[shape] * 3, name=name,
        compiler_params=_cp("parallel"))(w, g, m, v)


def _me():
    return lax.axis_index("x"), lax.axis_index("y"), lax.axis_index("c")


def _flip(v, bit):
    return 1 - v if bit else v


HBM = pl.BlockSpec(memory_space=pl.ANY)
VMEM = pl.BlockSpec(memory_space=pltpu.VMEM)


def all_gather_rows(v, name):
    n = v.shape[1]

    def body(v_ref, out_ref, send_sems, recv_sems):
        x, y, c = _me()
        me = 4 * x + 2 * y + c
        out_ref[pl.ds(me, 1), :] = v_ref[...]
        copies = []
        for j in range(1, N_DEV):
            peer = (_flip(x, j & 4), _flip(y, j & 2), _flip(c, j & 1))
            copies.append(pltpu.make_async_remote_copy(
                src_ref=v_ref, dst_ref=out_ref.at[pl.ds(me, 1), :], send_sem=send_sems.at[j - 1],
                recv_sem=recv_sems.at[j - 1], device_id=peer, device_id_type=MESH))
        for cp in copies:
            cp.start()
        for cp in copies:
            cp.wait()

    return pl.pallas_call(
        body, in_specs=[VMEM], out_specs=VMEM, out_shape=jax.ShapeDtypeStruct((N_DEV, n), F32),
        scratch_shapes=[pltpu.SemaphoreType.DMA((N_DEV - 1,)), pltpu.SemaphoreType.DMA((N_DEV - 1,))], name=name)(v)


def _chip_peers(x, y):
    return [((1 - x, y), 2 * (1 - x) + y), ((x, 1 - y), 2 * x + (1 - y)), ((1 - x, 1 - y), 2 * (1 - x) + (1 - y))]


def cast_to_slot(w, name):
    r, c = w.shape
    tr = _row_tile(r, c, 1 << 18)

    def body(w_ref, out_ref):
        out_ref[...] = w_ref[...].astype(BF16)

    return pl.pallas_call(
        body, grid=(r // tr,), in_specs=[pl.BlockSpec((tr, c), lambda i: (i, 0))],
        out_specs=pl.BlockSpec((None, tr, c), lambda i: (2 * lax.axis_index("x") + lax.axis_index("y"), i, 0)),
        out_shape=jax.ShapeDtypeStruct((N_CHIPS, r, c), BF16), name=name, compiler_params=_cp("parallel"))(w)


def chip_all_gather(slots, name):
    n = len(slots)

    def body(*refs):
        gather_start(refs[n:2 * n], *refs[2 * n:2 * n + 2])
        gather_finish(refs[n:2 * n], *refs[2 * n:])

    return pl.pallas_call(
        body, in_specs=[HBM] * n, out_specs=[HBM] * n,
        out_shape=[jax.ShapeDtypeStruct(s.shape, s.dtype) for s in slots],
        input_output_aliases={a: a for a in range(n)}, scratch_shapes=gather_semaphores(n), name=name)(*slots)


def gather_semaphores(n):
    return [pltpu.SemaphoreType.DMA((n, 3))] * 4


def _chip_copies(outs, send_sems, recv_sems):
    x, y, c = _me()
    copies = []
    for a, out in enumerate(outs):
        half = out.shape[1] // 2
        rows = out.at[2 * x + y, pl.ds(c * half, half), :]
        for p, (chip, _) in enumerate(_chip_peers(x, y)):
            copies.append(pltpu.make_async_remote_copy(
                src_ref=rows, dst_ref=rows, send_sem=send_sems.at[a, p], recv_sem=recv_sems.at[a, p],
                device_id=(*chip, c), device_id_type=MESH))
    return copies


def gather_start(outs, send_sems, recv_sems):
    for cp in _chip_copies(outs, send_sems, recv_sems):
        cp.start()


def gather_finish(outs, send_sems, recv_sems, pass_send_sems, pass_recv_sems):
    x, y, c = _me()
    peers = _chip_peers(x, y)
    passed = []
    for a, out in enumerate(outs):
        half = out.shape[1] // 2
        for p, (chip, slot) in enumerate(peers):
            rows = out.at[slot, pl.ds(c * half, half), :]
            pltpu.make_async_remote_copy(
                src_ref=rows, dst_ref=rows, send_sem=send_sems.at[a, p], recv_sem=recv_sems.at[a, p],
                device_id=(*chip, c), device_id_type=MESH).wait_recv()
            cp = pltpu.make_async_remote_copy(
                src_ref=rows, dst_ref=rows, send_sem=pass_send_sems.at[a, p], recv_sem=pass_recv_sems.at[a, p],
                device_id=(x, y, 1 - c), device_id_type=MESH)
            cp.start()
            passed.append(cp)
    for a, out in enumerate(outs):
        half = out.shape[1] // 2
        for p, (_, slot) in enumerate(peers):
            theirs = out.at[slot, pl.ds((1 - c) * half, half), :]
            pltpu.make_async_remote_copy(
                src_ref=theirs, dst_ref=theirs, send_sem=pass_send_sems.at[a, p], recv_sem=pass_recv_sems.at[a, p],
                device_id=(x, y, 1 - c), device_id_type=MESH).wait_recv()
    for cp in _chip_copies(outs, send_sems, recv_sems) + passed:
        cp.wait_send()


def sibling_split(grads, name):
    n = len(grads)

    def body(*refs):
        ins, got = refs[:n], refs[n:2 * n]
        send_sems, recv_sems = refs[2 * n:]
        x, y, c = _me()
        started = []
        for a in range(n):
            half = ins[a].shape[1] // 2
            give = pltpu.make_async_remote_copy(
                src_ref=ins[a].at[:, pl.ds((1 - c) * half, half), :], dst_ref=got[a], send_sem=send_sems.at[a],
                recv_sem=recv_sems.at[a], device_id=(x, y, 1 - c), device_id_type=MESH)
            give.start()
            started.append(give)
        for cp in started:
            cp.wait()

    return pl.pallas_call(
        body, in_specs=[HBM] * n, out_specs=[HBM] * n,
        out_shape=[jax.ShapeDtypeStruct((g.shape[0], g.shape[1] // 2, g.shape[2]), g.dtype) for g in grads],
        scratch_shapes=[pltpu.SemaphoreType.DMA((n,)), pltpu.SemaphoreType.DMA((n,))], name=name)(*grads)


def pair_sum(g, got, name):
    k, half, c = got.shape
    tr = _row_tile(half, c, 1 << 18)
    nb = half // tr

    def body(g_ref, got_ref, out_ref):
        out_ref[...] = (g_ref[...] + got_ref[...]).astype(BF16)

    spec = pl.BlockSpec((None, tr, c), lambda j, i: (j, i, 0))
    return pl.pallas_call(
        body, grid=(k, nb),
        in_specs=[pl.BlockSpec((None, tr, c), lambda j, i: (j, lax.axis_index("c") * nb + i, 0)), spec],
        out_specs=spec, out_shape=jax.ShapeDtypeStruct(got.shape, BF16), name=name,
        compiler_params=_cp("parallel", "parallel"))(g, got)


def chip_scatter(parts, name):
    n = len(parts)

    def body(*refs):
        copies = scatter_copies(refs[:n], refs[n:2 * n], *refs[2 * n:])
        for cp in copies:
            cp.start()
        for cp in copies:
            cp.wait()

    return pl.pallas_call(
        body, in_specs=[HBM] * n, out_specs=[HBM] * n, out_shape=scatter_shapes(parts),
        scratch_shapes=scatter_semaphores(n), name=name)(*parts)


def scatter_shapes(parts):
    return [jax.ShapeDtypeStruct((3, *p.shape[1:]), p.dtype) for p in parts]


def scatter_semaphores(n):
    return [pltpu.SemaphoreType.DMA((n, 3))] * 2


def scatter_copies(ins, outs, send_sems, recv_sems):
    x, y, c = _me()
    return [pltpu.make_async_remote_copy(
        src_ref=ins[a].at[slot], dst_ref=outs[a].at[p], send_sem=send_sems.at[a, p], recv_sem=recv_sems.at[a, p],
        device_id=(*chip, c), device_id_type=MESH)
        for a in range(len(ins)) for p, (chip, slot) in enumerate(_chip_peers(x, y))]


def chip_sum(part, landed, name):
    _, half, c = landed.shape
    tr = _row_tile(half, c, 1 << 17)
    nb = half // tr

    def body(part_ref, landed_ref, out_ref):
        up = lambda v: v.astype(F32)
        out_ref[...] = ((up(part_ref[...]) + up(landed_ref[0])) + up(landed_ref[1])) + up(landed_ref[2])

    return pl.pallas_call(
        body, grid=(nb,),
        in_specs=[pl.BlockSpec((None, tr, c), lambda i: (2 * lax.axis_index("x") + lax.axis_index("y"), i, 0)),
                  pl.BlockSpec((3, tr, c), lambda i: (0, i, 0))],
        out_specs=pl.BlockSpec((tr, c), lambda i: (lax.axis_index("c") * nb + i, 0)),
        out_shape=jax.ShapeDtypeStruct((2 * half, c), F32), name=name, compiler_params=_cp("parallel"))(part, landed)


def sibling_join(arrays, name):
    n = len(arrays)

    def body(*refs):
        ins, outs = refs[:n], refs[n:2 * n]
        send_sems, recv_sems = refs[2 * n:]
        x, y, c = _me()
        started = []
        for a in range(n):
            half = ins[a].shape[0] // 2
            give = pltpu.make_async_remote_copy(
                src_ref=ins[a].at[pl.ds(c * half, half), :], dst_ref=outs[a].at[pl.ds(c * half, half), :],
                send_sem=send_sems.at[a], recv_sem=recv_sems.at[a], device_id=(x, y, 1 - c), device_id_type=MESH)
            give.start()
            started.append(give)
        for cp in started:
            cp.wait()

    return pl.pallas_call(
        body, in_specs=[HBM] * n, out_specs=[HBM] * n,
        out_shape=[jax.ShapeDtypeStruct(h.shape, h.dtype) for h in arrays],
        input_output_aliases={a: a for a in range(n)},
        scratch_shapes=[pltpu.SemaphoreType.DMA((n,)), pltpu.SemaphoreType.DMA((n,))], name=name)(*arrays)


def _as_chunks(g):
    return g if g.ndim == 3 else g.reshape(N_CHIPS, g.shape[0] // N_CHIPS, g.shape[1])


def reduce_parts(grads, tag):
    grads = [_as_chunks(g) for g in grads]
    got = sibling_split(grads, f"grad_sibling_split_{tag}")
    return [pair_sum(g, h, f"grad_pair_sum_{tag}{a}") for a, (g, h) in enumerate(zip(grads, got))]


def reduce_finish(parts, landed):
    halves = [chip_sum(p, l, f"grad_chip_sum_{a}") for a, (p, l) in enumerate(zip(parts, landed))]
    return sibling_join(halves, "grad_sibling_join")


def _pad_row(v, n):
    return jnp.pad(v.reshape(1, -1), ((0, 0), (0, n - v.size)))


FIRST_LAYER_GRADS = ("w_in", "w_out_ab", "w1_0", "w2_0")
SECOND_LAYER_GRADS = ("w_qkv", "w_out_c", "w1_1", "w2_1")


def local_step(x, target, mod, wts, ride_exchange=False):
    d = D_MODEL
    row = lambda v: v.reshape(1, -1)
    mods = [[row(mod[l, i * d:(i + 1) * d]) for i in range(6)] for l in range(2)]
    saved = []
    for l in range(2):
        shift1, scale1, gate1, shift2, scale2, gate2 = mods[l]
        g_mix, g_mlp = row(wts["norm_mix"][l]), row(wts["norm_mlp"][l])
        h = norm_mod_fwd(x, g_mix, scale1, shift1, f"norm_mix_fwd_{l}")
        if l == 0:
            u = matmul_nn_chunked(h, wts["w_in"], F32, "in_proj_ab")[0]
            y_a = conv_mixer_fwd(u, wts["conv_w"])
            y_b, o_raw, states, *rode = hgrn_fwd(u, wts["lb_logits"], wts["hg_norm"], wts.get("riding", ()))
            if rode:
                r_qkv, r_out_c, r_w1, r_w2 = rode
                wts = dict(wts, w_qkv=r_qkv, w_out_c=r_out_c.reshape(d, d), w1=[wts["w1"][0], r_w1],
                           w2=[wts["w2"][0], r_w2.reshape(D_FF, d)])
            mix = jnp.concatenate([y_a, y_b], axis=1)
            y, x1 = proj_residual(mix, wts["w_out_ab"], x, gate1, "out_proj_ab")
            ctx = (u, o_raw, states)
        else:
            qkv = matmul_nn_chunked(h, wts["w_qkv"], F32, "in_proj_c")[0]
            qn, kn, vb = qk_norm_fwd(qkv, wts["qg"], wts["kg"])
            o, mix = sb_attn_fwd(qn, kn, vb)
            y, x1 = proj_residual(mix, wts["w_out_c"], x, gate1, "out_proj_c")
            ctx = (qkv, qn, kn, vb, o)
        h2 = norm_mod_fwd(x1, g_mlp, scale2, shift2, f"norm_mlp_fwd_{l}")
        act, r = mlp_up(h2, wts["w1"][l], f"mlp_up_{l}")
        y2, x2 = proj_residual(act, wts["w2"][l], x1, gate2, f"mlp_down_{l}")
        saved.append((x, h, mix, y, x1, h2, act, r, y2, ctx))
        x = x2

    dx, loss_row = loss_and_grad(x, target)
    small, big = {}, {}
    dmod = [None, None]
    d_norm_mix, d_norm_mlp = [None, None], [None, None]
    for l in (1, 0):
        shift1, scale1, gate1, shift2, scale2, gate2 = mods[l]
        g_mix, g_mlp = row(wts["norm_mix"][l]), row(wts["norm_mlp"][l])
        x0, h, mix, y, x1, h2, act, r, y2, ctx = saved[l]
        dy2, dgate2 = gate_bwd(dx, y2, gate2, f"mlp_gate_bwd_{l}")
        dz = mlp_down_bwd(dy2, wts["w2"][l], r, f"mlp_down_bwd_{l}")
        big[f"w2_{l}"] = matmul_tn_plain(act, dy2, f"mlp_w2_grad_{l}")
        dh2 = matmul_nt_chunked(dz, wts["w1"][l], f"mlp_up_bwd_{l}")
        big[f"w1_{l}"] = matmul_tn_chunked(h2, dz, f"mlp_w1_grad_{l}")
        dx1, d_norm_mlp[l], dscale2, dshift2 = norm_mod_bwd(dh2, x1, g_mlp, scale2, dx, f"norm_mlp_bwd_{l}")
        dy, dgate1 = gate_bwd(dx1, y, gate1, f"mix_gate_bwd_{l}")
        if l == 0:
            u, o_raw, states = ctx
            dmix = matmul_nt_plain(dy, wts["w_out_ab"], "out_proj_ab_bwd")
            big["w_out_ab"] = matmul_tn_plain(mix, dy, "w_out_ab_grad")
            dab, dac, dah, small["conv_w"] = conv_mixer_bwd(dmix, u, wts["conv_w"])
            late = [big[k] for k in SECOND_LAYER_GRADS] if ride_exchange else []
            big["second_parts"] = reduce_parts(late, "second") if late else []
            dhq, dhf, dhi, dhg, small["hg_norm"], small["lb_logits"], *big["second_landed"] = hgrn_bwd(
                dmix, u, o_raw, states, wts["lb_logits"], wts["hg_norm"], big["second_parts"])
            du = jnp.concatenate([dab, dac, dah, dhq, dhf, dhi, dhg], axis=1)
            dh = matmul_nt_chunked(du, wts["w_in"], "in_proj_ab_bwd")
            big["w_in"] = matmul_tn_chunked(h, du, "w_in_grad")
        else:
            qkv, qn, kn, vb, o = ctx
            do = matmul_nt_plain(dy, wts["w_out_c"], "out_proj_c_bwd")
            big["w_out_c"] = matmul_tn_plain(mix, dy, "w_out_c_grad")
            dqn, dkn, dv = sb_attn_bwd(qn, kn, vb, o, do)
            dq, dk, dvb, dqg, dkg = qk_norm_bwd(dqn, dkn, dv, qkv, wts["qg"], wts["kg"])
            small["q_norm"] = dqg[:, :SB_HEAD_DIM] + dqg[:, SB_HEAD_DIM:]
            small["k_norm"] = dkg[:, :SB_HEAD_DIM] + dkg[:, SB_HEAD_DIM:]
            dqkv = jnp.concatenate([dq, dk, dvb], axis=1)
            dh = matmul_nt_chunked(dqkv, wts["w_qkv"], "in_proj_c_bwd")
            big["w_qkv"] = matmul_tn_chunked(h, dqkv, "w_qkv_grad")
        dx, d_norm_mix[l], dscale1, dshift1 = norm_mod_bwd(dh, x0, g_mix, scale1, dx1, f"norm_mix_bwd_{l}")
        dmod[l] = jnp.concatenate([dshift1, dscale1, dgate1, dshift2, dscale2, dgate2], axis=1)
    small["mod"] = jnp.concatenate(dmod, axis=0)
    small["norm_mix"] = jnp.concatenate(d_norm_mix, axis=0)
    small["norm_mlp"] = jnp.concatenate(d_norm_mlp, axis=0)
    return loss_row, dx, small, big


SMALL_ORDER = ("mod", "norm_mix", "norm_mlp", "conv_w", "hg_norm", "lb_logits", "q_norm", "k_norm")


def kernel(x, c, ada_w, ada_b, norm_mix, norm_mlp, w_in_ab, conv_w, hg_norm, lb_logits, w_out_ab, w_qkv, q_norm, k_norm, w_out_c, mlp_w1, mlp_w2, loss_target, m_ada_w, m_ada_b, m_norm_mix, m_norm_mlp, m_w_in_ab, m_conv_w, m_hg_norm, m_lb_logits, m_w_out_ab, m_w_qkv, m_q_norm, m_k_norm, m_w_out_c, m_mlp_w1, m_mlp_w2, v_ada_w, v_ada_b, v_norm_mix, v_norm_mlp, v_w_in_ab, v_conv_w, v_hg_norm, v_lb_logits, v_w_out_ab, v_w_qkv, v_q_norm, v_k_norm, v_w_out_c, v_mlp_w1, v_mlp_w2):
    d = D_MODEL
    ax, ay, ac = _me()
    chip = 2 * ax + ay
    dev = 2 * chip + ac
    cols = ada_w.shape[2]

    first = all_gather_rows(_pad_row(jnp.concatenate([c.reshape(-1), conv_w.reshape(-1)]), 1536), "gather_cond")
    c_all = first[:, :d]
    conv_full = first[::2, d:d + 3 * HEAD_TILE].reshape(N_CHIPS, 3, HEAD_TILE).transpose(1, 0, 2).reshape(3, CONV_DIM)
    ada_b_shard = lax.dynamic_slice(ada_b, (0, chip * cols), (2, cols)).reshape(2, 1, cols)
    mod_cols = ada_fwd(c_all, ada_w, ada_b_shard)
    mod_all = all_gather_rows(mod_cols.reshape(1, -1), "gather_mod").reshape(N_DEV, 2, N_DEV, cols)
    mod = lax.dynamic_index_in_dim(mod_all[::2], dev, axis=2, keepdims=False).transpose(1, 0, 2).reshape(2, 6 * d)

    shards = [w_in_ab[0], w_out_ab[0], mlp_w1[0], mlp_w2[0], w_qkv[0], w_out_c[0], mlp_w1[1], mlp_w2[1]]
    slots = [cast_to_slot(s, f"cast_weight_{a}") for a, s in enumerate(shards)]
    g_in, g_out_ab, g_w1a, g_w2a = chip_all_gather(slots[:4], "gather_weights")
    wts = dict(
        norm_mix=norm_mix, norm_mlp=norm_mlp, w_in=g_in, conv_w=conv_full, hg_norm=hg_norm, lb_logits=lb_logits,
        w_out_ab=g_out_ab.reshape(d, d), qg=jnp.tile(q_norm, (1, 2)), kg=jnp.tile(k_norm, (1, 2)),
        w1=[g_w1a], w2=[g_w2a.reshape(D_FF, d)], riding=slots[4:])

    loss_row, grad_x, small, big = local_step(x[0], loss_target[0], mod, wts, ride_exchange=True)

    flat = jnp.concatenate([small[k].reshape(-1) for k in SMALL_ORDER] + [loss_row[0, :1]])
    n_small = -(-flat.size // 1024) * 1024
    gathered = all_gather_rows(_pad_row(flat, n_small), "gather_small")
    total = sum_rows(gathered.reshape(N_DEV, 8, n_small // 8), "small_sum").reshape(-1)
    sizes = [small[k].size for k in SMALL_ORDER]
    offs = [sum(sizes[:i]) for i in range(len(sizes) + 1)]
    tot = {k: total[offs[i]:offs[i + 1]].reshape(small[k].shape) for i, k in enumerate(SMALL_ORDER)}
    loss = total[offs[-1]]
    mod_rows = gathered[:, :2 * 6 * d].reshape(N_DEV, 2, 6 * d)
    dmod_cols = lax.dynamic_slice(mod_rows, (0, 0, chip * cols), (N_DEV, 2, cols)).transpose(1, 0, 2)
    g_ada_w = ada_w_grad(c_all, dmod_cols)

    first_parts = reduce_parts([big[k] for k in FIRST_LAYER_GRADS], "first")
    first_landed = chip_scatter(first_parts, "grad_chip_scatter")
    r_in, r_out_ab, r_w1a, r_w2a, r_qkv, r_out_c, r_w1b, r_w2b = reduce_finish(
        list(first_parts) + list(big["second_parts"]), list(first_landed) + list(big["second_landed"]))

    grads = dict(
        ada_w=g_ada_w, ada_b=tot["mod"], norm_mix=tot["norm_mix"], norm_mlp=tot["norm_mlp"], w_in_ab=r_in[None],
        conv_w=lax.dynamic_slice(tot["conv_w"], (0, chip * HEAD_TILE), (3, HEAD_TILE))[None], hg_norm=tot["hg_norm"],
        lb_logits=tot["lb_logits"], w_out_ab=r_out_ab[None], w_qkv=r_qkv[None], q_norm=tot["q_norm"],
        k_norm=tot["k_norm"], w_out_c=r_out_c[None], mlp_w1=jnp.stack([r_w1a, r_w1b]), mlp_w2=jnp.stack([r_w2a, r_w2b]))
    weights = dict(ada_w=ada_w, ada_b=ada_b, norm_mix=norm_mix, norm_mlp=norm_mlp, w_in_ab=w_in_ab, conv_w=conv_w,
                   hg_norm=hg_norm, lb_logits=lb_logits, w_out_ab=w_out_ab, w_qkv=w_qkv, q_norm=q_norm, k_norm=k_norm,
                   w_out_c=w_out_c, mlp_w1=mlp_w1, mlp_w2=mlp_w2)
    m_in = dict(ada_w=m_ada_w, ada_b=m_ada_b, norm_mix=m_norm_mix, norm_mlp=m_norm_mlp, w_in_ab=m_w_in_ab,
                conv_w=m_conv_w, hg_norm=m_hg_norm, lb_logits=m_lb_logits, w_out_ab=m_w_out_ab, w_qkv=m_w_qkv,
                q_norm=m_q_norm, k_norm=m_k_norm, w_out_c=m_w_out_c, mlp_w1=m_mlp_w1, mlp_w2=m_mlp_w2)
    v_in = dict(ada_w=v_ada_w, ada_b=v_ada_b, norm_mix=v_norm_mix, norm_mlp=v_norm_mlp, w_in_ab=v_w_in_ab,
                conv_w=v_conv_w, hg_norm=v_hg_norm, lb_logits=v_lb_logits, w_out_ab=v_w_out_ab, w_qkv=v_w_qkv,
                q_norm=v_q_norm, k_norm=v_k_norm, w_out_c=v_w_out_c, mlp_w1=v_mlp_w1, mlp_w2=v_mlp_w2)
    order = list(weights)
    large = ("ada_w", "w_in_ab", "w_out_ab", "w_qkv", "w_out_c", "mlp_w1", "mlp_w2")
    delta, new_m, new_v = {}, {}, {}
    for k in large:
        shape = weights[k].shape
        flat2 = lambda a: a.reshape(-1, shape[-1])
        dl, nm, nv = adamw(flat2(weights[k]), flat2(grads[k]), flat2(m_in[k]), flat2(v_in[k]), f"adamw_{k}")
        delta[k], new_m[k], new_v[k] = dl.reshape(shape), nm.reshape(shape), nv.reshape(shape)
    rest = [k for k in order if k not in large]
    n_rest = -(-sum(weights[k].size for k in rest) // 1024) * 1024
    pack = lambda tree: _pad_row(jnp.concatenate([tree[k].reshape(-1) for k in rest]), n_rest).reshape(8, n_rest // 8)
    dl, nm, nv = adamw(pack(weights), pack(grads), pack(m_in), pack(v_in), "adamw_small")
    off = 0
    for k in rest:
        size, shape = weights[k].size, weights[k].shape
        delta[k], new_m[k], new_v[k] = (a.reshape(-1)[off:off + size].reshape(shape) for a in (dl, nm, nv))
        off += size
    grads = {k: grads[k].reshape(weights[k].shape) for k in order}
    return (loss, grad_x[None], *[grads[k] for k in order], *[delta[k] for k in order],
            *[new_m[k] for k in order], *[new_v[k] for k in order])
```

```python
import functools

import jax
import jax.numpy as jnp
from jax import lax
from jax.experimental import pallas as pl
from jax.experimental.pallas import tpu as pltpu

F32 = jnp.float32
BF16 = jnp.bfloat16
HIGHEST = lax.Precision.HIGHEST
MESH = pl.DeviceIdType.MESH

D_MODEL = 1024
D_FF = 4096
CHUNK = 64
HEAD_TILE = 128
SB_HEAD_DIM = 64
CONV_DIM = 512
HG_DIM = 512
AB_IN = 3584
N_CHIPS = 4
N_DEV = 8
EPS = 1e-6
ATT_BLOCK = 128
ATT_TILE = 512
HG_SLAB = 256

ADAM_LR = 0.001
ADAM_B1 = 0.9
ADAM_B2 = 0.999
ADAM_EPS = 1e-08
ADAM_WD = 0.01
ADAM_STEP = 10

NN = (((1,), (0,)), ((), ()))
NT = (((1,), (1,)), ((), ()))
TN = (((0,), (0,)), ((), ()))


def _cp(*dims):
    return pltpu.CompilerParams(dimension_semantics=dims) if dims else pltpu.CompilerParams()


def _dot(a, b, dn, precision=None):
    return lax.dot_general(a, b, dn, preferred_element_type=F32, precision=precision)


def _sigmoid(z):
    return 1.0 / (1.0 + jnp.exp(-z))


def _matmul(a, b, *, dn, grid, a_spec, b_spec, acc_shape, epilogue, extras=(), extra_specs=(),
            out_shapes, out_specs, name):
    nk = grid[2]
    n_extra = len(extras)
    n_out = len(out_shapes)

    def body(*refs):
        a_ref, b_ref = refs[0], refs[1]
        extra_refs = refs[2:2 + n_extra]
        out_refs = refs[2 + n_extra:2 + n_extra + n_out]
        acc_ref = refs[-1]
        k = pl.program_id(2)
        part = _dot(a_ref[...], b_ref[...], dn)

        if nk == 1:
            epilogue(part, None, extra_refs, out_refs)
        else:
            @pl.when(k == 0)
            def _():
                acc_ref[...] = part

            @pl.when(k > 0)
            def _():
                acc_ref[...] += part

            @pl.when(k == nk - 1)
            def _():
                epilogue(acc_ref[...], None, extra_refs, out_refs)

    return pl.pallas_call(
        body, grid=grid, in_specs=[a_spec, b_spec, *extra_specs], out_specs=out_specs, out_shape=out_shapes,
        scratch_shapes=[pltpu.VMEM(acc_shape, F32)], name=name,
        compiler_params=_cp("parallel", "parallel", "arbitrary"))(a, b, *extras)


def _cols(ref, cols):
    return ref.at[:, cols] if cols is not None else ref


def _store(dtype):
    def epilogue(acc, cols, extra_refs, out_refs):
        _cols(out_refs[0], cols)[...] = acc.astype(dtype)
    return epilogue


def _tok_tile(s):
    return min(512, s)


def _matmul_resident(a, w, *, dn, blocks, accumulate, epilogue, extras=(), extra_specs=(), out_shapes, out_specs,
                     name):
    s, ka = a.shape
    tm = _tok_tile(s)
    n_extra = len(extras)

    def body(*refs):
        a_ref, w_ref = refs[0], refs[1]
        extra_refs = refs[2:2 + n_extra]
        out_refs = refs[2 + n_extra:]
        acc = None
        for w_index, a_cols, out_cols in blocks:
            part = _dot(_cols(a_ref, a_cols)[...], w_ref[w_index], dn)
            if accumulate:
                acc = part if acc is None else acc + part
            else:
                epilogue(part, out_cols, extra_refs, out_refs)
        if accumulate:
            epilogue(acc, None, extra_refs, out_refs)

    return pl.pallas_call(
        body, grid=(s // tm,),
        in_specs=[pl.BlockSpec((tm, ka), lambda i: (i, 0)), pl.BlockSpec(w.shape, lambda i: (0,) * w.ndim),
                  *extra_specs],
        out_specs=out_specs, out_shape=out_shapes, name=name, compiler_params=_cp("parallel"))(a, w, *extras)


def _col_blocks(n_blocks, width):
    return [slice(j * width, (j + 1) * width) for j in range(n_blocks)]


def matmul_nn_chunked(a, w, out_dtype, name, epilogue=None, out_shapes=None):
    s = a.shape[0]
    n4 = w.shape[2]
    tm = _tok_tile(s)
    if out_shapes is None:
        out_shapes = [jax.ShapeDtypeStruct((s, N_CHIPS * n4), out_dtype)]
        epilogue = _store(out_dtype)
    return _matmul_resident(
        a, w, dn=NN, blocks=[(j, None, cols) for j, cols in enumerate(_col_blocks(N_CHIPS, n4))], accumulate=False,
        epilogue=epilogue, out_shapes=out_shapes,
        out_specs=[pl.BlockSpec((tm, N_CHIPS * n4), lambda i: (i, 0))] * len(out_shapes), name=name)


def matmul_nt_chunked(dy, w, name):
    s = dy.shape[0]
    _, k, n4 = w.shape
    tm = _tok_tile(s)
    return _matmul_resident(
        dy, w, dn=NT, blocks=[(j, cols, None) for j, cols in enumerate(_col_blocks(N_CHIPS, n4))], accumulate=True,
        epilogue=_store(F32), out_shapes=[jax.ShapeDtypeStruct((s, k), F32)],
        out_specs=[pl.BlockSpec((tm, k), lambda i: (i, 0))], name=name)[0]


def matmul_tn_chunked(x, dy, name):
    s, k = x.shape
    n = dy.shape[1]
    n4 = n // N_CHIPS
    ts = _tok_tile(s)

    def body(x_ref, dy_ref, out_ref):
        xt = x_ref[...].T
        first = pl.program_id(0) == 0
        for j, cols in enumerate(_col_blocks(N_CHIPS, n4)):
            part = _dot(xt, dy_ref[:, cols], NN)

            @pl.when(first)
            def _():
                out_ref[j] = part

            @pl.when(jnp.logical_not(first))
            def _():
                out_ref[j] += part

    return pl.pallas_call(
        body, grid=(s // ts,),
        in_specs=[pl.BlockSpec((ts, k), lambda i: (i, 0)), pl.BlockSpec((ts, n), lambda i: (i, 0))],
        out_specs=pl.BlockSpec((N_CHIPS, k, n4), lambda i: (0, 0, 0)),
        out_shape=jax.ShapeDtypeStruct((N_CHIPS, k, n4), F32), name=name, compiler_params=_cp("arbitrary"))(x, dy)


def matmul_nn_plain(a, w, name, epilogue, extras, extra_specs, out_shapes, out_specs):
    return _matmul_resident(
        a, w, dn=NN, blocks=[((slice(None), slice(None)), None, None)], accumulate=True, epilogue=epilogue,
        extras=extras, extra_specs=extra_specs, out_shapes=out_shapes, out_specs=out_specs, name=name)


def matmul_nt_plain(dy, w, name, epilogue=None, extras=(), extra_specs=(), out_dtype=F32, tn=1024):
    s = dy.shape[0]
    k = w.shape[0]
    tm = _tok_tile(s)
    return _matmul_resident(
        dy, w, dn=NT, blocks=[((cols, slice(None)), None, cols) for cols in _col_blocks(k // tn, tn)],
        accumulate=False, epilogue=epilogue or _store(out_dtype), extras=extras, extra_specs=extra_specs,
        out_shapes=[jax.ShapeDtypeStruct((s, k), out_dtype)], out_specs=[pl.BlockSpec((tm, k), lambda i: (i, 0))],
        name=name)[0]


def matmul_tn_plain(x, dy, name, tk=1024):
    s, k = x.shape
    n = dy.shape[1]
    ts = _tok_tile(s)
    return _matmul(
        x, dy, dn=TN, grid=(k // tk, 1, s // ts),
        a_spec=pl.BlockSpec((ts, tk), lambda i, j, kk: (kk, i)),
        b_spec=pl.BlockSpec((ts, n), lambda i, j, kk: (kk, 0)),
        acc_shape=(tk, n), epilogue=_store(F32),
        out_shapes=[jax.ShapeDtypeStruct((k, n), F32)], out_specs=[pl.BlockSpec((tk, n), lambda i, j, kk: (i, 0))],
        name=name)[0]


def _row_spec(n):
    return pl.BlockSpec((1, n), lambda i: (0, 0))


def norm_mod_fwd(x, g, scale, shift, name):
    s, d = x.shape
    tm = _tok_tile(s)

    def body(x_ref, g_ref, sc_ref, sh_ref, h_ref):
        xv = x_ref[...]
        r = lax.rsqrt(jnp.mean(xv * xv, axis=-1, keepdims=True) + EPS)
        h_ref[...] = ((xv * r * g_ref[...]) * (1.0 + sc_ref[...]) + sh_ref[...]).astype(BF16)

    tile = pl.BlockSpec((tm, d), lambda i: (i, 0))
    return pl.pallas_call(
        body, grid=(s // tm,), in_specs=[tile, _row_spec(d), _row_spec(d), _row_spec(d)], out_specs=tile,
        out_shape=jax.ShapeDtypeStruct((s, d), BF16), name=name, compiler_params=_cp("parallel"))(x, g, scale, shift)


def norm_mod_bwd(dh, x, g, scale, dres, name):
    s, d = x.shape
    tm = _tok_tile(s)

    def body(dh_ref, x_ref, g_ref, sc_ref, dres_ref, dx_ref, dg_ref, dsc_ref, dsh_ref):
        @pl.when(pl.program_id(0) == 0)
        def _():
            dg_ref[...] = jnp.zeros_like(dg_ref)
            dsc_ref[...] = jnp.zeros_like(dsc_ref)
            dsh_ref[...] = jnp.zeros_like(dsh_ref)

        xv = x_ref[...]
        dhv = dh_ref[...]
        r = lax.rsqrt(jnp.mean(xv * xv, axis=-1, keepdims=True) + EPS)
        xn = xv * r
        gv = g_ref[...]
        s1 = 1.0 + sc_ref[...]
        dsh_ref[...] += jnp.sum(dhv, axis=0, keepdims=True)
        dsc_ref[...] += jnp.sum(dhv * xn * gv, axis=0, keepdims=True)
        dg_ref[...] += jnp.sum(dhv * xn * s1, axis=0, keepdims=True)
        dxn = dhv * gv * s1
        dx_ref[...] = dres_ref[...] + r * (dxn - xn * jnp.mean(dxn * xn, axis=-1, keepdims=True))

    tile = pl.BlockSpec((tm, d), lambda i: (i, 0))
    row = jax.ShapeDtypeStruct((1, d), F32)
    return pl.pallas_call(
        body, grid=(s // tm,), in_specs=[tile, tile, _row_spec(d), _row_spec(d), tile],
        out_specs=[tile, _row_spec(d), _row_spec(d), _row_spec(d)],
        out_shape=[jax.ShapeDtypeStruct((s, d), F32), row, row, row], name=name,
        compiler_params=_cp("arbitrary"))(dh, x, g, scale, dres)


def gate_bwd(dx, y, gate, name):
    s, d = dx.shape
    tm = _tok_tile(s)

    def body(dx_ref, y_ref, gate_ref, dy_ref, dgate_ref):
        @pl.when(pl.program_id(0) == 0)
        def _():
            dgate_ref[...] = jnp.zeros_like(dgate_ref)

        dxv = dx_ref[...]
        dy_ref[...] = (gate_ref[...] * dxv).astype(BF16)
        dgate_ref[...] += jnp.sum(dxv * y_ref[...], axis=0, keepdims=True)

    tile = pl.BlockSpec((tm, d), lambda i: (i, 0))
    return pl.pallas_call(
        body, grid=(s // tm,), in_specs=[tile, tile, _row_spec(d)], out_specs=[tile, _row_spec(d)],
        out_shape=[jax.ShapeDtypeStruct((s, d), BF16), jax.ShapeDtypeStruct((1, d), F32)], name=name,
        compiler_params=_cp("arbitrary"))(dx, y, gate)


def loss_and_grad(y, target):
    s, d = y.shape
    tm = _tok_tile(s)

    def body(y_ref, t_ref, dy_ref, loss_ref):
        @pl.when(pl.program_id(0) == 0)
        def _():
            loss_ref[...] = jnp.zeros_like(loss_ref)

        err = y_ref[...] - t_ref[...]
        dy_ref[...] = err * (1.0 / d)
        loss_ref[...] += jnp.sum(err * err) * (0.5 / d)

    tile = pl.BlockSpec((tm, d), lambda i: (i, 0))
    return pl.pallas_call(
        body, grid=(s // tm,), in_specs=[tile, tile], out_specs=[tile, _row_spec(128)],
        out_shape=[jax.ShapeDtypeStruct((s, d), F32), jax.ShapeDtypeStruct((1, 128), F32)], name="loss_and_grad",
        compiler_params=_cp("arbitrary"))(y, target)


def _proj_residual(acc, cols, extra_refs, out_refs):
    x_ref, gate_ref = extra_refs
    out_refs[0][...] = acc
    out_refs[1][...] = x_ref[...] + gate_ref[...] * acc


def proj_residual(a, w, x, gate, name):
    s, d = x.shape
    tm = _tok_tile(s)
    tile = pl.BlockSpec((tm, d), lambda i: (i, 0))
    shape = jax.ShapeDtypeStruct((s, d), F32)
    return matmul_nn_plain(
        a, w, name, _proj_residual, extras=(x, gate), extra_specs=(tile, _row_spec(d)),
        out_shapes=[shape, shape], out_specs=[tile, tile])


def _mlp_up(acc, cols, extra_refs, out_refs):
    r = jnp.maximum(acc, 0.0)
    _cols(out_refs[0], cols)[...] = (r * r).astype(BF16)
    _cols(out_refs[1], cols)[...] = r.astype(BF16)


def mlp_up(h, w1, name):
    shape = jax.ShapeDtypeStruct((h.shape[0], N_CHIPS * w1.shape[2]), BF16)
    return matmul_nn_chunked(h, w1, BF16, name, epilogue=_mlp_up, out_shapes=[shape, shape])


def _dact(acc, cols, extra_refs, out_refs):
    _cols(out_refs[0], cols)[...] = (acc * (2.0 * _cols(extra_refs[0], cols)[...].astype(F32))).astype(BF16)


def mlp_down_bwd(dy, w2, r, name):
    s = dy.shape[0]
    tm = _tok_tile(s)
    return matmul_nt_plain(dy, w2, name, epilogue=_dact, extras=(r,),
                           extra_specs=(pl.BlockSpec((tm, r.shape[1]), lambda i: (i, 0)),), out_dtype=BF16)


def _shift_down(p, n, row):
    return jnp.where(row >= n, pltpu.roll(p, n, 0), 0.0)


def _shift_up(p, n, row):
    rows = p.shape[0]
    return jnp.where(row < rows - n, pltpu.roll(p, rows - n, 0), 0.0)


def _u_col(block):
    return lambda i: (0, block + i)


def conv_mixer_fwd(u, conv_w):
    s = u.shape[0]
    nb = CONV_DIM // HEAD_TILE

    def body(ab_ref, ac_ref, ah_ref, w_ref, y_ref):
        row = lax.broadcasted_iota(jnp.int32, (s, HEAD_TILE), 0)
        p = ac_ref[...] * ah_ref[...]
        w = w_ref[...]
        conv = w[0:1] * _shift_down(p, 2, row) + w[1:2] * _shift_down(p, 1, row) + w[2:3] * p
        y_ref[...] = (ab_ref[...] * conv).astype(BF16)

    col = lambda b: pl.BlockSpec((s, HEAD_TILE), _u_col(b * nb))
    return pl.pallas_call(
        body, grid=(nb,), in_specs=[col(0), col(1), col(2), pl.BlockSpec((3, HEAD_TILE), lambda i: (0, i))],
        out_specs=pl.BlockSpec((s, HEAD_TILE), lambda i: (0, i)),
        out_shape=jax.ShapeDtypeStruct((s, CONV_DIM), BF16), name="conv_mixer_fwd",
        compiler_params=_cp("parallel"))(u, u, u, conv_w)


def conv_mixer_bwd(dmix, u, conv_w):
    s = u.shape[0]
    nb = CONV_DIM // HEAD_TILE

    def body(dy_ref, ab_ref, ac_ref, ah_ref, w_ref, dab_ref, dac_ref, dah_ref, dw_ref):
        row = lax.broadcasted_iota(jnp.int32, (s, HEAD_TILE), 0)
        ac = ac_ref[...]
        ah = ah_ref[...]
        p = ac * ah
        w = w_ref[...]
        p1 = _shift_down(p, 1, row)
        p2 = _shift_down(p, 2, row)
        conv = w[0:1] * p2 + w[1:2] * p1 + w[2:3] * p
        dy = dy_ref[...]
        dab_ref[...] = (dy * conv).astype(BF16)
        dconv = dy * ab_ref[...]
        dp = w[0:1] * _shift_up(dconv, 2, row) + w[1:2] * _shift_up(dconv, 1, row) + w[2:3] * dconv
        dac_ref[...] = (dp * ah).astype(BF16)
        dah_ref[...] = (dp * ac).astype(BF16)
        dw_ref[...] = jnp.concatenate(
            [jnp.sum(dconv * p2, axis=0, keepdims=True), jnp.sum(dconv * p1, axis=0, keepdims=True),
             jnp.sum(dconv * p, axis=0, keepdims=True)], axis=0)

    col = lambda b: pl.BlockSpec((s, HEAD_TILE), _u_col(b * nb))
    out = pl.BlockSpec((s, HEAD_TILE), lambda i: (0, i))
    wspec = pl.BlockSpec((3, HEAD_TILE), lambda i: (0, i))
    shape = jax.ShapeDtypeStruct((s, CONV_DIM), BF16)
    return pl.pallas_call(
        body, grid=(nb,), in_specs=[out, col(0), col(1), col(2), wspec], out_specs=[out, out, out, wspec],
        out_shape=[shape, shape, shape, jax.ShapeDtypeStruct((3, CONV_DIM), F32)], name="conv_mixer_bwd",
        compiler_params=_cp("parallel"))(dmix, u, u, u, conv_w)


def _chunk_cumsum(g, pos):
    for sh in (1, 2, 4, 8, 16, 32):
        g = g + jnp.where(pos >= sh, pltpu.roll(g, sh, 0), 0.0)
    return g


def _chunk_rev_cumsum(g, pos):
    rows = g.shape[0]
    for sh in (1, 2, 4, 8, 16, 32):
        g = g + jnp.where(pos < CHUNK - sh, pltpu.roll(g, rows - sh, 0), 0.0)
    return g


def _lower_bound(lb_ref):
    logits = lb_ref[...]
    e = jnp.exp(logits - jnp.max(logits, axis=0, keepdims=True))
    p = e / jnp.sum(e, axis=0, keepdims=True)
    return p[0:1], p


def _hg_gates(hf, lb):
    sg = _sigmoid(hf)
    f = lb + (1.0 - lb) * sg
    return sg, f, jnp.log(f), 1.0 - f


def _hg_specs(s, slab, order):
    n = s // slab
    col = lambda b: pl.BlockSpec((slab, HEAD_TILE), lambda h, i: (order(i, n), b + h))
    return n, col


def _to_chunks(ref3, val):
    for c in range(ref3.shape[0]):
        ref3[c] = val[c * CHUNK:(c + 1) * CHUNK]


def _from_chunks(ref3):
    return jnp.concatenate([ref3[c] for c in range(ref3.shape[0])], axis=0)


def _rolled(x, r):
    return pltpu.roll(x, r, 1) if r else x


def _decay_window(ba, bs, r, row):
    lam = jnp.exp(ba - _rolled(bs, r))
    return jnp.where(row >= r, lam, 0.0) if r else lam


def hgrn_fwd(u, lb_logits, hg_norm, riding=()):
    s = u.shape[0]
    slab = min(HG_SLAB, s)
    cps = slab // CHUNK
    n, col = _hg_specs(s, slab, lambda i, n_: i)
    heads = HG_DIM // HEAD_TILE
    nr = len(riding)

    def body(*refs):
        q_ref, f_ref, i_ref, g_ref, lb_ref, nw_ref = refs[:6]
        y_ref, o_ref, st_ref = refs[6 + nr:9 + nr]
        rode = refs[9 + nr:9 + 2 * nr]
        at_ref, q3, b3, k3, v3, o3 = refs[9 + 2 * nr:15 + 2 * nr]
        sems = refs[15 + 2 * nr:]
        if nr:
            @pl.when((pl.program_id(0) == 0) & (pl.program_id(1) == 0))
            def _():
                gather_start(rode, *sems[:2])

        @pl.when(pl.program_id(1) == 0)
        def _():
            at_ref[...] = jnp.zeros_like(at_ref)

        pos = lax.broadcasted_iota(jnp.int32, (slab, HEAD_TILE), 0) & (CHUNK - 1)
        lb, _ = _lower_bound(lb_ref)
        q = q_ref[...]
        v = i_ref[...]
        _, _, g, kk = _hg_gates(f_ref[...], lb)
        b = _chunk_cumsum(g, pos)
        for ref, val in ((q3, q), (b3, b), (k3, kk), (v3, v)):
            _to_chunks(ref, val)

        for a in range(CHUNK // 8):
            wn = CHUNK - 8 * a
            qa, ba = q3[:, 8 * a:, :], b3[:, 8 * a:, :]
            bs, ks, vs = b3[:, :wn, :], k3[:, :wn, :], v3[:, :wn, :]
            row = lax.broadcasted_iota(jnp.int32, (cps, wn, HEAD_TILE), 1)
            acc = jnp.zeros((cps, wn, HEAD_TILE), F32)
            for r in range(8):
                lam = _decay_window(ba, bs, r, row)
                sc = jnp.sum(qa * _rolled(ks, r) * lam, axis=-1, keepdims=True)
                acc = acc + sc * _rolled(vs, r)
            if a == 0:
                o3[...] = acc
            else:
                o3[:, 8 * a:, :] += acc

        qhat = q * jnp.exp(b)
        for c in range(cps):
            rows = slice(c * CHUNK, (c + 1) * CHUNK)
            at = at_ref[...]
            st_ref[c] = at
            o_ref[rows, :] = o3[c] + _dot(qhat[rows], at, NT, HIGHEST)
            bc = b[(c + 1) * CHUNK - 1:(c + 1) * CHUNK]
            khat = kk[rows] * jnp.exp(bc - b[rows])
            at_ref[...] = at * jnp.exp(bc) + _dot(v[rows], khat, TN, HIGHEST)

        o = o_ref[...]
        r = lax.rsqrt(jnp.mean(o * o, axis=-1, keepdims=True) + EPS)
        hg = g_ref[...]
        y_ref[...] = (o * r * nw_ref[...] * (hg * _sigmoid(hg))).astype(BF16)

        if nr:
            @pl.when((pl.program_id(0) == heads - 1) & (pl.program_id(1) == n - 1))
            def _():
                gather_finish(rode, *sems)

    out = pl.BlockSpec((slab, HEAD_TILE), lambda h, i: (i, h))
    par = lambda rows: pl.BlockSpec((rows, HEAD_TILE), lambda h, i: (0, h))
    return pl.pallas_call(
        body, grid=(heads, n), in_specs=[col(12), col(16), col(20), col(24), par(3), par(1)] + [HBM] * nr,
        out_specs=[out, out, pl.BlockSpec((None, cps, HEAD_TILE, HEAD_TILE), lambda h, i: (h, i, 0, 0))] + [HBM] * nr,
        out_shape=[jax.ShapeDtypeStruct((s, HG_DIM), BF16), jax.ShapeDtypeStruct((s, HG_DIM), F32),
                   jax.ShapeDtypeStruct((heads, s // CHUNK, HEAD_TILE, HEAD_TILE), F32)]
        + [jax.ShapeDtypeStruct(w.shape, w.dtype) for w in riding],
        input_output_aliases={6 + a: 3 + a for a in range(nr)},
        scratch_shapes=[pltpu.VMEM((HEAD_TILE, HEAD_TILE), F32)] + [pltpu.VMEM((cps, CHUNK, HEAD_TILE), F32)] * 5
        + (gather_semaphores(nr) if nr else []),
        name="hgrn_fwd",
        compiler_params=_cp("arbitrary", "arbitrary"))(u, u, u, u, lb_logits, hg_norm, *riding)


def hgrn_bwd(dmix, u, o_raw, states, lb_logits, hg_norm, riding=()):
    s = u.shape[0]
    slab = min(HG_SLAB, s)
    cps = slab // CHUNK
    n, col = _hg_specs(s, slab, lambda i, n_: n_ - 1 - i)
    heads = HG_DIM // HEAD_TILE
    nr = len(riding)

    def body(*refs):
        dy_ref, q_ref, f_ref, i_ref, g_ref, o_ref, st_ref, lb_ref, nw_ref = refs[:9]
        parts = refs[9:9 + nr]
        dq_ref, df_ref, di_ref, dg_ref, dnw_ref, dlb_ref = refs[9 + nr:15 + nr]
        landed = refs[15 + nr:15 + 2 * nr]
        dat_ref, dlbacc_ref, dbc_ref, q3, b3, k3, v3, do3, dq3, dk3, dv3 = refs[15 + 2 * nr:26 + 2 * nr]
        sems = refs[26 + 2 * nr:]
        step = pl.program_id(1)
        if nr:
            @pl.when((pl.program_id(0) == 0) & (step == 0))
            def _():
                for cp in scatter_copies(parts, landed, *sems):
                    cp.start()

        @pl.when(step == 0)
        def _():
            dat_ref[...] = jnp.zeros_like(dat_ref)
            dlbacc_ref[...] = jnp.zeros_like(dlbacc_ref)
            dnw_ref[...] = jnp.zeros_like(dnw_ref)

        pos = lax.broadcasted_iota(jnp.int32, (slab, HEAD_TILE), 0) & (CHUNK - 1)
        lb, probs = _lower_bound(lb_ref)
        q = q_ref[...]
        v = i_ref[...]
        sg_f, f, g, kk = _hg_gates(f_ref[...], lb)
        b = _chunk_cumsum(g, pos)

        o = o_ref[...]
        nw = nw_ref[...]
        r = lax.rsqrt(jnp.mean(o * o, axis=-1, keepdims=True) + EPS)
        hg = g_ref[...]
        sg = _sigmoid(hg)
        dy = dy_ref[...]
        d_on = dy * (hg * sg)
        dg_ref[...] = (dy * (o * r * nw) * (sg * (1.0 + hg * (1.0 - sg)))).astype(BF16)
        dnw_ref[...] += jnp.sum(d_on * o * r, axis=0, keepdims=True)
        t1 = d_on * nw
        do = r * t1 - o * (r * r * r) * jnp.mean(t1 * o, axis=-1, keepdims=True)

        eb = jnp.exp(b)
        qhat = q * eb
        dbc_ref[...] = jnp.zeros_like(dbc_ref)
        for c in reversed(range(cps)):
            rows = slice(c * CHUNK, (c + 1) * CHUNK)
            last = (c + 1) * CHUNK - 1
            at = st_ref[c]
            dat = dat_ref[...]
            bc = b[last:last + 1]
            ebc = jnp.exp(bc)
            dec = jnp.exp(bc - b[rows])
            khat = kk[rows] * dec
            at_next = at * ebc + _dot(v[rows], khat, TN, HIGHEST)
            dbc_ref[last:last + 1, :] = jnp.sum(dat * at_next, axis=0, keepdims=True)
            dq3[c] = eb[rows] * _dot(do[rows], at, NN, HIGHEST)
            dk3[c] = dec * _dot(v[rows], dat, NN, HIGHEST)
            dv3[c] = _dot(khat, dat, NT, HIGHEST)
            dat_ref[...] = dat * ebc + _dot(do[rows], qhat[rows], TN, HIGHEST)

        for ref, val in ((q3, q), (b3, b), (k3, kk), (v3, v), (do3, do)):
            _to_chunks(ref, val)
        for a in range(CHUNK // 8):
            wn = CHUNK - 8 * a
            qa, ba, doa = q3[:, 8 * a:, :], b3[:, 8 * a:, :], do3[:, 8 * a:, :]
            bs, ks, vs = b3[:, :wn, :], k3[:, :wn, :], v3[:, :wn, :]
            row = lax.broadcasted_iota(jnp.int32, (cps, wn, HEAD_TILE), 1)
            zero = jnp.zeros((cps, wn, HEAD_TILE), F32)
            dqa, dka, dva = zero, zero, zero
            for r in range(8):
                lam = _decay_window(ba, bs, r, row)
                kd, vd = _rolled(ks, r), _rolled(vs, r)
                sc = jnp.sum(qa * kd * lam, axis=-1, keepdims=True)
                pd = jnp.sum(doa * vd, axis=-1, keepdims=True)
                dqa = dqa + pd * kd * lam
                dka = dka + _rolled(pd * qa * lam, (wn - r) % wn)
                dva = dva + _rolled(sc * doa, (wn - r) % wn)
            dq3[:, 8 * a:, :] += dqa
            dk3[:, :wn, :] += dka
            dv3[:, :wn, :] += dva
        dq, dk, dv = _from_chunks(dq3), _from_chunks(dk3), _from_chunks(dv3)

        db = q * dq - kk * dk + dbc_ref[...]
        dgl = _chunk_rev_cumsum(db, pos)
        dfv = dgl / f - dk
        dq_ref[...] = dq.astype(BF16)
        di_ref[...] = dv.astype(BF16)
        df_ref[...] = (dfv * (1.0 - lb) * sg_f * (1.0 - sg_f)).astype(BF16)
        dlbacc_ref[...] += jnp.sum(dfv * (1.0 - sg_f), axis=0, keepdims=True)

        @pl.when(step == n - 1)
        def _():
            dlb = dlbacc_ref[...]
            sel = (lax.broadcasted_iota(jnp.int32, (3, HEAD_TILE), 0) == 0).astype(F32)
            dlb_ref[...] = dlb * probs[0:1] * (sel - probs)

        if nr:
            @pl.when((pl.program_id(0) == heads - 1) & (step == n - 1))
            def _():
                for cp in scatter_copies(parts, landed, *sems):
                    cp.wait()

    out = pl.BlockSpec((slab, HEAD_TILE), lambda h, i: (n - 1 - i, h))
    par = lambda rows: pl.BlockSpec((rows, HEAD_TILE), lambda h, i: (0, h))
    dyspec = pl.BlockSpec((slab, HEAD_TILE), lambda h, i: (n - 1 - i, CONV_DIM // HEAD_TILE + h))
    shape = jax.ShapeDtypeStruct((s, HG_DIM), BF16)
    slab_f32 = pltpu.VMEM((slab, HEAD_TILE), F32)
    return pl.pallas_call(
        body, grid=(heads, n),
        in_specs=[dyspec, col(12), col(16), col(20), col(24), out,
                  pl.BlockSpec((None, cps, HEAD_TILE, HEAD_TILE), lambda h, i: (h, n - 1 - i, 0, 0)), par(3), par(1)]
        + [HBM] * nr,
        out_specs=[out, out, out, out, par(1), par(3)] + [HBM] * nr,
        out_shape=[shape, shape, shape, shape, jax.ShapeDtypeStruct((1, HG_DIM), F32),
                   jax.ShapeDtypeStruct((3, HG_DIM), F32)] + scatter_shapes(riding),
        scratch_shapes=[pltpu.VMEM((HEAD_TILE, HEAD_TILE), F32), pltpu.VMEM((1, HEAD_TILE), F32), slab_f32]
        + [pltpu.VMEM((cps, CHUNK, HEAD_TILE), F32)] * 8 + (scatter_semaphores(nr) if nr else []),
        name="hgrn_bwd", compiler_params=_cp("arbitrary", "arbitrary"))(
            dmix, u, u, u, u, o_raw, states, lb_logits, hg_norm, *riding)


def _pair_ones():
    row = lax.broadcasted_iota(jnp.int32, (HEAD_TILE, HEAD_TILE), 0) // SB_HEAD_DIM
    col = lax.broadcasted_iota(jnp.int32, (HEAD_TILE, HEAD_TILE), 1) // SB_HEAD_DIM
    ones = (row == col).astype(BF16)
    return jnp.concatenate([ones, ones], axis=0)


def _pair_mean(x, ones):
    return _split_dot(x, ones) * (1.0 / SB_HEAD_DIM)


def _pair_rstd(x, ones):
    return lax.rsqrt(_pair_mean(x * x, ones) + EPS)


def _lane_tiles():
    return _col_blocks(D_MODEL // HEAD_TILE, HEAD_TILE)


def qk_norm_fwd(qkv, qg, kg):
    s = qkv.shape[0]
    tm = min(256, s)

    def body(q_ref, k_ref, v_ref, qg_ref, kg_ref, qn_ref, kn_ref, vb_ref):
        ones = _pair_ones()
        for cols in _lane_tiles():
            qv = q_ref[:, cols]
            kv = k_ref[:, cols]
            qn_ref[:, cols] = (qv * _pair_rstd(qv, ones) * qg_ref[...]).astype(BF16)
            kn_ref[:, cols] = (kv * _pair_rstd(kv, ones) * kg_ref[...]).astype(BF16)
        vb_ref[...] = v_ref[...].astype(BF16)

    col = lambda b: pl.BlockSpec((tm, D_MODEL), lambda i: (i, b))
    out = col(0)
    shape = jax.ShapeDtypeStruct((s, D_MODEL), BF16)
    return pl.pallas_call(
        body, grid=(s // tm,), in_specs=[col(0), col(1), col(2), _row_spec(HEAD_TILE), _row_spec(HEAD_TILE)],
        out_specs=[out, out, out], out_shape=[shape, shape, shape], name="qk_norm_fwd",
        compiler_params=_cp("parallel"))(qkv, qkv, qkv, qg, kg)


def qk_norm_bwd(dqn, dkn, dv, qkv, qg, kg):
    s = qkv.shape[0]
    tm = min(256, s)

    def body(dqn_ref, dkn_ref, dv_ref, q_ref, k_ref, qg_ref, kg_ref, dq_ref, dk_ref, dvb_ref, dqg_ref, dkg_ref):
        @pl.when(pl.program_id(0) == 0)
        def _():
            dqg_ref[...] = jnp.zeros_like(dqg_ref)
            dkg_ref[...] = jnp.zeros_like(dkg_ref)

        ones = _pair_ones()

        def one(x_ref, g_ref, dn_ref, dx_ref, dgain_ref):
            dgain = jnp.zeros((1, HEAD_TILE), F32)
            for cols in _lane_tiles():
                xv = x_ref[:, cols]
                r = _pair_rstd(xv, ones)
                xn = xv * r
                dn = dn_ref[:, cols]
                dgain = dgain + jnp.sum(dn * xn, axis=0, keepdims=True)
                t1 = dn * g_ref[...]
                dx_ref[:, cols] = (r * (t1 - xn * _pair_mean(t1 * xn, ones))).astype(BF16)
            dgain_ref[...] += dgain

        one(q_ref, qg_ref, dqn_ref, dq_ref, dqg_ref)
        one(k_ref, kg_ref, dkn_ref, dk_ref, dkg_ref)
        dvb_ref[...] = dv_ref[...].astype(BF16)

    col = lambda b: pl.BlockSpec((tm, D_MODEL), lambda i: (i, b))
    out = col(0)
    gain = _row_spec(HEAD_TILE)
    shape = jax.ShapeDtypeStruct((s, D_MODEL), BF16)
    grow = jax.ShapeDtypeStruct((1, HEAD_TILE), F32)
    return pl.pallas_call(
        body, grid=(s // tm,), in_specs=[out, out, out, col(0), col(1), gain, gain],
        out_specs=[out, out, out, gain, gain], out_shape=[shape, shape, shape, grow, grow], name="qk_norm_bwd",
        compiler_params=_cp("arbitrary"))(dqn, dkn, dv, qkv, qkv, qg, kg)


def _split_dot(x, u):
    hi = x.astype(BF16)
    lo = (x - hi.astype(F32)).astype(BF16)
    if u.shape[0] == 2 * x.shape[1]:
        return _dot(jnp.concatenate([hi, lo], axis=1), u, NN)
    return _dot(hi, u, NN) + _dot(lo, u, NN)


def _sb_tile(qs, kb, carry, causal, suffix, diag):
    z = _dot(qs, kb, NT)
    lb = jnp.minimum(z, 0.0) - jnp.log(1.0 + jnp.exp(-jnp.abs(z)))
    lom = lb - z
    if diag:
        lom = jnp.where(causal, lom, 0.0)
    ws = []
    for j in reversed(range(ATT_TILE // ATT_BLOCK)):
        cols = slice(j * ATT_BLOCK, (j + 1) * ATT_BLOCK)
        ws.append(jnp.exp(z[:, cols] + _split_dot(lom[:, cols], suffix) + carry))
        carry = carry + jnp.sum(lom[:, cols], axis=-1, keepdims=True)
    w = jnp.concatenate(ws[::-1], axis=1)
    if diag:
        w = jnp.where(causal, w, 0.0)
    return lb, w, carry


def _sb_consts():
    row = lax.broadcasted_iota(jnp.int32, (ATT_BLOCK, ATT_BLOCK), 0)
    col = lax.broadcasted_iota(jnp.int32, (ATT_BLOCK, ATT_BLOCK), 1)
    suffix = (row >= col).astype(BF16)
    suffix = jnp.concatenate([suffix, suffix], axis=0)
    trow = lax.broadcasted_iota(jnp.int32, (ATT_TILE, ATT_TILE), 0)
    tcol = lax.broadcasted_iota(jnp.int32, (ATT_TILE, ATT_TILE), 1)
    causal = tcol < trow
    lo = lax.broadcasted_iota(jnp.int32, (ATT_TILE, HEAD_TILE), 1) < SB_HEAD_DIM
    return suffix, causal, lo


def _rows(i):
    return pl.ds(pl.multiple_of(i * ATT_TILE, ATT_TILE), ATT_TILE)


def _head_query(qb, mask):
    return (jnp.where(mask, qb, 0.0) * (SB_HEAD_DIM ** -0.5)).astype(BF16)


def sb_attn_fwd(qn, kn, vb):
    s = qn.shape[0]
    nq = s // ATT_TILE
    nt = D_MODEL // HEAD_TILE

    def body(q_ref, k_ref, v_ref, o_ref, ob_ref):
        suffix, causal, lo = _sb_consts()

        def qtile(qi, _):
            qb = q_ref[_rows(qi), :].astype(F32)
            qs = [_head_query(qb, lo), _head_query(qb, ~lo)]

            def step(kj, state, diag):
                kb = k_ref[_rows(kj), :]
                vt = v_ref[_rows(kj), :]
                out = []
                for hh in range(2):
                    carry, acc = state[hh]
                    _, w, carry = _sb_tile(qs[hh], kb, carry, causal, suffix, diag)
                    out.append((carry, acc + _dot(w.astype(BF16), vt, NN)))
                return tuple(out)

            start = (jnp.zeros((ATT_TILE, 1), F32), jnp.zeros((ATT_TILE, HEAD_TILE), F32))
            state = step(qi, (start, start), True)
            state = lax.fori_loop(0, qi, lambda jj, st: step(qi - 1 - jj, st, False), state)
            o = jnp.where(lo, state[0][1], state[1][1])
            o_ref[_rows(qi), :] = o
            ob_ref[_rows(qi), :] = o.astype(BF16)
            return 0

        lax.fori_loop(0, nq, qtile, 0)

    spec = pl.BlockSpec((s, HEAD_TILE), lambda p: (0, p))
    return pl.pallas_call(
        body, grid=(nt,), in_specs=[spec, spec, spec], out_specs=[spec, spec],
        out_shape=[jax.ShapeDtypeStruct((s, D_MODEL), F32), jax.ShapeDtypeStruct((s, D_MODEL), BF16)],
        name="sb_attn_fwd", compiler_params=_cp("parallel"))(qn, kn, vb)


def sb_attn_bwd(qn, kn, vb, o, do):
    s = qn.shape[0]
    nq = s // ATT_TILE
    nt = D_MODEL // HEAD_TILE

    def body(q_ref, k_ref, v_ref, o_ref, do_ref, dq_ref, dk_ref, dv_ref):
        suffix, causal, lo = _sb_consts()
        dk_ref[...] = jnp.zeros_like(dk_ref)
        dv_ref[...] = jnp.zeros_like(dv_ref)

        def qtile(qi, _):
            qb = q_ref[_rows(qi), :].astype(F32)
            dob = do_ref[_rows(qi), :].astype(BF16).astype(F32)
            prod = dob * o_ref[_rows(qi), :]
            masks = [lo, ~lo]
            qs = [_head_query(qb, m) for m in masks]
            dos = [jnp.where(m, dob, 0.0).astype(BF16) for m in masks]
            totals = [jnp.sum(jnp.where(m, prod, 0.0), axis=-1, keepdims=True) for m in masks]

            def step(kj, state, diag):
                kb = k_ref[_rows(kj), :]
                vt = v_ref[_rows(kj), :]
                out = []
                dk = jnp.zeros((ATT_TILE, HEAD_TILE), F32)
                dv = jnp.zeros((ATT_TILE, HEAD_TILE), F32)
                for hh in range(2):
                    carry, carry_e, dq = state[hh]
                    lb, w, carry = _sb_tile(qs[hh], kb, carry, causal, suffix, diag)
                    wb = w.astype(BF16)
                    e = _dot(dos[hh], vt, NT) * wb.astype(F32)
                    befores = []
                    for j in reversed(range(ATT_TILE // ATT_BLOCK)):
                        cols = slice(j * ATT_BLOCK, (j + 1) * ATT_BLOCK)
                        befores.append(totals[hh] - carry_e - _split_dot(e[:, cols], suffix))
                        carry_e = carry_e + jnp.sum(e[:, cols], axis=-1, keepdims=True)
                    before = jnp.concatenate(befores[::-1], axis=1)
                    beta = jnp.exp(lb)
                    dz = e * (1.0 - beta) - before * beta
                    if diag:
                        dz = jnp.where(causal, dz, 0.0)
                    dzb = dz.astype(BF16)
                    dq = dq + _dot(dzb, kb, NN)
                    dk = dk + _dot(dzb, qs[hh], TN)
                    dv = dv + _dot(wb, dos[hh], TN)
                    out.append((carry, carry_e, dq))
                dk_ref[_rows(kj), :] += dk
                dv_ref[_rows(kj), :] += dv
                return tuple(out)

            zero = jnp.zeros((ATT_TILE, 1), F32)
            start = (zero, zero, jnp.zeros((ATT_TILE, HEAD_TILE), F32))
            state = step(qi, (start, start), True)
            state = lax.fori_loop(0, qi, lambda jj, st: step(qi - 1 - jj, st, False), state)
            dq_ref[_rows(qi), :] = jnp.where(lo, state[0][2], state[1][2]) * (SB_HEAD_DIM ** -0.5)
            return 0

        lax.fori_loop(0, nq, qtile, 0)

    spec = pl.BlockSpec((s, HEAD_TILE), lambda p: (0, p))
    shape = jax.ShapeDtypeStruct((s, D_MODEL), F32)
    return pl.pallas_call(
        body, grid=(nt,), in_specs=[spec] * 5, out_specs=[spec] * 3, out_shape=[shape] * 3,
        name="sb_attn_bwd", compiler_params=_cp("parallel"))(qn, kn, vb, o, do)


def ada_fwd(c_all, ada_w, ada_b_shard):
    layers, d, cols = ada_w.shape
    tn = 512

    def body(c_ref, w_ref, b_ref, out_ref):
        cv = c_ref[...]
        act = (cv * _sigmoid(cv)).astype(BF16)
        out_ref[...] = _dot(act, w_ref[...].astype(BF16), NN) + b_ref[...]

    return pl.pallas_call(
        body, grid=(layers, cols // tn),
        in_specs=[pl.BlockSpec((N_DEV, d), lambda l, j: (0, 0)), pl.BlockSpec((None, d, tn), lambda l, j: (l, 0, j)),
                  pl.BlockSpec((None, 1, tn), lambda l, j: (l, 0, j))],
        out_specs=pl.BlockSpec((None, N_DEV, tn), lambda l, j: (l, 0, j)),
        out_shape=jax.ShapeDtypeStruct((layers, N_DEV, cols), F32), name="ada_fwd",
        compiler_params=_cp("parallel", "parallel"))(c_all, ada_w, ada_b_shard)


def ada_w_grad(c_all, dmod):
    layers, _, cols = dmod.shape
    d = c_all.shape[1]
    tn = 512

    def body(c_ref, dm_ref, out_ref):
        cv = c_ref[...]
        out_ref[...] = _dot(cv * _sigmoid(cv), dm_ref[...], TN, HIGHEST)

    return pl.pallas_call(
        body, grid=(layers, cols // tn),
        in_specs=[pl.BlockSpec((N_DEV, d), lambda l, j: (0, 0)), pl.BlockSpec((None, N_DEV, tn), lambda l, j: (l, 0, j))],
        out_specs=pl.BlockSpec((None, d, tn), lambda l, j: (l, 0, j)),
        out_shape=jax.ShapeDtypeStruct((layers, d, cols), F32), name="ada_w_grad",
        compiler_params=_cp("parallel", "parallel"))(c_all, dmod)


def _row_tile(r, c, elems):
    best = 8
    for t in range(8, r + 1, 8):
        if r % t == 0 and t * c <= elems:
            best = t
    return best


def sum_rows(x, name):
    n, r, c = x.shape
    tr = _row_tile(r, c, 1 << 17)

    def body(x_ref, out_ref):
        acc = x_ref[0]
        for i in range(1, n):
            acc = acc + x_ref[i]
        out_ref[...] = acc

    return pl.pallas_call(
        body, grid=(r // tr,), in_specs=[pl.BlockSpec((n, tr, c), lambda i: (0, i, 0))],
        out_specs=pl.BlockSpec((tr, c), lambda i: (i, 0)), out_shape=jax.ShapeDtypeStruct((r, c), F32), name=name,
        compiler_params=_cp("parallel"))(x)


def adamw(w, g, m, v, name):
    r, c = w.shape
    tr = _row_tile(r, c, 1 << 17)
    c1 = 1.0 - ADAM_B1 ** ADAM_STEP
    c2 = 1.0 - ADAM_B2 ** ADAM_STEP

    def body(w_ref, g_ref, m_ref, v_ref, d_ref, nm_ref, nv_ref):
        gv = g_ref[...]
        nm = ADAM_B1 * m_ref[...] + (1.0 - ADAM_B1) * gv
        nv = ADAM_B2 * v_ref[...] + (1.0 - ADAM_B2) * (gv * gv)
        d_ref[...] = -ADAM_LR * ((nm / c1) / (jnp.sqrt(nv / c2) + ADAM_EPS) + ADAM_WD * w_ref[...])
        nm_ref[...] = nm
        nv_ref[...] = nv

    spec = pl.BlockSpec((tr, c), lambda i: (i, 0))
    shape = jax.ShapeDtypeStruct((r, c), F32)
    return pl.pallas_call(
        body, grid=(r // tr,), in_specs=[spec] * 4, out_specs=[spec] * 3, out_shape=[shape] * 3, name=name,
        compiler_params=_cp("parallel"))(w, g, m, v)


def _me():
    return lax.axis_index("x"), lax.axis_index("y"), lax.axis_index("c")


def _flip(v, bit):
    return 1 - v if bit else v


HBM = pl.BlockSpec(memory_space=pl.ANY)
VMEM = pl.BlockSpec(memory_space=pltpu.VMEM)


def all_gather_rows(v, name):
    n = v.shape[1]

    def body(v_ref, out_ref, send_sems, recv_sems):
        x, y, c = _me()
        me = 4 * x + 2 * y + c
        out_ref[pl.ds(me, 1), :] = v_ref[...]
        copies = []
        for j in range(1, N_DEV):
            peer = (_flip(x, j & 4), _flip(y, j & 2), _flip(c, j & 1))
            copies.append(pltpu.make_async_remote_copy(
                src_ref=v_ref, dst_ref=out_ref.at[pl.ds(me, 1), :], send_sem=send_sems.at[j - 1],
                recv_sem=recv_sems.at[j - 1], device_id=peer, device_id_type=MESH))
        for cp in copies:
            cp.start()
        for cp in copies:
            cp.wait()

    return pl.pallas_call(
        body, in_specs=[VMEM], out_specs=VMEM, out_shape=jax.ShapeDtypeStruct((N_DEV, n), F32),
        scratch_shapes=[pltpu.SemaphoreType.DMA((N_DEV - 1,)), pltpu.SemaphoreType.DMA((N_DEV - 1,))], name=name)(v)


def _chip_peers(x, y):
    return [((1 - x, y), 2 * (1 - x) + y), ((x, 1 - y), 2 * x + (1 - y)), ((1 - x, 1 - y), 2 * (1 - x) + (1 - y))]


def cast_to_slot(w, name):
    r, c = w.shape
    tr = _row_tile(r, c, 1 << 18)

    def body(w_ref, out_ref):
        out_ref[...] = w_ref[...].astype(BF16)

    return pl.pallas_call(
        body, grid=(r // tr,), in_specs=[pl.BlockSpec((tr, c), lambda i: (i, 0))],
        out_specs=pl.BlockSpec((None, tr, c), lambda i: (2 * lax.axis_index("x") + lax.axis_index("y"), i, 0)),
        out_shape=jax.ShapeDtypeStruct((N_CHIPS, r, c), BF16), name=name, compiler_params=_cp("parallel"))(w)


def chip_all_gather(slots, name):
    n = len(slots)

    def body(*refs):
        gather_start(refs[n:2 * n], *refs[2 * n:2 * n + 2])
        gather_finish(refs[n:2 * n], *refs[2 * n:])

    return pl.pallas_call(
        body, in_specs=[HBM] * n, out_specs=[HBM] * n,
        out_shape=[jax.ShapeDtypeStruct(s.shape, s.dtype) for s in slots],
        input_output_aliases={a: a for a in range(n)}, scratch_shapes=gather_semaphores(n), name=name)(*slots)


def gather_semaphores(n):
    return [pltpu.SemaphoreType.DMA((n, 3))] * 4


def _chip_copies(outs, send_sems, recv_sems):
    x, y, c = _me()
    copies = []
    for a, out in enumerate(outs):
        half = out.shape[1] // 2
        rows = out.at[2 * x + y, pl.ds(c * half, half), :]
        for p, (chip, _) in enumerate(_chip_peers(x, y)):
            copies.append(pltpu.make_async_remote_copy(
                src_ref=rows, dst_ref=rows, send_sem=send_sems.at[a, p], recv_sem=recv_sems.at[a, p],
                device_id=(*chip, c), device_id_type=MESH))
    return copies


def gather_start(outs, send_sems, recv_sems):
    for cp in _chip_copies(outs, send_sems, recv_sems):
        cp.start()


def gather_finish(outs, send_sems, recv_sems, pass_send_sems, pass_recv_sems):
    x, y, c = _me()
    peers = _chip_peers(x, y)
    passed = []
    for a, out in enumerate(outs):
        half = out.shape[1] // 2
        for p, (chip, slot) in enumerate(peers):
            rows = out.at[slot, pl.ds(c * half, half), :]
            pltpu.make_async_remote_copy(
                src_ref=rows, dst_ref=rows, send_sem=send_sems.at[a, p], recv_sem=recv_sems.at[a, p],
                device_id=(*chip, c), device_id_type=MESH).wait_recv()
            cp = pltpu.make_async_remote_copy(
                src_ref=rows, dst_ref=rows, send_sem=pass_send_sems.at[a, p], recv_sem=pass_recv_sems.at[a, p],
                device_id=(x, y, 1 - c), device_id_type=MESH)
            cp.start()
            passed.append(cp)
    for a, out in enumerate(outs):
        half = out.shape[1] // 2
        for p, (_, slot) in enumerate(peers):
            theirs = out.at[slot, pl.ds((1 - c) * half, half), :]
            pltpu.make_async_remote_copy(
                src_ref=theirs, dst_ref=theirs, send_sem=pass_send_sems.at[a, p], recv_sem=pass_recv_sems.at[a, p],
                device_id=(x, y, 1 - c), device_id_type=MESH).wait_recv()
    for cp in _chip_copies(outs, send_sems, recv_sems) + passed:
        cp.wait_send()


def sibling_split(grads, name):
    n = len(grads)

    def body(*refs):
        ins, got = refs[:n], refs[n:2 * n]
        send_sems, recv_sems = refs[2 * n:]
        x, y, c = _me()
        started = []
        for a in range(n):
            half = ins[a].shape[1] // 2
            give = pltpu.make_async_remote_copy(
                src_ref=ins[a].at[:, pl.ds((1 - c) * half, half), :], dst_ref=got[a], send_sem=send_sems.at[a],
                recv_sem=recv_sems.at[a], device_id=(x, y, 1 - c), device_id_type=MESH)
            give.start()
            started.append(give)
        for cp in started:
            cp.wait()

    return pl.pallas_call(
        body, in_specs=[HBM] * n, out_specs=[HBM] * n,
        out_shape=[jax.ShapeDtypeStruct((g.shape[0], g.shape[1] // 2, g.shape[2]), g.dtype) for g in grads],
        scratch_shapes=[pltpu.SemaphoreType.DMA((n,)), pltpu.SemaphoreType.DMA((n,))], name=name)(*grads)


def pair_sum(g, got, name):
    k, half, c = got.shape
    tr = _row_tile(half, c, 1 << 18)
    nb = half // tr

    def body(g_ref, got_ref, out_ref):
        out_ref[...] = (g_ref[...] + got_ref[...]).astype(BF16)

    spec = pl.BlockSpec((None, tr, c), lambda j, i: (j, i, 0))
    return pl.pallas_call(
        body, grid=(k, nb),
        in_specs=[pl.BlockSpec((None, tr, c), lambda j, i: (j, lax.axis_index("c") * nb + i, 0)), spec],
        out_specs=spec, out_shape=jax.ShapeDtypeStruct(got.shape, BF16), name=name,
        compiler_params=_cp("parallel", "parallel"))(g, got)


def chip_scatter(parts, name):
    n = len(parts)

    def body(*refs):
        copies = scatter_copies(refs[:n], refs[n:2 * n], *refs[2 * n:])
        for cp in copies:
            cp.start()
        for cp in copies:
            cp.wait()

    return pl.pallas_call(
        body, in_specs=[HBM] * n, out_specs=[HBM] * n, out_shape=scatter_shapes(parts),
        scratch_shapes=scatter_semaphores(n), name=name)(*parts)


def scatter_shapes(parts):
    return [jax.ShapeDtypeStruct((3, *p.shape[1:]), p.dtype) for p in parts]


def scatter_semaphores(n):
    return [pltpu.SemaphoreType.DMA((n, 3))] * 2


def scatter_copies(ins, outs, send_sems, recv_sems):
    x, y, c = _me()
    return [pltpu.make_async_remote_copy(
        src_ref=ins[a].at[slot], dst_ref=outs[a].at[p], send_sem=send_sems.at[a, p], recv_sem=recv_sems.at[a, p],
        device_id=(*chip, c), device_id_type=MESH)
        for a in range(len(ins)) for p, (chip, slot) in enumerate(_chip_peers(x, y))]


def chip_sum(part, landed, name):
    _, half, c = landed.shape
    tr = _row_tile(half, c, 1 << 17)
    nb = half // tr

    def body(part_ref, landed_ref, out_ref):
        up = lambda v: v.astype(F32)
        out_ref[...] = ((up(part_ref[...]) + up(landed_ref[0])) + up(landed_ref[1])) + up(landed_ref[2])

    return pl.pallas_call(
        body, grid=(nb,),
        in_specs=[pl.BlockSpec((None, tr, c), lambda i: (2 * lax.axis_index("x") + lax.axis_index("y"), i, 0)),
                  pl.BlockSpec((3, tr, c), lambda i: (0, i, 0))],
        out_specs=pl.BlockSpec((tr, c), lambda i: (lax.axis_index("c") * nb + i, 0)),
        out_shape=jax.ShapeDtypeStruct((2 * half, c), F32), name=name, compiler_params=_cp("parallel"))(part, landed)


def sibling_join(arrays, name):
    n = len(arrays)

    def body(*refs):
        ins, outs = refs[:n], refs[n:2 * n]
        send_sems, recv_sems = refs[2 * n:]
        x, y, c = _me()
        started = []
        for a in range(n):
            half = ins[a].shape[0] // 2
            give = pltpu.make_async_remote_copy(
                src_ref=ins[a].at[pl.ds(c * half, half), :], dst_ref=outs[a].at[pl.ds(c * half, half), :],
                send_sem=send_sems.at[a], recv_sem=recv_sems.at[a], device_id=(x, y, 1 - c), device_id_type=MESH)
            give.start()
            started.append(give)
        for cp in started:
            cp.wait()

    return pl.pallas_call(
        body, in_specs=[HBM] * n, out_specs=[HBM] * n,
        out_shape=[jax.ShapeDtypeStruct(h.shape, h.dtype) for h in arrays],
        input_output_aliases={a: a for a in range(n)},
        scratch_shapes=[pltpu.SemaphoreType.DMA((n,)), pltpu.SemaphoreType.DMA((n,))], name=name)(*arrays)


def _as_chunks(g):
    return g if g.ndim == 3 else g.reshape(N_CHIPS, g.shape[0] // N_CHIPS, g.shape[1])


def reduce_parts(grads, tag):
    grads = [_as_chunks(g) for g in grads]
    got = sibling_split(grads, f"grad_sibling_split_{tag}")
    return [pair_sum(g, h, f"grad_pair_sum_{tag}{a}") for a, (g, h) in enumerate(zip(grads, got))]


def reduce_finish(parts, landed):
    halves = [chip_sum(p, l, f"grad_chip_sum_{a}") for a, (p, l) in enumerate(zip(parts, landed))]
    return sibling_join(halves, "grad_sibling_join")


def _pad_row(v, n):
    return jnp.pad(v.reshape(1, -1), ((0, 0), (0, n - v.size)))


LAST_GRADS = ("w_in",)
RIDING_GRADS = ("w_out_ab", "w1_0", "w2_0", "w_qkv", "w_out_c", "w1_1", "w2_1")


def local_step(x, target, mod, wts, ride_exchange=False):
    d = D_MODEL
    row = lambda v: v.reshape(1, -1)
    mods = [[row(mod[l, i * d:(i + 1) * d]) for i in range(6)] for l in range(2)]
    saved = []
    for l in range(2):
        shift1, scale1, gate1, shift2, scale2, gate2 = mods[l]
        g_mix, g_mlp = row(wts["norm_mix"][l]), row(wts["norm_mlp"][l])
        h = norm_mod_fwd(x, g_mix, scale1, shift1, f"norm_mix_fwd_{l}")
        if l == 0:
            u = matmul_nn_chunked(h, wts["w_in"], F32, "in_proj_ab")[0]
            y_a = conv_mixer_fwd(u, wts["conv_w"])
            y_b, o_raw, states, *rode = hgrn_fwd(u, wts["lb_logits"], wts["hg_norm"], wts.get("riding", ()))
            if rode:
                r_qkv, r_out_c, r_w1, r_w2 = rode
                wts = dict(wts, w_qkv=r_qkv, w_out_c=r_out_c.reshape(d, d), w1=[wts["w1"][0], r_w1],
                           w2=[wts["w2"][0], r_w2.reshape(D_FF, d)])
            mix = jnp.concatenate([y_a, y_b], axis=1)
            y, x1 = proj_residual(mix, wts["w_out_ab"], x, gate1, "out_proj_ab")
            ctx = (u, o_raw, states)
        else:
            qkv = matmul_nn_chunked(h, wts["w_qkv"], F32, "in_proj_c")[0]
            qn, kn, vb = qk_norm_fwd(qkv, wts["qg"], wts["kg"])
            o, mix = sb_attn_fwd(qn, kn, vb)
            y, x1 = proj_residual(mix, wts["w_out_c"], x, gate1, "out_proj_c")
            ctx = (qkv, qn, kn, vb, o)
        h2 = norm_mod_fwd(x1, g_mlp, scale2, shift2, f"norm_mlp_fwd_{l}")
        act, r = mlp_up(h2, wts["w1"][l], f"mlp_up_{l}")
        y2, x2 = proj_residual(act, wts["w2"][l], x1, gate2, f"mlp_down_{l}")
        saved.append((x, h, mix, y, x1, h2, act, r, y2, ctx))
        x = x2

    dx, loss_row = loss_and_grad(x, target)
    small, big = {}, {}
    dmod = [None, None]
    d_norm_mix, d_norm_mlp = [None, None], [None, None]
    for l in (1, 0):
        shift1, scale1, gate1, shift2, scale2, gate2 = mods[l]
        g_mix, g_mlp = row(wts["norm_mix"][l]), row(wts["norm_mlp"][l])
        x0, h, mix, y, x1, h2, act, r, y2, ctx = saved[l]
        dy2, dgate2 = gate_bwd(dx, y2, gate2, f"mlp_gate_bwd_{l}")
        dz = mlp_down_bwd(dy2, wts["w2"][l], r, f"mlp_down_bwd_{l}")
        big[f"w2_{l}"] = matmul_tn_plain(act, dy2, f"mlp_w2_grad_{l}")
        dh2 = matmul_nt_chunked(dz, wts["w1"][l], f"mlp_up_bwd_{l}")
        big[f"w1_{l}"] = matmul_tn_chunked(h2, dz, f"mlp_w1_grad_{l}")
        dx1, d_norm_mlp[l], dscale2, dshift2 = norm_mod_bwd(dh2, x1, g_mlp, scale2, dx, f"norm_mlp_bwd_{l}")
        dy, dgate1 = gate_bwd(dx1, y, gate1, f"mix_gate_bwd_{l}")
        if l == 0:
            u, o_raw, states = ctx
            dmix = matmul_nt_plain(dy, wts["w_out_ab"], "out_proj_ab_bwd")
            big["w_out_ab"] = matmul_tn_plain(mix, dy, "w_out_ab_grad")
            dab, dac, dah, small["conv_w"] = conv_mixer_bwd(dmix, u, wts["conv_w"])
            late = [big[k] for k in RIDING_GRADS] if ride_exchange else []
            big["second_parts"] = reduce_parts(late, "second") if late else []
            dhq, dhf, dhi, dhg, small["hg_norm"], small["lb_logits"], *big["second_landed"] = hgrn_bwd(
                dmix, u, o_raw, states, wts["lb_logits"], wts["hg_norm"], big["second_parts"])
            du = jnp.concatenate([dab, dac, dah, dhq, dhf, dhi, dhg], axis=1)
            dh = matmul_nt_chunked(du, wts["w_in"], "in_proj_ab_bwd")
            big["w_in"] = matmul_tn_chunked(h, du, "w_in_grad")
        else:
            qkv, qn, kn, vb, o = ctx
            do = matmul_nt_plain(dy, wts["w_out_c"], "out_proj_c_bwd")
            big["w_out_c"] = matmul_tn_plain(mix, dy, "w_out_c_grad")
            dqn, dkn, dv = sb_attn_bwd(qn, kn, vb, o, do)
            dq, dk, dvb, dqg, dkg = qk_norm_bwd(dqn, dkn, dv, qkv, wts["qg"], wts["kg"])
            small["q_norm"] = dqg[:, :SB_HEAD_DIM] + dqg[:, SB_HEAD_DIM:]
            small["k_norm"] = dkg[:, :SB_HEAD_DIM] + dkg[:, SB_HEAD_DIM:]
            dqkv = jnp.concatenate([dq, dk, dvb], axis=1)
            dh = matmul_nt_chunked(dqkv, wts["w_qkv"], "in_proj_c_bwd")
            big["w_qkv"] = matmul_tn_chunked(h, dqkv, "w_qkv_grad")
        dx, d_norm_mix[l], dscale1, dshift1 = norm_mod_bwd(dh, x0, g_mix, scale1, dx1, f"norm_mix_bwd_{l}")
        dmod[l] = jnp.concatenate([dshift1, dscale1, dgate1, dshift2, dscale2, dgate2], axis=1)
    small["mod"] = jnp.concatenate(dmod, axis=0)
    small["norm_mix"] = jnp.concatenate(d_norm_mix, axis=0)
    small["norm_mlp"] = jnp.concatenate(d_norm_mlp, axis=0)
    return loss_row, dx, small, big


SMALL_ORDER = ("mod", "norm_mix", "norm_mlp", "conv_w", "hg_norm", "lb_logits", "q_norm", "k_norm")


def kernel(x, c, ada_w, ada_b, norm_mix, norm_mlp, w_in_ab, conv_w, hg_norm, lb_logits, w_out_ab, w_qkv, q_norm, k_norm, w_out_c, mlp_w1, mlp_w2, loss_target, m_ada_w, m_ada_b, m_norm_mix, m_norm_mlp, m_w_in_ab, m_conv_w, m_hg_norm, m_lb_logits, m_w_out_ab, m_w_qkv, m_q_norm, m_k_norm, m_w_out_c, m_mlp_w1, m_mlp_w2, v_ada_w, v_ada_b, v_norm_mix, v_norm_mlp, v_w_in_ab, v_conv_w, v_hg_norm, v_lb_logits, v_w_out_ab, v_w_qkv, v_q_norm, v_k_norm, v_w_out_c, v_mlp_w1, v_mlp_w2):
    d = D_MODEL
    ax, ay, ac = _me()
    chip = 2 * ax + ay
    dev = 2 * chip + ac
    cols = ada_w.shape[2]

    first = all_gather_rows(_pad_row(jnp.concatenate([c.reshape(-1), conv_w.reshape(-1)]), 1536), "gather_cond")
    c_all = first[:, :d]
    conv_full = first[::2, d:d + 3 * HEAD_TILE].reshape(N_CHIPS, 3, HEAD_TILE).transpose(1, 0, 2).reshape(3, CONV_DIM)
    ada_b_shard = lax.dynamic_slice(ada_b, (0, chip * cols), (2, cols)).reshape(2, 1, cols)
    mod_cols = ada_fwd(c_all, ada_w, ada_b_shard)
    mod_all = all_gather_rows(mod_cols.reshape(1, -1), "gather_mod").reshape(N_DEV, 2, N_DEV, cols)
    mod = lax.dynamic_index_in_dim(mod_all[::2], dev, axis=2, keepdims=False).transpose(1, 0, 2).reshape(2, 6 * d)

    shards = [w_in_ab[0], w_out_ab[0], mlp_w1[0], mlp_w2[0], w_qkv[0], w_out_c[0], mlp_w1[1], mlp_w2[1]]
    slots = [cast_to_slot(s, f"cast_weight_{a}") for a, s in enumerate(shards)]
    g_in, g_out_ab, g_w1a, g_w2a = chip_all_gather(slots[:4], "gather_weights")
    wts = dict(
        norm_mix=norm_mix, norm_mlp=norm_mlp, w_in=g_in, conv_w=conv_full, hg_norm=hg_norm, lb_logits=lb_logits,
        w_out_ab=g_out_ab.reshape(d, d), qg=jnp.tile(q_norm, (1, 2)), kg=jnp.tile(k_norm, (1, 2)),
        w1=[g_w1a], w2=[g_w2a.reshape(D_FF, d)], riding=slots[4:])

    loss_row, grad_x, small, big = local_step(x[0], loss_target[0], mod, wts, ride_exchange=True)

    flat = jnp.concatenate([small[k].reshape(-1) for k in SMALL_ORDER] + [loss_row[0, :1]])
    n_small = -(-flat.size // 1024) * 1024
    gathered = all_gather_rows(_pad_row(flat, n_small), "gather_small")
    total = sum_rows(gathered.reshape(N_DEV, 8, n_small // 8), "small_sum").reshape(-1)
    sizes = [small[k].size for k in SMALL_ORDER]
    offs = [sum(sizes[:i]) for i in range(len(sizes) + 1)]
    tot = {k: total[offs[i]:offs[i + 1]].reshape(small[k].shape) for i, k in enumerate(SMALL_ORDER)}
    loss = total[offs[-1]]
    mod_rows = gathered[:, :2 * 6 * d].reshape(N_DEV, 2, 6 * d)
    dmod_cols = lax.dynamic_slice(mod_rows, (0, 0, chip * cols), (N_DEV, 2, cols)).transpose(1, 0, 2)
    g_ada_w = ada_w_grad(c_all, dmod_cols)

    first_parts = reduce_parts([big[k] for k in LAST_GRADS], "first")
    first_landed = chip_scatter(first_parts, "grad_chip_scatter")
    r_in, r_out_ab, r_w1a, r_w2a, r_qkv, r_out_c, r_w1b, r_w2b = reduce_finish(
        list(first_parts) + list(big["second_parts"]), list(first_landed) + list(big["second_landed"]))

    grads = dict(
        ada_w=g_ada_w, ada_b=tot["mod"], norm_mix=tot["norm_mix"], norm_mlp=tot["norm_mlp"], w_in_ab=r_in[None],
        conv_w=lax.dynamic_slice(tot["conv_w"], (0, chip * HEAD_TILE), (3, HEAD_TILE))[None], hg_norm=tot["hg_norm"],
        lb_logits=tot["lb_logits"], w_out_ab=r_out_ab[None], w_qkv=r_qkv[None], q_norm=tot["q_norm"],
        k_norm=tot["k_norm"], w_out_c=r_out_c[None], mlp_w1=jnp.stack([r_w1a, r_w1b]), mlp_w2=jnp.stack([r_w2a, r_w2b]))
    weights = dict(ada_w=ada_w, ada_b=ada_b, norm_mix=norm_mix, norm_mlp=norm_mlp, w_in_ab=w_in_ab, conv_w=conv_w,
                   hg_norm=hg_norm, lb_logits=lb_logits, w_out_ab=w_out_ab, w_qkv=w_qkv, q_norm=q_norm, k_norm=k_norm,
                   w_out_c=w_out_c, mlp_w1=mlp_w1, mlp_w2=mlp_w2)
    m_in = dict(ada_w=m_ada_w, ada_b=m_ada_b, norm_mix=m_norm_mix, norm_mlp=m_norm_mlp, w_in_ab=m_w_in_ab,
                conv_w=m_conv_w, hg_norm=m_hg_norm, lb_logits=m_lb_logits, w_out_ab=m_w_out_ab, w_qkv=m_w_qkv,
                q_norm=m_q_norm, k_norm=m_k_norm, w_out_c=m_w_out_c, mlp_w1=m_mlp_w1, mlp_w2=m_mlp_w2)
    v_in = dict(ada_w=v_ada_w, ada_b=v_ada_b, norm_mix=v_norm_mix, norm_mlp=v_norm_mlp, w_in_ab=v_w_in_ab,
                conv_w=v_conv_w, hg_norm=v_hg_norm, lb_logits=v_lb_logits, w_out_ab=v_w_out_ab, w_qkv=v_w_qkv,
                q_norm=v_q_norm, k_norm=v_k_norm, w_out_c=v_w_out_c, mlp_w1=v_mlp_w1, mlp_w2=v_mlp_w2)
    order = list(weights)
    large = ("ada_w", "w_in_ab", "w_out_ab", "w_qkv", "w_out_c", "mlp_w1", "mlp_w2")
    delta, new_m, new_v = {}, {}, {}
    for k in large:
        shape = weights[k].shape
        flat2 = lambda a: a.reshape(-1, shape[-1])
        dl, nm, nv = adamw(flat2(weights[k]), flat2(grads[k]), flat2(m_in[k]), flat2(v_in[k]), f"adamw_{k}")
        delta[k], new_m[k], new_v[k] = dl.reshape(shape), nm.reshape(shape), nv.reshape(shape)
    rest = [k for k in order if k not in large]
    n_rest = -(-sum(weights[k].size for k in rest) // 1024) * 1024
    pack = lambda tree: _pad_row(jnp.concatenate([tree[k].reshape(-1) for k in rest]), n_rest).reshape(8, n_rest // 8)
    dl, nm, nv = adamw(pack(weights), pack(grads), pack(m_in), pack(v_in), "adamw_small")
    off = 0
    for k in rest:
        size, shape = weights[k].size, weights[k].shape
        delta[k], new_m[k], new_v[k] = (a.reshape(-1)[off:off + size].reshape(shape) for a in (dl, nm, nv))
        off += size
    grads = {k: grads[k].reshape(weights[k].shape) for k in order}
    return (loss, grad_x[None], *[grads[k] for k in order], *[delta[k] for k in order],
            *[new_m[k] for k in order], *[new_v[k] for k in order])
```
